```python
import math
import jax, jax.numpy as jnp
from jax import lax
import numpy as np

D_MODEL = 1024
BATCH = 2
SEQ = 8192
DEPTH = 2

CHUNK = 64
Q_BLOCK = 128
ROPE_THETA = 10000.0
EPS = 1e-6

A_HEADS = 4
A_HEAD_DIM = 64
A_WIDTH = A_HEADS * 2 * A_HEAD_DIM
B_HEADS = 8
B_HEAD_DIM = 64
B_WIDTH = B_HEADS * B_HEAD_DIM
IDX_HEADS = 4
IDX_DIM = 64
TOPK_MAX = 256
C_HEADS = 8
C_HEAD_DIM = 64
C_WIDTH = C_HEADS * C_HEAD_DIM
C_LEFT_CHUNKS = 8
REL_CLIP = 256
N_EXPERTS = 16
N_GROUPS = 4
EXPERTS_PER_GROUP = N_EXPERTS // N_GROUPS
TOP_K = 2
D_FF_EXPERT = 512
N_BRANCHES = 3

IN_SIZES = (
    A_WIDTH, A_WIDTH, A_WIDTH,
    B_WIDTH, B_HEAD_DIM, B_HEAD_DIM,
    IDX_HEADS * IDX_DIM, IDX_DIM, IDX_HEADS,
    C_WIDTH, C_WIDTH, C_WIDTH,
    N_BRANCHES * D_MODEL,
)
IN_COLS = sum(IN_SIZES)

kernel_name = "hybrid_gated_diff_dsa_chunkband_grouped_moe"


def rms_norm(x, g):
    xf = x.astype(jnp.float32)
    y = xf * lax.rsqrt(jnp.mean(xf * xf, axis=-1, keepdims=True) + EPS)
    return (y * g.astype(jnp.float32)).astype(x.dtype)


def rope(x, positions):
    half = x.shape[-1] // 2
    inv = ROPE_THETA ** (-jnp.arange(half, dtype=jnp.float32) / half)
    ang = positions.astype(jnp.float32)[..., None] * inv
    cos = jnp.cos(ang)[:, :, None, :].astype(x.dtype)
    sin = jnp.sin(ang)[:, :, None, :].astype(x.dtype)
    x1, x2 = x[..., :half], x[..., half:]
    return jnp.concatenate([x1 * cos - x2 * sin, x2 * cos + x1 * sin], axis=-1)


def split_cols(p, sizes):
    out, start = [], 0
    for s in sizes:
        out.append(p[..., start:start + s])
        start += s
    return out


def chunk_causal_mask(q_start, n_q, n_k):
    qc = (q_start + jnp.arange(n_q)) // CHUNK
    kc = jnp.arange(n_k) // CHUNK
    return kc[None, :] <= qc[:, None]


def diff_attention(q, k, v, lam, lam_init, norm_g):
    B, S, H, _, d = q.shape
    scale = d ** -0.5

    def block(i):
        qs = i * Q_BLOCK
        qb = lax.dynamic_slice_in_dim(q, qs, Q_BLOCK, axis=1)
        s = jnp.einsum('bqhmd,bkhmd->bhmqk', qb, k).astype(jnp.float32) * scale
        s = jnp.where(chunk_causal_mask(qs, Q_BLOCK, S), s, -jnp.inf)
        p = jax.nn.softmax(s, axis=-1)
        p = p[:, :, 0] - lam * p[:, :, 1]
        return jnp.einsum('bhqk,bkhe->bqhe', p.astype(v.dtype), v)

    o = lax.map(block, jnp.arange(S // Q_BLOCK))
    o = jnp.moveaxis(o, 0, 1).reshape(B, S, H, 2 * d)
    o = rms_norm(o, norm_g) * (1.0 - lam_init)
    return o.reshape(B, S, H * 2 * d)


def dsa_attention(q, k, v, iq, ik, iw):
    B, S, H, d = q.shape
    topk = min(TOPK_MAX, S // 4)
    scale = d ** -0.5
    b_idx = jnp.arange(B)[:, None, None]

    def block(i):
        qs = i * Q_BLOCK
        qb = lax.dynamic_slice_in_dim(q, qs, Q_BLOCK, axis=1)
        iqb = lax.dynamic_slice_in_dim(iq, qs, Q_BLOCK, axis=1)
        iwb = lax.dynamic_slice_in_dim(iw, qs, Q_BLOCK, axis=1)
        hs = jax.nn.relu(jnp.einsum('bqhe,bke->bqhk', iqb, ik))
        score = jnp.einsum('bqh,bqhk->bqk', iwb, hs).astype(jnp.float32)
        score = jnp.where(chunk_causal_mask(qs, Q_BLOCK, S)[None], score, -jnp.inf)
        _, sel = lax.top_k(score, topk)
        q_chunk = (qs + jnp.arange(Q_BLOCK)) // CHUNK
        valid = (sel // CHUNK) <= q_chunk[None, :, None]
        kg = k[b_idx, sel]
        vg = v[b_idx, sel]
        s = jnp.einsum('bqhd,bqkd->bhqk', qb, kg).astype(jnp.float32) * scale
        s = jnp.where(valid[:, None], s, -jnp.inf)
        p = jax.nn.softmax(s, axis=-1)
        return jnp.einsum('bhqk,bqkd->bqhd', p.astype(vg.dtype), vg)

    o = lax.map(block, jnp.arange(S // Q_BLOCK))
    return jnp.moveaxis(o, 0, 1).reshape(B, S, H * d)


def chunk_band_attention(q, k, v, rel_bias):
    B, S, H, d = q.shape
    nc = S // CHUNK
    band = C_LEFT_CHUNKS + 1
    qc = q.reshape(B, nc, CHUNK, H, d)
    pad = ((0, 0), (C_LEFT_CHUNKS * CHUNK, 0), (0, 0), (0, 0))
    kp = jnp.pad(k, pad).reshape(B, nc + C_LEFT_CHUNKS, CHUNK, H, d)
    vp = jnp.pad(v, pad).reshape(B, nc + C_LEFT_CHUNKS, CHUNK, H, d)
    band_idx = jnp.arange(nc)[:, None] + jnp.arange(band)[None, :]
    kb = kp[:, band_idx].reshape(B, nc, band * CHUNK, H, d)
    vb = vp[:, band_idx].reshape(B, nc, band * CHUNK, H, d)
    qi = jnp.arange(CHUNK)[:, None]
    kj = jnp.arange(band * CHUNK)[None, :]
    rel = C_LEFT_CHUNKS * CHUNK + qi - kj
    bias = rel_bias.astype(jnp.float32)[:, jnp.clip(rel, -REL_CLIP, REL_CLIP) + REL_CLIP]
    key_ok = (jnp.arange(nc)[:, None] - C_LEFT_CHUNKS + jnp.arange(band)[None, :]) >= 0
    key_ok = jnp.repeat(key_ok, CHUNK, axis=1)
    s = jnp.einsum('bnqhd,bnkhd->bnhqk', qc, kb).astype(jnp.float32) * (d ** -0.5)
    s = s + bias[None, None]
    s = jnp.where(key_ok[None, :, None, None, :], s, -jnp.inf)
    p = jax.nn.softmax(s, axis=-1)
    o = jnp.einsum('bnhqk,bnkhd->bnqhd', p.astype(vb.dtype), vb)
    return o.reshape(B, S, H * d)


def token_mixers(u, positions, w_in, lq1, lk1, lq2, lk2, a_norm_g, rel_bias,
                 wa, wb, wc, w_out, lam_init):
    B, S, _ = u.shape
    (aq, ak, av, bq, bk, bv, iq, ik, iw, cq, ck, cv, gates) = split_cols(
        jnp.einsum('bsd,dn->bsn', u, w_in), IN_SIZES)
    aq = rope(aq.reshape(B, S, A_HEADS * 2, A_HEAD_DIM), positions).reshape(B, S, A_HEADS, 2, A_HEAD_DIM)
    ak = rope(ak.reshape(B, S, A_HEADS * 2, A_HEAD_DIM), positions).reshape(B, S, A_HEADS, 2, A_HEAD_DIM)
    av = av.reshape(B, S, A_HEADS, 2 * A_HEAD_DIM)
    f32 = jnp.float32
    lam = (jnp.exp(jnp.sum(lq1.astype(f32) * lk1.astype(f32)))
           - jnp.exp(jnp.sum(lq2.astype(f32) * lk2.astype(f32))) + lam_init)
    ya = diff_attention(aq, ak, av, lam, lam_init, a_norm_g)
    bq = rope(bq.reshape(B, S, B_HEADS, B_HEAD_DIM), positions)
    bk = rope(bk[:, :, None, :], positions)[:, :, 0]
    iq = rope(iq.reshape(B, S, IDX_HEADS, IDX_DIM), positions)
    ik = rope(ik[:, :, None, :], positions)[:, :, 0]
    iw = iw * (IDX_HEADS ** -0.5 * IDX_DIM ** -0.5)
    yb = dsa_attention(bq, bk, bv, iq, ik, iw)
    yc = chunk_band_attention(cq.reshape(B, S, C_HEADS, C_HEAD_DIM),
                              ck.reshape(B, S, C_HEADS, C_HEAD_DIM),
                              cv.reshape(B, S, C_HEADS, C_HEAD_DIM), rel_bias)
    ga, gb, gc = jnp.split(jax.nn.sigmoid(gates), N_BRANCHES, axis=-1)
    merged = ga * (ya @ wa) + gb * (yb @ wb) + gc * (yc @ wc)
    return merged @ w_out


def grouped_moe(u, router_w, router_b, w1, w3, w2):
    B, S, D = u.shape
    t = u.reshape(B * S, D)
    aff = jax.nn.sigmoid(jnp.dot(t, router_w).astype(jnp.float32))
    sel = aff + router_b.astype(jnp.float32)
    grp_top = lax.top_k(sel.reshape(-1, N_GROUPS, EXPERTS_PER_GROUP), 2)[0]
    best_g = jnp.argmax(grp_top.sum(-1), axis=-1)
    in_group = (jnp.arange(N_EXPERTS) // EXPERTS_PER_GROUP)[None, :] == best_g[:, None]
    _, top_idx = lax.top_k(jnp.where(in_group, sel, -jnp.inf), TOP_K)
    top_w = jnp.take_along_axis(aff, top_idx, axis=-1)
    top_w = top_w / jnp.sum(top_w, axis=-1, keepdims=True)
    combine = jnp.einsum('nk,nke->ne', top_w, jax.nn.one_hot(top_idx, N_EXPERTS, dtype=jnp.float32))
    combine = combine.astype(t.dtype)
    y = jnp.zeros_like(t)
    for e in range(N_EXPERTS):
        h = jax.nn.silu(t @ w1[e]) * (t @ w3[e])
        y = y + combine[:, e:e + 1] * (h @ w2[e])
    return y.reshape(B, S, D)


def setup_inputs(seed: int = 0) -> dict:
    key = jax.random.key(seed)
    ks = jax.random.split(key, 32)
    f32 = jnp.float32

    def nrm(k, shape, fan_in, gain=1.0):
        return (gain * fan_in ** -0.5) * jax.random.normal(k, shape, f32)

    x = jax.random.normal(ks[0], (BATCH, SEQ, D_MODEL), f32)
    c = jax.random.normal(ks[1], (BATCH, D_MODEL), f32)
    positions = (jnp.arange(SEQ, dtype=jnp.int32)[None, :]
                 + jax.random.randint(ks[2], (BATCH, 1), 0, 1024, dtype=jnp.int32))
    return {
        "x": x,
        "c": c,
        "positions": positions,
        "norm1_g": 1.0 + 0.02 * jax.random.normal(ks[3], (DEPTH, D_MODEL), f32),
        "norm2_g": 1.0 + 0.02 * jax.random.normal(ks[4], (DEPTH, D_MODEL), f32),
        "w_mod": nrm(ks[5], (DEPTH, D_MODEL, 6 * D_MODEL), D_MODEL, 0.5),
        "b_mod": 0.02 * jax.random.normal(ks[6], (DEPTH, 6 * D_MODEL), f32),
        "w_in": nrm(ks[7], (DEPTH, D_MODEL, IN_COLS), D_MODEL),
        "lambda_q1": 0.1 * jax.random.normal(ks[8], (DEPTH, A_HEAD_DIM), f32),
        "lambda_k1": 0.1 * jax.random.normal(ks[9], (DEPTH, A_HEAD_DIM), f32),
        "lambda_q2": 0.1 * jax.random.normal(ks[10], (DEPTH, A_HEAD_DIM), f32),
        "lambda_k2": 0.1 * jax.random.normal(ks[11], (DEPTH, A_HEAD_DIM), f32),
        "a_norm_g": 1.0 + 0.02 * jax.random.normal(ks[12], (DEPTH, 2 * A_HEAD_DIM), f32),
        "c_rel_bias": 0.2 * jax.random.normal(ks[13], (DEPTH, C_HEADS, 2 * REL_CLIP + 1), f32),
        "w_branch_a": nrm(ks[14], (DEPTH, A_WIDTH, D_MODEL), A_WIDTH),
        "w_branch_b": nrm(ks[15], (DEPTH, B_WIDTH, D_MODEL), B_WIDTH),
        "w_branch_c": nrm(ks[16], (DEPTH, C_WIDTH, D_MODEL), C_WIDTH),
        "w_out": nrm(ks[17], (DEPTH, D_MODEL, D_MODEL), D_MODEL),
        "router_w": nrm(ks[18], (D_MODEL, N_EXPERTS), D_MODEL),
        "router_b": 0.01 * jax.random.normal(ks[19], (N_EXPERTS,), f32),
        "exp_w1": nrm(ks[20], (DEPTH, N_EXPERTS, D_MODEL, D_FF_EXPERT), D_MODEL),
        "exp_w3": nrm(ks[21], (DEPTH, N_EXPERTS, D_MODEL, D_FF_EXPERT), D_MODEL),
        "exp_w2": nrm(ks[22], (DEPTH, N_EXPERTS, D_FF_EXPERT, D_MODEL), D_FF_EXPERT),
        "final_g": 1.0 + 0.02 * jax.random.normal(ks[23], (D_MODEL,), f32),
    }


def reference(x, c, positions, norm1_g, norm2_g, w_mod, b_mod, w_in,
              lambda_q1, lambda_k1, lambda_q2, lambda_k2, a_norm_g, c_rel_bias,
              w_branch_a, w_branch_b, w_branch_c, w_out, router_w, router_b,
              exp_w1, exp_w3, exp_w2, final_g):
    c_act = jax.nn.silu(c)
    for layer in range(DEPTH):
        lam_init = 0.8 - 0.6 * math.exp(-0.3 * layer)
        mod = jnp.dot(c_act, w_mod[layer]) + b_mod[layer]
        sh1, sc1, g1, sh2, sc2, g2 = jnp.split(mod[:, None, :], 6, axis=-1)
        u = rms_norm(x, norm1_g[layer]) * (1.0 + sc1) + sh1
        x = x + g1 * token_mixers(u, positions, w_in[layer],
                                  lambda_q1[layer], lambda_k1[layer],
                                  lambda_q2[layer], lambda_k2[layer],
                                  a_norm_g[layer], c_rel_bias[layer],
                                  w_branch_a[layer], w_branch_b[layer], w_branch_c[layer],
                                  w_out[layer], lam_init)
        u = rms_norm(x, norm2_g[layer]) * (1.0 + sc2) + sh2
        x = x + g2 * grouped_moe(u, router_w, router_b,
                                 exp_w1[layer], exp_w3[layer], exp_w2[layer])
    return rms_norm(x, final_g)
```

```python
import functools
import math

import jax
import jax.numpy as jnp
from jax import lax
from jax.experimental import pallas as pl
from jax.experimental.pallas import tpu as pltpu

F32 = jnp.float32
BF16 = jnp.bfloat16

CHUNK = 64
ROPE_THETA = 10000.0
EPS = 1e-6
A_HEADS = 4
HEAD_DIM = 64
B_HEADS = 8
IDX_HEADS = 4
TOPK_MAX = 256
C_HEADS = 8
C_LEFT_CHUNKS = 8
REL_CLIP = 256
N_EXPERTS = 16
N_GROUPS = 4
EXPERTS_PER_GROUP = 4
N_BRANCHES = 3

LANES = 128
NEG = -1e30
VMEM_LIMIT = 56 * 1024 * 1024

COL_TILE = 512
T_AQ, T_AK, T_AV, T_BQ, T_CQ, T_CK, T_CV = 0, 1, 2, 3, 4, 5, 6
T_IDX = 7
T_GATE0 = 8
T_MISC = 14
N_COL_TILES = 15
ROPE_TILES = (T_AQ, T_AK, T_BQ, T_IDX)


def _cparams(sem):
    return pltpu.CompilerParams(dimension_semantics=sem, vmem_limit_bytes=VMEM_LIMIT)


def _mod_kernel(c_ref, w_ref, b_ref, o_ref):
    c = c_ref[...]
    ca = c * jax.nn.sigmoid(c)
    o_ref[0] = jnp.dot(ca, w_ref[0], preferred_element_type=F32) + b_ref[0]


def _modulation(c, w_mod, b_mod):
    depth, d, n6 = w_mod.shape
    bsz = c.shape[0]
    tn = 1024
    return pl.pallas_call(
        _mod_kernel,
        grid=(depth, n6 // tn),
        in_specs=[
            pl.BlockSpec((bsz, d), lambda l, j: (0, 0)),
            pl.BlockSpec((1, d, tn), lambda l, j: (l, 0, j)),
            pl.BlockSpec((1, 1, tn), lambda l, j: (l, 0, j)),
        ],
        out_specs=pl.BlockSpec((1, bsz, tn), lambda l, j: (l, 0, j)),
        out_shape=jax.ShapeDtypeStruct((depth, bsz, n6), F32),
        compiler_params=_cparams(("arbitrary", "arbitrary")),
        name="modulation",
    )(c, w_mod, b_mod.reshape(depth, 1, n6))


def _modulated_norm(x, g, sc, sh):
    y = x * lax.rsqrt(jnp.mean(x * x, axis=-1, keepdims=True) + EPS)
    return (y * g) * (1.0 + sc) + sh


def _inproj_kernel(x_ref, g_ref, sc_ref, sh_ref, cos_ref, sin_ref, w_ref, o_ref, u_scr):
    j = pl.program_id(2)

    @pl.when(j == 0)
    def _():
        u_scr[...] = _modulated_norm(x_ref[0], g_ref[...], sc_ref[0], sh_ref[0]).astype(BF16)

    acc = jnp.dot(u_scr[...], w_ref[...], preferred_element_type=F32)
    is_rope = functools.reduce(jnp.logical_or, [j == t for t in ROPE_TILES])
    is_gate = jnp.logical_and(j >= T_GATE0, j < T_GATE0 + 6)

    @pl.when(is_rope)
    def _():
        reps = COL_TILE // LANES
        cos = jnp.tile(cos_ref[0], (1, reps))
        sin = jnp.tile(sin_ref[0], (1, reps))
        lane = lax.broadcasted_iota(jnp.int32, acc.shape, 1)
        first_half = (lane & (HEAD_DIM - 1)) < (HEAD_DIM // 2)
        partner = jnp.where(first_half,
                            pltpu.roll(acc, COL_TILE - HEAD_DIM // 2, axis=1),
                            pltpu.roll(acc, HEAD_DIM // 2, axis=1))
        o_ref[0] = (acc * cos + partner * sin).astype(BF16)

    @pl.when(is_gate)
    def _():
        o_ref[0] = jax.nn.sigmoid(acc).astype(BF16)

    @pl.when(jnp.logical_not(jnp.logical_or(is_rope, is_gate)))
    def _():
        o_ref[0] = acc.astype(BF16)


def _in_proj(x, g, sc, sh, cos_t, sin_t, w_slab, tm):
    bsz, s, d = x.shape
    ncols = w_slab.shape[1]
    return pl.pallas_call(
        _inproj_kernel,
        grid=(bsz, s // tm, ncols // COL_TILE),
        in_specs=[
            pl.BlockSpec((1, tm, d), lambda b, i, j: (b, i, 0)),
            pl.BlockSpec((1, d), lambda b, i, j: (0, 0)),
            pl.BlockSpec((1, 1, d), lambda b, i, j: (b, 0, 0)),
            pl.BlockSpec((1, 1, d), lambda b, i, j: (b, 0, 0)),
            pl.BlockSpec((1, tm, LANES), lambda b, i, j: (b, i, 0)),
            pl.BlockSpec((1, tm, LANES), lambda b, i, j: (b, i, 0)),
            pl.BlockSpec((d, COL_TILE), lambda b, i, j: (0, j)),
        ],
        out_specs=pl.BlockSpec((1, tm, COL_TILE), lambda b, i, j: (b, i, j)),
        out_shape=jax.ShapeDtypeStruct((bsz, s, ncols), BF16),
        scratch_shapes=[pltpu.VMEM((tm, d), BF16)],
        compiler_params=_cparams(("arbitrary", "arbitrary", "arbitrary")),
        name="in_proj",
    )(x, g.reshape(1, d), sc, sh, cos_t, sin_t, w_slab)


def _build_w_slab(w_in):
    sizes = (512, 512, 512, 512, 64, 64, 256, 64, 4, 512, 512, 512, 3072)
    parts, start = [], 0
    for sz in sizes:
        parts.append(w_in[:, start:start + sz])
        start += sz
    aq, ak, av, bq, bk, bv, iq, ik, iw, cq, ck, cv, gates = parts
    d = w_in.shape[0]
    qscale = HEAD_DIM ** -0.5
    iw_scale = IDX_HEADS ** -0.5 * HEAD_DIM ** -0.5
    zeros = lambda n: jnp.zeros((d, n), w_in.dtype)
    slab = jnp.concatenate([
        aq * qscale, ak, av, bq * qscale, cq * qscale, ck, cv,
        iq, bk, bk, ik, ik,
        gates,
        bv, bv, iw * iw_scale, zeros(LANES - IDX_HEADS), zeros(256),
    ], axis=1)
    return slab.astype(BF16)


def _rope_tables(positions):
    half = HEAD_DIM // 2
    inv = ROPE_THETA ** (-jnp.arange(half, dtype=F32) / half)
    ang = positions.astype(F32)[..., None] * inv
    cos, sin = jnp.cos(ang), jnp.sin(ang)
    cos_t = jnp.tile(cos, (1, 1, LANES // half))
    sin_t = jnp.tile(jnp.concatenate([-sin, sin], axis=-1), (1, 1, LANES // HEAD_DIM))
    return cos_t, sin_t


def _half_masks(shape):
    lane = lax.broadcasted_iota(jnp.int32, shape, len(shape) - 1)
    return lane < HEAD_DIM, lane >= HEAD_DIM


def _qk(q, k):
    return lax.dot_general(q, k, (((1,), (1,)), ((), ())), preferred_element_type=F32)


def _flash_update(s, v_parts, m_scr, l_scr, acc_scr):
    m_old = m_scr[...]
    m_new = jnp.maximum(m_old, jnp.max(s, axis=-1, keepdims=True))
    alpha = jnp.exp(m_old - m_new)
    p = jnp.exp(s - m_new)
    l_scr[...] = alpha * l_scr[...] + jnp.sum(p, axis=-1, keepdims=True)
    m_scr[...] = m_new
    pb = p.astype(BF16)
    for (r0, r1), v in v_parts:
        acc_scr[r0:r1, :] = alpha[r0:r1] * acc_scr[r0:r1, :] + jnp.dot(
            pb[r0:r1], v, preferred_element_type=F32)


def _diff_attn_kernel(lam_ref, q_ref, k_ref, v_ref, ng_ref, o_ref, m_scr, l_scr, acc_scr,
                      *, tq, lam_init):
    i = pl.program_id(2)
    q = q_ref[0]
    lo, hi = _half_masks(q.shape)
    zero = jnp.zeros_like(q)
    qs = jnp.concatenate([jnp.where(lo, q, zero), jnp.where(hi, q, zero)], axis=0)
    m_scr[...] = jnp.full(m_scr.shape, NEG, F32)
    l_scr[...] = jnp.zeros(l_scr.shape, F32)
    acc_scr[...] = jnp.zeros(acc_scr.shape, F32)

    def kv_block(j):
        start = pl.multiple_of(j * tq, tq)
        return k_ref[0, pl.ds(start, tq), :], v_ref[0, pl.ds(start, tq), :]

    def body(j, carry):
        k, v = kv_block(j)
        _flash_update(_qk(qs, k), [((0, 2 * tq), v)], m_scr, l_scr, acc_scr)
        return carry

    lax.fori_loop(0, i, body, 0)
    k, v = kv_block(i)
    s = _qk(qs, k)
    row = lax.broadcasted_iota(jnp.int32, s.shape, 0) % tq
    col = lax.broadcasted_iota(jnp.int32, s.shape, 1)
    s = jnp.where((col // CHUNK) <= (row // CHUNK), s, NEG)
    _flash_update(s, [((0, 2 * tq), v)], m_scr, l_scr, acc_scr)

    o = acc_scr[...] / l_scr[...]
    o = o[:tq] - lam_ref[0] * o[tq:]
    o = o * lax.rsqrt(jnp.mean(o * o, axis=-1, keepdims=True) + EPS)
    o_ref[0] = ((o * ng_ref[...]) * (1.0 - lam_init)).astype(BF16)


def _diff_attention(slab, lam, norm_g, lam_init, tq):
    bsz, s, _ = slab.shape
    kern = functools.partial(_diff_attn_kernel, tq=tq, lam_init=lam_init)
    cb = COL_TILE // LANES
    return pl.pallas_call(
        kern,
        grid=(bsz, A_HEADS, s // tq),
        in_specs=[
            pl.BlockSpec(memory_space=pltpu.SMEM),
            pl.BlockSpec((1, tq, LANES), lambda b, h, i: (b, i, T_AQ * cb + h)),
            pl.BlockSpec((1, s, LANES), lambda b, h, i: (b, 0, T_AK * cb + h)),
            pl.BlockSpec((1, s, LANES), lambda b, h, i: (b, 0, T_AV * cb + h)),
            pl.BlockSpec((1, LANES), lambda b, h, i: (0, 0)),
        ],
        out_specs=pl.BlockSpec((1, tq, LANES), lambda b, h, i: (b, i, h)),
        out_shape=jax.ShapeDtypeStruct((bsz, s, A_HEADS * LANES), BF16),
        scratch_shapes=[pltpu.VMEM((2 * tq, 1), F32), pltpu.VMEM((2 * tq, 1), F32),
                        pltpu.VMEM((2 * tq, LANES), F32)],
        compiler_params=_cparams(("arbitrary", "arbitrary", "arbitrary")),
        name="diff_attn",
    )(lam.reshape(1), slab, slab, slab, norm_g.reshape(1, LANES))


KEY_NEG_INF = -2139095041
KEY_POS_INF = 2139095040


def _key_to_float(key):
    bits = jnp.where(key >= 0, key, key ^ jnp.int32(0x7FFFFFFF))
    return pltpu.bitcast(bits, F32)


def _dsa_kernel(q_ref, iq_ref, iw_ref, k_ref, ik_ref, v_ref, o_ref,
                sc_scr, qs_scr, m_scr, l_scr, acc_scr, tie_scr, *, tq, topk, tie_iters):
    i = pl.program_id(1)
    nblk = i + 1

    iq = iq_ref[0]
    parts = []
    for p in range(IDX_HEADS // 2):
        pair = iq[:, p * LANES:(p + 1) * LANES]
        lo, hi = _half_masks(pair.shape)
        zero = jnp.zeros_like(pair)
        parts += [jnp.where(lo, pair, zero), jnp.where(hi, pair, zero)]
    iqs = jnp.concatenate(parts, axis=0)
    iw = iw_ref[0].astype(F32)

    def index_block(j):
        start = pl.multiple_of(j * tq, tq)
        hs = jnp.maximum(_qk(iqs, ik_ref[0, pl.ds(start, tq), :]), 0.0)
        score = iw[:, 0:1] * hs[0:tq]
        for h in range(1, IDX_HEADS):
            score = score + iw[:, h:h + 1] * hs[h * tq:(h + 1) * tq]
        return score

    def p1_body(j, carry):
        sc_scr[j] = index_block(j)
        return carry

    lax.fori_loop(0, i, p1_body, 0)
    score = index_block(i)
    row = lax.broadcasted_iota(jnp.int32, score.shape, 0)
    col = lax.broadcasted_iota(jnp.int32, score.shape, 1)
    sc_scr[i] = jnp.where((col // CHUNK) <= (row // CHUNK), score, -jnp.inf)

    def count(pred):
        def body(j, cnt):
            hit = jnp.where(pred(sc_scr[j], j), 1.0, 0.0)
            for c0 in range(0, tq, LANES):
                cnt = cnt + hit[:, c0:c0 + LANES]
            return cnt
        cnt = lax.fori_loop(0, nblk, body, jnp.zeros((tq, LANES), F32))
        return jnp.sum(cnt, axis=-1, keepdims=True)

    kf = float(topk)

    def bisect(_, carry):
        lo_k, hi_k = carry
        mid = (lo_k & hi_k) + ((lo_k ^ hi_k) >> 1)
        midf = _key_to_float(mid)
        ok = count(lambda sblk, j: sblk >= midf) >= kf
        return jnp.where(ok, mid, lo_k), jnp.where(ok, hi_k, mid)

    lo_k, _ = lax.fori_loop(
        0, 32, bisect,
        (jnp.full((tq, 1), KEY_NEG_INF, jnp.int32), jnp.full((tq, 1), KEY_POS_INF, jnp.int32)))
    tau = _key_to_float(lo_k)
    n_gt = count(lambda sblk, j: sblk > tau)
    n_eq = count(lambda sblk, j: sblk == tau)
    need = kf - n_gt
    open_row = lo_k == KEY_NEG_INF
    excess = jnp.logical_and(n_eq > need, jnp.logical_not(open_row))
    tie_scr[...] = jnp.where(open_row, -1, jnp.int32(2 ** 30))

    @pl.when(jnp.max(excess.astype(jnp.int32)) > 0)
    def _():
        colb = lax.broadcasted_iota(jnp.int32, (tq, tq), 1)

        def tie_bisect(_, carry):
            lo_j, hi_j = carry
            mid = (lo_j + hi_j) >> 1
            c = count(lambda sblk, j: jnp.logical_and(sblk == tau, colb + j * tq <= mid))
            ok = c >= need
            return jnp.where(ok, lo_j, mid), jnp.where(ok, mid, hi_j)

        _, hi_j = lax.fori_loop(
            0, tie_iters, tie_bisect,
            (jnp.full((tq, 1), -1, jnp.int32), jnp.full((tq, 1), 1, jnp.int32) * (nblk * tq - 1)))
        tie_scr[...] = jnp.where(excess, hi_j, tie_scr[...])

    tie_j = tie_scr[...]

    q = q_ref[0]
    for p in range(B_HEADS // 2):
        pair = q[:, p * LANES:(p + 1) * LANES]
        lo, hi = _half_masks(pair.shape)
        zero = jnp.zeros_like(pair)
        qs_scr[p * tq:(p + 1) * tq, :] = jnp.where(lo, pair, zero)
        qs_scr[(B_HEADS // 2 + p) * tq:(B_HEADS // 2 + p + 1) * tq, :] = jnp.where(hi, pair, zero)
    m_scr[...] = jnp.full(m_scr.shape, NEG, F32)
    l_scr[...] = jnp.zeros(l_scr.shape, F32)
    acc_scr[...] = jnp.zeros(acc_scr.shape, F32)
    half_rows = (B_HEADS // 2) * tq
    colb = lax.broadcasted_iota(jnp.int32, (tq, tq), 1)

    def p3_body(j, carry):
        start = pl.multiple_of(j * tq, tq)
        sblk = sc_scr[j]
        sel = jnp.logical_or(sblk > tau,
                             jnp.logical_and(sblk == tau, colb + j * tq <= tie_j))
        bias = jnp.where(sel, 0.0, NEG)
        s = _qk(qs_scr[...], k_ref[0, pl.ds(start, tq), :])
        s = (s.reshape(B_HEADS, tq, tq) + bias[None]).reshape(B_HEADS * tq, tq)
        v = v_ref[0, pl.ds(start, tq), :]
        lo, hi = _half_masks(v.shape)
        zero = jnp.zeros_like(v)
        _flash_update(s, [((0, half_rows), jnp.where(lo, v, zero)),
                          ((half_rows, 2 * half_rows), jnp.where(hi, v, zero))],
                      m_scr, l_scr, acc_scr)
        return carry

    lax.fori_loop(0, nblk, p3_body, 0)
    o = acc_scr[...] / l_scr[...]
    for p in range(B_HEADS // 2):
        o_ref[0, :, p * LANES:(p + 1) * LANES] = (
            o[p * tq:(p + 1) * tq] + o[half_rows + p * tq:half_rows + (p + 1) * tq]).astype(BF16)


def _dsa_attention(slab, tq):
    bsz, s, _ = slab.shape
    topk = min(TOPK_MAX, s // 4)
    assert tq >= topk and s % tq == 0
    tie_iters = max(1, math.ceil(math.log2(s))) + 1
    kern = functools.partial(_dsa_kernel, tq=tq, topk=topk, tie_iters=tie_iters)
    idx0 = T_IDX * COL_TILE
    misc0 = T_MISC * COL_TILE
    nblk = s // tq
    width = B_HEADS * HEAD_DIM
    return pl.pallas_call(
        kern,
        grid=(bsz, s // tq),
        in_specs=[
            pl.BlockSpec((1, tq, COL_TILE), lambda b, i: (b, i, T_BQ)),
            pl.BlockSpec((1, tq, 256), lambda b, i: (b, i, idx0 // 256)),
            pl.BlockSpec((1, tq, LANES), lambda b, i: (b, i, (misc0 + LANES) // LANES)),
            pl.BlockSpec((1, s, LANES), lambda b, i: (b, 0, (idx0 + 256) // LANES)),
            pl.BlockSpec((1, s, LANES), lambda b, i: (b, 0, (idx0 + 384) // LANES)),
            pl.BlockSpec((1, s, LANES), lambda b, i: (b, 0, misc0 // LANES)),
        ],
        out_specs=pl.BlockSpec((1, tq, width), lambda b, i: (b, i, 0)),
        out_shape=jax.ShapeDtypeStruct((bsz, s, width), BF16),
        scratch_shapes=[
            pltpu.VMEM((nblk, tq, tq), F32),
            pltpu.VMEM((B_HEADS * tq, LANES), BF16),
            pltpu.VMEM((B_HEADS * tq, 1), F32),
            pltpu.VMEM((B_HEADS * tq, 1), F32),
            pltpu.VMEM((B_HEADS * tq, LANES), F32),
            pltpu.VMEM((tq, 1), jnp.int32),
        ],
        compiler_params=_cparams(("arbitrary", "arbitrary")),
        name="dsa_attn",
    )(slab, slab, slab, slab, slab, slab)


BAND_TQ = 2 * CHUNK
BAND_KEYS = (C_LEFT_CHUNKS + BAND_TQ // CHUNK) * CHUNK
BAND_PAD = C_LEFT_CHUNKS * CHUNK


def _band_kernel(q_ref, k_ref, v_ref, bias_ref, o_ref):
    i = pl.program_id(2)
    tq = BAND_TQ
    start = pl.multiple_of(i * tq, tq)
    q = q_ref[0]
    k = k_ref[0, pl.ds(start, BAND_KEYS), :]
    v = v_ref[0, pl.ds(start, BAND_KEYS), :]
    lo, hi = _half_masks(q.shape)
    zq = jnp.zeros_like(q)
    qs = jnp.concatenate([jnp.where(lo, q, zq), jnp.where(hi, q, zq)], axis=0)
    s = _qk(qs, k) + bias_ref[...].reshape(2 * tq, BAND_KEYS)
    col = lax.broadcasted_iota(jnp.int32, s.shape, 1)
    s = jnp.where(col + start >= BAND_PAD, s, NEG)
    m = jnp.max(s, axis=-1, keepdims=True)
    p = jnp.exp(s - m)
    l = jnp.sum(p, axis=-1, keepdims=True)
    pb = p.astype(BF16)
    vlo, vhi = _half_masks(v.shape)
    zv = jnp.zeros_like(v)
    o = (jnp.dot(pb[:tq], jnp.where(vlo, v, zv), preferred_element_type=F32) / l[:tq]
         + jnp.dot(pb[tq:], jnp.where(vhi, v, zv), preferred_element_type=F32) / l[tq:])
    o_ref[0] = o.astype(BF16)


def _band_bias(rel_bias):
    r = jnp.arange(BAND_TQ)[:, None]
    cidx = jnp.arange(BAND_KEYS)[None, :]
    rel = BAND_PAD + r - cidx
    bias = rel_bias.astype(F32)[:, jnp.clip(rel, -REL_CLIP, REL_CLIP) + REL_CLIP]
    dchunk = (r // CHUNK + C_LEFT_CHUNKS) - cidx // CHUNK
    in_band = jnp.logical_and(dchunk >= 0, dchunk <= C_LEFT_CHUNKS)
    return jnp.where(in_band[None], bias, NEG)


def _band_attention(slab, kp, vp, bias):
    bsz, s, _ = slab.shape
    tq = BAND_TQ
    cb = COL_TILE // LANES
    sp = kp.shape[1]
    width = C_HEADS * HEAD_DIM
    return pl.pallas_call(
        _band_kernel,
        grid=(bsz, C_HEADS // 2, s // tq),
        in_specs=[
            pl.BlockSpec((1, tq, LANES), lambda b, p, i: (b, i, T_CQ * cb + p)),
            pl.BlockSpec((1, sp, LANES), lambda b, p, i: (b, 0, p)),
            pl.BlockSpec((1, sp, LANES), lambda b, p, i: (b, 0, p)),
            pl.BlockSpec((2, tq, BAND_KEYS), lambda b, p, i: (p, 0, 0)),
        ],
        out_specs=pl.BlockSpec((1, tq, LANES), lambda b, p, i: (b, i, p)),
        out_shape=jax.ShapeDtypeStruct((bsz, s, width), BF16),
        compiler_params=_cparams(("arbitrary", "arbitrary", "arbitrary")),
        name="band_attn",
    )(slab, kp, vp, bias)


def _merge_kernel(x_ref, g1_ref, ya_ref, yb_ref, yc_ref, ga_ref, gb_ref, gc_ref,
                  wa_ref, wb_ref, wc_ref, wo_ref, o_ref):
    def branch(y_ref, w_ref, gate_ref):
        return gate_ref[0].astype(F32) * jnp.dot(y_ref[0], w_ref[...], preferred_element_type=F32)

    merged = (branch(ya_ref, wa_ref, ga_ref) + branch(yb_ref, wb_ref, gb_ref)
              + branch(yc_ref, wc_ref, gc_ref))
    mixed = jnp.dot(merged.astype(BF16), wo_ref[...], preferred_element_type=F32)
    o_ref[0] = x_ref[0] + g1_ref[0] * mixed


def _merge(x, g1, ya, yb, yc, slab, wa, wb, wc, wo, tm):
    bsz, s, d = x.shape
    gate_blk = T_GATE0 * COL_TILE // d
    tok = lambda b, i: (b, i, 0)
    full = lambda b, i: (0, 0)
    return pl.pallas_call(
        _merge_kernel,
        grid=(bsz, s // tm),
        in_specs=[
            pl.BlockSpec((1, tm, d), tok),
            pl.BlockSpec((1, 1, d), lambda b, i: (b, 0, 0)),
            pl.BlockSpec((1, tm, ya.shape[2]), tok),
            pl.BlockSpec((1, tm, yb.shape[2]), tok),
            pl.BlockSpec((1, tm, yc.shape[2]), tok),
            pl.BlockSpec((1, tm, d), lambda b, i: (b, i, gate_blk)),
            pl.BlockSpec((1, tm, d), lambda b, i: (b, i, gate_blk + 1)),
            pl.BlockSpec((1, tm, d), lambda b, i: (b, i, gate_blk + 2)),
            pl.BlockSpec(wa.shape, full),
            pl.BlockSpec(wb.shape, full),
            pl.BlockSpec(wc.shape, full),
            pl.BlockSpec(wo.shape, full),
        ],
        out_specs=pl.BlockSpec((1, tm, d), tok),
        out_shape=jax.ShapeDtypeStruct((bsz, s, d), F32),
        compiler_params=_cparams(("arbitrary", "arbitrary")),
        name="merge",
    )(x, g1, ya, yb, yc, slab, slab, slab, wa, wb, wc, wo)


def _router_kernel(x_ref, g_ref, sc_ref, sh_ref, rw_ref, rb_ref, u_ref, comb_ref):
    u = _modulated_norm(x_ref[0], g_ref[...], sc_ref[0], sh_ref[0])
    u_ref[0] = u.astype(BF16)
    logits = lax.dot_general(rw_ref[...], u, (((1,), (1,)), ((), ())),
                             preferred_element_type=F32, precision=lax.Precision.HIGHEST)
    aff = jax.nn.sigmoid(logits)
    sel = aff + rb_ref[...]
    rows = [sel[e:e + 1] for e in range(N_EXPERTS)]
    gscore = []
    for g in range(N_GROUPS):
        r = rows[g * EXPERTS_PER_GROUP:(g + 1) * EXPERTS_PER_GROUP]
        best = None
        for a in range(EXPERTS_PER_GROUP):
            for b in range(a + 1, EXPERTS_PER_GROUP):
                pair = r[a] + r[b]
                best = pair if best is None else jnp.maximum(best, pair)
        gscore.append(best)
    gmax = functools.reduce(jnp.maximum, gscore)
    taken = jnp.zeros_like(gmax) > 1.0
    in_best = []
    for g in range(N_GROUPS):
        is_g = jnp.logical_and(gscore[g] == gmax, jnp.logical_not(taken))
        in_best.append(is_g)
        taken = jnp.logical_or(taken, is_g)
    keep = []
    for e in range(N_EXPERTS):
        g = e // EXPERTS_PER_GROUP
        rank = jnp.zeros_like(gmax)
        for o in range(g * EXPERTS_PER_GROUP, (g + 1) * EXPERTS_PER_GROUP):
            if o == e:
                continue
            ahead = rows[o] > rows[e] if o > e else rows[o] >= rows[e]
            rank = rank + jnp.where(ahead, 1.0, 0.0)
        keep.append(jnp.logical_and(in_best[g], rank < 2.0))
    w = [jnp.where(keep[e], aff[e:e + 1], 0.0) for e in range(N_EXPERTS)]
    total = functools.reduce(lambda a, b: a + b, w)
    comb = jnp.concatenate([we / total for we in w]
                           + [jnp.zeros((LANES - N_EXPERTS, total.shape[1]), F32)], axis=0)
    comb_ref[0] = comb.T


def _router(x, g, sc, sh, router_w, router_b, tm):
    bsz, s, d = x.shape
    tok = lambda b, i: (b, i, 0)
    return pl.pallas_call(
        _router_kernel,
        grid=(bsz, s // tm),
        in_specs=[
            pl.BlockSpec((1, tm, d), tok),
            pl.BlockSpec((1, d), lambda b, i: (0, 0)),
            pl.BlockSpec((1, 1, d), lambda b, i: (b, 0, 0)),
            pl.BlockSpec((1, 1, d), lambda b, i: (b, 0, 0)),
            pl.BlockSpec((N_EXPERTS, d), lambda b, i: (0, 0)),
            pl.BlockSpec((N_EXPERTS, 1), lambda b, i: (0, 0)),
        ],
        out_specs=[pl.BlockSpec((1, tm, d), tok), pl.BlockSpec((1, tm, LANES), tok)],
        out_shape=[jax.ShapeDtypeStruct((bsz, s, d), BF16),
                   jax.ShapeDtypeStruct((bsz, s, LANES), F32)],
        compiler_params=_cparams(("arbitrary", "arbitrary")),
        name="router",
    )(x, g.reshape(1, d), sc, sh, router_w.T, router_b.reshape(N_EXPERTS, 1))


def _moe_kernel(x_ref, g2_ref, u_ref, comb_ref, w1_ref, w3_ref, w2_ref, o_ref, acc_scr):
    e = pl.program_id(2)

    @pl.when(e == 0)
    def _():
        acc_scr[...] = jnp.zeros(acc_scr.shape, F32)

    u = u_ref[0]
    h1 = jnp.dot(u, w1_ref[0], preferred_element_type=F32)
    h3 = jnp.dot(u, w3_ref[0], preferred_element_type=F32)
    h = (h1 * jax.nn.sigmoid(h1)) * h3
    y = jnp.dot(h.astype(BF16), w2_ref[0], preferred_element_type=F32)
    comb = comb_ref[0]
    lane = lax.broadcasted_iota(jnp.int32, comb.shape, 1)
    ce = jnp.sum(jnp.where(lane == e, comb, 0.0), axis=-1, keepdims=True)
    acc_scr[...] += ce * y

    @pl.when(e == pl.num_programs(2) - 1)
    def _():
        o_ref[0] = x_ref[0] + g2_ref[0] * acc_scr[...]


def _moe(x, g2, u, comb, w1, w3, w2, tm):
    bsz, s, d = x.shape
    ne, _, dff = w1.shape
    tok = lambda b, i, e: (b, i, 0)
    return pl.pallas_call(
        _moe_kernel,
        grid=(bsz, s // tm, ne),
        in_specs=[
            pl.BlockSpec((1, tm, d), tok),
            pl.BlockSpec((1, 1, d), lambda b, i, e: (b, 0, 0)),
            pl.BlockSpec((1, tm, d), tok),
            pl.BlockSpec((1, tm, LANES), tok),
            pl.BlockSpec((1, d, dff), lambda b, i, e: (e, 0, 0)),
            pl.BlockSpec((1, d, dff), lambda b, i, e: (e, 0, 0)),
            pl.BlockSpec((1, dff, d), lambda b, i, e: (e, 0, 0)),
        ],
        out_specs=pl.BlockSpec((1, tm, d), tok),
        out_shape=jax.ShapeDtypeStruct((bsz, s, d), F32),
        scratch_shapes=[pltpu.VMEM((tm, d), F32)],
        compiler_params=_cparams(("arbitrary", "arbitrary", "arbitrary")),
        name="moe",
    )(x, g2, u, comb, w1, w3, w2)


def _final_norm_kernel(x_ref, g_ref, o_ref):
    x = x_ref[0]
    o_ref[0] = (x * lax.rsqrt(jnp.mean(x * x, axis=-1, keepdims=True) + EPS)) * g_ref[...]


def _final_norm(x, g, tm):
    bsz, s, d = x.shape
    tok = lambda b, i: (b, i, 0)
    return pl.pallas_call(
        _final_norm_kernel,
        grid=(bsz, s // tm),
        in_specs=[pl.BlockSpec((1, tm, d), tok), pl.BlockSpec((1, d), lambda b, i: (0, 0))],
        out_specs=pl.BlockSpec((1, tm, d), tok),
        out_shape=jax.ShapeDtypeStruct((bsz, s, d), F32),
        compiler_params=_cparams(("arbitrary", "arbitrary")),
        name="final_norm",
    )(x, g.reshape(1, d))


def _tile(s, want):
    t = min(want, s)
    assert s % t == 0
    return t


def kernel(x, c, positions, norm1_g, norm2_g, w_mod, b_mod, w_in, lambda_q1, lambda_k1, lambda_q2, lambda_k2, a_norm_g, c_rel_bias, w_branch_a, w_branch_b, w_branch_c, w_out, router_w, router_b, exp_w1, exp_w3, exp_w2, final_g):
    bsz, s, d = x.shape
    depth = w_mod.shape[0]
    tm = _tile(s, 1024)
    tq_attn = _tile(s, 256)

    mod = _modulation(c, w_mod, b_mod)
    cos_t, sin_t = _rope_tables(positions)

    for layer in range(depth):
        lam_init = 0.8 - 0.6 * math.exp(-0.3 * layer)
        sh1, sc1, g1, sh2, sc2, g2 = [m[:, None, :] for m in jnp.split(mod[layer], 6, axis=-1)]
        lam = (jnp.exp(jnp.sum(lambda_q1[layer] * lambda_k1[layer]))
               - jnp.exp(jnp.sum(lambda_q2[layer] * lambda_k2[layer])) + lam_init)

        slab = _in_proj(x, norm1_g[layer], sc1, sh1, cos_t, sin_t, _build_w_slab(w_in[layer]), tm)
        ya = _diff_attention(slab, lam, a_norm_g[layer], lam_init, tq_attn)
        yb = _dsa_attention(slab, tq_attn)
        pad = ((0, 0), (BAND_PAD, 0), (0, 0))
        kp = jnp.pad(slab[:, :, T_CK * COL_TILE:(T_CK + 1) * COL_TILE], pad)
        vp = jnp.pad(slab[:, :, T_CV * COL_TILE:(T_CV + 1) * COL_TILE], pad)
        yc = _band_attention(slab, kp, vp, _band_bias(c_rel_bias[layer]))
        x = _merge(x, g1, ya, yb, yc, slab,
                   w_branch_a[layer].astype(BF16), w_branch_b[layer].astype(BF16),
                   w_branch_c[layer].astype(BF16), w_out[layer].astype(BF16), _tile(s, 512))

        u, comb = _router(x, norm2_g[layer], sc2, sh2, router_w, router_b, _tile(s, 512))
        x = _moe(x, g2, u, comb, exp_w1[layer].astype(BF16), exp_w3[layer].astype(BF16),
                 exp_w2[layer].astype(BF16), tm)

    return _final_norm(x, final_g, tm)
```

```python
import functools
import math

import jax
import jax.numpy as jnp
from jax import lax
from jax.experimental import pallas as pl
from jax.experimental.pallas import tpu as pltpu

F32 = jnp.float32
BF16 = jnp.bfloat16

CHUNK = 64
ROPE_THETA = 10000.0
EPS = 1e-6
A_HEADS = 4
HEAD_DIM = 64
B_HEADS = 8
IDX_HEADS = 4
TOPK_MAX = 256
C_HEADS = 8
C_LEFT_CHUNKS = 8
REL_CLIP = 256
N_EXPERTS = 16
N_GROUPS = 4
EXPERTS_PER_GROUP = 4
N_BRANCHES = 3

LANES = 128
NEG = -1e30
VMEM_LIMIT = 56 * 1024 * 1024

COL_TILE = 512
T_AQ, T_AK, T_AV, T_BQ, T_CQ, T_CK, T_CV = 0, 1, 2, 3, 4, 5, 6
T_IDX = 7
T_GATE0 = 8
T_MISC = 14
N_COL_TILES = 15
ROPE_TILES = (T_AQ, T_AK, T_BQ, T_IDX)


def _cparams(sem):
    return pltpu.CompilerParams(dimension_semantics=sem, vmem_limit_bytes=VMEM_LIMIT)


def _mod_kernel(c_ref, w_ref, b_ref, o_ref):
    c = c_ref[...]
    ca = c * jax.nn.sigmoid(c)
    o_ref[0] = jnp.dot(ca, w_ref[0], preferred_element_type=F32) + b_ref[0]


def _modulation(c, w_mod, b_mod):
    depth, d, n6 = w_mod.shape
    bsz = c.shape[0]
    tn = 1024
    return pl.pallas_call(
        _mod_kernel,
        grid=(depth, n6 // tn),
        in_specs=[
            pl.BlockSpec((bsz, d), lambda l, j: (0, 0)),
            pl.BlockSpec((1, d, tn), lambda l, j: (l, 0, j)),
            pl.BlockSpec((1, 1, tn), lambda l, j: (l, 0, j)),
        ],
        out_specs=pl.BlockSpec((1, bsz, tn), lambda l, j: (l, 0, j)),
        out_shape=jax.ShapeDtypeStruct((depth, bsz, n6), F32),
        compiler_params=_cparams(("arbitrary", "arbitrary")),
        name="modulation",
    )(c, w_mod, b_mod.reshape(depth, 1, n6))


def _modulated_norm(x, g, sc, sh):
    y = x * lax.rsqrt(jnp.mean(x * x, axis=-1, keepdims=True) + EPS)
    return (y * g) * (1.0 + sc) + sh


def _inproj_kernel(x_ref, g_ref, sc_ref, sh_ref, cos_ref, sin_ref, w_ref, o_ref, u_scr):
    j = pl.program_id(2)

    @pl.when(j == 0)
    def _():
        u_scr[...] = _modulated_norm(x_ref[0], g_ref[...], sc_ref[0], sh_ref[0]).astype(BF16)

    acc = jnp.dot(u_scr[...], w_ref[...], preferred_element_type=F32)
    is_rope = functools.reduce(jnp.logical_or, [j == t for t in ROPE_TILES])
    is_gate = jnp.logical_and(j >= T_GATE0, j < T_GATE0 + 6)

    @pl.when(is_rope)
    def _():
        reps = COL_TILE // LANES
        cos = jnp.tile(cos_ref[0], (1, reps))
        sin = jnp.tile(sin_ref[0], (1, reps))
        lane = lax.broadcasted_iota(jnp.int32, acc.shape, 1)
        first_half = (lane & (HEAD_DIM - 1)) < (HEAD_DIM // 2)
        partner = jnp.where(first_half,
                            pltpu.roll(acc, COL_TILE - HEAD_DIM // 2, axis=1),
                            pltpu.roll(acc, HEAD_DIM // 2, axis=1))
        o_ref[0] = (acc * cos + partner * sin).astype(BF16)

    @pl.when(is_gate)
    def _():
        o_ref[0] = jax.nn.sigmoid(acc).astype(BF16)

    @pl.when(jnp.logical_not(jnp.logical_or(is_rope, is_gate)))
    def _():
        o_ref[0] = acc.astype(BF16)


def _in_proj(x, g, sc, sh, cos_t, sin_t, w_slab, tm):
    bsz, s, d = x.shape
    ncols = w_slab.shape[1]
    return pl.pallas_call(
        _inproj_kernel,
        grid=(bsz, s // tm, ncols // COL_TILE),
        in_specs=[
            pl.BlockSpec((1, tm, d), lambda b, i, j: (b, i, 0)),
            pl.BlockSpec((1, d), lambda b, i, j: (0, 0)),
            pl.BlockSpec((1, 1, d), lambda b, i, j: (b, 0, 0)),
            pl.BlockSpec((1, 1, d), lambda b, i, j: (b, 0, 0)),
            pl.BlockSpec((1, tm, LANES), lambda b, i, j: (b, i, 0)),
            pl.BlockSpec((1, tm, LANES), lambda b, i, j: (b, i, 0)),
            pl.BlockSpec((d, COL_TILE), lambda b, i, j: (0, j)),
        ],
        out_specs=pl.BlockSpec((1, tm, COL_TILE), lambda b, i, j: (b, i, j)),
        out_shape=jax.ShapeDtypeStruct((bsz, s, ncols), BF16),
        scratch_shapes=[pltpu.VMEM((tm, d), BF16)],
        compiler_params=_cparams(("arbitrary", "arbitrary", "arbitrary")),
        name="in_proj",
    )(x, g.reshape(1, d), sc, sh, cos_t, sin_t, w_slab)


def _build_w_slab(w_in):
    sizes = (512, 512, 512, 512, 64, 64, 256, 64, 4, 512, 512, 512, 3072)
    parts, start = [], 0
    for sz in sizes:
        parts.append(w_in[:, start:start + sz])
        start += sz
    aq, ak, av, bq, bk, bv, iq, ik, iw, cq, ck, cv, gates = parts
    d = w_in.shape[0]
    qscale = HEAD_DIM ** -0.5
    iw_scale = IDX_HEADS ** -0.5 * HEAD_DIM ** -0.5
    zeros = lambda n: jnp.zeros((d, n), w_in.dtype)
    slab = jnp.concatenate([
        aq * qscale, ak, av, bq * qscale, cq * qscale, ck, cv,
        iq, bk, bk, ik, ik,
        gates,
        bv, bv, iw * iw_scale, zeros(LANES - IDX_HEADS), zeros(256),
    ], axis=1)
    return slab.astype(BF16)


def _rope_tables(positions):
    half = HEAD_DIM // 2
    inv = ROPE_THETA ** (-jnp.arange(half, dtype=F32) / half)
    ang = positions.astype(F32)[..., None] * inv
    cos, sin = jnp.cos(ang), jnp.sin(ang)
    cos_t = jnp.tile(cos, (1, 1, LANES // half))
    sin_t = jnp.tile(jnp.concatenate([-sin, sin], axis=-1), (1, 1, LANES // HEAD_DIM))
    return cos_t, sin_t


def _half_masks(shape):
    lane = lax.broadcasted_iota(jnp.int32, shape, len(shape) - 1)
    return lane < HEAD_DIM, lane >= HEAD_DIM


def _qk(q, k):
    return lax.dot_general(q, k, (((1,), (1,)), ((), ())), preferred_element_type=F32)


def _flash_update(s, v_parts, m_scr, l_scr, acc_scr):
    groups = [s[:, c:c + LANES] for c in range(0, s.shape[1], LANES)]
    m_old = m_scr[...]
    lane_max = functools.reduce(jnp.maximum, groups)
    m_new = jnp.maximum(m_old, jnp.max(lane_max, axis=-1, keepdims=True))
    alpha = jnp.exp(m_old - m_new)
    p_groups = [jnp.exp(g - m_new) for g in groups]
    l_scr[...] = alpha * l_scr[...] + functools.reduce(lambda a, b: a + b, p_groups)
    m_scr[...] = m_new
    pb = jnp.concatenate([g.astype(BF16) for g in p_groups], axis=1)
    for (r0, r1), v in v_parts:
        acc_scr[r0:r1, :] = alpha[r0:r1] * acc_scr[r0:r1, :] + jnp.dot(
            pb[r0:r1], v, preferred_element_type=F32)


def _diff_attn_kernel(lam_ref, q_ref, k_ref, v_ref, ng_ref, o_ref, m_scr, l_scr, acc_scr,
                      *, tq, lam_init):
    i = pl.program_id(2)
    q = q_ref[0]
    lo, hi = _half_masks(q.shape)
    zero = jnp.zeros_like(q)
    qs = jnp.concatenate([jnp.where(lo, q, zero), jnp.where(hi, q, zero)], axis=0)
    m_scr[...] = jnp.full(m_scr.shape, NEG, F32)
    l_scr[...] = jnp.zeros(l_scr.shape, F32)
    acc_scr[...] = jnp.zeros(acc_scr.shape, F32)

    def kv_block(j):
        start = pl.multiple_of(j * tq, tq)
        return k_ref[0, pl.ds(start, tq), :], v_ref[0, pl.ds(start, tq), :]

    def body(j, carry):
        k, v = kv_block(j)
        _flash_update(_qk(qs, k), [((0, 2 * tq), v)], m_scr, l_scr, acc_scr)
        return carry

    lax.fori_loop(0, i, body, 0)
    k, v = kv_block(i)
    s = _qk(qs, k)
    row = lax.broadcasted_iota(jnp.int32, s.shape, 0) % tq
    col = lax.broadcasted_iota(jnp.int32, s.shape, 1)
    s = jnp.where((col // CHUNK) <= (row // CHUNK), s, NEG)
    _flash_update(s, [((0, 2 * tq), v)], m_scr, l_scr, acc_scr)

    o = acc_scr[...] / jnp.sum(l_scr[...], axis=-1, keepdims=True)
    o = o[:tq] - lam_ref[0] * o[tq:]
    o = o * lax.rsqrt(jnp.mean(o * o, axis=-1, keepdims=True) + EPS)
    o_ref[0] = ((o * ng_ref[...]) * (1.0 - lam_init)).astype(BF16)


def _diff_attention(slab, lam, norm_g, lam_init, tq):
    bsz, s, _ = slab.shape
    kern = functools.partial(_diff_attn_kernel, tq=tq, lam_init=lam_init)
    cb = COL_TILE // LANES
    return pl.pallas_call(
        kern,
        grid=(bsz, A_HEADS, s // tq),
        in_specs=[
            pl.BlockSpec(memory_space=pltpu.SMEM),
            pl.BlockSpec((1, tq, LANES), lambda b, h, i: (b, i, T_AQ * cb + h)),
            pl.BlockSpec((1, s, LANES), lambda b, h, i: (b, 0, T_AK * cb + h)),
            pl.BlockSpec((1, s, LANES), lambda b, h, i: (b, 0, T_AV * cb + h)),
            pl.BlockSpec((1, LANES), lambda b, h, i: (0, 0)),
        ],
        out_specs=pl.BlockSpec((1, tq, LANES), lambda b, h, i: (b, i, h)),
        out_shape=jax.ShapeDtypeStruct((bsz, s, A_HEADS * LANES), BF16),
        scratch_shapes=[pltpu.VMEM((2 * tq, LANES), F32), pltpu.VMEM((2 * tq, LANES), F32),
                        pltpu.VMEM((2 * tq, LANES), F32)],
        compiler_params=_cparams(("arbitrary", "arbitrary", "arbitrary")),
        name="diff_attn",
    )(lam.reshape(1), slab, slab, slab, norm_g.reshape(1, LANES))


KEY_NEG_INF = -2139095041
KEY_POS_INF = 2139095040
P3_SUB = 2


def _key_to_float(key):
    bits = jnp.where(key >= 0, key, key ^ jnp.int32(0x7FFFFFFF))
    return pltpu.bitcast(bits, F32)


def _dsa_kernel(q_ref, iq_ref, iw_ref, k_ref, ik_ref, v_ref, o_ref,
                sc_scr, qs_scr, m_scr, l_scr, acc_scr, tie_scr, *, tq, topk, tie_iters):
    i = pl.program_id(1)
    nblk = i + 1

    iq = iq_ref[0]
    parts = []
    for p in range(IDX_HEADS // 2):
        pair = iq[:, p * LANES:(p + 1) * LANES]
        lo, hi = _half_masks(pair.shape)
        zero = jnp.zeros_like(pair)
        parts += [jnp.where(lo, pair, zero), jnp.where(hi, pair, zero)]
    iqs = jnp.concatenate(parts, axis=0)
    iw = iw_ref[0].astype(F32)

    def index_block(j):
        start = pl.multiple_of(j * tq, tq)
        hs = jnp.maximum(_qk(iqs, ik_ref[0, pl.ds(start, tq), :]), 0.0)
        score = iw[:, 0:1] * hs[0:tq]
        for h in range(1, IDX_HEADS):
            score = score + iw[:, h:h + 1] * hs[h * tq:(h + 1) * tq]
        return score

    def p1_body(j, carry):
        sc_scr[j] = index_block(j)
        return carry

    lax.fori_loop(0, i, p1_body, 0)
    score = index_block(i)
    row = lax.broadcasted_iota(jnp.int32, score.shape, 0)
    col = lax.broadcasted_iota(jnp.int32, score.shape, 1)
    sc_scr[i] = jnp.where((col // CHUNK) <= (row // CHUNK), score, -jnp.inf)

    def count(pred):
        def body(j, cnt):
            hit = jnp.where(pred(sc_scr[j], j), 1.0, 0.0)
            for c0 in range(0, tq, LANES):
                cnt = cnt + hit[:, c0:c0 + LANES]
            return cnt
        cnt = lax.fori_loop(0, nblk, body, jnp.zeros((tq, LANES), F32))
        return jnp.sum(cnt, axis=-1, keepdims=True)

    kf = float(topk)

    def bisect(_, carry):
        lo_k, hi_k = carry
        mid = (lo_k & hi_k) + ((lo_k ^ hi_k) >> 1)
        midf = _key_to_float(mid)
        ok = count(lambda sblk, j: sblk >= midf) >= kf
        return jnp.where(ok, mid, lo_k), jnp.where(ok, hi_k, mid)

    lo_k, _ = lax.fori_loop(
        0, 32, bisect,
        (jnp.full((tq, 1), KEY_NEG_INF, jnp.int32), jnp.full((tq, 1), KEY_POS_INF, jnp.int32)))
    tau = _key_to_float(lo_k)
    n_gt = count(lambda sblk, j: sblk > tau)
    n_eq = count(lambda sblk, j: sblk == tau)
    need = kf - n_gt
    open_row = lo_k == KEY_NEG_INF
    excess = jnp.logical_and(n_eq > need, jnp.logical_not(open_row))
    tie_scr[...] = jnp.where(open_row, -1, jnp.int32(2 ** 30))

    @pl.when(jnp.max(excess.astype(jnp.int32)) > 0)
    def _():
        colb = lax.broadcasted_iota(jnp.int32, (tq, tq), 1)

        def tie_bisect(_, carry):
            lo_j, hi_j = carry
            mid = (lo_j + hi_j) >> 1
            c = count(lambda sblk, j: jnp.logical_and(sblk == tau, colb + j * tq <= mid))
            ok = c >= need
            return jnp.where(ok, lo_j, mid), jnp.where(ok, mid, hi_j)

        _, hi_j = lax.fori_loop(
            0, tie_iters, tie_bisect,
            (jnp.full((tq, 1), -1, jnp.int32), jnp.full((tq, 1), 1, jnp.int32) * (nblk * tq - 1)))
        tie_scr[...] = jnp.where(excess, hi_j, tie_scr[...])

    tie_j = tie_scr[...]

    q = q_ref[0]
    for p in range(B_HEADS // 2):
        pair = q[:, p * LANES:(p + 1) * LANES]
        lo, hi = _half_masks(pair.shape)
        zero = jnp.zeros_like(pair)
        qs_scr[p * tq:(p + 1) * tq, :] = jnp.where(lo, pair, zero)
        qs_scr[(B_HEADS // 2 + p) * tq:(B_HEADS // 2 + p + 1) * tq, :] = jnp.where(hi, pair, zero)
    m_scr[...] = jnp.full(m_scr.shape, NEG, F32)
    l_scr[...] = jnp.zeros(l_scr.shape, F32)
    acc_scr[...] = jnp.zeros(acc_scr.shape, F32)
    half_rows = (B_HEADS // 2) * tq
    colb = lax.broadcasted_iota(jnp.int32, (tq, tq), 1)

    def select_bias(j):
        sblk = sc_scr[j]
        sel = jnp.logical_or(sblk > tau,
                             jnp.logical_and(sblk == tau, colb + j * tq <= tie_j))
        return jnp.where(sel, 0.0, NEG)

    def p3_step(j, nsub):
        width = nsub * tq
        start = pl.multiple_of(j * tq, tq)
        bias = jnp.concatenate([select_bias(j + t) for t in range(nsub)], axis=1)
        s = _qk(qs_scr[...], k_ref[0, pl.ds(start, width), :])
        s = (s.reshape(B_HEADS, tq, width) + bias[None]).reshape(B_HEADS * tq, width)
        v = v_ref[0, pl.ds(start, width), :]
        lo, hi = _half_masks(v.shape)
        zero = jnp.zeros_like(v)
        _flash_update(s, [((0, half_rows), jnp.where(lo, v, zero)),
                          ((half_rows, 2 * half_rows), jnp.where(hi, v, zero))],
                      m_scr, l_scr, acc_scr)

    def p3_body(jj, carry):
        p3_step(jj * P3_SUB, P3_SUB)
        return carry

    lax.fori_loop(0, nblk // P3_SUB, p3_body, 0)
    for r in range(1, P3_SUB):
        @pl.when(nblk % P3_SUB == r)
        def _():
            p3_step(nblk - r, r)

    o = acc_scr[...] / jnp.sum(l_scr[...], axis=-1, keepdims=True)
    for p in range(B_HEADS // 2):
        o_ref[0, :, p * LANES:(p + 1) * LANES] = (
            o[p * tq:(p + 1) * tq] + o[half_rows + p * tq:half_rows + (p + 1) * tq]).astype(BF16)


def _dsa_attention(slab, tq):
    bsz, s, _ = slab.shape
    topk = min(TOPK_MAX, s // 4)
    assert tq >= topk and s % tq == 0
    tie_iters = max(1, math.ceil(math.log2(s))) + 1
    kern = functools.partial(_dsa_kernel, tq=tq, topk=topk, tie_iters=tie_iters)
    idx0 = T_IDX * COL_TILE
    misc0 = T_MISC * COL_TILE
    nblk = s // tq
    width = B_HEADS * HEAD_DIM
    return pl.pallas_call(
        kern,
        grid=(bsz, s // tq),
        in_specs=[
            pl.BlockSpec((1, tq, COL_TILE), lambda b, i: (b, i, T_BQ)),
            pl.BlockSpec((1, tq, 256), lambda b, i: (b, i, idx0 // 256)),
            pl.BlockSpec((1, tq, LANES), lambda b, i: (b, i, (misc0 + LANES) // LANES)),
            pl.BlockSpec((1, s, LANES), lambda b, i: (b, 0, (idx0 + 256) // LANES)),
            pl.BlockSpec((1, s, LANES), lambda b, i: (b, 0, (idx0 + 384) // LANES)),
            pl.BlockSpec((1, s, LANES), lambda b, i: (b, 0, misc0 // LANES)),
        ],
        out_specs=pl.BlockSpec((1, tq, width), lambda b, i: (b, i, 0)),
        out_shape=jax.ShapeDtypeStruct((bsz, s, width), BF16),
        scratch_shapes=[
            pltpu.VMEM((nblk, tq, tq), F32),
            pltpu.VMEM((B_HEADS * tq, LANES), BF16),
            pltpu.VMEM((B_HEADS * tq, LANES), F32),
            pltpu.VMEM((B_HEADS * tq, LANES), F32),
            pltpu.VMEM((B_HEADS * tq, LANES), F32),
            pltpu.VMEM((tq, 1), jnp.int32),
        ],
        compiler_params=_cparams(("arbitrary", "arbitrary")),
        name="dsa_attn",
    )(slab, slab, slab, slab, slab, slab)


BAND_TQ = 2 * CHUNK
BAND_KEYS = (C_LEFT_CHUNKS + BAND_TQ // CHUNK) * CHUNK
BAND_PAD = C_LEFT_CHUNKS * CHUNK


def _band_kernel(q_ref, k_ref, v_ref, bias_ref, o_ref):
    i = pl.program_id(2)
    tq = BAND_TQ
    start = pl.multiple_of(i * tq, tq)
    q = q_ref[0]
    k = k_ref[0, pl.ds(start, BAND_KEYS), :]
    v = v_ref[0, pl.ds(start, BAND_KEYS), :]
    lo, hi = _half_masks(q.shape)
    zq = jnp.zeros_like(q)
    qs = jnp.concatenate([jnp.where(lo, q, zq), jnp.where(hi, q, zq)], axis=0)
    s = _qk(qs, k) + bias_ref[...].reshape(2 * tq, BAND_KEYS)
    col = lax.broadcasted_iota(jnp.int32, s.shape, 1)
    s = jnp.where(col + start >= BAND_PAD, s, NEG)
    m = jnp.max(s, axis=-1, keepdims=True)
    p = jnp.exp(s - m)
    l = jnp.sum(p, axis=-1, keepdims=True)
    pb = p.astype(BF16)
    vlo, vhi = _half_masks(v.shape)
    zv = jnp.zeros_like(v)
    o = (jnp.dot(pb[:tq], jnp.where(vlo, v, zv), preferred_element_type=F32) / l[:tq]
         + jnp.dot(pb[tq:], jnp.where(vhi, v, zv), preferred_element_type=F32) / l[tq:])
    o_ref[0] = o.astype(BF16)


def _band_bias(rel_bias):
    r = jnp.arange(BAND_TQ)[:, None]
    cidx = jnp.arange(BAND_KEYS)[None, :]
    n_diag = BAND_TQ + BAND_KEYS - 1
    rel = BAND_PAD - (BAND_KEYS - 1) + jnp.arange(n_diag)
    g = rel_bias.astype(F32)[:, jnp.clip(rel, -REL_CLIP, REL_CLIP) + REL_CLIP]
    skew = jnp.tile(g, (1, BAND_TQ + 2))[:, :BAND_TQ * (n_diag + 1)]
    bias = skew.reshape(-1, BAND_TQ, n_diag + 1)[:, :, :BAND_KEYS][:, :, ::-1]
    dchunk = (r // CHUNK + C_LEFT_CHUNKS) - cidx // CHUNK
    in_band = jnp.logical_and(dchunk >= 0, dchunk <= C_LEFT_CHUNKS)
    return jnp.where(in_band[None], bias, NEG)


def _band_attention(slab, kp, vp, bias):
    bsz, s, _ = slab.shape
    tq = BAND_TQ
    cb = COL_TILE // LANES
    sp = kp.shape[1]
    width = C_HEADS * HEAD_DIM
    return pl.pallas_call(
        _band_kernel,
        grid=(bsz, C_HEADS // 2, s // tq),
        in_specs=[
            pl.BlockSpec((1, tq, LANES), lambda b, p, i: (b, i, T_CQ * cb + p)),
            pl.BlockSpec((1, sp, LANES), lambda b, p, i: (b, 0, p)),
            pl.BlockSpec((1, sp, LANES), lambda b, p, i: (b, 0, p)),
            pl.BlockSpec((2, tq, BAND_KEYS), lambda b, p, i: (p, 0, 0)),
        ],
        out_specs=pl.BlockSpec((1, tq, LANES), lambda b, p, i: (b, i, p)),
        out_shape=jax.ShapeDtypeStruct((bsz, s, width), BF16),
        compiler_params=_cparams(("arbitrary", "arbitrary", "arbitrary")),
        name="band_attn",
    )(slab, kp, vp, bias)


def _merge_kernel(x_ref, g1_ref, ya_ref, yb_ref, yc_ref, ga_ref, gb_ref, gc_ref,
                  wa_ref, wb_ref, wc_ref, wo_ref, o_ref):
    def branch(y_ref, w_ref, gate_ref):
        return gate_ref[0].astype(F32) * jnp.dot(y_ref[0], w_ref[...], preferred_element_type=F32)

    merged = (branch(ya_ref, wa_ref, ga_ref) + branch(yb_ref, wb_ref, gb_ref)
              + branch(yc_ref, wc_ref, gc_ref))
    mixed = jnp.dot(merged.astype(BF16), wo_ref[...], preferred_element_type=F32)
    o_ref[0] = x_ref[0] + g1_ref[0] * mixed


def _merge(x, g1, ya, yb, yc, slab, wa, wb, wc, wo, tm):
    bsz, s, d = x.shape
    gate_blk = T_GATE0 * COL_TILE // d
    tok = lambda b, i: (b, i, 0)
    full = lambda b, i: (0, 0)
    return pl.pallas_call(
        _merge_kernel,
        grid=(bsz, s // tm),
        in_specs=[
            pl.BlockSpec((1, tm, d), tok),
            pl.BlockSpec((1, 1, d), lambda b, i: (b, 0, 0)),
            pl.BlockSpec((1, tm, ya.shape[2]), tok),
            pl.BlockSpec((1, tm, yb.shape[2]), tok),
            pl.BlockSpec((1, tm, yc.shape[2]), tok),
            pl.BlockSpec((1, tm, d), lambda b, i: (b, i, gate_blk)),
            pl.BlockSpec((1, tm, d), lambda b, i: (b, i, gate_blk + 1)),
            pl.BlockSpec((1, tm, d), lambda b, i: (b, i, gate_blk + 2)),
            pl.BlockSpec(wa.shape, full),
            pl.BlockSpec(wb.shape, full),
            pl.BlockSpec(wc.shape, full),
            pl.BlockSpec(wo.shape, full),
        ],
        out_specs=pl.BlockSpec((1, tm, d), tok),
        out_shape=jax.ShapeDtypeStruct((bsz, s, d), F32),
        compiler_params=_cparams(("arbitrary", "arbitrary")),
        name="merge",
    )(x, g1, ya, yb, yc, slab, slab, slab, wa, wb, wc, wo)


def _router_kernel(x_ref, g_ref, sc_ref, sh_ref, rw_ref, rb_ref, u_ref, comb_ref):
    u = _modulated_norm(x_ref[0], g_ref[...], sc_ref[0], sh_ref[0])
    u_ref[0] = u.astype(BF16)
    logits = lax.dot_general(rw_ref[...], u, (((1,), (1,)), ((), ())),
                             preferred_element_type=F32, precision=lax.Precision.HIGHEST)
    aff = jax.nn.sigmoid(logits)
    sel = aff + rb_ref[...]
    rows = [sel[e:e + 1] for e in range(N_EXPERTS)]
    gscore = []
    for g in range(N_GROUPS):
        r = rows[g * EXPERTS_PER_GROUP:(g + 1) * EXPERTS_PER_GROUP]
        best = None
        for a in range(EXPERTS_PER_GROUP):
            for b in range(a + 1, EXPERTS_PER_GROUP):
                pair = r[a] + r[b]
                best = pair if best is None else jnp.maximum(best, pair)
        gscore.append(best)
    gmax = functools.reduce(jnp.maximum, gscore)
    taken = jnp.zeros_like(gmax) > 1.0
    in_best = []
    for g in range(N_GROUPS):
        is_g = jnp.logical_and(gscore[g] == gmax, jnp.logical_not(taken))
        in_best.append(is_g)
        taken = jnp.logical_or(taken, is_g)
    keep = []
    for e in range(N_EXPERTS):
        g = e // EXPERTS_PER_GROUP
        rank = jnp.zeros_like(gmax)
        for o in range(g * EXPERTS_PER_GROUP, (g + 1) * EXPERTS_PER_GROUP):
            if o == e:
                continue
            ahead = rows[o] > rows[e] if o > e else rows[o] >= rows[e]
            rank = rank + jnp.where(ahead, 1.0, 0.0)
        keep.append(jnp.logical_and(in_best[g], rank < 2.0))
    w = [jnp.where(keep[e], aff[e:e + 1], 0.0) for e in range(N_EXPERTS)]
    total = functools.reduce(lambda a, b: a + b, w)
    comb = jnp.concatenate([we / total for we in w]
                           + [jnp.zeros((LANES - N_EXPERTS, total.shape[1]), F32)], axis=0)
    comb_ref[0] = comb.T


def _router(x, g, sc, sh, router_w, router_b, tm):
    bsz, s, d = x.shape
    tok = lambda b, i: (b, i, 0)
    return pl.pallas_call(
        _router_kernel,
        grid=(bsz, s // tm),
        in_specs=[
            pl.BlockSpec((1, tm, d), tok),
            pl.BlockSpec((1, d), lambda b, i: (0, 0)),
            pl.BlockSpec((1, 1, d), lambda b, i: (b, 0, 0)),
            pl.BlockSpec((1, 1, d), lambda b, i: (b, 0, 0)),
            pl.BlockSpec((N_EXPERTS, d), lambda b, i: (0, 0)),
            pl.BlockSpec((N_EXPERTS, 1), lambda b, i: (0, 0)),
        ],
        out_specs=[pl.BlockSpec((1, tm, d), tok), pl.BlockSpec((1, tm, LANES), tok)],
        out_shape=[jax.ShapeDtypeStruct((bsz, s, d), BF16),
                   jax.ShapeDtypeStruct((bsz, s, LANES), F32)],
        compiler_params=_cparams(("arbitrary", "arbitrary")),
        name="router",
    )(x, g.reshape(1, d), sc, sh, router_w.T, router_b.reshape(N_EXPERTS, 1))


def _moe_kernel(x_ref, g2_ref, u_ref, comb_ref, w1_ref, w3_ref, w2_ref, o_ref, acc_scr):
    e = pl.program_id(2)

    @pl.when(e == 0)
    def _():
        acc_scr[...] = jnp.zeros(acc_scr.shape, F32)

    u = u_ref[0]
    h1 = jnp.dot(u, w1_ref[0], preferred_element_type=F32)
    h3 = jnp.dot(u, w3_ref[0], preferred_element_type=F32)
    h = (h1 * jax.nn.sigmoid(h1)) * h3
    y = jnp.dot(h.astype(BF16), w2_ref[0], preferred_element_type=F32)
    comb = comb_ref[0]
    lane = lax.broadcasted_iota(jnp.int32, comb.shape, 1)
    ce = jnp.sum(jnp.where(lane == e, comb, 0.0), axis=-1, keepdims=True)
    acc_scr[...] += ce * y

    @pl.when(e == pl.num_programs(2) - 1)
    def _():
        o_ref[0] = x_ref[0] + g2_ref[0] * acc_scr[...]


def _moe(x, g2, u, comb, w1, w3, w2, tm):
    bsz, s, d = x.shape
    ne, _, dff = w1.shape
    tok = lambda b, i, e: (b, i, 0)
    return pl.pallas_call(
        _moe_kernel,
        grid=(bsz, s // tm, ne),
        in_specs=[
            pl.BlockSpec((1, tm, d), tok),
            pl.BlockSpec((1, 1, d), lambda b, i, e: (b, 0, 0)),
            pl.BlockSpec((1, tm, d), tok),
            pl.BlockSpec((1, tm, LANES), tok),
            pl.BlockSpec((1, d, dff), lambda b, i, e: (e, 0, 0)),
            pl.BlockSpec((1, d, dff), lambda b, i, e: (e, 0, 0)),
            pl.BlockSpec((1, dff, d), lambda b, i, e: (e, 0, 0)),
        ],
        out_specs=pl.BlockSpec((1, tm, d), tok),
        out_shape=jax.ShapeDtypeStruct((bsz, s, d), F32),
        scratch_shapes=[pltpu.VMEM((tm, d), F32)],
        compiler_params=_cparams(("arbitrary", "arbitrary", "arbitrary")),
        name="moe",
    )(x, g2, u, comb, w1, w3, w2)


def _final_norm_kernel(x_ref, g_ref, o_ref):
    x = x_ref[0]
    o_ref[0] = (x * lax.rsqrt(jnp.mean(x * x, axis=-1, keepdims=True) + EPS)) * g_ref[...]


def _final_norm(x, g, tm):
    bsz, s, d = x.shape
    tok = lambda b, i: (b, i, 0)
    return pl.pallas_call(
        _final_norm_kernel,
        grid=(bsz, s // tm),
        in_specs=[pl.BlockSpec((1, tm, d), tok), pl.BlockSpec((1, d), lambda b, i: (0, 0))],
        out_specs=pl.BlockSpec((1, tm, d), tok),
        out_shape=jax.ShapeDtypeStruct((bsz, s, d), F32),
        compiler_params=_cparams(("arbitrary", "arbitrary")),
        name="final_norm",
    )(x, g.reshape(1, d))


def _tile(s, want):
    t = min(want, s)
    assert s % t == 0
    return t


def kernel(x, c, positions, norm1_g, norm2_g, w_mod, b_mod, w_in, lambda_q1, lambda_k1, lambda_q2, lambda_k2, a_norm_g, c_rel_bias, w_branch_a, w_branch_b, w_branch_c, w_out, router_w, router_b, exp_w1, exp_w3, exp_w2, final_g):
    bsz, s, d = x.shape
    depth = w_mod.shape[0]
    tm = _tile(s, 1024)
    tq_attn = _tile(s, 256)

    mod = _modulation(c, w_mod, b_mod)
    cos_t, sin_t = _rope_tables(positions)

    for layer in range(depth):
        lam_init = 0.8 - 0.6 * math.exp(-0.3 * layer)
        sh1, sc1, g1, sh2, sc2, g2 = [m[:, None, :] for m in jnp.split(mod[layer], 6, axis=-1)]
        lam = (jnp.exp(jnp.sum(lambda_q1[layer] * lambda_k1[layer]))
               - jnp.exp(jnp.sum(lambda_q2[layer] * lambda_k2[layer])) + lam_init)

        slab = _in_proj(x, norm1_g[layer], sc1, sh1, cos_t, sin_t, _build_w_slab(w_in[layer]), tm)
        ya = _diff_attention(slab, lam, a_norm_g[layer], lam_init, _tile(s, 512))
        yb = _dsa_attention(slab, tq_attn)
        pad = ((0, 0), (BAND_PAD, 0), (0, 0))
        kp = jnp.pad(slab[:, :, T_CK * COL_TILE:(T_CK + 1) * COL_TILE], pad)
        vp = jnp.pad(slab[:, :, T_CV * COL_TILE:(T_CV + 1) * COL_TILE], pad)
        yc = _band_attention(slab, kp, vp, _band_bias(c_rel_bias[layer]))
        x = _merge(x, g1, ya, yb, yc, slab,
                   w_branch_a[layer].astype(BF16), w_branch_b[layer].astype(BF16),
                   w_branch_c[layer].astype(BF16), w_out[layer].astype(BF16), _tile(s, 512))

        u, comb = _router(x, norm2_g[layer], sc2, sh2, router_w, router_b, _tile(s, 512))
        x = _moe(x, g2, u, comb, exp_w1[layer].astype(BF16), exp_w3[layer].astype(BF16),
                 exp_w2[layer].astype(BF16), tm)

    return _final_norm(x, final_g, tm)
```

```python
import functools
import math

import jax
import jax.numpy as jnp
from jax import lax
from jax.experimental import pallas as pl
from jax.experimental.pallas import tpu as pltpu

F32 = jnp.float32
BF16 = jnp.bfloat16

CHUNK = 64
ROPE_THETA = 10000.0
EPS = 1e-6
A_HEADS = 4
HEAD_DIM = 64
B_HEADS = 8
IDX_HEADS = 4
TOPK_MAX = 256
C_HEADS = 8
C_LEFT_CHUNKS = 8
REL_CLIP = 256
N_EXPERTS = 16
N_GROUPS = 4
EXPERTS_PER_GROUP = 4
N_BRANCHES = 3

LANES = 128
NEG = -1e30
VMEM_LIMIT = 56 * 1024 * 1024

COL_TILE = 512
T_AQ, T_AK, T_AV, T_BQ, T_CQ, T_CK, T_CV = 0, 1, 2, 3, 4, 5, 6
T_IDX = 7
T_GATE0 = 8
T_MISC = 14
N_COL_TILES = 15
ROPE_TILES = (T_AQ, T_AK, T_BQ, T_IDX)


def _cparams(sem):
    return pltpu.CompilerParams(dimension_semantics=sem, vmem_limit_bytes=VMEM_LIMIT)


def _mod_kernel(c_ref, w_ref, b_ref, o_ref):
    c = c_ref[...]
    ca = c * jax.nn.sigmoid(c)
    o_ref[0] = jnp.dot(ca, w_ref[0], preferred_element_type=F32) + b_ref[0]


def _modulation(c, w_mod, b_mod):
    depth, d, n6 = w_mod.shape
    bsz = c.shape[0]
    tn = 1024
    return pl.pallas_call(
        _mod_kernel,
        grid=(depth, n6 // tn),
        in_specs=[
            pl.BlockSpec((bsz, d), lambda l, j: (0, 0)),
            pl.BlockSpec((1, d, tn), lambda l, j: (l, 0, j)),
            pl.BlockSpec((1, 1, tn), lambda l, j: (l, 0, j)),
        ],
        out_specs=pl.BlockSpec((1, bsz, tn), lambda l, j: (l, 0, j)),
        out_shape=jax.ShapeDtypeStruct((depth, bsz, n6), F32),
        compiler_params=_cparams(("arbitrary", "arbitrary")),
        name="modulation",
    )(c, w_mod, b_mod.reshape(depth, 1, n6))


def _modulated_norm(x, g, sc, sh):
    y = x * lax.rsqrt(jnp.mean(x * x, axis=-1, keepdims=True) + EPS)
    return (y * g) * (1.0 + sc) + sh


def _inproj_kernel(x_ref, g_ref, sc_ref, sh_ref, cos_ref, sin_ref, w_ref, o_ref, u_scr):
    j = pl.program_id(2)

    @pl.when(j == 0)
    def _():
        u_scr[...] = _modulated_norm(x_ref[0], g_ref[...], sc_ref[0], sh_ref[0]).astype(BF16)

    acc = jnp.dot(u_scr[...], w_ref[...], preferred_element_type=F32)
    is_rope = functools.reduce(jnp.logical_or, [j == t for t in ROPE_TILES])
    is_gate = jnp.logical_and(j >= T_GATE0, j < T_GATE0 + 6)

    @pl.when(is_rope)
    def _():
        reps = COL_TILE // LANES
        cos = jnp.tile(cos_ref[0], (1, reps))
        sin = jnp.tile(sin_ref[0], (1, reps))
        lane = lax.broadcasted_iota(jnp.int32, acc.shape, 1)
        first_half = (lane & (HEAD_DIM - 1)) < (HEAD_DIM // 2)
        partner = jnp.where(first_half,
                            pltpu.roll(acc, COL_TILE - HEAD_DIM // 2, axis=1),
                            pltpu.roll(acc, HEAD_DIM // 2, axis=1))
        o_ref[0] = (acc * cos + partner * sin).astype(BF16)

    @pl.when(is_gate)
    def _():
        o_ref[0] = jax.nn.sigmoid(acc).astype(BF16)

    @pl.when(jnp.logical_not(jnp.logical_or(is_rope, is_gate)))
    def _():
        o_ref[0] = acc.astype(BF16)


def _in_proj(x, g, sc, sh, cos_t, sin_t, w_slab, tm):
    bsz, s, d = x.shape
    ncols = w_slab.shape[1]
    return pl.pallas_call(
        _inproj_kernel,
        grid=(bsz, s // tm, ncols // COL_TILE),
        in_specs=[
            pl.BlockSpec((1, tm, d), lambda b, i, j: (b, i, 0)),
            pl.BlockSpec((1, d), lambda b, i, j: (0, 0)),
            pl.BlockSpec((1, 1, d), lambda b, i, j: (b, 0, 0)),
            pl.BlockSpec((1, 1, d), lambda b, i, j: (b, 0, 0)),
            pl.BlockSpec((1, tm, LANES), lambda b, i, j: (b, i, 0)),
            pl.BlockSpec((1, tm, LANES), lambda b, i, j: (b, i, 0)),
            pl.BlockSpec((d, COL_TILE), lambda b, i, j: (0, j)),
        ],
        out_specs=pl.BlockSpec((1, tm, COL_TILE), lambda b, i, j: (b, i, j)),
        out_shape=jax.ShapeDtypeStruct((bsz, s, ncols), BF16),
        scratch_shapes=[pltpu.VMEM((tm, d), BF16)],
        compiler_params=_cparams(("arbitrary", "arbitrary", "arbitrary")),
        name="in_proj",
    )(x, g.reshape(1, d), sc, sh, cos_t, sin_t, w_slab)


def _build_w_slab(w_in):
    sizes = (512, 512, 512, 512, 64, 64, 256, 64, 4, 512, 512, 512, 3072)
    parts, start = [], 0
    for sz in sizes:
        parts.append(w_in[:, start:start + sz])
        start += sz
    aq, ak, av, bq, bk, bv, iq, ik, iw, cq, ck, cv, gates = parts
    d = w_in.shape[0]
    qscale = HEAD_DIM ** -0.5
    iw_scale = IDX_HEADS ** -0.5 * HEAD_DIM ** -0.5
    zeros = lambda n: jnp.zeros((d, n), w_in.dtype)
    slab = jnp.concatenate([
        aq * qscale, ak, av, bq * qscale, cq * qscale, ck, cv,
        iq, bk, bk, ik, ik,
        gates,
        bv, bv, iw * iw_scale, zeros(LANES - IDX_HEADS), zeros(256),
    ], axis=1)
    return slab.astype(BF16)


def _rope_tables(positions):
    half = HEAD_DIM // 2
    inv = ROPE_THETA ** (-jnp.arange(half, dtype=F32) / half)
    ang = positions.astype(F32)[..., None] * inv
    cos, sin = jnp.cos(ang), jnp.sin(ang)
    cos_t = jnp.tile(cos, (1, 1, LANES // half))
    sin_t = jnp.tile(jnp.concatenate([-sin, sin], axis=-1), (1, 1, LANES // HEAD_DIM))
    return cos_t, sin_t


def _half_masks(shape):
    lane = lax.broadcasted_iota(jnp.int32, shape, len(shape) - 1)
    return lane < HEAD_DIM, lane >= HEAD_DIM


def _qk(q, k):
    return lax.dot_general(q, k, (((1,), (1,)), ((), ())), preferred_element_type=F32)


def _flash_update(s, v_parts, m_scr, l_scr, acc_scr):
    groups = [s[:, c:c + LANES] for c in range(0, s.shape[1], LANES)]
    m_old = m_scr[...]
    lane_max = functools.reduce(jnp.maximum, groups)
    m_new = jnp.maximum(m_old, jnp.max(lane_max, axis=-1, keepdims=True))
    alpha = jnp.exp(m_old - m_new)
    p_groups = [jnp.exp(g - m_new) for g in groups]
    l_scr[...] = alpha * l_scr[...] + functools.reduce(lambda a, b: a + b, p_groups)
    m_scr[...] = m_new
    pb = jnp.concatenate([g.astype(BF16) for g in p_groups], axis=1)
    for (r0, r1), v in v_parts:
        acc_scr[r0:r1, :] = alpha[r0:r1] * acc_scr[r0:r1, :] + jnp.dot(
            pb[r0:r1], v, preferred_element_type=F32)


def _diff_attn_kernel(lam_ref, q_ref, k_ref, v_ref, ng_ref, o_ref, m_scr, l_scr, acc_scr,
                      *, tq, lam_init):
    i = pl.program_id(2)
    q = q_ref[0]
    lo, hi = _half_masks(q.shape)
    zero = jnp.zeros_like(q)
    qs = jnp.concatenate([jnp.where(lo, q, zero), jnp.where(hi, q, zero)], axis=0)
    m_scr[...] = jnp.full(m_scr.shape, NEG, F32)
    l_scr[...] = jnp.zeros(l_scr.shape, F32)
    acc_scr[...] = jnp.zeros(acc_scr.shape, F32)

    def kv_block(j):
        start = pl.multiple_of(j * tq, tq)
        return k_ref[0, pl.ds(start, tq), :], v_ref[0, pl.ds(start, tq), :]

    def body(j, carry):
        k, v = kv_block(j)
        _flash_update(_qk(qs, k), [((0, 2 * tq), v)], m_scr, l_scr, acc_scr)
        return carry

    lax.fori_loop(0, i, body, 0)
    k, v = kv_block(i)
    s = _qk(qs, k)
    row = lax.broadcasted_iota(jnp.int32, s.shape, 0) % tq
    col = lax.broadcasted_iota(jnp.int32, s.shape, 1)
    s = jnp.where((col // CHUNK) <= (row // CHUNK), s, NEG)
    _flash_update(s, [((0, 2 * tq), v)], m_scr, l_scr, acc_scr)

    o = acc_scr[...] / jnp.sum(l_scr[...], axis=-1, keepdims=True)
    o = o[:tq] - lam_ref[0] * o[tq:]
    o = o * lax.rsqrt(jnp.mean(o * o, axis=-1, keepdims=True) + EPS)
    o_ref[0] = ((o * ng_ref[...]) * (1.0 - lam_init)).astype(BF16)


def _diff_attention(slab, lam, norm_g, lam_init, tq):
    bsz, s, _ = slab.shape
    kern = functools.partial(_diff_attn_kernel, tq=tq, lam_init=lam_init)
    cb = COL_TILE // LANES
    return pl.pallas_call(
        kern,
        grid=(bsz, A_HEADS, s // tq),
        in_specs=[
            pl.BlockSpec(memory_space=pltpu.SMEM),
            pl.BlockSpec((1, tq, LANES), lambda b, h, i: (b, i, T_AQ * cb + h)),
            pl.BlockSpec((1, s, LANES), lambda b, h, i: (b, 0, T_AK * cb + h)),
            pl.BlockSpec((1, s, LANES), lambda b, h, i: (b, 0, T_AV * cb + h)),
            pl.BlockSpec((1, LANES), lambda b, h, i: (0, 0)),
        ],
        out_specs=pl.BlockSpec((1, tq, LANES), lambda b, h, i: (b, i, h)),
        out_shape=jax.ShapeDtypeStruct((bsz, s, A_HEADS * LANES), BF16),
        scratch_shapes=[pltpu.VMEM((2 * tq, LANES), F32), pltpu.VMEM((2 * tq, LANES), F32),
                        pltpu.VMEM((2 * tq, LANES), F32)],
        compiler_params=_cparams(("arbitrary", "arbitrary", "arbitrary")),
        name="diff_attn",
    )(lam.reshape(1), slab, slab, slab, norm_g.reshape(1, LANES))


KEY_NEG_INF = -2139095040
KEY_POS_INF = 2139095040
P3_SUB = 2
SEARCH_ROWS = 128
SEARCH_MAX_STEPS = 80
NO_TIE_LIMIT = 1e9
VT_ROWS = HEAD_DIM + 16


def _key_to_float(key):
    bits = jnp.where(key >= 0, key, (key - 1) ^ jnp.int32(0x7FFFFFFF))
    return pltpu.bitcast(bits, F32)


def _float_to_key(t):
    bits = pltpu.bitcast(t, jnp.int32)
    return jnp.where(bits >= 0, bits, (bits ^ jnp.int32(0x7FFFFFFF)) + 1)


def _dsa_kernel(q_ref, iq_ref, iw_ref, k_ref, ik_ref, v_ref, o_ref,
                sc_scr, qs_scr, m_scr, l_scr, acc_scr, tau_scr, need_scr, tie_scr, *, tq, topk):
    i = pl.program_id(1)
    nblk = i + 1

    iq = iq_ref[0]
    parts = []
    for p in range(IDX_HEADS // 2):
        pair = iq[:, p * LANES:(p + 1) * LANES]
        lo, hi = _half_masks(pair.shape)
        zero = jnp.zeros_like(pair)
        parts += [jnp.where(lo, pair, zero), jnp.where(hi, pair, zero)]
    iqs = jnp.concatenate(parts, axis=0)
    iw = iw_ref[0].astype(F32)

    def index_block(j):
        start = pl.multiple_of(j * tq, tq)
        hs = jnp.maximum(_qk(iqs, ik_ref[0, pl.ds(start, tq), :]), 0.0)
        score = iw[:, 0:1] * hs[0:tq]
        for h in range(1, IDX_HEADS):
            score = score + iw[:, h:h + 1] * hs[h * tq:(h + 1) * tq]
        return score

    def p1_body(j, carry):
        sc_scr[j] = index_block(j)
        return carry

    lax.fori_loop(0, i, p1_body, 0)
    score = index_block(i)
    row = lax.broadcasted_iota(jnp.int32, score.shape, 0)
    col = lax.broadcasted_iota(jnp.int32, score.shape, 1)
    sc_scr[i] = jnp.where((col // CHUNK) <= (row // CHUNK), score, -jnp.inf)

    kf = float(topk)
    for r0 in range(0, tq, SEARCH_ROWS):
        rows = pl.ds(r0, SEARCH_ROWS)

        def fold(x, op):
            return functools.reduce(op, [x[:, c0:c0 + LANES] for c0 in range(0, tq, LANES)])

        def scan(hit_fn, ext_fn, ext_op, ext_init):
            def body(j, carry):
                cnt, ext = carry
                sblk = sc_scr[j, rows, :]
                cnt = cnt + fold(jnp.where(hit_fn(sblk), 1.0, 0.0), lambda a, b: a + b)
                if ext_fn is not None:
                    ext = ext_op(ext, fold(ext_fn(sblk), ext_op))
                return cnt, ext
            cnt, ext = lax.fori_loop(
                0, nblk, body, (jnp.zeros((SEARCH_ROWS, LANES), F32),
                                jnp.full((SEARCH_ROWS, LANES), ext_init, F32)))
            return jnp.sum(cnt, axis=-1, keepdims=True), ext

        def count_ge(t):
            return scan(lambda sblk: sblk >= t, None, None, 0.0)[0]

        c_ge0, ext = scan(lambda sblk: sblk >= 0.0, lambda sblk: sblk, jnp.maximum, -jnp.inf)
        row_max = jnp.max(ext, axis=-1, keepdims=True)
        c_gt0, ext = scan(lambda sblk: sblk > 0.0,
                          lambda sblk: jnp.where(sblk > -jnp.inf, sblk, jnp.inf), jnp.minimum, jnp.inf)
        row_min = jnp.min(ext, axis=-1, keepdims=True)
        rowpos = i * tq + r0 + lax.broadcasted_iota(jnp.int32, (SEARCH_ROWS, 1), 0)
        n_valid = (((rowpos // CHUNK) + 1) * CHUNK).astype(F32)
        open_row = n_valid < kf
        above = c_gt0 >= kf
        below = c_ge0 < kf
        ikey = lambda v: jnp.full((SEARCH_ROWS, 1), v, jnp.int32)
        lo_k = jnp.where(below, _float_to_key(row_min), ikey(0))
        hi_k = jnp.where(above, _float_to_key(row_max) + 1, jnp.where(below, ikey(0), ikey(1)))
        c_lo = jnp.where(below, n_valid, c_ge0)
        c_hi = jnp.where(above, 0.0, jnp.where(below, c_ge0, c_gt0))
        done = jnp.logical_or(open_row, jnp.logical_not(jnp.logical_or(above, below)))
        state = (jnp.int32(0), lo_k, hi_k, c_lo, c_hi, done.astype(jnp.int32), ikey(0))

        def search_cond(st):
            return jnp.logical_and(st[0] < SEARCH_MAX_STEPS, jnp.min(st[5]) == 0)

        def search_step(st):
            step, lo_k, hi_k, c_lo, c_hi, done, force_bisect = st
            finite = jnp.logical_and(lo_k > KEY_NEG_INF, hi_k < KEY_POS_INF)
            t_lo, t_hi = _key_to_float(lo_k), _key_to_float(hi_k)
            log_lo = jnp.log(c_lo)
            frac = (log_lo - math.log(kf - 0.5)) / (log_lo - jnp.log(jnp.maximum(c_hi, 0.5)))
            k_interp = _float_to_key(t_lo + (t_hi - t_lo) * frac)
            k_mid = (lo_k & hi_k) + ((lo_k ^ hi_k) >> 1)
            interp = jnp.logical_and(finite, force_bisect == 0)
            k = jnp.where(interp, k_interp, k_mid)
            k = jnp.minimum(jnp.maximum(k, lo_k + 1), hi_k - 1)
            c = count_ge(_key_to_float(k))
            ok = c >= kf
            live = done == 0
            new_lo = jnp.where(jnp.logical_and(live, ok), k, lo_k)
            new_hi = jnp.where(jnp.logical_and(live, jnp.logical_not(ok)), k, hi_k)
            new_c_lo = jnp.where(jnp.logical_and(live, ok), c, c_lo)
            new_c_hi = jnp.where(jnp.logical_and(live, jnp.logical_not(ok)), c, c_hi)
            width = lambda a, b: b.astype(F32) - a.astype(F32)
            slow = width(new_lo, new_hi) > 0.5 * width(lo_k, hi_k)
            new_force = jnp.logical_and(interp, slow).astype(jnp.int32)
            finished = jnp.logical_or(new_c_lo == kf, new_hi - new_lo == 1)
            new_done = jnp.maximum(done, finished.astype(jnp.int32))
            return step + 1, new_lo, new_hi, new_c_lo, new_c_hi, new_done, new_force

        _, lo_k, hi_k, c_lo, c_hi, _, _ = lax.while_loop(search_cond, search_step, state)
        tau_scr[rows, :] = jnp.where(open_row, -jnp.inf, _key_to_float(lo_k))
        need_scr[rows, :] = jnp.where(open_row, 0.0,
                                      jnp.where(c_lo == kf, NO_TIE_LIMIT, kf - c_hi))

    tau = tau_scr[...]
    need = need_scr[...]

    q = q_ref[0]
    for p in range(B_HEADS // 2):
        pair = q[:, p * LANES:(p + 1) * LANES]
        lo, hi = _half_masks(pair.shape)
        zero = jnp.zeros_like(pair)
        qs_scr[p * tq:(p + 1) * tq, :] = jnp.where(lo, pair, zero)
        qs_scr[(B_HEADS // 2 + p) * tq:(B_HEADS // 2 + p + 1) * tq, :] = jnp.where(hi, pair, zero)
    m_scr[...] = jnp.full(m_scr.shape, NEG, F32)
    l_scr[...] = jnp.zeros(l_scr.shape, F32)
    acc_scr[...] = jnp.zeros(acc_scr.shape, F32)
    half_rows = (B_HEADS // 2) * tq
    colb = lax.broadcasted_iota(jnp.int32, (tq, tq), 1)

    tie_scr[...] = jnp.zeros(tie_scr.shape, F32)
    upper = (lax.broadcasted_iota(jnp.int32, (tq, tq), 0) <= colb).astype(BF16)

    def select_bias(j):
        sblk = sc_scr[j]
        tie = sblk == tau
        rank = tie_scr[...] + jnp.dot(jnp.where(tie, 1.0, 0.0).astype(BF16), upper,
                                      preferred_element_type=F32)
        tie_scr[...] = rank[:, tq - 1:tq]
        keep_tie = jnp.where(rank <= need, 0.0, NEG)
        return jnp.where(sblk > tau, 0.0, jnp.where(tie, keep_tie, NEG))

    def p3_step(j, nsub):
        width = nsub * tq
        start = pl.multiple_of(j * tq, tq)
        bias = jnp.concatenate([select_bias(j + t) for t in range(nsub)], axis=1)
        s = _qk(qs_scr[...], k_ref[0, pl.ds(start, width), :])
        s = (s.reshape(B_HEADS, tq, width) + bias[None]).reshape(B_HEADS * tq, width)
        v = v_ref[0, pl.ds(start, width), :]
        lo, hi = _half_masks(v.shape)
        zero = jnp.zeros_like(v)
        _flash_update(s, [((0, half_rows), jnp.where(lo, v, zero)),
                          ((half_rows, 2 * half_rows), jnp.where(hi, v, zero))],
                      m_scr, l_scr, acc_scr)

    def p3_body(jj, carry):
        p3_step(jj * P3_SUB, P3_SUB)
        return carry

    lax.fori_loop(0, nblk // P3_SUB, p3_body, 0)
    for r in range(1, P3_SUB):
        @pl.when(nblk % P3_SUB == r)
        def _():
            p3_step(nblk - r, r)

    o = acc_scr[...] / jnp.sum(l_scr[...], axis=-1, keepdims=True)
    for p in range(B_HEADS // 2):
        o_ref[0, :, p * LANES:(p + 1) * LANES] = (
            o[p * tq:(p + 1) * tq] + o[half_rows + p * tq:half_rows + (p + 1) * tq]).astype(BF16)


def _dsa_attention(slab, tq):
    bsz, s, _ = slab.shape
    topk = min(TOPK_MAX, s // 4)
    assert tq >= topk and s % tq == 0
    assert tq % SEARCH_ROWS == 0
    kern = functools.partial(_dsa_kernel, tq=tq, topk=topk)
    idx0 = T_IDX * COL_TILE
    misc0 = T_MISC * COL_TILE
    nblk = s // tq
    width = B_HEADS * HEAD_DIM
    return pl.pallas_call(
        kern,
        grid=(bsz, s // tq),
        in_specs=[
            pl.BlockSpec((1, tq, COL_TILE), lambda b, i: (b, i, T_BQ)),
            pl.BlockSpec((1, tq, 256), lambda b, i: (b, i, idx0 // 256)),
            pl.BlockSpec((1, tq, LANES), lambda b, i: (b, i, (misc0 + LANES) // LANES)),
            pl.BlockSpec((1, s, LANES), lambda b, i: (b, 0, (idx0 + 256) // LANES)),
            pl.BlockSpec((1, s, LANES), lambda b, i: (b, 0, (idx0 + 384) // LANES)),
            pl.BlockSpec((1, s, LANES), lambda b, i: (b, 0, misc0 // LANES)),
        ],
        out_specs=pl.BlockSpec((1, tq, width), lambda b, i: (b, i, 0)),
        out_shape=jax.ShapeDtypeStruct((bsz, s, width), BF16),
        scratch_shapes=[
            pltpu.VMEM((nblk, tq, tq), F32),
            pltpu.VMEM((B_HEADS * tq, LANES), BF16),
            pltpu.VMEM((B_HEADS * tq, LANES), F32),
            pltpu.VMEM((B_HEADS * tq, LANES), F32),
            pltpu.VMEM((B_HEADS * tq, LANES), F32),
            pltpu.VMEM((tq, 1), F32),
            pltpu.VMEM((tq, 1), F32),
            pltpu.VMEM((tq, 1), F32),
        ],
        compiler_params=_cparams(("arbitrary", "arbitrary")),
        name="dsa_attn",
    )(slab, slab, slab, slab, slab, slab)


def _fold_rows(x, op):
    parts = x.reshape(x.shape[0] // 8, 8, x.shape[1])
    return functools.reduce(op, [parts[g] for g in range(parts.shape[0])])


def _dsa_t_kernel(q_ref, iq_ref, iw_ref, k_ref, ik_ref, vt_ref, o_ref,
                  sc_scr, qs_scr, m_scr, acc_scr, *, tq, topk):
    i = pl.program_id(1)
    nblk = i + 1
    nq = B_HEADS * tq

    iq = iq_ref[0]
    parts = []
    for p in range(IDX_HEADS // 2):
        pair = iq[:, p * LANES:(p + 1) * LANES]
        lo, hi = _half_masks(pair.shape)
        zero = jnp.zeros_like(pair)
        parts += [jnp.where(lo, pair, zero), jnp.where(hi, pair, zero)]
    iqs = jnp.concatenate(parts, axis=0)
    iw_t = iw_ref[0].astype(F32).T

    def index_block(j):
        start = pl.multiple_of(j * tq, tq)
        hs = jnp.maximum(_qk(ik_ref[0, pl.ds(start, tq), :], iqs), 0.0)
        score = iw_t[0:1] * hs[:, 0:tq]
        for h in range(1, IDX_HEADS):
            score = score + iw_t[h:h + 1] * hs[:, h * tq:(h + 1) * tq]
        return score

    def p1_body(j, carry):
        sc_scr[j] = index_block(j)
        return carry

    lax.fori_loop(0, i, p1_body, 0)
    score = index_block(i)
    key_i = lax.broadcasted_iota(jnp.int32, score.shape, 0)
    qry_i = lax.broadcasted_iota(jnp.int32, score.shape, 1)
    sc_scr[i] = jnp.where((key_i // CHUNK) <= (qry_i // CHUNK), score, -jnp.inf)

    kf = float(topk)

    def scan(hit_fn, ext_fn, ext_op, ext_init):
        def body(j, carry):
            cnt, ext = carry
            sblk = sc_scr[j]
            cnt = cnt + _fold_rows(jnp.where(hit_fn(sblk), 1.0, 0.0), lambda a, b: a + b)
            if ext_fn is not None:
                ext = ext_op(ext, _fold_rows(ext_fn(sblk), ext_op))
            return cnt, ext
        cnt, ext = lax.fori_loop(0, nblk, body, (jnp.zeros((8, tq), F32),
                                                 jnp.full((8, tq), ext_init, F32)))
        return jnp.sum(cnt, axis=0, keepdims=True), ext

    def count_ge(t):
        return scan(lambda sblk: sblk >= t, None, None, 0.0)[0]

    c_ge0, ext = scan(lambda sblk: sblk >= 0.0, lambda sblk: sblk, jnp.maximum, -jnp.inf)
    col_max = jnp.max(ext, axis=0, keepdims=True)
    c_gt0, ext = scan(lambda sblk: sblk > 0.0,
                      lambda sblk: jnp.where(sblk > -jnp.inf, sblk, jnp.inf), jnp.minimum, jnp.inf)
    col_min = jnp.min(ext, axis=0, keepdims=True)
    qpos = i * tq + lax.broadcasted_iota(jnp.int32, (1, tq), 1)
    n_valid = (((qpos // CHUNK) + 1) * CHUNK).astype(F32)
    open_q = n_valid < kf
    above = c_gt0 >= kf
    below = c_ge0 < kf
    ikey = lambda v: jnp.full((1, tq), v, jnp.int32)
    lo_k = jnp.where(below, _float_to_key(col_min), ikey(0))
    hi_k = jnp.where(above, _float_to_key(col_max) + 1, jnp.where(below, ikey(0), ikey(1)))
    c_lo = jnp.where(below, n_valid, c_ge0)
    c_hi = jnp.where(above, 0.0, jnp.where(below, c_ge0, c_gt0))
    done = jnp.logical_or(open_q, jnp.logical_not(jnp.logical_or(above, below)))
    state = (jnp.int32(0), lo_k, hi_k, c_lo, c_hi, done.astype(jnp.int32), ikey(0))

    def search_cond(st):
        return jnp.logical_and(st[0] < SEARCH_MAX_STEPS, jnp.min(st[5]) == 0)

    def search_step(st):
        step, lo_k, hi_k, c_lo, c_hi, done, force_bisect = st
        finite = jnp.logical_and(lo_k > KEY_NEG_INF, hi_k < KEY_POS_INF)
        t_lo, t_hi = _key_to_float(lo_k), _key_to_float(hi_k)
        log_lo = jnp.log(c_lo)
        frac = (log_lo - math.log(kf - 0.5)) / (log_lo - jnp.log(jnp.maximum(c_hi, 0.5)))
        k_interp = _float_to_key(t_lo + (t_hi - t_lo) * frac)
        k_mid = (lo_k & hi_k) + ((lo_k ^ hi_k) >> 1)
        interp = jnp.logical_and(finite, force_bisect == 0)
        k = jnp.where(interp, k_interp, k_mid)
        k = jnp.minimum(jnp.maximum(k, lo_k + 1), hi_k - 1)
        c = count_ge(_key_to_float(k))
        ok = c >= kf
        live = done == 0
        new_lo = jnp.where(jnp.logical_and(live, ok), k, lo_k)
        new_hi = jnp.where(jnp.logical_and(live, jnp.logical_not(ok)), k, hi_k)
        new_c_lo = jnp.where(jnp.logical_and(live, ok), c, c_lo)
        new_c_hi = jnp.where(jnp.logical_and(live, jnp.logical_not(ok)), c, c_hi)
        width = lambda a, b: b.astype(F32) - a.astype(F32)
        slow = width(new_lo, new_hi) > 0.5 * width(lo_k, hi_k)
        new_force = jnp.logical_and(interp, slow).astype(jnp.int32)
        finished = jnp.logical_or(new_c_lo == kf, new_hi - new_lo == 1)
        new_done = jnp.maximum(done, finished.astype(jnp.int32))
        return step + 1, new_lo, new_hi, new_c_lo, new_c_hi, new_done, new_force

    _, lo_k, hi_k, c_lo, c_hi, _, _ = lax.while_loop(search_cond, search_step, state)
    tau = jnp.where(open_q, -jnp.inf, _key_to_float(lo_k))
    need = jnp.where(open_q, 0.0, jnp.where(c_lo == kf, NO_TIE_LIMIT, kf - c_hi))

    q = q_ref[0]
    for p in range(B_HEADS // 2):
        pair = q[:, p * LANES:(p + 1) * LANES]
        lo, hi = _half_masks(pair.shape)
        zero = jnp.zeros_like(pair)
        qs_scr[p * tq:(p + 1) * tq, :] = jnp.where(lo, pair, zero)
        qs_scr[(B_HEADS // 2 + p) * tq:(B_HEADS // 2 + p + 1) * tq, :] = jnp.where(hi, pair, zero)
    m_scr[...] = jnp.full(m_scr.shape, NEG, F32)
    acc_scr[...] = jnp.zeros(acc_scr.shape, F32)
    key_i = lax.broadcasted_iota(jnp.int32, (tq, tq), 0)
    lower = (lax.broadcasted_iota(jnp.int32, (tq, tq), 1) <= key_i).astype(BF16)

    def select_bias(j, tie_count):
        sblk = sc_scr[j]
        tie = sblk == tau
        rank = tie_count + jnp.dot(lower, jnp.where(tie, 1.0, 0.0).astype(BF16),
                                   preferred_element_type=F32)
        keep_tie = jnp.where(rank <= need, 0.0, NEG)
        bias = jnp.where(sblk > tau, 0.0, jnp.where(tie, keep_tie, NEG))
        return bias, rank[tq - 1:tq, :]

    def p3_step(j, nsub, tie_count):
        start = pl.multiple_of(j * tq, tq)
        biases = []
        for t in range(nsub):
            bias, tie_count = select_bias(j + t, tie_count)
            biases.append(bias)
        bias = jnp.concatenate(biases, axis=0)
        s = _qk(k_ref[0, pl.ds(start, nsub * tq), :], qs_scr[...])
        s = jnp.concatenate([s[:, h * tq:(h + 1) * tq] + bias for h in range(B_HEADS)], axis=1)
        m_old = m_scr[...]
        m_new = jnp.maximum(m_old, jnp.max(_fold_rows(s, jnp.maximum), axis=0, keepdims=True))
        alpha = jnp.exp(m_old - m_new)
        pb = jnp.exp(s - m_new).astype(BF16)
        m_scr[...] = m_new
        pv = functools.reduce(lambda a, b: a + b, [
            jnp.dot(vt_ref[0, j + t], pb[t * tq:(t + 1) * tq], preferred_element_type=F32)
            for t in range(nsub)])
        acc_scr[...] = alpha * acc_scr[...] + pv
        return tie_count

    tie_count = lax.fori_loop(0, nblk // P3_SUB,
                              lambda jj, tc: p3_step(jj * P3_SUB, P3_SUB, tc),
                              jnp.zeros((1, tq), F32))
    for r in range(1, P3_SUB):
        @pl.when(nblk % P3_SUB == r)
        def _():
            p3_step(nblk - r, r, tie_count)

    acc = acc_scr[...]
    o = acc[0:HEAD_DIM] / acc[HEAD_DIM:HEAD_DIM + 1]
    half_cols = (B_HEADS // 2) * tq
    for p in range(B_HEADS // 2):
        pair = jnp.concatenate([o[:, p * tq:(p + 1) * tq],
                                o[:, half_cols + p * tq:half_cols + (p + 1) * tq]], axis=0)
        o_ref[0, :, p * LANES:(p + 1) * LANES] = pair.T.astype(BF16)


def _dsa_t_attention(slab, tq):
    bsz, s, _ = slab.shape
    topk = min(TOPK_MAX, s // 4)
    assert tq >= topk and s % tq == 0
    kern = functools.partial(_dsa_t_kernel, tq=tq, topk=topk)
    idx0 = T_IDX * COL_TILE
    misc0 = T_MISC * COL_TILE
    nblk = s // tq
    width = B_HEADS * HEAD_DIM
    vt = slab[:, :, misc0:misc0 + HEAD_DIM].reshape(bsz, nblk, tq, HEAD_DIM).transpose(0, 1, 3, 2)
    vt = jnp.concatenate([vt, jnp.ones((bsz, nblk, 1, tq), BF16),
                          jnp.zeros((bsz, nblk, VT_ROWS - HEAD_DIM - 1, tq), BF16)], axis=2)
    return pl.pallas_call(
        kern,
        grid=(bsz, s // tq),
        in_specs=[
            pl.BlockSpec((1, tq, COL_TILE), lambda b, i: (b, i, T_BQ)),
            pl.BlockSpec((1, tq, 256), lambda b, i: (b, i, idx0 // 256)),
            pl.BlockSpec((1, tq, LANES), lambda b, i: (b, i, (misc0 + LANES) // LANES)),
            pl.BlockSpec((1, s, LANES), lambda b, i: (b, 0, (idx0 + 256) // LANES)),
            pl.BlockSpec((1, s, LANES), lambda b, i: (b, 0, (idx0 + 384) // LANES)),
            pl.BlockSpec((1, nblk, VT_ROWS, tq), lambda b, i: (b, 0, 0, 0)),
        ],
        out_specs=pl.BlockSpec((1, tq, width), lambda b, i: (b, i, 0)),
        out_shape=jax.ShapeDtypeStruct((bsz, s, width), BF16),
        scratch_shapes=[
            pltpu.VMEM((nblk, tq, tq), F32),
            pltpu.VMEM((B_HEADS * tq, LANES), BF16),
            pltpu.VMEM((1, B_HEADS * tq), F32),
            pltpu.VMEM((VT_ROWS, B_HEADS * tq), F32),
        ],
        compiler_params=_cparams(("arbitrary", "arbitrary")),
        name="dsa_attn",
    )(slab, slab, slab, slab, slab, vt)


BAND_TQ = 2 * CHUNK
BAND_KEYS = (C_LEFT_CHUNKS + BAND_TQ // CHUNK) * CHUNK
BAND_PAD = C_LEFT_CHUNKS * CHUNK


def _band_kernel(q_ref, k_ref, v_ref, bias_ref, o_ref):
    i = pl.program_id(2)
    tq = BAND_TQ
    start = pl.multiple_of(i * tq, tq)
    q = q_ref[0]
    k = k_ref[0, pl.ds(start, BAND_KEYS), :]
    v = v_ref[0, pl.ds(start, BAND_KEYS), :]
    lo, hi = _half_masks(q.shape)
    zq = jnp.zeros_like(q)
    qs = jnp.concatenate([jnp.where(lo, q, zq), jnp.where(hi, q, zq)], axis=0)
    s = _qk(qs, k) + bias_ref[...].reshape(2 * tq, BAND_KEYS)
    col = lax.broadcasted_iota(jnp.int32, s.shape, 1)
    s = jnp.where(col + start >= BAND_PAD, s, NEG)
    m = jnp.max(s, axis=-1, keepdims=True)
    p = jnp.exp(s - m)
    l = jnp.sum(p, axis=-1, keepdims=True)
    pb = p.astype(BF16)
    vlo, vhi = _half_masks(v.shape)
    zv = jnp.zeros_like(v)
    o = (jnp.dot(pb[:tq], jnp.where(vlo, v, zv), preferred_element_type=F32) / l[:tq]
         + jnp.dot(pb[tq:], jnp.where(vhi, v, zv), preferred_element_type=F32) / l[tq:])
    o_ref[0] = o.astype(BF16)


def _band_bias(rel_bias):
    r = jnp.arange(BAND_TQ)[:, None]
    cidx = jnp.arange(BAND_KEYS)[None, :]
    n_diag = BAND_TQ + BAND_KEYS - 1
    rel = BAND_PAD - (BAND_KEYS - 1) + jnp.arange(n_diag)
    g = rel_bias.astype(F32)[:, jnp.clip(rel, -REL_CLIP, REL_CLIP) + REL_CLIP]
    skew = jnp.tile(g, (1, BAND_TQ + 2))[:, :BAND_TQ * (n_diag + 1)]
    bias = skew.reshape(-1, BAND_TQ, n_diag + 1)[:, :, :BAND_KEYS][:, :, ::-1]
    dchunk = (r // CHUNK + C_LEFT_CHUNKS) - cidx // CHUNK
    in_band = jnp.logical_and(dchunk >= 0, dchunk <= C_LEFT_CHUNKS)
    return jnp.where(in_band[None], bias, NEG)


def _band_attention(slab, kp, vp, bias):
    bsz, s, _ = slab.shape
    tq = BAND_TQ
    cb = COL_TILE // LANES
    sp = kp.shape[1]
    width = C_HEADS * HEAD_DIM
    return pl.pallas_call(
        _band_kernel,
        grid=(bsz, C_HEADS // 2, s // tq),
        in_specs=[
            pl.BlockSpec((1, tq, LANES), lambda b, p, i: (b, i, T_CQ * cb + p)),
            pl.BlockSpec((1, sp, LANES), lambda b, p, i: (b, 0, p)),
            pl.BlockSpec((1, sp, LANES), lambda b, p, i: (b, 0, p)),
            pl.BlockSpec((2, tq, BAND_KEYS), lambda b, p, i: (p, 0, 0)),
        ],
        out_specs=pl.BlockSpec((1, tq, LANES), lambda b, p, i: (b, i, p)),
        out_shape=jax.ShapeDtypeStruct((bsz, s, width), BF16),
        compiler_params=_cparams(("arbitrary", "arbitrary", "arbitrary")),
        name="band_attn",
    )(slab, kp, vp, bias)


def _merge_kernel(x_ref, g1_ref, ya_ref, yb_ref, yc_ref, ga_ref, gb_ref, gc_ref,
                  wa_ref, wb_ref, wc_ref, wo_ref, o_ref):
    def branch(y_ref, w_ref, gate_ref):
        return gate_ref[0].astype(F32) * jnp.dot(y_ref[0], w_ref[...], preferred_element_type=F32)

    merged = (branch(ya_ref, wa_ref, ga_ref) + branch(yb_ref, wb_ref, gb_ref)
              + branch(yc_ref, wc_ref, gc_ref))
    mixed = jnp.dot(merged.astype(BF16), wo_ref[...], preferred_element_type=F32)
    o_ref[0] = x_ref[0] + g1_ref[0] * mixed


def _merge(x, g1, ya, yb, yc, slab, wa, wb, wc, wo, tm):
    bsz, s, d = x.shape
    gate_blk = T_GATE0 * COL_TILE // d
    tok = lambda b, i: (b, i, 0)
    full = lambda b, i: (0, 0)
    return pl.pallas_call(
        _merge_kernel,
        grid=(bsz, s // tm),
        in_specs=[
            pl.BlockSpec((1, tm, d), tok),
            pl.BlockSpec((1, 1, d), lambda b, i: (b, 0, 0)),
            pl.BlockSpec((1, tm, ya.shape[2]), tok),
            pl.BlockSpec((1, tm, yb.shape[2]), tok),
            pl.BlockSpec((1, tm, yc.shape[2]), tok),
            pl.BlockSpec((1, tm, d), lambda b, i: (b, i, gate_blk)),
            pl.BlockSpec((1, tm, d), lambda b, i: (b, i, gate_blk + 1)),
            pl.BlockSpec((1, tm, d), lambda b, i: (b, i, gate_blk + 2)),
            pl.BlockSpec(wa.shape, full),
            pl.BlockSpec(wb.shape, full),
            pl.BlockSpec(wc.shape, full),
            pl.BlockSpec(wo.shape, full),
        ],
        out_specs=pl.BlockSpec((1, tm, d), tok),
        out_shape=jax.ShapeDtypeStruct((bsz, s, d), F32),
        compiler_params=_cparams(("arbitrary", "arbitrary")),
        name="merge",
    )(x, g1, ya, yb, yc, slab, slab, slab, wa, wb, wc, wo)


def _router_kernel(x_ref, g_ref, sc_ref, sh_ref, rw_ref, rb_ref, u_ref, comb_ref):
    u = _modulated_norm(x_ref[0], g_ref[...], sc_ref[0], sh_ref[0])
    u_ref[0] = u.astype(BF16)
    logits = lax.dot_general(rw_ref[...], u, (((1,), (1,)), ((), ())),
                             preferred_element_type=F32, precision=lax.Precision.HIGHEST)
    aff = jax.nn.sigmoid(logits)
    sel = aff + rb_ref[...]
    rows = [sel[e:e + 1] for e in range(N_EXPERTS)]
    gscore = []
    for g in range(N_GROUPS):
        r = rows[g * EXPERTS_PER_GROUP:(g + 1) * EXPERTS_PER_GROUP]
        best = None
        for a in range(EXPERTS_PER_GROUP):
            for b in range(a + 1, EXPERTS_PER_GROUP):
                pair = r[a] + r[b]
                best = pair if best is None else jnp.maximum(best, pair)
        gscore.append(best)
    gmax = functools.reduce(jnp.maximum, gscore)
    taken = jnp.zeros_like(gmax) > 1.0
    in_best = []
    for g in range(N_GROUPS):
        is_g = jnp.logical_and(gscore[g] == gmax, jnp.logical_not(taken))
        in_best.append(is_g)
        taken = jnp.logical_or(taken, is_g)
    keep = []
    for e in range(N_EXPERTS):
        g = e // EXPERTS_PER_GROUP
        rank = jnp.zeros_like(gmax)
        for o in range(g * EXPERTS_PER_GROUP, (g + 1) * EXPERTS_PER_GROUP):
            if o == e:
                continue
            ahead = rows[o] > rows[e] if o > e else rows[o] >= rows[e]
            rank = rank + jnp.where(ahead, 1.0, 0.0)
        keep.append(jnp.logical_and(in_best[g], rank < 2.0))
    w = [jnp.where(keep[e], aff[e:e + 1], 0.0) for e in range(N_EXPERTS)]
    total = functools.reduce(lambda a, b: a + b, w)
    comb = jnp.concatenate([we / total for we in w]
                           + [jnp.zeros((LANES - N_EXPERTS, total.shape[1]), F32)], axis=0)
    comb_ref[0] = comb.T


def _router(x, g, sc, sh, router_w, router_b, tm):
    bsz, s, d = x.shape
    tok = lambda b, i: (b, i, 0)
    return pl.pallas_call(
        _router_kernel,
        grid=(bsz, s // tm),
        in_specs=[
            pl.BlockSpec((1, tm, d), tok),
            pl.BlockSpec((1, d), lambda b, i: (0, 0)),
            pl.BlockSpec((1, 1, d), lambda b, i: (b, 0, 0)),
            pl.BlockSpec((1, 1, d), lambda b, i: (b, 0, 0)),
            pl.BlockSpec((N_EXPERTS, d), lambda b, i: (0, 0)),
            pl.BlockSpec((N_EXPERTS, 1), lambda b, i: (0, 0)),
        ],
        out_specs=[pl.BlockSpec((1, tm, d), tok), pl.BlockSpec((1, tm, LANES), tok)],
        out_shape=[jax.ShapeDtypeStruct((bsz, s, d), BF16),
                   jax.ShapeDtypeStruct((bsz, s, LANES), F32)],
        compiler_params=_cparams(("arbitrary", "arbitrary")),
        name="router",
    )(x, g.reshape(1, d), sc, sh, router_w.T, router_b.reshape(N_EXPERTS, 1))


def _moe_kernel(x_ref, g2_ref, u_ref, comb_ref, w1_ref, w3_ref, w2_ref, o_ref, acc_scr):
    e = pl.program_id(2)

    @pl.when(e == 0)
    def _():
        acc_scr[...] = jnp.zeros(acc_scr.shape, F32)

    u = u_ref[0]
    h1 = jnp.dot(u, w1_ref[0], preferred_element_type=F32)
    h3 = jnp.dot(u, w3_ref[0], preferred_element_type=F32)
    h = (h1 * jax.nn.sigmoid(h1)) * h3
    y = jnp.dot(h.astype(BF16), w2_ref[0], preferred_element_type=F32)
    comb = comb_ref[0]
    lane = lax.broadcasted_iota(jnp.int32, comb.shape, 1)
    ce = jnp.sum(jnp.where(lane == e, comb, 0.0), axis=-1, keepdims=True)
    acc_scr[...] += ce * y

    @pl.when(e == pl.num_programs(2) - 1)
    def _():
        o_ref[0] = x_ref[0] + g2_ref[0] * acc_scr[...]


def _moe(x, g2, u, comb, w1, w3, w2, tm):
    bsz, s, d = x.shape
    ne, _, dff = w1.shape
    tok = lambda b, i, e: (b, i, 0)
    return pl.pallas_call(
        _moe_kernel,
        grid=(bsz, s // tm, ne),
        in_specs=[
            pl.BlockSpec((1, tm, d), tok),
            pl.BlockSpec((1, 1, d), lambda b, i, e: (b, 0, 0)),
            pl.BlockSpec((1, tm, d), tok),
            pl.BlockSpec((1, tm, LANES), tok),
            pl.BlockSpec((1, d, dff), lambda b, i, e: (e, 0, 0)),
            pl.BlockSpec((1, d, dff), lambda b, i, e: (e, 0, 0)),
            pl.BlockSpec((1, dff, d), lambda b, i, e: (e, 0, 0)),
        ],
        out_specs=pl.BlockSpec((1, tm, d), tok),
        out_shape=jax.ShapeDtypeStruct((bsz, s, d), F32),
        scratch_shapes=[pltpu.VMEM((tm, d), F32)],
        compiler_params=_cparams(("arbitrary", "arbitrary", "arbitrary")),
        name="moe",
    )(x, g2, u, comb, w1, w3, w2)


def _final_norm_kernel(x_ref, g_ref, o_ref):
    x = x_ref[0]
    o_ref[0] = (x * lax.rsqrt(jnp.mean(x * x, axis=-1, keepdims=True) + EPS)) * g_ref[...]


def _final_norm(x, g, tm):
    bsz, s, d = x.shape
    tok = lambda b, i: (b, i, 0)
    return pl.pallas_call(
        _final_norm_kernel,
        grid=(bsz, s // tm),
        in_specs=[pl.BlockSpec((1, tm, d), tok), pl.BlockSpec((1, d), lambda b, i: (0, 0))],
        out_specs=pl.BlockSpec((1, tm, d), tok),
        out_shape=jax.ShapeDtypeStruct((bsz, s, d), F32),
        compiler_params=_cparams(("arbitrary", "arbitrary")),
        name="final_norm",
    )(x, g.reshape(1, d))


def _tile(s, want):
    t = min(want, s)
    assert s % t == 0
    return t


def kernel(x, c, positions, norm1_g, norm2_g, w_mod, b_mod, w_in, lambda_q1, lambda_k1, lambda_q2, lambda_k2, a_norm_g, c_rel_bias, w_branch_a, w_branch_b, w_branch_c, w_out, router_w, router_b, exp_w1, exp_w3, exp_w2, final_g):
    bsz, s, d = x.shape
    depth = w_mod.shape[0]
    tm = _tile(s, 1024)
    tq_attn = _tile(s, 256)

    mod = _modulation(c, w_mod, b_mod)
    cos_t, sin_t = _rope_tables(positions)

    for layer in range(depth):
        lam_init = 0.8 - 0.6 * math.exp(-0.3 * layer)
        sh1, sc1, g1, sh2, sc2, g2 = [m[:, None, :] for m in jnp.split(mod[layer], 6, axis=-1)]
        lam = (jnp.exp(jnp.sum(lambda_q1[layer] * lambda_k1[layer]))
               - jnp.exp(jnp.sum(lambda_q2[layer] * lambda_k2[layer])) + lam_init)

        slab = _in_proj(x, norm1_g[layer], sc1, sh1, cos_t, sin_t, _build_w_slab(w_in[layer]), tm)
        ya = _diff_attention(slab, lam, a_norm_g[layer], lam_init, _tile(s, 512))
        yb = _dsa_t_attention(slab, tq_attn)
        pad = ((0, 0), (BAND_PAD, 0), (0, 0))
        kp = jnp.pad(slab[:, :, T_CK * COL_TILE:(T_CK + 1) * COL_TILE], pad)
        vp = jnp.pad(slab[:, :, T_CV * COL_TILE:(T_CV + 1) * COL_TILE], pad)
        yc = _band_attention(slab, kp, vp, _band_bias(c_rel_bias[layer]))
        x = _merge(x, g1, ya, yb, yc, slab,
                   w_branch_a[layer].astype(BF16), w_branch_b[layer].astype(BF16),
                   w_branch_c[layer].astype(BF16), w_out[layer].astype(BF16), _tile(s, 512))

        u, comb = _router(x, norm2_g[layer], sc2, sh2, router_w, router_b, _tile(s, 512))
        x = _moe(x, g2, u, comb, exp_w1[layer].astype(BF16), exp_w3[layer].astype(BF16),
                 exp_w2[layer].astype(BF16), tm)

    return _final_norm(x, final_g, tm)
```

```python
import functools
import math

import jax
import jax.numpy as jnp
from jax import lax
from jax.experimental import pallas as pl
from jax.experimental.pallas import tpu as pltpu

F32 = jnp.float32
BF16 = jnp.bfloat16

CHUNK = 64
ROPE_THETA = 10000.0
EPS = 1e-6
A_HEADS = 4
HEAD_DIM = 64
B_HEADS = 8
IDX_HEADS = 4
TOPK_MAX = 256
C_HEADS = 8
C_LEFT_CHUNKS = 8
REL_CLIP = 256
N_EXPERTS = 16
N_GROUPS = 4
EXPERTS_PER_GROUP = 4
N_BRANCHES = 3

LANES = 128
NEG = -1e30
LOG2E = math.log2(math.e)
VMEM_LIMIT = 56 * 1024 * 1024

COL_TILE = 512
R_AQ, R_AK, R_BQ, R_IDX = 0, 1, 2, 3
P_AV, P_CQ, P_CK, P_CV, P_MISC = 0, 1, 2, 3, 4


def _cparams(sem):
    return pltpu.CompilerParams(dimension_semantics=sem, vmem_limit_bytes=VMEM_LIMIT)


def _mod_kernel(c_ref, w_ref, b_ref, o_ref):
    c = c_ref[...]
    ca = c * jax.nn.sigmoid(c)
    o_ref[0] = jnp.dot(ca, w_ref[0], preferred_element_type=F32) + b_ref[0]


def _modulation(c, w_mod, b_mod):
    depth, d, n6 = w_mod.shape
    bsz = c.shape[0]
    tn = 1024
    return pl.pallas_call(
        _mod_kernel,
        grid=(depth, n6 // tn),
        in_specs=[
            pl.BlockSpec((bsz, d), lambda l, j: (0, 0)),
            pl.BlockSpec((1, d, tn), lambda l, j: (l, 0, j)),
            pl.BlockSpec((1, 1, tn), lambda l, j: (l, 0, j)),
        ],
        out_specs=pl.BlockSpec((1, bsz, tn), lambda l, j: (l, 0, j)),
        out_shape=jax.ShapeDtypeStruct((depth, bsz, n6), F32),
        compiler_params=_cparams(("arbitrary", "arbitrary")),
        name="modulation",
    )(c, w_mod, b_mod.reshape(depth, 1, n6))


def _modulated_norm(x, g, sc, sh):
    y = x * lax.rsqrt(jnp.mean(x * x, axis=-1, keepdims=True) + EPS)
    return (y * g) * (1.0 + sc) + sh


def _norm_kernel(x_ref, g_ref, sc_ref, sh_ref, u_ref):
    u_ref[0] = _modulated_norm(x_ref[0], g_ref[...], sc_ref[0], sh_ref[0]).astype(BF16)


def _norm(x, g, sc, sh, tm):
    bsz, s, d = x.shape
    tok = lambda b, i: (b, i, 0)
    per_batch = lambda b, i: (b, 0, 0)
    return pl.pallas_call(
        _norm_kernel,
        grid=(bsz, s // tm),
        in_specs=[pl.BlockSpec((1, tm, d), tok), pl.BlockSpec((1, d), lambda b, i: (0, 0)),
                  pl.BlockSpec((1, 1, d), per_batch), pl.BlockSpec((1, 1, d), per_batch)],
        out_specs=pl.BlockSpec((1, tm, d), tok),
        out_shape=jax.ShapeDtypeStruct((bsz, s, d), BF16),
        compiler_params=_cparams(("arbitrary", "arbitrary")),
        name="norm",
    )(x, g.reshape(1, d), sc, sh)


def _proj_kernel(*refs, mode):
    if mode == "rope":
        u_ref, cos_ref, sin_ref, w_ref, o_ref = refs
    else:
        u_ref, w_ref, o_ref = refs
    acc = jnp.dot(u_ref[0], w_ref[...], preferred_element_type=F32)
    if mode == "rope":
        cos, sin = cos_ref[0], sin_ref[0]
        groups = [acc[:, c:c + LANES] for c in range(0, COL_TILE, LANES)]
        acc = jnp.concatenate([g * cos + pltpu.roll(g, LANES // 2, axis=1) * sin for g in groups],
                              axis=1)
    elif mode == "gate":
        acc = 1.0 / (1.0 + jnp.exp(-acc))
    o_ref[0] = acc.astype(BF16)


def _project(u, w, rope_tables, mode, tm):
    bsz, s, d = u.shape
    ncols = w.shape[1]
    tok = lambda b, i, j: (b, i, 0)
    in_specs = [pl.BlockSpec((1, tm, d), tok)]
    in_specs += [pl.BlockSpec((1, tm, LANES), tok) for _ in rope_tables]
    in_specs += [pl.BlockSpec((d, COL_TILE), lambda b, i, j: (0, j))]
    return pl.pallas_call(
        functools.partial(_proj_kernel, mode=mode),
        grid=(bsz, s // tm, ncols // COL_TILE),
        in_specs=in_specs,
        out_specs=pl.BlockSpec((1, tm, COL_TILE), lambda b, i, j: (b, i, j)),
        out_shape=jax.ShapeDtypeStruct((bsz, s, ncols), BF16),
        compiler_params=_cparams(("arbitrary", "arbitrary", "arbitrary")),
        name="proj_" + mode,
    )(u, *rope_tables, w)


def _pair_layout(w):
    half = HEAD_DIM // 2
    col = jnp.arange(w.shape[1])
    base, r = (col // LANES) * LANES, col % LANES
    src = base + ((r // half) % 2) * HEAD_DIM + (r // HEAD_DIM) * half + r % half
    return w[:, src]


def _build_weights(w_in):
    sizes = (512, 512, 512, 512, 64, 64, 256, 64, 4, 512, 512, 512, 3072)
    parts, start = [], 0
    for sz in sizes:
        parts.append(w_in[:, start:start + sz])
        start += sz
    aq, ak, av, bq, bk, bv, iq, ik, iw, cq, ck, cv, gates = parts
    d = w_in.shape[0]
    qscale = HEAD_DIM ** -0.5 * LOG2E
    iw_scale = IDX_HEADS ** -0.5 * HEAD_DIM ** -0.5
    zeros = lambda n: jnp.zeros((d, n), w_in.dtype)
    w_rope = _pair_layout(jnp.concatenate([aq * qscale, ak, bq * qscale, iq, bk, bk, ik, ik], axis=1))
    w_plain = jnp.concatenate([av, cq * qscale, ck, cv,
                               bv, bv, iw * iw_scale, zeros(LANES - IDX_HEADS), zeros(256)], axis=1)
    return w_rope.astype(BF16), gates.astype(BF16), w_plain.astype(BF16)


def _rope_tables(positions):
    half = HEAD_DIM // 2
    inv = ROPE_THETA ** (-jnp.arange(half, dtype=F32) / half)
    ang = positions.astype(F32)[..., None] * inv
    cos, sin = jnp.cos(ang), jnp.sin(ang)
    cos_t = jnp.tile(cos, (1, 1, LANES // half))
    sin_t = jnp.concatenate([-sin, -sin, sin, sin], axis=-1)
    return cos_t, sin_t


def _split_pair(pair, interleaved):
    lane = lax.broadcasted_iota(jnp.int32, pair.shape, 1)
    first = ((lane // (HEAD_DIM // 2)) % 2 == 0) if interleaved else (lane < HEAD_DIM)
    zero = jnp.zeros_like(pair)
    return jnp.where(first, pair, zero), jnp.where(first, zero, pair)


def _qk(q, k):
    return lax.dot_general(q, k, (((1,), (1,)), ((), ())), preferred_element_type=F32)


def _flash_update(s, v_parts, m_scr, l_scr, acc_scr):
    groups = [s[:, c:c + LANES] for c in range(0, s.shape[1], LANES)]
    m_old = m_scr[...]
    lane_max = functools.reduce(jnp.maximum, groups)
    m_new = jnp.maximum(m_old, jnp.max(lane_max, axis=-1, keepdims=True))
    alpha = jnp.exp2(m_old - m_new)
    p_groups = [jnp.exp2(g - m_new) for g in groups]
    l_scr[...] = alpha * l_scr[...] + functools.reduce(lambda a, b: a + b, p_groups)
    m_scr[...] = m_new
    pb = jnp.concatenate([g.astype(BF16) for g in p_groups], axis=1)
    for (r0, r1), v in v_parts:
        acc_scr[r0:r1, :] = alpha[r0:r1] * acc_scr[r0:r1, :] + jnp.dot(
            pb[r0:r1], v, preferred_element_type=F32)


def _diff_attn_kernel(lam_ref, q_ref, k_ref, v_ref, ng_ref, o_ref, m_scr, l_scr, acc_scr,
                      *, tq, lam_init):
    i = pl.program_id(2)
    qs = jnp.concatenate(_split_pair(q_ref[0], True), axis=0)
    m_scr[...] = jnp.full(m_scr.shape, NEG, F32)
    l_scr[...] = jnp.zeros(l_scr.shape, F32)
    acc_scr[...] = jnp.zeros(acc_scr.shape, F32)

    def kv_block(j):
        start = pl.multiple_of(j * tq, tq)
        return k_ref[0, pl.ds(start, tq), :], v_ref[0, pl.ds(start, tq), :]

    def body(j, carry):
        k, v = kv_block(j)
        _flash_update(_qk(qs, k), [((0, 2 * tq), v)], m_scr, l_scr, acc_scr)
        return carry

    lax.fori_loop(0, i, body, 0)
    k, v = kv_block(i)
    s = _qk(qs, k)
    row = lax.broadcasted_iota(jnp.int32, s.shape, 0) % tq
    col = lax.broadcasted_iota(jnp.int32, s.shape, 1)
    s = jnp.where((col // CHUNK) <= (row // CHUNK), s, NEG)
    _flash_update(s, [((0, 2 * tq), v)], m_scr, l_scr, acc_scr)

    o = acc_scr[...] / jnp.sum(l_scr[...], axis=-1, keepdims=True)
    o = o[:tq] - lam_ref[0] * o[tq:]
    o = o * lax.rsqrt(jnp.mean(o * o, axis=-1, keepdims=True) + EPS)
    o_ref[0] = ((o * ng_ref[...]) * (1.0 - lam_init)).astype(BF16)


def _diff_attention(rope_slab, plain_slab, lam, norm_g, lam_init, tq):
    bsz, s, _ = rope_slab.shape
    kern = functools.partial(_diff_attn_kernel, tq=tq, lam_init=lam_init)
    cb = COL_TILE // LANES
    return pl.pallas_call(
        kern,
        grid=(bsz, A_HEADS, s // tq),
        in_specs=[
            pl.BlockSpec(memory_space=pltpu.SMEM),
            pl.BlockSpec((1, tq, LANES), lambda b, h, i: (b, i, R_AQ * cb + h)),
            pl.BlockSpec((1, s, LANES), lambda b, h, i: (b, 0, R_AK * cb + h)),
            pl.BlockSpec((1, s, LANES), lambda b, h, i: (b, 0, P_AV * cb + h)),
            pl.BlockSpec((1, LANES), lambda b, h, i: (0, 0)),
        ],
        out_specs=pl.BlockSpec((1, tq, LANES), lambda b, h, i: (b, i, h)),
        out_shape=jax.ShapeDtypeStruct((bsz, s, A_HEADS * LANES), BF16),
        scratch_shapes=[pltpu.VMEM((2 * tq, LANES), F32), pltpu.VMEM((2 * tq, LANES), F32),
                        pltpu.VMEM((2 * tq, LANES), F32)],
        compiler_params=_cparams(("arbitrary", "arbitrary", "arbitrary")),
        name="diff_attn",
    )(lam.reshape(1), rope_slab, rope_slab, plain_slab, norm_g.reshape(1, LANES))


KEY_NEG_INF = -2139095040
KEY_POS_INF = 2139095040
SEARCH_MAX_STEPS = 80
NO_TIE_LIMIT = 1e9
VT_ROWS = HEAD_DIM + 16


def _key_to_float(key):
    bits = jnp.where(key >= 0, key, (key - 1) ^ jnp.int32(0x7FFFFFFF))
    return pltpu.bitcast(bits, F32)


def _float_to_key(t):
    bits = pltpu.bitcast(t, jnp.int32)
    return jnp.where(bits >= 0, bits, (bits ^ jnp.int32(0x7FFFFFFF)) + 1)


def _fold_rows(x, op):
    parts = x.reshape(x.shape[0] // 8, 8, x.shape[1])
    level = [parts[g] for g in range(parts.shape[0])]
    while len(level) > 1:
        nxt = [op(level[a], level[a + 1]) for a in range(0, len(level) - 1, 2)]
        level = nxt + ([level[-1]] if len(level) % 2 else [])
    return level[0]


def _dsa_kernel(q_ref, iq_ref, iw_ref, k_ref, ik_ref, vt_ref, o_ref,
                sc_scr, qs_scr, m_scr, acc_scr, *, tq, topk):
    i = pl.program_id(1)
    nblk = i + 1
    npair = (nblk + 1) // 2
    nq = B_HEADS * tq

    iq = iq_ref[0]
    parts = []
    for p in range(IDX_HEADS // 2):
        parts += list(_split_pair(iq[:, p * LANES:(p + 1) * LANES], True))
    iqs = jnp.concatenate(parts, axis=0)
    iw_t = iw_ref[0].astype(F32).T

    def index_block(j):
        start = pl.multiple_of(j * tq, tq)
        hs = jnp.maximum(_qk(ik_ref[0, pl.ds(start, tq), :], iqs), 0.0)
        score = iw_t[0:1] * hs[:, 0:tq]
        for h in range(1, IDX_HEADS):
            score = score + iw_t[h:h + 1] * hs[:, h * tq:(h + 1) * tq]
        return score

    def score_slot(j):
        return j // 2, pl.ds(pl.multiple_of((j % 2) * tq, tq), tq)

    def p1_body(j, carry):
        pair, rows = score_slot(j)
        sc_scr[pair, rows, :] = index_block(j)
        return carry

    lax.fori_loop(0, i, p1_body, 0)
    score = index_block(i)
    key_i = lax.broadcasted_iota(jnp.int32, score.shape, 0)
    qry_i = lax.broadcasted_iota(jnp.int32, score.shape, 1)
    pair, rows = score_slot(i)
    sc_scr[pair, rows, :] = jnp.where((key_i // CHUNK) <= (qry_i // CHUNK), score, -jnp.inf)

    @pl.when(nblk % 2 == 1)
    def _():
        sc_scr[npair - 1, tq:2 * tq, :] = jnp.full((tq, tq), -jnp.inf, F32)

    kf = float(topk)

    def scan(hit_fn, ext_fn, ext_op, ext_init):
        def body(jj, carry):
            cnt, ext = carry
            sblk = sc_scr[jj]
            cnt = cnt + _fold_rows(jnp.where(hit_fn(sblk), 1.0, 0.0), lambda a, b: a + b)
            if ext_fn is not None:
                ext = ext_op(ext, _fold_rows(ext_fn(sblk), ext_op))
            return cnt, ext
        cnt, ext = lax.fori_loop(0, npair, body, (jnp.zeros((8, tq), F32),
                                                  jnp.full((8, tq), ext_init, F32)))
        return jnp.sum(cnt, axis=0, keepdims=True), ext

    def count_ge(t):
        return scan(lambda sblk: sblk >= t, None, None, 0.0)[0]

    c_ge0, ext = scan(lambda sblk: sblk >= 0.0, lambda sblk: sblk, jnp.maximum, -jnp.inf)
    col_max = jnp.max(ext, axis=0, keepdims=True)
    c_gt0, ext = scan(lambda sblk: sblk > 0.0,
                      lambda sblk: jnp.where(sblk > -jnp.inf, sblk, jnp.inf), jnp.minimum, jnp.inf)
    col_min = jnp.min(ext, axis=0, keepdims=True)
    qpos = i * tq + lax.broadcasted_iota(jnp.int32, (1, tq), 1)
    n_valid = (((qpos // CHUNK) + 1) * CHUNK).astype(F32)
    open_q = n_valid < kf
    above = c_gt0 >= kf
    below = c_ge0 < kf
    ikey = lambda v: jnp.full((1, tq), v, jnp.int32)
    lo_k = jnp.where(below, _float_to_key(col_min), ikey(0))
    hi_k = jnp.where(above, _float_to_key(col_max) + 1, jnp.where(below, ikey(0), ikey(1)))
    c_lo = jnp.where(below, n_valid, c_ge0)
    c_hi = jnp.where(above, 0.0, jnp.where(below, c_ge0, c_gt0))
    done = jnp.logical_or(open_q, jnp.logical_not(jnp.logical_or(above, below)))
    state = (jnp.int32(0), lo_k, hi_k, c_lo, c_hi, done.astype(jnp.int32), ikey(0))

    def search_cond(st):
        return jnp.logical_and(st[0] < SEARCH_MAX_STEPS, jnp.min(st[5]) == 0)

    def search_step(st):
        step, lo_k, hi_k, c_lo, c_hi, done, force_bisect = st
        finite = jnp.logical_and(lo_k > KEY_NEG_INF, hi_k < KEY_POS_INF)
        t_lo, t_hi = _key_to_float(lo_k), _key_to_float(hi_k)
        log_lo = jnp.log(c_lo)
        frac = (log_lo - math.log(kf - 0.5)) / (log_lo - jnp.log(jnp.maximum(c_hi, 0.5)))
        k_interp = _float_to_key(t_lo + (t_hi - t_lo) * frac)
        k_mid = (lo_k & hi_k) + ((lo_k ^ hi_k) >> 1)
        interp = jnp.logical_and(finite, force_bisect == 0)
        k = jnp.where(interp, k_interp, k_mid)
        k = jnp.minimum(jnp.maximum(k, lo_k + 1), hi_k - 1)
        c = count_ge(_key_to_float(k))
        ok = c >= kf
        live = done == 0
        new_lo = jnp.where(jnp.logical_and(live, ok), k, lo_k)
        new_hi = jnp.where(jnp.logical_and(live, jnp.logical_not(ok)), k, hi_k)
        new_c_lo = jnp.where(jnp.logical_and(live, ok), c, c_lo)
        new_c_hi = jnp.where(jnp.logical_and(live, jnp.logical_not(ok)), c, c_hi)
        width = lambda a, b: b.astype(F32) - a.astype(F32)
        slow = width(new_lo, new_hi) > 0.5 * width(lo_k, hi_k)
        new_force = jnp.logical_and(interp, slow).astype(jnp.int32)
        finished = jnp.logical_or(new_c_lo == kf, new_hi - new_lo == 1)
        new_done = jnp.maximum(done, finished.astype(jnp.int32))
        return step + 1, new_lo, new_hi, new_c_lo, new_c_hi, new_done, new_force

    _, lo_k, hi_k, c_lo, c_hi, _, _ = lax.while_loop(search_cond, search_step, state)
    tau = jnp.where(open_q, -jnp.inf, _key_to_float(lo_k))
    need = jnp.where(open_q, 0.0, jnp.where(c_lo == kf, NO_TIE_LIMIT, kf - c_hi))

    q = q_ref[0]
    for p in range(B_HEADS // 2):
        even, odd = _split_pair(q[:, p * LANES:(p + 1) * LANES], True)
        qs_scr[p * tq:(p + 1) * tq, :] = even
        qs_scr[(B_HEADS // 2 + p) * tq:(B_HEADS // 2 + p + 1) * tq, :] = odd
    m_scr[...] = jnp.full(m_scr.shape, NEG, F32)
    acc_scr[...] = jnp.zeros(acc_scr.shape, F32)
    key_i = lax.broadcasted_iota(jnp.int32, (tq, tq), 0)
    lower = (lax.broadcasted_iota(jnp.int32, (tq, tq), 1) <= key_i).astype(BF16)

    def p3_body(jj, tie_count):
        start = pl.multiple_of(jj * 2 * tq, 2 * tq)
        sblk = sc_scr[jj]
        tie = sblk == tau
        tie01 = jnp.where(tie, 1.0, 0.0).astype(BF16)
        rank_a = tie_count + jnp.dot(lower, tie01[0:tq], preferred_element_type=F32)
        rank_b = rank_a[tq - 1:tq, :] + jnp.dot(lower, tie01[tq:2 * tq], preferred_element_type=F32)
        rank = jnp.concatenate([rank_a, rank_b], axis=0)
        keep_tie = jnp.where(rank <= need, 0.0, NEG)
        bias = jnp.where(sblk > tau, 0.0, jnp.where(tie, keep_tie, NEG))
        s = _qk(k_ref[0, pl.ds(start, 2 * tq), :], qs_scr[...])
        s = jnp.concatenate([s[:, h * tq:(h + 1) * tq] + bias for h in range(B_HEADS)], axis=1)
        m_old = m_scr[...]
        m_new = jnp.maximum(m_old, jnp.max(_fold_rows(s, jnp.maximum), axis=0, keepdims=True))
        alpha = jnp.exp2(m_old - m_new)
        pb = jnp.exp2(s - m_new).astype(BF16)
        m_scr[...] = m_new
        pv = (jnp.dot(vt_ref[0, 2 * jj], pb[0:tq], preferred_element_type=F32)
              + jnp.dot(vt_ref[0, 2 * jj + 1], pb[tq:2 * tq], preferred_element_type=F32))
        acc_scr[...] = alpha * acc_scr[...] + pv
        return rank[2 * tq - 1:2 * tq, :]

    lax.fori_loop(0, npair, p3_body, jnp.zeros((1, tq), F32))
    acc = acc_scr[...]
    o = acc[0:HEAD_DIM] / acc[HEAD_DIM:HEAD_DIM + 1]
    half_cols = (B_HEADS // 2) * tq
    for p in range(B_HEADS // 2):
        pair = jnp.concatenate([o[:, p * tq:(p + 1) * tq],
                                o[:, half_cols + p * tq:half_cols + (p + 1) * tq]], axis=0)
        o_ref[0, :, p * LANES:(p + 1) * LANES] = pair.T.astype(BF16)


def _dsa_attention(rope_slab, plain_slab, tq):
    bsz, s, _ = rope_slab.shape
    topk = min(TOPK_MAX, s // 4)
    nblk = s // tq
    assert tq >= topk and s % tq == 0 and nblk % 2 == 0
    kern = functools.partial(_dsa_kernel, tq=tq, topk=topk)
    idx0 = R_IDX * COL_TILE
    misc0 = P_MISC * COL_TILE
    width = B_HEADS * HEAD_DIM
    vt = plain_slab[:, :, misc0:misc0 + HEAD_DIM].reshape(bsz, nblk, tq, HEAD_DIM).transpose(0, 1, 3, 2)
    vt = jnp.concatenate([vt, jnp.ones((bsz, nblk, 1, tq), BF16),
                          jnp.zeros((bsz, nblk, VT_ROWS - HEAD_DIM - 1, tq), BF16)], axis=2)
    return pl.pallas_call(
        kern,
        grid=(bsz, s // tq),
        in_specs=[
            pl.BlockSpec((1, tq, COL_TILE), lambda b, i: (b, i, R_BQ)),
            pl.BlockSpec((1, tq, 256), lambda b, i: (b, i, idx0 // 256)),
            pl.BlockSpec((1, tq, LANES), lambda b, i: (b, i, (misc0 + LANES) // LANES)),
            pl.BlockSpec((1, s, LANES), lambda b, i: (b, 0, (idx0 + 256) // LANES)),
            pl.BlockSpec((1, s, LANES), lambda b, i: (b, 0, (idx0 + 384) // LANES)),
            pl.BlockSpec((1, nblk, VT_ROWS, tq), lambda b, i: (b, 0, 0, 0)),
        ],
        out_specs=pl.BlockSpec((1, tq, width), lambda b, i: (b, i, 0)),
        out_shape=jax.ShapeDtypeStruct((bsz, s, width), BF16),
        scratch_shapes=[
            pltpu.VMEM((nblk // 2, 2 * tq, tq), F32),
            pltpu.VMEM((B_HEADS * tq, LANES), BF16),
            pltpu.VMEM((1, B_HEADS * tq), F32),
            pltpu.VMEM((VT_ROWS, B_HEADS * tq), F32),
        ],
        compiler_params=_cparams(("arbitrary", "arbitrary")),
        name="dsa_attn",
    )(rope_slab, rope_slab, plain_slab, rope_slab, rope_slab, vt)


BAND_TQ = 2 * CHUNK
BAND_KEYS = (C_LEFT_CHUNKS + BAND_TQ // CHUNK) * CHUNK
BAND_PAD = C_LEFT_CHUNKS * CHUNK


def _band_kernel(q_ref, k_ref, v_ref, bias_ref, o_ref):
    i = pl.program_id(1)
    tq = BAND_TQ
    start = pl.multiple_of(i * tq, tq)
    col = lax.broadcasted_iota(jnp.int32, (2 * tq, BAND_KEYS), 1)
    key_ok = col + start >= BAND_PAD
    for p in range(C_HEADS // 2):
        lanes = slice(p * LANES, (p + 1) * LANES)
        k = k_ref[0, pl.ds(start, BAND_KEYS), lanes]
        v = v_ref[0, pl.ds(start, BAND_KEYS), lanes]
        qs = jnp.concatenate(_split_pair(q_ref[0, :, lanes], False), axis=0)
        s = _qk(qs, k) + bias_ref[2 * p:2 * p + 2].reshape(2 * tq, BAND_KEYS)
        s = jnp.where(key_ok, s, NEG)
        m = jnp.max(s, axis=-1, keepdims=True)
        e = jnp.exp2(s - m)
        l = jnp.sum(e, axis=-1, keepdims=True)
        eb = e.astype(BF16)
        v_even, v_odd = _split_pair(v, False)
        o = (jnp.dot(eb[:tq], v_even, preferred_element_type=F32) / l[:tq]
             + jnp.dot(eb[tq:], v_odd, preferred_element_type=F32) / l[tq:])
        o_ref[0, :, lanes] = o.astype(BF16)


def _band_bias(rel_bias):
    r = jnp.arange(BAND_TQ)[:, None]
    cidx = jnp.arange(BAND_KEYS)[None, :]
    n_diag = BAND_TQ + BAND_KEYS - 1
    rel = BAND_PAD - (BAND_KEYS - 1) + jnp.arange(n_diag)
    g = rel_bias.astype(F32)[:, jnp.clip(rel, -REL_CLIP, REL_CLIP) + REL_CLIP] * LOG2E
    skew = jnp.tile(g, (1, BAND_TQ + 2))[:, :BAND_TQ * (n_diag + 1)]
    bias = skew.reshape(-1, BAND_TQ, n_diag + 1)[:, :, :BAND_KEYS][:, :, ::-1]
    dchunk = (r // CHUNK + C_LEFT_CHUNKS) - cidx // CHUNK
    in_band = jnp.logical_and(dchunk >= 0, dchunk <= C_LEFT_CHUNKS)
    return jnp.where(in_band[None], bias, NEG)


def _band_attention(plain_slab, kp, vp, bias):
    bsz, s, _ = plain_slab.shape
    tq = BAND_TQ
    sp = kp.shape[1]
    width = C_HEADS * HEAD_DIM
    return pl.pallas_call(
        _band_kernel,
        grid=(bsz, s // tq),
        in_specs=[
            pl.BlockSpec((1, tq, width), lambda b, i: (b, i, P_CQ)),
            pl.BlockSpec((1, sp, width), lambda b, i: (b, 0, 0)),
            pl.BlockSpec((1, sp, width), lambda b, i: (b, 0, 0)),
            pl.BlockSpec((C_HEADS, tq, BAND_KEYS), lambda b, i: (0, 0, 0)),
        ],
        out_specs=pl.BlockSpec((1, tq, width), lambda b, i: (b, i, 0)),
        out_shape=jax.ShapeDtypeStruct((bsz, s, width), BF16),
        compiler_params=_cparams(("arbitrary", "arbitrary")),
        name="band_attn",
    )(plain_slab, kp, vp, bias)


def _merge_kernel(x_ref, g1_ref, ya_ref, yb_ref, yc_ref, ga_ref, gb_ref, gc_ref,
                  wa_ref, wb_ref, wc_ref, wo_ref, o_ref):
    def branch(y_ref, w_ref, gate_ref):
        return gate_ref[0].astype(F32) * jnp.dot(y_ref[0], w_ref[...], preferred_element_type=F32)

    merged = (branch(ya_ref, wa_ref, ga_ref) + branch(yb_ref, wb_ref, gb_ref)
              + branch(yc_ref, wc_ref, gc_ref))
    mixed = jnp.dot(merged.astype(BF16), wo_ref[...], preferred_element_type=F32)
    o_ref[0] = x_ref[0] + g1_ref[0] * mixed


def _merge(x, g1, ya, yb, yc, gate_slab, wa, wb, wc, wo, tm):
    bsz, s, d = x.shape
    tok = lambda b, i: (b, i, 0)
    full = lambda b, i: (0, 0)
    return pl.pallas_call(
        _merge_kernel,
        grid=(bsz, s // tm),
        in_specs=[
            pl.BlockSpec((1, tm, d), tok),
            pl.BlockSpec((1, 1, d), lambda b, i: (b, 0, 0)),
            pl.BlockSpec((1, tm, ya.shape[2]), tok),
            pl.BlockSpec((1, tm, yb.shape[2]), tok),
            pl.BlockSpec((1, tm, yc.shape[2]), tok),
            pl.BlockSpec((1, tm, d), lambda b, i: (b, i, 0)),
            pl.BlockSpec((1, tm, d), lambda b, i: (b, i, 1)),
            pl.BlockSpec((1, tm, d), lambda b, i: (b, i, 2)),
            pl.BlockSpec(wa.shape, full),
            pl.BlockSpec(wb.shape, full),
            pl.BlockSpec(wc.shape, full),
            pl.BlockSpec(wo.shape, full),
        ],
        out_specs=pl.BlockSpec((1, tm, d), tok),
        out_shape=jax.ShapeDtypeStruct((bsz, s, d), F32),
        compiler_params=_cparams(("arbitrary", "arbitrary")),
        name="merge",
    )(x, g1, ya, yb, yc, gate_slab, gate_slab, gate_slab, wa, wb, wc, wo)


def _router_kernel(x_ref, g_ref, sc_ref, sh_ref, rw_ref, rb_ref, u_ref, comb_ref):
    u = _modulated_norm(x_ref[0], g_ref[...], sc_ref[0], sh_ref[0])
    u_ref[0] = u.astype(BF16)
    logits = lax.dot_general(rw_ref[...], u, (((1,), (1,)), ((), ())),
                             preferred_element_type=F32, precision=lax.Precision.HIGHEST)
    aff = jax.nn.sigmoid(logits)
    sel = aff + rb_ref[...]
    rows = [sel[e:e + 1] for e in range(N_EXPERTS)]
    gscore = []
    for g in range(N_GROUPS):
        r = rows[g * EXPERTS_PER_GROUP:(g + 1) * EXPERTS_PER_GROUP]
        best = None
        for a in range(EXPERTS_PER_GROUP):
            for b in range(a + 1, EXPERTS_PER_GROUP):
                pair = r[a] + r[b]
                best = pair if best is None else jnp.maximum(best, pair)
        gscore.append(best)
    gmax = functools.reduce(jnp.maximum, gscore)
    taken = jnp.zeros_like(gmax) > 1.0
    in_best = []
    for g in range(N_GROUPS):
        is_g = jnp.logical_and(gscore[g] == gmax, jnp.logical_not(taken))
        in_best.append(is_g)
        taken = jnp.logical_or(taken, is_g)
    keep = []
    for e in range(N_EXPERTS):
        g = e // EXPERTS_PER_GROUP
        rank = jnp.zeros_like(gmax)
        for o in range(g * EXPERTS_PER_GROUP, (g + 1) * EXPERTS_PER_GROUP):
            if o == e:
                continue
            ahead = rows[o] > rows[e] if o > e else rows[o] >= rows[e]
            rank = rank + jnp.where(ahead, 1.0, 0.0)
        keep.append(jnp.logical_and(in_best[g], rank < 2.0))
    w = [jnp.where(keep[e], aff[e:e + 1], 0.0) for e in range(N_EXPERTS)]
    total = functools.reduce(lambda a, b: a + b, w)
    comb = jnp.concatenate([we / total for we in w]
                           + [jnp.zeros((LANES - N_EXPERTS, total.shape[1]), F32)], axis=0)
    comb_ref[0] = comb.T


def _router(x, g, sc, sh, router_w, router_b, tm):
    bsz, s, d = x.shape
    tok = lambda b, i: (b, i, 0)
    return pl.pallas_call(
        _router_kernel,
        grid=(bsz, s // tm),
        in_specs=[
            pl.BlockSpec((1, tm, d), tok),
            pl.BlockSpec((1, d), lambda b, i: (0, 0)),
            pl.BlockSpec((1, 1, d), lambda b, i: (b, 0, 0)),
            pl.BlockSpec((1, 1, d), lambda b, i: (b, 0, 0)),
            pl.BlockSpec((N_EXPERTS, d), lambda b, i: (0, 0)),
            pl.BlockSpec((N_EXPERTS, 1), lambda b, i: (0, 0)),
        ],
        out_specs=[pl.BlockSpec((1, tm, d), tok), pl.BlockSpec((1, tm, LANES), tok)],
        out_shape=[jax.ShapeDtypeStruct((bsz, s, d), BF16),
                   jax.ShapeDtypeStruct((bsz, s, LANES), F32)],
        compiler_params=_cparams(("arbitrary", "arbitrary")),
        name="router",
    )(x, g.reshape(1, d), sc, sh, router_w.T, router_b.reshape(N_EXPERTS, 1))


def _moe_kernel(x_ref, g2_ref, u_ref, comb_ref, w1_ref, w3_ref, w2_ref, o_ref, acc_scr):
    e = pl.program_id(2)

    @pl.when(e == 0)
    def _():
        acc_scr[...] = jnp.zeros(acc_scr.shape, F32)

    u = u_ref[0]
    h1 = jnp.dot(u, w1_ref[0], preferred_element_type=F32)
    h3 = jnp.dot(u, w3_ref[0], preferred_element_type=F32)
    h = (h1 * jax.nn.sigmoid(h1)) * h3
    y = jnp.dot(h.astype(BF16), w2_ref[0], preferred_element_type=F32)
    comb = comb_ref[0]
    lane = lax.broadcasted_iota(jnp.int32, comb.shape, 1)
    ce = jnp.sum(jnp.where(lane == e, comb, 0.0), axis=-1, keepdims=True)
    acc_scr[...] += ce * y

    @pl.when(e == pl.num_programs(2) - 1)
    def _():
        o_ref[0] = x_ref[0] + g2_ref[0] * acc_scr[...]


def _moe(x, g2, u, comb, w1, w3, w2, tm):
    bsz, s, d = x.shape
    ne, _, dff = w1.shape
    tok = lambda b, i, e: (b, i, 0)
    return pl.pallas_call(
        _moe_kernel,
        grid=(bsz, s // tm, ne),
        in_specs=[
            pl.BlockSpec((1, tm, d), tok),
            pl.BlockSpec((1, 1, d), lambda b, i, e: (b, 0, 0)),
            pl.BlockSpec((1, tm, d), tok),
            pl.BlockSpec((1, tm, LANES), tok),
            pl.BlockSpec((1, d, dff), lambda b, i, e: (e, 0, 0)),
            pl.BlockSpec((1, d, dff), lambda b, i, e: (e, 0, 0)),
            pl.BlockSpec((1, dff, d), lambda b, i, e: (e, 0, 0)),
        ],
        out_specs=pl.BlockSpec((1, tm, d), tok),
        out_shape=jax.ShapeDtypeStruct((bsz, s, d), F32),
        scratch_shapes=[pltpu.VMEM((tm, d), F32)],
        compiler_params=_cparams(("arbitrary", "arbitrary", "arbitrary")),
        name="moe",
    )(x, g2, u, comb, w1, w3, w2)


def _final_norm_kernel(x_ref, g_ref, o_ref):
    x = x_ref[0]
    o_ref[0] = (x * lax.rsqrt(jnp.mean(x * x, axis=-1, keepdims=True) + EPS)) * g_ref[...]


def _final_norm(x, g, tm):
    bsz, s, d = x.shape
    tok = lambda b, i: (b, i, 0)
    return pl.pallas_call(
        _final_norm_kernel,
        grid=(bsz, s // tm),
        in_specs=[pl.BlockSpec((1, tm, d), tok), pl.BlockSpec((1, d), lambda b, i: (0, 0))],
        out_specs=pl.BlockSpec((1, tm, d), tok),
        out_shape=jax.ShapeDtypeStruct((bsz, s, d), F32),
        compiler_params=_cparams(("arbitrary", "arbitrary")),
        name="final_norm",
    )(x, g.reshape(1, d))


def _tile(s, want):
    t = min(want, s)
    assert s % t == 0
    return t


def kernel(x, c, positions, norm1_g, norm2_g, w_mod, b_mod, w_in, lambda_q1, lambda_k1, lambda_q2, lambda_k2, a_norm_g, c_rel_bias, w_branch_a, w_branch_b, w_branch_c, w_out, router_w, router_b, exp_w1, exp_w3, exp_w2, final_g):
    bsz, s, d = x.shape
    depth = w_mod.shape[0]
    tm = _tile(s, 1024)

    mod = _modulation(c, w_mod, b_mod)
    rope_tables = _rope_tables(positions)

    for layer in range(depth):
        lam_init = 0.8 - 0.6 * math.exp(-0.3 * layer)
        sh1, sc1, g1, sh2, sc2, g2 = [m[:, None, :] for m in jnp.split(mod[layer], 6, axis=-1)]
        lam = (jnp.exp(jnp.sum(lambda_q1[layer] * lambda_k1[layer]))
               - jnp.exp(jnp.sum(lambda_q2[layer] * lambda_k2[layer])) + lam_init)

        w_rope, w_gate, w_plain = _build_weights(w_in[layer])
        u = _norm(x, norm1_g[layer], sc1, sh1, tm)
        rope_slab = _project(u, w_rope, rope_tables, "rope", tm)
        gate_slab = _project(u, w_gate, (), "gate", tm)
        plain_slab = _project(u, w_plain, (), "plain", tm)

        ya = _diff_attention(rope_slab, plain_slab, lam, a_norm_g[layer], lam_init, _tile(s, 512))
        yb = _dsa_attention(rope_slab, plain_slab, _tile(s, 256))
        pad = ((0, 0), (BAND_PAD, 0), (0, 0))
        kp = jnp.pad(plain_slab[:, :, P_CK * COL_TILE:(P_CK + 1) * COL_TILE], pad)
        vp = jnp.pad(plain_slab[:, :, P_CV * COL_TILE:(P_CV + 1) * COL_TILE], pad)
        yc = _band_attention(plain_slab, kp, vp, _band_bias(c_rel_bias[layer]))
        x = _merge(x, g1, ya, yb, yc, gate_slab,
                   w_branch_a[layer].astype(BF16), w_branch_b[layer].astype(BF16),
                   w_branch_c[layer].astype(BF16), w_out[layer].astype(BF16), _tile(s, 512))

        u, comb = _router(x, norm2_g[layer], sc2, sh2, router_w, router_b, _tile(s, 512))
        x = _moe(x, g2, u, comb, exp_w1[layer].astype(BF16), exp_w3[layer].astype(BF16),
                 exp_w2[layer].astype(BF16), tm)

    return _final_norm(x, final_g, tm)
```

```python
import functools
import math

import jax
import jax.numpy as jnp
from jax import lax
from jax.experimental import pallas as pl
from jax.experimental.pallas import tpu as pltpu

F32 = jnp.float32
BF16 = jnp.bfloat16

CHUNK = 64
ROPE_THETA = 10000.0
EPS = 1e-6
A_HEADS = 4
HEAD_DIM = 64
B_HEADS = 8
IDX_HEADS = 4
TOPK_MAX = 256
C_HEADS = 8
C_LEFT_CHUNKS = 8
REL_CLIP = 256
N_EXPERTS = 16
N_GROUPS = 4
EXPERTS_PER_GROUP = 4
N_BRANCHES = 3

LANES = 128
NEG = -1e30
LOG2E = math.log2(math.e)
VMEM_LIMIT = 56 * 1024 * 1024

COL_TILE = 512
R_AQ, R_AK, R_BQ, R_IDX = 0, 1, 2, 3
P_AV, P_CQ, P_CK, P_CV, P_MISC = 0, 1, 2, 3, 4


def _cparams(sem):
    return pltpu.CompilerParams(dimension_semantics=sem, vmem_limit_bytes=VMEM_LIMIT)


def _mod_kernel(c_ref, w_ref, b_ref, o_ref):
    c = c_ref[...]
    ca = c * jax.nn.sigmoid(c)
    o_ref[0] = jnp.dot(ca, w_ref[0], preferred_element_type=F32) + b_ref[0]


def _modulation(c, w_mod, b_mod):
    depth, d, n6 = w_mod.shape
    bsz = c.shape[0]
    tn = 1024
    return pl.pallas_call(
        _mod_kernel,
        grid=(depth, n6 // tn),
        in_specs=[
            pl.BlockSpec((bsz, d), lambda l, j: (0, 0)),
            pl.BlockSpec((1, d, tn), lambda l, j: (l, 0, j)),
            pl.BlockSpec((1, 1, tn), lambda l, j: (l, 0, j)),
        ],
        out_specs=pl.BlockSpec((1, bsz, tn), lambda l, j: (l, 0, j)),
        out_shape=jax.ShapeDtypeStruct((depth, bsz, n6), F32),
        compiler_params=_cparams(("arbitrary", "arbitrary")),
        name="modulation",
    )(c, w_mod, b_mod.reshape(depth, 1, n6))


def _modulated_norm(x, g, sc, sh):
    y = x * lax.rsqrt(jnp.mean(x * x, axis=-1, keepdims=True) + EPS)
    return (y * g) * (1.0 + sc) + sh


def _norm_kernel(x_ref, g_ref, sc_ref, sh_ref, u_ref):
    u_ref[0] = _modulated_norm(x_ref[0], g_ref[...], sc_ref[0], sh_ref[0]).astype(BF16)


def _norm(x, g, sc, sh, tm):
    bsz, s, d = x.shape
    tok = lambda b, i: (b, i, 0)
    per_batch = lambda b, i: (b, 0, 0)
    return pl.pallas_call(
        _norm_kernel,
        grid=(bsz, s // tm),
        in_specs=[pl.BlockSpec((1, tm, d), tok), pl.BlockSpec((1, d), lambda b, i: (0, 0)),
                  pl.BlockSpec((1, 1, d), per_batch), pl.BlockSpec((1, 1, d), per_batch)],
        out_specs=pl.BlockSpec((1, tm, d), tok),
        out_shape=jax.ShapeDtypeStruct((bsz, s, d), BF16),
        compiler_params=_cparams(("arbitrary", "arbitrary")),
        name="norm",
    )(x, g.reshape(1, d), sc, sh)


def _proj_kernel(*refs, mode):
    if mode == "rope":
        u_ref, cos_ref, sin_ref, w_ref, o_ref = refs
    else:
        u_ref, w_ref, o_ref = refs
    acc = jnp.dot(u_ref[0], w_ref[...], preferred_element_type=F32)
    if mode == "rope":
        cos, sin = cos_ref[0], sin_ref[0]
        groups = [acc[:, c:c + LANES] for c in range(0, COL_TILE, LANES)]
        acc = jnp.concatenate([g * cos + pltpu.roll(g, LANES // 2, axis=1) * sin for g in groups],
                              axis=1)
    elif mode == "gate":
        acc = 1.0 / (1.0 + jnp.exp(-acc))
    o_ref[0] = acc.astype(BF16)


def _project(u, w, rope_tables, mode, tm):
    bsz, s, d = u.shape
    ncols = w.shape[1]
    tok = lambda b, i, j: (b, i, 0)
    in_specs = [pl.BlockSpec((1, tm, d), tok)]
    in_specs += [pl.BlockSpec((1, tm, LANES), tok) for _ in rope_tables]
    in_specs += [pl.BlockSpec((d, COL_TILE), lambda b, i, j: (0, j))]
    return pl.pallas_call(
        functools.partial(_proj_kernel, mode=mode),
        grid=(bsz, s // tm, ncols // COL_TILE),
        in_specs=in_specs,
        out_specs=pl.BlockSpec((1, tm, COL_TILE), lambda b, i, j: (b, i, j)),
        out_shape=jax.ShapeDtypeStruct((bsz, s, ncols), BF16),
        compiler_params=_cparams(("arbitrary", "arbitrary", "arbitrary")),
        name="proj_" + mode,
    )(u, *rope_tables, w)


def _pair_layout(w):
    half = HEAD_DIM // 2
    col = jnp.arange(w.shape[1])
    base, r = (col // LANES) * LANES, col % LANES
    src = base + ((r // half) % 2) * HEAD_DIM + (r // HEAD_DIM) * half + r % half
    return w[:, src]


def _build_weights(w_in):
    sizes = (512, 512, 512, 512, 64, 64, 256, 64, 4, 512, 512, 512, 3072)
    parts, start = [], 0
    for sz in sizes:
        parts.append(w_in[:, start:start + sz])
        start += sz
    aq, ak, av, bq, bk, bv, iq, ik, iw, cq, ck, cv, gates = parts
    d = w_in.shape[0]
    qscale = HEAD_DIM ** -0.5 * LOG2E
    iw_scale = IDX_HEADS ** -0.5 * HEAD_DIM ** -0.5
    zeros = lambda n: jnp.zeros((d, n), w_in.dtype)
    w_rope = _pair_layout(jnp.concatenate([aq * qscale, ak, bq * qscale, iq, bk, bk, ik, ik], axis=1))
    w_plain = jnp.concatenate([av, cq * qscale, ck, cv,
                               bv, bv, iw * iw_scale, zeros(LANES - IDX_HEADS), zeros(256)], axis=1)
    return w_rope.astype(BF16), gates.astype(BF16), w_plain.astype(BF16)


def _rope_tables(positions):
    half = HEAD_DIM // 2
    inv = ROPE_THETA ** (-jnp.arange(half, dtype=F32) / half)
    ang = positions.astype(F32)[..., None] * inv
    cos, sin = jnp.cos(ang), jnp.sin(ang)
    cos_t = jnp.tile(cos, (1, 1, LANES // half))
    sin_t = jnp.concatenate([-sin, -sin, sin, sin], axis=-1)
    return cos_t, sin_t


def _split_pair(pair, interleaved):
    lane = lax.broadcasted_iota(jnp.int32, pair.shape, 1)
    first = ((lane // (HEAD_DIM // 2)) % 2 == 0) if interleaved else (lane < HEAD_DIM)
    zero = jnp.zeros_like(pair)
    return jnp.where(first, pair, zero), jnp.where(first, zero, pair)


def _qk(q, k):
    return lax.dot_general(q, k, (((1,), (1,)), ((), ())), preferred_element_type=F32)


def _fold_rows(x, op):
    parts = x.reshape(x.shape[0] // 8, 8, x.shape[1])
    level = [parts[g] for g in range(parts.shape[0])]
    while len(level) > 1:
        nxt = [op(level[a], level[a + 1]) for a in range(0, len(level) - 1, 2)]
        level = nxt + ([level[-1]] if len(level) % 2 else [])
    return level[0]


def _flash_t_update(s, vt_blocks, m_scr, acc_scr):
    m_old = m_scr[...]
    m_new = jnp.maximum(m_old, jnp.max(_fold_rows(s, jnp.maximum), axis=0, keepdims=True))
    alpha = jnp.exp2(m_old - m_new)
    pb = jnp.exp2(s - m_new).astype(BF16)
    m_scr[...] = m_new
    pv = functools.reduce(lambda a, b: a + b, [
        jnp.dot(vt, pb[k0:k1], preferred_element_type=F32) for (k0, k1), vt in vt_blocks])
    acc_scr[...] = alpha * acc_scr[...] + pv


ONES_ROWS = 16


def _with_ones_rows(vt):
    shape = vt.shape[:-2]
    return jnp.concatenate([vt, jnp.ones(shape + (1, vt.shape[-1]), vt.dtype),
                            jnp.zeros(shape + (ONES_ROWS - 1, vt.shape[-1]), vt.dtype)], axis=-2)


def _flash_update(s, v, m_scr, l_scr, acc_scr):
    groups = [s[:, c:c + LANES] for c in range(0, s.shape[1], LANES)]
    m_old = m_scr[...]
    lane_max = functools.reduce(jnp.maximum, groups)
    m_new = jnp.maximum(m_old, jnp.max(lane_max, axis=-1, keepdims=True))
    alpha = jnp.exp2(m_old - m_new)
    p_groups = [jnp.exp2(g - m_new) for g in groups]
    l_scr[...] = alpha * l_scr[...] + functools.reduce(lambda a, b: a + b, p_groups)
    m_scr[...] = m_new
    pb = jnp.concatenate([g.astype(BF16) for g in p_groups], axis=1)
    acc_scr[...] = alpha * acc_scr[...] + jnp.dot(pb, v, preferred_element_type=F32)


def _diff_attn_kernel(lam_ref, q_ref, k_ref, v_ref, ng_ref, o_ref, m_scr, l_scr, acc_scr,
                      *, tq, lam_init):
    i = pl.program_id(2)
    qs = jnp.concatenate(_split_pair(q_ref[0], True), axis=0)
    m_scr[...] = jnp.full(m_scr.shape, NEG, F32)
    l_scr[...] = jnp.zeros(l_scr.shape, F32)
    acc_scr[...] = jnp.zeros(acc_scr.shape, F32)

    def kv_block(j):
        start = pl.multiple_of(j * tq, tq)
        return k_ref[0, pl.ds(start, tq), :], v_ref[0, pl.ds(start, tq), :]

    def body(j, carry):
        k, v = kv_block(j)
        _flash_update(_qk(qs, k), v, m_scr, l_scr, acc_scr)
        return carry

    lax.fori_loop(0, i, body, 0)
    k, v = kv_block(i)
    s = _qk(qs, k)
    row = lax.broadcasted_iota(jnp.int32, s.shape, 0) % tq
    col = lax.broadcasted_iota(jnp.int32, s.shape, 1)
    s = jnp.where((col // CHUNK) <= (row // CHUNK), s, NEG)
    _flash_update(s, v, m_scr, l_scr, acc_scr)

    o = acc_scr[...] / jnp.sum(l_scr[...], axis=-1, keepdims=True)
    o = o[:tq] - lam_ref[0] * o[tq:]
    o = o * lax.rsqrt(jnp.mean(o * o, axis=-1, keepdims=True) + EPS)
    o_ref[0] = ((o * ng_ref[...]) * (1.0 - lam_init)).astype(BF16)


def _diff_attention(rope_slab, plain_slab, lam, norm_g, lam_init, tq):
    bsz, s, _ = rope_slab.shape
    kern = functools.partial(_diff_attn_kernel, tq=tq, lam_init=lam_init)
    cb = COL_TILE // LANES
    return pl.pallas_call(
        kern,
        grid=(bsz, A_HEADS, s // tq),
        in_specs=[
            pl.BlockSpec(memory_space=pltpu.SMEM),
            pl.BlockSpec((1, tq, LANES), lambda b, h, i: (b, i, R_AQ * cb + h)),
            pl.BlockSpec((1, s, LANES), lambda b, h, i: (b, 0, R_AK * cb + h)),
            pl.BlockSpec((1, s, LANES), lambda b, h, i: (b, 0, P_AV * cb + h)),
            pl.BlockSpec((1, LANES), lambda b, h, i: (0, 0)),
        ],
        out_specs=pl.BlockSpec((1, tq, LANES), lambda b, h, i: (b, i, h)),
        out_shape=jax.ShapeDtypeStruct((bsz, s, A_HEADS * LANES), BF16),
        scratch_shapes=[pltpu.VMEM((2 * tq, LANES), F32), pltpu.VMEM((2 * tq, LANES), F32),
                        pltpu.VMEM((2 * tq, LANES), F32)],
        compiler_params=_cparams(("arbitrary", "arbitrary", "arbitrary")),
        name="diff_attn",
    )(lam.reshape(1), rope_slab, rope_slab, plain_slab, norm_g.reshape(1, LANES))


KEY_NEG_INF = -2139095040
KEY_POS_INF = 2139095040
SEARCH_MAX_STEPS = 80
NO_TIE_LIMIT = 1e9
VT_ROWS = HEAD_DIM + ONES_ROWS


def _key_to_float(key):
    bits = jnp.where(key >= 0, key, (key - 1) ^ jnp.int32(0x7FFFFFFF))
    return pltpu.bitcast(bits, F32)


def _float_to_key(t):
    bits = pltpu.bitcast(t, jnp.int32)
    return jnp.where(bits >= 0, bits, (bits ^ jnp.int32(0x7FFFFFFF)) + 1)


def _dsa_kernel(q_ref, iq_ref, iw_ref, k_ref, ik_ref, vt_ref, o_ref,
                sc_scr, qs_scr, m_scr, acc_scr, *, tq, topk):
    i = pl.program_id(1)
    nblk = i + 1
    npair = (nblk + 1) // 2
    nq = B_HEADS * tq

    iq = iq_ref[0]
    parts = []
    for p in range(IDX_HEADS // 2):
        parts += list(_split_pair(iq[:, p * LANES:(p + 1) * LANES], True))
    iqs = jnp.concatenate(parts, axis=0)
    iw_t = iw_ref[0].astype(F32).T

    def index_block(j):
        start = pl.multiple_of(j * tq, tq)
        hs = jnp.maximum(_qk(ik_ref[0, pl.ds(start, tq), :], iqs), 0.0)
        score = iw_t[0:1] * hs[:, 0:tq]
        for h in range(1, IDX_HEADS):
            score = score + iw_t[h:h + 1] * hs[:, h * tq:(h + 1) * tq]
        return score

    def score_slot(j):
        return j // 2, pl.ds(pl.multiple_of((j % 2) * tq, tq), tq)

    def p1_body(j, carry):
        pair, rows = score_slot(j)
        sc_scr[pair, rows, :] = index_block(j)
        return carry

    lax.fori_loop(0, i, p1_body, 0)
    score = index_block(i)
    key_i = lax.broadcasted_iota(jnp.int32, score.shape, 0)
    qry_i = lax.broadcasted_iota(jnp.int32, score.shape, 1)
    pair, rows = score_slot(i)
    sc_scr[pair, rows, :] = jnp.where((key_i // CHUNK) <= (qry_i // CHUNK), score, -jnp.inf)

    @pl.when(nblk % 2 == 1)
    def _():
        sc_scr[npair - 1, tq:2 * tq, :] = jnp.full((tq, tq), -jnp.inf, F32)

    kf = float(topk)

    def scan(hit_fn, ext_fn, ext_op, ext_init):
        def body(jj, carry):
            cnt, ext = carry
            sblk = sc_scr[jj]
            cnt = cnt + _fold_rows(jnp.where(hit_fn(sblk), 1.0, 0.0), lambda a, b: a + b)
            if ext_fn is not None:
                ext = ext_op(ext, _fold_rows(ext_fn(sblk), ext_op))
            return cnt, ext
        cnt, ext = lax.fori_loop(0, npair, body, (jnp.zeros((8, tq), F32),
                                                  jnp.full((8, tq), ext_init, F32)))
        return jnp.sum(cnt, axis=0, keepdims=True), ext

    def count_ge(t):
        return scan(lambda sblk: sblk >= t, None, None, 0.0)[0]

    c_ge0, ext = scan(lambda sblk: sblk >= 0.0, lambda sblk: sblk, jnp.maximum, -jnp.inf)
    col_max = jnp.max(ext, axis=0, keepdims=True)
    c_gt0, ext = scan(lambda sblk: sblk > 0.0,
                      lambda sblk: jnp.where(sblk > -jnp.inf, sblk, jnp.inf), jnp.minimum, jnp.inf)
    col_min = jnp.min(ext, axis=0, keepdims=True)
    qpos = i * tq + lax.broadcasted_iota(jnp.int32, (1, tq), 1)
    n_valid = (((qpos // CHUNK) + 1) * CHUNK).astype(F32)
    open_q = n_valid < kf
    above = c_gt0 >= kf
    below = c_ge0 < kf
    ikey = lambda v: jnp.full((1, tq), v, jnp.int32)
    lo_k = jnp.where(below, _float_to_key(col_min), ikey(0))
    hi_k = jnp.where(above, _float_to_key(col_max) + 1, jnp.where(below, ikey(0), ikey(1)))
    c_lo = jnp.where(below, n_valid, c_ge0)
    c_hi = jnp.where(above, 0.0, jnp.where(below, c_ge0, c_gt0))
    done = jnp.logical_or(open_q, jnp.logical_not(jnp.logical_or(above, below)))
    state = (jnp.int32(0), lo_k, hi_k, c_lo, c_hi, done.astype(jnp.int32), ikey(0))

    def search_cond(st):
        return jnp.logical_and(st[0] < SEARCH_MAX_STEPS, jnp.min(st[5]) == 0)

    def search_step(st):
        step, lo_k, hi_k, c_lo, c_hi, done, force_bisect = st
        finite = jnp.logical_and(lo_k > KEY_NEG_INF, hi_k < KEY_POS_INF)
        t_lo, t_hi = _key_to_float(lo_k), _key_to_float(hi_k)
        log_lo = jnp.log(c_lo)
        frac = (log_lo - math.log(kf - 0.5)) / (log_lo - jnp.log(jnp.maximum(c_hi, 0.5)))
        k_interp = _float_to_key(t_lo + (t_hi - t_lo) * frac)
        k_mid = (lo_k & hi_k) + ((lo_k ^ hi_k) >> 1)
        interp = jnp.logical_and(finite, force_bisect == 0)
        k = jnp.where(interp, k_interp, k_mid)
        k = jnp.minimum(jnp.maximum(k, lo_k + 1), hi_k - 1)
        c = count_ge(_key_to_float(k))
        ok = c >= kf
        live = done == 0
        new_lo = jnp.where(jnp.logical_and(live, ok), k, lo_k)
        new_hi = jnp.where(jnp.logical_and(live, jnp.logical_not(ok)), k, hi_k)
        new_c_lo = jnp.where(jnp.logical_and(live, ok), c, c_lo)
        new_c_hi = jnp.where(jnp.logical_and(live, jnp.logical_not(ok)), c, c_hi)
        width = lambda a, b: b.astype(F32) - a.astype(F32)
        slow = width(new_lo, new_hi) > 0.5 * width(lo_k, hi_k)
        new_force = jnp.logical_and(interp, slow).astype(jnp.int32)
        finished = jnp.logical_or(new_c_lo == kf, new_hi - new_lo == 1)
        new_done = jnp.maximum(done, finished.astype(jnp.int32))
        return step + 1, new_lo, new_hi, new_c_lo, new_c_hi, new_done, new_force

    _, lo_k, hi_k, c_lo, c_hi, _, _ = lax.while_loop(search_cond, search_step, state)
    tau = jnp.where(open_q, -jnp.inf, _key_to_float(lo_k))
    need = jnp.where(open_q, 0.0, jnp.where(c_lo == kf, NO_TIE_LIMIT, kf - c_hi))

    q = q_ref[0]
    for p in range(B_HEADS // 2):
        even, odd = _split_pair(q[:, p * LANES:(p + 1) * LANES], True)
        qs_scr[p * tq:(p + 1) * tq, :] = even
        qs_scr[(B_HEADS // 2 + p) * tq:(B_HEADS // 2 + p + 1) * tq, :] = odd
    m_scr[...] = jnp.full(m_scr.shape, NEG, F32)
    acc_scr[...] = jnp.zeros(acc_scr.shape, F32)
    key_i = lax.broadcasted_iota(jnp.int32, (tq, tq), 0)
    lower = (lax.broadcasted_iota(jnp.int32, (tq, tq), 1) <= key_i).astype(BF16)

    def p3_body(jj, tie_count):
        start = pl.multiple_of(jj * 2 * tq, 2 * tq)
        sblk = sc_scr[jj]
        tie = sblk == tau
        tie01 = jnp.where(tie, 1.0, 0.0).astype(BF16)
        rank_a = tie_count + jnp.dot(lower, tie01[0:tq], preferred_element_type=F32)
        rank_b = rank_a[tq - 1:tq, :] + jnp.dot(lower, tie01[tq:2 * tq], preferred_element_type=F32)
        rank = jnp.concatenate([rank_a, rank_b], axis=0)
        keep_tie = jnp.where(rank <= need, 0.0, NEG)
        bias = jnp.where(sblk > tau, 0.0, jnp.where(tie, keep_tie, NEG))
        s = _qk(k_ref[0, pl.ds(start, 2 * tq), :], qs_scr[...])
        s = jnp.concatenate([s[:, h * tq:(h + 1) * tq] + bias for h in range(B_HEADS)], axis=1)
        _flash_t_update(s, [((0, tq), vt_ref[0, 2 * jj]), ((tq, 2 * tq), vt_ref[0, 2 * jj + 1])],
                        m_scr, acc_scr)
        return rank[2 * tq - 1:2 * tq, :]

    lax.fori_loop(0, npair, p3_body, jnp.zeros((1, tq), F32))
    acc = acc_scr[...]
    o = acc[0:HEAD_DIM] / acc[HEAD_DIM:HEAD_DIM + 1]
    half_cols = (B_HEADS // 2) * tq
    for p in range(B_HEADS // 2):
        pair = jnp.concatenate([o[:, p * tq:(p + 1) * tq],
                                o[:, half_cols + p * tq:half_cols + (p + 1) * tq]], axis=0)
        o_ref[0, :, p * LANES:(p + 1) * LANES] = pair.T.astype(BF16)


def _dsa_attention(rope_slab, plain_slab, tq):
    bsz, s, _ = rope_slab.shape
    topk = min(TOPK_MAX, s // 4)
    nblk = s // tq
    assert tq >= topk and s % tq == 0 and nblk % 2 == 0
    kern = functools.partial(_dsa_kernel, tq=tq, topk=topk)
    idx0 = R_IDX * COL_TILE
    misc0 = P_MISC * COL_TILE
    width = B_HEADS * HEAD_DIM
    vt = _with_ones_rows(
        plain_slab[:, :, misc0:misc0 + HEAD_DIM].reshape(bsz, nblk, tq, HEAD_DIM).transpose(0, 1, 3, 2))
    return pl.pallas_call(
        kern,
        grid=(bsz, s // tq),
        in_specs=[
            pl.BlockSpec((1, tq, COL_TILE), lambda b, i: (b, i, R_BQ)),
            pl.BlockSpec((1, tq, 256), lambda b, i: (b, i, idx0 // 256)),
            pl.BlockSpec((1, tq, LANES), lambda b, i: (b, i, (misc0 + LANES) // LANES)),
            pl.BlockSpec((1, s, LANES), lambda b, i: (b, 0, (idx0 + 256) // LANES)),
            pl.BlockSpec((1, s, LANES), lambda b, i: (b, 0, (idx0 + 384) // LANES)),
            pl.BlockSpec((1, nblk, VT_ROWS, tq), lambda b, i: (b, 0, 0, 0)),
        ],
        out_specs=pl.BlockSpec((1, tq, width), lambda b, i: (b, i, 0)),
        out_shape=jax.ShapeDtypeStruct((bsz, s, width), BF16),
        scratch_shapes=[
            pltpu.VMEM((nblk // 2, 2 * tq, tq), F32),
            pltpu.VMEM((B_HEADS * tq, LANES), BF16),
            pltpu.VMEM((1, B_HEADS * tq), F32),
            pltpu.VMEM((VT_ROWS, B_HEADS * tq), F32),
        ],
        compiler_params=_cparams(("arbitrary", "arbitrary")),
        name="dsa_attn",
    )(rope_slab, rope_slab, plain_slab, rope_slab, rope_slab, vt)


BAND_TQ = 2 * CHUNK
BAND_KEYS = (C_LEFT_CHUNKS + BAND_TQ // CHUNK) * CHUNK
BAND_PAD = C_LEFT_CHUNKS * CHUNK


def _band_kernel(q_ref, k_ref, v_ref, bias_ref, o_ref):
    i = pl.program_id(1)
    tq = BAND_TQ
    start = pl.multiple_of(i * tq, tq)
    col = lax.broadcasted_iota(jnp.int32, (2 * tq, BAND_KEYS), 1)
    key_ok = col + start >= BAND_PAD
    for p in range(C_HEADS // 2):
        lanes = slice(p * LANES, (p + 1) * LANES)
        k = k_ref[0, pl.ds(start, BAND_KEYS), lanes]
        v = v_ref[0, pl.ds(start, BAND_KEYS), lanes]
        qs = jnp.concatenate(_split_pair(q_ref[0, :, lanes], False), axis=0)
        s = _qk(qs, k) + bias_ref[2 * p:2 * p + 2].reshape(2 * tq, BAND_KEYS)
        s = jnp.where(key_ok, s, NEG)
        m = jnp.max(s, axis=-1, keepdims=True)
        e = jnp.exp2(s - m)
        l = jnp.sum(e, axis=-1, keepdims=True)
        eb = e.astype(BF16)
        v_even, v_odd = _split_pair(v, False)
        o = (jnp.dot(eb[:tq], v_even, preferred_element_type=F32) / l[:tq]
             + jnp.dot(eb[tq:], v_odd, preferred_element_type=F32) / l[tq:])
        o_ref[0, :, lanes] = o.astype(BF16)


def _band_bias(rel_bias):
    r = jnp.arange(BAND_TQ)[:, None]
    cidx = jnp.arange(BAND_KEYS)[None, :]
    n_diag = BAND_TQ + BAND_KEYS - 1
    rel = BAND_PAD - (BAND_KEYS - 1) + jnp.arange(n_diag)
    g = rel_bias.astype(F32)[:, jnp.clip(rel, -REL_CLIP, REL_CLIP) + REL_CLIP] * LOG2E
    skew = jnp.tile(g, (1, BAND_TQ + 2))[:, :BAND_TQ * (n_diag + 1)]
    bias = skew.reshape(-1, BAND_TQ, n_diag + 1)[:, :, :BAND_KEYS][:, :, ::-1]
    dchunk = (r // CHUNK + C_LEFT_CHUNKS) - cidx // CHUNK
    in_band = jnp.logical_and(dchunk >= 0, dchunk <= C_LEFT_CHUNKS)
    return jnp.where(in_band[None], bias, NEG)


def _band_attention(plain_slab, kp, vp, bias):
    bsz, s, _ = plain_slab.shape
    tq = BAND_TQ
    sp = kp.shape[1]
    width = C_HEADS * HEAD_DIM
    return pl.pallas_call(
        _band_kernel,
        grid=(bsz, s // tq),
        in_specs=[
            pl.BlockSpec((1, tq, width), lambda b, i: (b, i, P_CQ)),
            pl.BlockSpec((1, sp, width), lambda b, i: (b, 0, 0)),
            pl.BlockSpec((1, sp, width), lambda b, i: (b, 0, 0)),
            pl.BlockSpec((C_HEADS, tq, BAND_KEYS), lambda b, i: (0, 0, 0)),
        ],
        out_specs=pl.BlockSpec((1, tq, width), lambda b, i: (b, i, 0)),
        out_shape=jax.ShapeDtypeStruct((bsz, s, width), BF16),
        compiler_params=_cparams(("arbitrary", "arbitrary")),
        name="band_attn",
    )(plain_slab, kp, vp, bias)


def _merge_kernel(x_ref, g1_ref, ya_ref, yb_ref, yc_ref, ga_ref, gb_ref, gc_ref,
                  wa_ref, wb_ref, wc_ref, wo_ref, o_ref):
    def branch(y_ref, w_ref, gate_ref):
        return gate_ref[0].astype(F32) * jnp.dot(y_ref[0], w_ref[...], preferred_element_type=F32)

    merged = (branch(ya_ref, wa_ref, ga_ref) + branch(yb_ref, wb_ref, gb_ref)
              + branch(yc_ref, wc_ref, gc_ref))
    mixed = jnp.dot(merged.astype(BF16), wo_ref[...], preferred_element_type=F32)
    o_ref[0] = x_ref[0] + g1_ref[0] * mixed


def _merge(x, g1, ya, yb, yc, gate_slab, wa, wb, wc, wo, tm):
    bsz, s, d = x.shape
    tok = lambda b, i: (b, i, 0)
    full = lambda b, i: (0, 0)
    return pl.pallas_call(
        _merge_kernel,
        grid=(bsz, s // tm),
        in_specs=[
            pl.BlockSpec((1, tm, d), tok),
            pl.BlockSpec((1, 1, d), lambda b, i: (b, 0, 0)),
            pl.BlockSpec((1, tm, ya.shape[2]), tok),
            pl.BlockSpec((1, tm, yb.shape[2]), tok),
            pl.BlockSpec((1, tm, yc.shape[2]), tok),
            pl.BlockSpec((1, tm, d), lambda b, i: (b, i, 0)),
            pl.BlockSpec((1, tm, d), lambda b, i: (b, i, 1)),
            pl.BlockSpec((1, tm, d), lambda b, i: (b, i, 2)),
            pl.BlockSpec(wa.shape, full),
            pl.BlockSpec(wb.shape, full),
            pl.BlockSpec(wc.shape, full),
            pl.BlockSpec(wo.shape, full),
        ],
        out_specs=pl.BlockSpec((1, tm, d), tok),
        out_shape=jax.ShapeDtypeStruct((bsz, s, d), F32),
        compiler_params=_cparams(("arbitrary", "arbitrary")),
        name="merge",
    )(x, g1, ya, yb, yc, gate_slab, gate_slab, gate_slab, wa, wb, wc, wo)


def _router_kernel(x_ref, g_ref, sc_ref, sh_ref, rw_ref, rb_ref, u_ref, comb_ref):
    u = _modulated_norm(x_ref[0], g_ref[...], sc_ref[0], sh_ref[0])
    u_ref[0] = u.astype(BF16)
    logits = lax.dot_general(rw_ref[...], u, (((1,), (1,)), ((), ())),
                             preferred_element_type=F32, precision=lax.Precision.HIGHEST)
    aff = jax.nn.sigmoid(logits)
    sel = aff + rb_ref[...]
    rows = [sel[e:e + 1] for e in range(N_EXPERTS)]
    gscore = []
    for g in range(N_GROUPS):
        r = rows[g * EXPERTS_PER_GROUP:(g + 1) * EXPERTS_PER_GROUP]
        best = None
        for a in range(EXPERTS_PER_GROUP):
            for b in range(a + 1, EXPERTS_PER_GROUP):
                pair = r[a] + r[b]
                best = pair if best is None else jnp.maximum(best, pair)
        gscore.append(best)
    gmax = functools.reduce(jnp.maximum, gscore)
    taken = jnp.zeros_like(gmax) > 1.0
    in_best = []
    for g in range(N_GROUPS):
        is_g = jnp.logical_and(gscore[g] == gmax, jnp.logical_not(taken))
        in_best.append(is_g)
        taken = jnp.logical_or(taken, is_g)
    keep = []
    for e in range(N_EXPERTS):
        g = e // EXPERTS_PER_GROUP
        rank = jnp.zeros_like(gmax)
        for o in range(g * EXPERTS_PER_GROUP, (g + 1) * EXPERTS_PER_GROUP):
            if o == e:
                continue
            ahead = rows[o] > rows[e] if o > e else rows[o] >= rows[e]
            rank = rank + jnp.where(ahead, 1.0, 0.0)
        keep.append(jnp.logical_and(in_best[g], rank < 2.0))
    w = [jnp.where(keep[e], aff[e:e + 1], 0.0) for e in range(N_EXPERTS)]
    total = functools.reduce(lambda a, b: a + b, w)
    comb = jnp.concatenate([we / total for we in w]
                           + [jnp.zeros((LANES - N_EXPERTS, total.shape[1]), F32)], axis=0)
    comb_ref[0] = comb.T


def _router(x, g, sc, sh, router_w, router_b, tm):
    bsz, s, d = x.shape
    tok = lambda b, i: (b, i, 0)
    return pl.pallas_call(
        _router_kernel,
        grid=(bsz, s // tm),
        in_specs=[
            pl.BlockSpec((1, tm, d), tok),
            pl.BlockSpec((1, d), lambda b, i: (0, 0)),
            pl.BlockSpec((1, 1, d), lambda b, i: (b, 0, 0)),
            pl.BlockSpec((1, 1, d), lambda b, i: (b, 0, 0)),
            pl.BlockSpec((N_EXPERTS, d), lambda b, i: (0, 0)),
            pl.BlockSpec((N_EXPERTS, 1), lambda b, i: (0, 0)),
        ],
        out_specs=[pl.BlockSpec((1, tm, d), tok), pl.BlockSpec((1, tm, LANES), tok)],
        out_shape=[jax.ShapeDtypeStruct((bsz, s, d), BF16),
                   jax.ShapeDtypeStruct((bsz, s, LANES), F32)],
        compiler_params=_cparams(("arbitrary", "arbitrary")),
        name="router",
    )(x, g.reshape(1, d), sc, sh, router_w.T, router_b.reshape(N_EXPERTS, 1))


MOE_CAP = 128


def _expert_ffn(rows, w1_ref, w3_ref, w2_ref):
    h1 = jnp.dot(rows, w1_ref[0], preferred_element_type=F32)
    h3 = jnp.dot(rows, w3_ref[0], preferred_element_type=F32)
    h = (h1 * jax.nn.sigmoid(h1)) * h3
    return jnp.dot(h.astype(BF16), w2_ref[0], preferred_element_type=F32)


def _moe_kernel(x_ref, g2_ref, u_ref, comb_ref, w1_ref, w3_ref, w2_ref, o_ref,
                acc_scr, pos_scr, pos_t_scr, comb_t_scr, scatter_scr, y_scr):
    e = pl.program_id(2)
    tm = u_ref.shape[1]

    @pl.when(e == 0)
    def _():
        acc_scr[...] = jnp.zeros(acc_scr.shape, F32)
        comb0 = comb_ref[0]
        member = jnp.where(comb0 > 0.0, 1.0, 0.0).astype(BF16)
        before = (lax.broadcasted_iota(jnp.int32, (tm, tm), 1)
                  < lax.broadcasted_iota(jnp.int32, (tm, tm), 0)).astype(BF16)
        pos = jnp.dot(before, member, preferred_element_type=F32)
        pos_scr[...] = pos
        pos_t_scr[...] = pos.T
        comb_t_scr[...] = comb0.T

    comb = comb_ref[0]
    lane = lax.broadcasted_iota(jnp.int32, comb.shape, 1)
    ce = jnp.sum(jnp.where(lane == e, comb, 0.0), axis=-1, keepdims=True)
    pe = jnp.sum(jnp.where(lane == e, pos_scr[...], 0.0), axis=-1, keepdims=True)
    routed = jnp.sum(jnp.where(ce > 0.0, 1.0, 0.0)) <= float(MOE_CAP)
    slots = pl.ds(pl.multiple_of(e * MOE_CAP, MOE_CAP), MOE_CAP)

    @pl.when(routed)
    def _():
        slot_l = lax.broadcasted_iota(jnp.int32, (1, MOE_CAP), 1).astype(F32)
        scatter_scr[e] = jnp.where(jnp.logical_and(ce > 0.0, pe == slot_l), 1.0, 0.0).astype(BF16)
        ce_t = comb_t_scr[pl.ds(e, 1), :]
        pe_t = pos_t_scr[pl.ds(e, 1), :]
        slot_s = lax.broadcasted_iota(jnp.int32, (MOE_CAP, 1), 0).astype(F32)
        hit = jnp.logical_and(ce_t > 0.0, pe_t == slot_s)
        w_slot = jnp.sum(jnp.where(hit, ce_t, 0.0), axis=-1, keepdims=True)
        rows = jnp.dot(jnp.where(hit, 1.0, 0.0).astype(BF16), u_ref[0],
                       preferred_element_type=F32).astype(BF16)
        y_scr[slots, :] = (w_slot * _expert_ffn(rows, w1_ref, w3_ref, w2_ref)).astype(BF16)

    @pl.when(jnp.logical_not(routed))
    def _():
        scatter_scr[e] = jnp.zeros(scatter_scr.shape[1:], BF16)
        y_scr[slots, :] = jnp.zeros((MOE_CAP, y_scr.shape[1]), BF16)
        acc_scr[...] += ce * _expert_ffn(u_ref[0], w1_ref, w3_ref, w2_ref)

    @pl.when(e == pl.num_programs(2) - 1)
    def _():
        scatter = jnp.concatenate([scatter_scr[k] for k in range(N_EXPERTS)], axis=1)
        y = acc_scr[...] + jnp.dot(scatter, y_scr[...], preferred_element_type=F32)
        o_ref[0] = x_ref[0] + g2_ref[0] * y


def _moe(x, g2, u, comb, w1, w3, w2, tm):
    bsz, s, d = x.shape
    ne, _, dff = w1.shape
    assert ne == N_EXPERTS
    tok = lambda b, i, e: (b, i, 0)
    return pl.pallas_call(
        _moe_kernel,
        grid=(bsz, s // tm, ne),
        in_specs=[
            pl.BlockSpec((1, tm, d), tok),
            pl.BlockSpec((1, 1, d), lambda b, i, e: (b, 0, 0)),
            pl.BlockSpec((1, tm, d), tok),
            pl.BlockSpec((1, tm, LANES), tok),
            pl.BlockSpec((1, d, dff), lambda b, i, e: (e, 0, 0)),
            pl.BlockSpec((1, d, dff), lambda b, i, e: (e, 0, 0)),
            pl.BlockSpec((1, dff, d), lambda b, i, e: (e, 0, 0)),
        ],
        out_specs=pl.BlockSpec((1, tm, d), tok),
        out_shape=jax.ShapeDtypeStruct((bsz, s, d), F32),
        scratch_shapes=[
            pltpu.VMEM((tm, d), F32),
            pltpu.VMEM((tm, LANES), F32),
            pltpu.VMEM((LANES, tm), F32),
            pltpu.VMEM((LANES, tm), F32),
            pltpu.VMEM((ne, tm, MOE_CAP), BF16),
            pltpu.VMEM((ne * MOE_CAP, d), BF16),
        ],
        compiler_params=_cparams(("arbitrary", "arbitrary", "arbitrary")),
        name="moe",
    )(x, g2, u, comb, w1, w3, w2)


def _final_norm_kernel(x_ref, g_ref, o_ref):
    x = x_ref[0]
    o_ref[0] = (x * lax.rsqrt(jnp.mean(x * x, axis=-1, keepdims=True) + EPS)) * g_ref[...]


def _final_norm(x, g, tm):
    bsz, s, d = x.shape
    tok = lambda b, i: (b, i, 0)
    return pl.pallas_call(
        _final_norm_kernel,
        grid=(bsz, s // tm),
        in_specs=[pl.BlockSpec((1, tm, d), tok), pl.BlockSpec((1, d), lambda b, i: (0, 0))],
        out_specs=pl.BlockSpec((1, tm, d), tok),
        out_shape=jax.ShapeDtypeStruct((bsz, s, d), F32),
        compiler_params=_cparams(("arbitrary", "arbitrary")),
        name="final_norm",
    )(x, g.reshape(1, d))


def _tile(s, want):
    t = min(want, s)
    assert s % t == 0
    return t


def kernel(x, c, positions, norm1_g, norm2_g, w_mod, b_mod, w_in, lambda_q1, lambda_k1, lambda_q2, lambda_k2, a_norm_g, c_rel_bias, w_branch_a, w_branch_b, w_branch_c, w_out, router_w, router_b, exp_w1, exp_w3, exp_w2, final_g):
    bsz, s, d = x.shape
    depth = w_mod.shape[0]
    tm = _tile(s, 1024)

    mod = _modulation(c, w_mod, b_mod)
    rope_tables = _rope_tables(positions)

    for layer in range(depth):
        lam_init = 0.8 - 0.6 * math.exp(-0.3 * layer)
        sh1, sc1, g1, sh2, sc2, g2 = [m[:, None, :] for m in jnp.split(mod[layer], 6, axis=-1)]
        lam = (jnp.exp(jnp.sum(lambda_q1[layer] * lambda_k1[layer]))
               - jnp.exp(jnp.sum(lambda_q2[layer] * lambda_k2[layer])) + lam_init)

        w_rope, w_gate, w_plain = _build_weights(w_in[layer])
        u = _norm(x, norm1_g[layer], sc1, sh1, tm)
        rope_slab = _project(u, w_rope, rope_tables, "rope", tm)
        gate_slab = _project(u, w_gate, (), "gate", tm)
        plain_slab = _project(u, w_plain, (), "plain", tm)

        ya = _diff_attention(rope_slab, plain_slab, lam, a_norm_g[layer], lam_init, _tile(s, 512))
        yb = _dsa_attention(rope_slab, plain_slab, _tile(s, 256))
        pad = ((0, 0), (BAND_PAD, 0), (0, 0))
        kp = jnp.pad(plain_slab[:, :, P_CK * COL_TILE:(P_CK + 1) * COL_TILE], pad)
        vp = jnp.pad(plain_slab[:, :, P_CV * COL_TILE:(P_CV + 1) * COL_TILE], pad)
        yc = _band_attention(plain_slab, kp, vp, _band_bias(c_rel_bias[layer]))
        x = _merge(x, g1, ya, yb, yc, gate_slab,
                   w_branch_a[layer].astype(BF16), w_branch_b[layer].astype(BF16),
                   w_branch_c[layer].astype(BF16), w_out[layer].astype(BF16), _tile(s, 512))

        u, comb = _router(x, norm2_g[layer], sc2, sh2, router_w, router_b, _tile(s, 512))
        x = _moe(x, g2, u, comb, exp_w1[layer].astype(BF16), exp_w3[layer].astype(BF16),
                 exp_w2[layer].astype(BF16), _tile(s, 512))

    return _final_norm(x, final_g, tm)
```

```python
import functools
import math

import jax
import jax.numpy as jnp
from jax import lax
from jax.experimental import pallas as pl
from jax.experimental.pallas import tpu as pltpu

F32 = jnp.float32
BF16 = jnp.bfloat16

CHUNK = 64
ROPE_THETA = 10000.0
EPS = 1e-6
A_HEADS = 4
HEAD_DIM = 64
B_HEADS = 8
IDX_HEADS = 4
TOPK_MAX = 256
C_HEADS = 8
C_LEFT_CHUNKS = 8
REL_CLIP = 256
N_EXPERTS = 16
N_GROUPS = 4
EXPERTS_PER_GROUP = 4
N_BRANCHES = 3

LANES = 128
NEG = -1e30
LOG2E = math.log2(math.e)
VMEM_LIMIT = 56 * 1024 * 1024

COL_TILE = 512
R_AQ, R_AK, R_BQ, R_IDX = 0, 1, 2, 3
P_AV, P_CQ, P_CK, P_CV, P_MISC = 0, 1, 2, 3, 4


def _cparams(sem):
    return pltpu.CompilerParams(dimension_semantics=sem, vmem_limit_bytes=VMEM_LIMIT)


def _mod_kernel(c_ref, w_ref, b_ref, o_ref):
    c = c_ref[...]
    ca = c * jax.nn.sigmoid(c)
    o_ref[0] = jnp.dot(ca, w_ref[0], preferred_element_type=F32) + b_ref[0]


def _modulation(c, w_mod, b_mod):
    depth, d, n6 = w_mod.shape
    bsz = c.shape[0]
    tn = 1024
    return pl.pallas_call(
        _mod_kernel,
        grid=(depth, n6 // tn),
        in_specs=[
            pl.BlockSpec((bsz, d), lambda l, j: (0, 0)),
            pl.BlockSpec((1, d, tn), lambda l, j: (l, 0, j)),
            pl.BlockSpec((1, 1, tn), lambda l, j: (l, 0, j)),
        ],
        out_specs=pl.BlockSpec((1, bsz, tn), lambda l, j: (l, 0, j)),
        out_shape=jax.ShapeDtypeStruct((depth, bsz, n6), F32),
        compiler_params=_cparams(("arbitrary", "arbitrary")),
        name="modulation",
    )(c, w_mod, b_mod.reshape(depth, 1, n6))


def _modulated_norm(x, g, sc, sh):
    y = x * lax.rsqrt(jnp.mean(x * x, axis=-1, keepdims=True) + EPS)
    return (y * g) * (1.0 + sc) + sh


def _norm_kernel(x_ref, g_ref, sc_ref, sh_ref, u_ref):
    u_ref[0] = _modulated_norm(x_ref[0], g_ref[...], sc_ref[0], sh_ref[0]).astype(BF16)


def _norm(x, g, sc, sh, tm):
    bsz, s, d = x.shape
    tok = lambda b, i: (b, i, 0)
    per_batch = lambda b, i: (b, 0, 0)
    return pl.pallas_call(
        _norm_kernel,
        grid=(bsz, s // tm),
        in_specs=[pl.BlockSpec((1, tm, d), tok), pl.BlockSpec((1, d), lambda b, i: (0, 0)),
                  pl.BlockSpec((1, 1, d), per_batch), pl.BlockSpec((1, 1, d), per_batch)],
        out_specs=pl.BlockSpec((1, tm, d), tok),
        out_shape=jax.ShapeDtypeStruct((bsz, s, d), BF16),
        compiler_params=_cparams(("arbitrary", "arbitrary")),
        name="norm",
    )(x, g.reshape(1, d), sc, sh)


def _proj_kernel(*refs, mode):
    if mode == "rope":
        u_ref, cos_ref, sin_ref, w_ref, o_ref = refs
    else:
        u_ref, w_ref, o_ref = refs
    acc = jnp.dot(u_ref[0], w_ref[...], preferred_element_type=F32)
    if mode == "rope":
        cos, sin = cos_ref[0], sin_ref[0]
        groups = [acc[:, c:c + LANES] for c in range(0, COL_TILE, LANES)]
        acc = jnp.concatenate([g * cos + pltpu.roll(g, LANES // 2, axis=1) * sin for g in groups],
                              axis=1)
    elif mode == "gate":
        acc = 1.0 / (1.0 + jnp.exp(-acc))
    o_ref[0] = acc.astype(BF16)


def _project(u, w, rope_tables, mode, tm):
    bsz, s, d = u.shape
    ncols = w.shape[1]
    tok = lambda b, i, j: (b, i, 0)
    in_specs = [pl.BlockSpec((1, tm, d), tok)]
    in_specs += [pl.BlockSpec((1, tm, LANES), tok) for _ in rope_tables]
    in_specs += [pl.BlockSpec((d, COL_TILE), lambda b, i, j: (0, j))]
    return pl.pallas_call(
        functools.partial(_proj_kernel, mode=mode),
        grid=(bsz, s // tm, ncols // COL_TILE),
        in_specs=in_specs,
        out_specs=pl.BlockSpec((1, tm, COL_TILE), lambda b, i, j: (b, i, j)),
        out_shape=jax.ShapeDtypeStruct((bsz, s, ncols), BF16),
        compiler_params=_cparams(("arbitrary", "arbitrary", "arbitrary")),
        name="proj_" + mode,
    )(u, *rope_tables, w)


def _pair_layout(w):
    half = HEAD_DIM // 2
    col = jnp.arange(w.shape[1])
    base, r = (col // LANES) * LANES, col % LANES
    src = base + ((r // half) % 2) * HEAD_DIM + (r // HEAD_DIM) * half + r % half
    return w[:, src]


def _build_weights(w_in):
    sizes = (512, 512, 512, 512, 64, 64, 256, 64, 4, 512, 512, 512, 3072)
    parts, start = [], 0
    for sz in sizes:
        parts.append(w_in[:, start:start + sz])
        start += sz
    aq, ak, av, bq, bk, bv, iq, ik, iw, cq, ck, cv, gates = parts
    d = w_in.shape[0]
    qscale = HEAD_DIM ** -0.5 * LOG2E
    iw_scale = IDX_HEADS ** -0.5 * HEAD_DIM ** -0.5
    zeros = lambda n: jnp.zeros((d, n), w_in.dtype)
    w_rope = _pair_layout(jnp.concatenate([aq * qscale, ak, bq * qscale, iq, bk, bk, ik, ik], axis=1))
    w_plain = jnp.concatenate([av, cq * qscale, ck, cv,
                               bv, bv, iw * iw_scale, zeros(LANES - IDX_HEADS), zeros(256)], axis=1)
    return w_rope.astype(BF16), gates.astype(BF16), w_plain.astype(BF16)


def _rope_tables(positions):
    half = HEAD_DIM // 2
    inv = ROPE_THETA ** (-jnp.arange(half, dtype=F32) / half)
    ang = positions.astype(F32)[..., None] * inv
    cos, sin = jnp.cos(ang), jnp.sin(ang)
    cos_t = jnp.tile(cos, (1, 1, LANES // half))
    sin_t = jnp.concatenate([-sin, -sin, sin, sin], axis=-1)
    return cos_t, sin_t


def _split_pair(pair, interleaved):
    lane = lax.broadcasted_iota(jnp.int32, pair.shape, 1)
    first = ((lane // (HEAD_DIM // 2)) % 2 == 0) if interleaved else (lane < HEAD_DIM)
    zero = jnp.zeros_like(pair)
    return jnp.where(first, pair, zero), jnp.where(first, zero, pair)


def _qk(q, k):
    return lax.dot_general(q, k, (((1,), (1,)), ((), ())), preferred_element_type=F32)


def _fold_rows(x, op):
    parts = x.reshape(x.shape[0] // 8, 8, x.shape[1])
    level = [parts[g] for g in range(parts.shape[0])]
    while len(level) > 1:
        nxt = [op(level[a], level[a + 1]) for a in range(0, len(level) - 1, 2)]
        level = nxt + ([level[-1]] if len(level) % 2 else [])
    return level[0]


def _flash_t_update(s, vt_blocks, m_scr, acc_scr):
    m_old = m_scr[...]
    m_new = jnp.maximum(m_old, jnp.max(_fold_rows(s, jnp.maximum), axis=0, keepdims=True))
    alpha = jnp.exp2(m_old - m_new)
    pb = jnp.exp2(s - m_new).astype(BF16)
    m_scr[...] = m_new
    pv = functools.reduce(lambda a, b: a + b, [
        jnp.dot(vt, pb[k0:k1], preferred_element_type=F32) for (k0, k1), vt in vt_blocks])
    acc_scr[...] = alpha * acc_scr[...] + pv


ONES_ROWS = 16


def _with_ones_rows(vt):
    shape = vt.shape[:-2]
    return jnp.concatenate([vt, jnp.ones(shape + (1, vt.shape[-1]), vt.dtype),
                            jnp.zeros(shape + (ONES_ROWS - 1, vt.shape[-1]), vt.dtype)], axis=-2)


def _flash_update(s, v, m_scr, l_scr, acc_scr):
    groups = [s[:, c:c + LANES] for c in range(0, s.shape[1], LANES)]
    m_old = m_scr[...]
    lane_max = functools.reduce(jnp.maximum, groups)
    m_new = jnp.maximum(m_old, jnp.max(lane_max, axis=-1, keepdims=True))
    alpha = jnp.exp2(m_old - m_new)
    p_groups = [jnp.exp2(g - m_new) for g in groups]
    l_scr[...] = alpha * l_scr[...] + functools.reduce(lambda a, b: a + b, p_groups)
    m_scr[...] = m_new
    pb = jnp.concatenate([g.astype(BF16) for g in p_groups], axis=1)
    acc_scr[...] = alpha * acc_scr[...] + jnp.dot(pb, v, preferred_element_type=F32)


def _diff_attn_kernel(lam_ref, q_ref, k_ref, v_ref, ng_ref, o_ref, m_scr, l_scr, acc_scr,
                      *, tq, lam_init):
    i = pl.program_id(2)
    qs = jnp.concatenate(_split_pair(q_ref[0], True), axis=0)
    m_scr[...] = jnp.full(m_scr.shape, NEG, F32)
    l_scr[...] = jnp.zeros(l_scr.shape, F32)
    acc_scr[...] = jnp.zeros(acc_scr.shape, F32)

    def kv_block(j):
        start = pl.multiple_of(j * tq, tq)
        return k_ref[0, pl.ds(start, tq), :], v_ref[0, pl.ds(start, tq), :]

    def body(j, carry):
        k, v = kv_block(j)
        _flash_update(_qk(qs, k), v, m_scr, l_scr, acc_scr)
        return carry

    lax.fori_loop(0, i, body, 0)
    k, v = kv_block(i)
    s = _qk(qs, k)
    row = lax.broadcasted_iota(jnp.int32, s.shape, 0) % tq
    col = lax.broadcasted_iota(jnp.int32, s.shape, 1)
    s = jnp.where((col // CHUNK) <= (row // CHUNK), s, NEG)
    _flash_update(s, v, m_scr, l_scr, acc_scr)

    o = acc_scr[...] / jnp.sum(l_scr[...], axis=-1, keepdims=True)
    o = o[:tq] - lam_ref[0] * o[tq:]
    o = o * lax.rsqrt(jnp.mean(o * o, axis=-1, keepdims=True) + EPS)
    o_ref[0] = ((o * ng_ref[...]) * (1.0 - lam_init)).astype(BF16)


def _diff_attention(rope_slab, plain_slab, lam, norm_g, lam_init, tq):
    bsz, s, _ = rope_slab.shape
    kern = functools.partial(_diff_attn_kernel, tq=tq, lam_init=lam_init)
    cb = COL_TILE // LANES
    return pl.pallas_call(
        kern,
        grid=(bsz, A_HEADS, s // tq),
        in_specs=[
            pl.BlockSpec(memory_space=pltpu.SMEM),
            pl.BlockSpec((1, tq, LANES), lambda b, h, i: (b, i, R_AQ * cb + h)),
            pl.BlockSpec((1, s, LANES), lambda b, h, i: (b, 0, R_AK * cb + h)),
            pl.BlockSpec((1, s, LANES), lambda b, h, i: (b, 0, P_AV * cb + h)),
            pl.BlockSpec((1, LANES), lambda b, h, i: (0, 0)),
        ],
        out_specs=pl.BlockSpec((1, tq, LANES), lambda b, h, i: (b, i, h)),
        out_shape=jax.ShapeDtypeStruct((bsz, s, A_HEADS * LANES), BF16),
        scratch_shapes=[pltpu.VMEM((2 * tq, LANES), F32), pltpu.VMEM((2 * tq, LANES), F32),
                        pltpu.VMEM((2 * tq, LANES), F32)],
        compiler_params=_cparams(("arbitrary", "arbitrary", "arbitrary")),
        name="diff_attn",
    )(lam.reshape(1), rope_slab, rope_slab, plain_slab, norm_g.reshape(1, LANES))


KEY_NEG_INF = -2139095040
KEY_POS_INF = 2139095040
SEARCH_MAX_STEPS = 80
NO_TIE_LIMIT = 1e9
VT_ROWS = HEAD_DIM + ONES_ROWS


def _key_to_float(key):
    bits = jnp.where(key >= 0, key, (key - 1) ^ jnp.int32(0x7FFFFFFF))
    return pltpu.bitcast(bits, F32)


def _float_to_key(t):
    bits = pltpu.bitcast(t, jnp.int32)
    return jnp.where(bits >= 0, bits, (bits ^ jnp.int32(0x7FFFFFFF)) + 1)


def _dsa_kernel(q_ref, iq_ref, iw_ref, k_ref, ik_ref, vt_ref, o_ref,
                sc_scr, qs_scr, m_scr, acc_scr, *, tq, topk):
    i = pl.program_id(1)
    nblk = i + 1
    npair = (nblk + 1) // 2
    nq = B_HEADS * tq

    iq = iq_ref[0]
    parts = []
    for p in range(IDX_HEADS // 2):
        parts += list(_split_pair(iq[:, p * LANES:(p + 1) * LANES], True))
    iqs = jnp.concatenate(parts, axis=0)
    iw_t = iw_ref[0].astype(F32).T

    def index_block(j):
        start = pl.multiple_of(j * tq, tq)
        hs = jnp.maximum(_qk(ik_ref[0, pl.ds(start, tq), :], iqs), 0.0)
        score = iw_t[0:1] * hs[:, 0:tq]
        for h in range(1, IDX_HEADS):
            score = score + iw_t[h:h + 1] * hs[:, h * tq:(h + 1) * tq]
        return score

    def score_slot(j):
        return j // 2, pl.ds(pl.multiple_of((j % 2) * tq, tq), tq)

    def p1_body(j, carry):
        pair, rows = score_slot(j)
        sc_scr[pair, rows, :] = index_block(j)
        return carry

    lax.fori_loop(0, i, p1_body, 0)
    score = index_block(i)
    key_i = lax.broadcasted_iota(jnp.int32, score.shape, 0)
    qry_i = lax.broadcasted_iota(jnp.int32, score.shape, 1)
    pair, rows = score_slot(i)
    sc_scr[pair, rows, :] = jnp.where((key_i // CHUNK) <= (qry_i // CHUNK), score, -jnp.inf)

    @pl.when(nblk % 2 == 1)
    def _():
        sc_scr[npair - 1, tq:2 * tq, :] = jnp.full((tq, tq), -jnp.inf, F32)

    kf = float(topk)

    def scan(hit_fn, ext_fn, ext_op, ext_init):
        def body(jj, carry):
            cnt, ext = carry
            sblk = sc_scr[jj]
            cnt = cnt + _fold_rows(jnp.where(hit_fn(sblk), 1.0, 0.0), lambda a, b: a + b)
            if ext_fn is not None:
                ext = ext_op(ext, _fold_rows(ext_fn(sblk), ext_op))
            return cnt, ext
        cnt, ext = lax.fori_loop(0, npair, body, (jnp.zeros((8, tq), F32),
                                                  jnp.full((8, tq), ext_init, F32)))
        return jnp.sum(cnt, axis=0, keepdims=True), ext

    def count_ge(t):
        return scan(lambda sblk: sblk >= t, None, None, 0.0)[0]

    c_ge0, ext = scan(lambda sblk: sblk >= 0.0, lambda sblk: sblk, jnp.maximum, -jnp.inf)
    col_max = jnp.max(ext, axis=0, keepdims=True)
    c_gt0, ext = scan(lambda sblk: sblk > 0.0,
                      lambda sblk: jnp.where(sblk > -jnp.inf, sblk, jnp.inf), jnp.minimum, jnp.inf)
    col_min = jnp.min(ext, axis=0, keepdims=True)
    qpos = i * tq + lax.broadcasted_iota(jnp.int32, (1, tq), 1)
    n_valid = (((qpos // CHUNK) + 1) * CHUNK).astype(F32)
    open_q = n_valid < kf
    above = c_gt0 >= kf
    below = c_ge0 < kf
    ikey = lambda v: jnp.full((1, tq), v, jnp.int32)
    lo_k = jnp.where(below, _float_to_key(col_min), ikey(0))
    hi_k = jnp.where(above, _float_to_key(col_max) + 1, jnp.where(below, ikey(0), ikey(1)))
    c_lo = jnp.where(below, n_valid, c_ge0)
    c_hi = jnp.where(above, 0.0, jnp.where(below, c_ge0, c_gt0))
    done = jnp.logical_or(open_q, jnp.logical_not(jnp.logical_or(above, below)))
    state = (jnp.int32(0), lo_k, hi_k, c_lo, c_hi, done.astype(jnp.int32), ikey(0))

    def search_cond(st):
        return jnp.logical_and(st[0] < SEARCH_MAX_STEPS, jnp.min(st[5]) == 0)

    def search_step(st):
        step, lo_k, hi_k, c_lo, c_hi, done, force_bisect = st
        finite = jnp.logical_and(lo_k > KEY_NEG_INF, hi_k < KEY_POS_INF)
        t_lo, t_hi = _key_to_float(lo_k), _key_to_float(hi_k)
        log_lo = jnp.log(c_lo)
        frac = (log_lo - math.log(kf - 0.5)) / (log_lo - jnp.log(jnp.maximum(c_hi, 0.5)))
        k_interp = _float_to_key(t_lo + (t_hi - t_lo) * frac)
        k_mid = (lo_k & hi_k) + ((lo_k ^ hi_k) >> 1)
        interp = jnp.logical_and(finite, force_bisect == 0)
        k = jnp.where(interp, k_interp, k_mid)
        k = jnp.minimum(jnp.maximum(k, lo_k + 1), hi_k - 1)
        c = count_ge(_key_to_float(k))
        ok = c >= kf
        live = done == 0
        new_lo = jnp.where(jnp.logical_and(live, ok), k, lo_k)
        new_hi = jnp.where(jnp.logical_and(live, jnp.logical_not(ok)), k, hi_k)
        new_c_lo = jnp.where(jnp.logical_and(live, ok), c, c_lo)
        new_c_hi = jnp.where(jnp.logical_and(live, jnp.logical_not(ok)), c, c_hi)
        width = lambda a, b: b.astype(F32) - a.astype(F32)
        slow = width(new_lo, new_hi) > 0.5 * width(lo_k, hi_k)
        new_force = jnp.logical_and(interp, slow).astype(jnp.int32)
        finished = jnp.logical_or(new_c_lo == kf, new_hi - new_lo == 1)
        new_done = jnp.maximum(done, finished.astype(jnp.int32))
        return step + 1, new_lo, new_hi, new_c_lo, new_c_hi, new_done, new_force

    _, lo_k, hi_k, c_lo, c_hi, _, _ = lax.while_loop(search_cond, search_step, state)
    tau = jnp.where(open_q, -jnp.inf, _key_to_float(lo_k))
    need = jnp.where(open_q, 0.0, jnp.where(c_lo == kf, NO_TIE_LIMIT, kf - c_hi))

    q = q_ref[0]
    for p in range(B_HEADS // 2):
        even, odd = _split_pair(q[:, p * LANES:(p + 1) * LANES], True)
        qs_scr[p * tq:(p + 1) * tq, :] = even
        qs_scr[(B_HEADS // 2 + p) * tq:(B_HEADS // 2 + p + 1) * tq, :] = odd
    m_scr[...] = jnp.full(m_scr.shape, NEG, F32)
    acc_scr[...] = jnp.zeros(acc_scr.shape, F32)
    key_i = lax.broadcasted_iota(jnp.int32, (tq, tq), 0)
    lower = (lax.broadcasted_iota(jnp.int32, (tq, tq), 1) <= key_i).astype(BF16)

    def p3_body(jj, tie_count):
        start = pl.multiple_of(jj * 2 * tq, 2 * tq)
        sblk = sc_scr[jj]
        tie = sblk == tau
        tie01 = jnp.where(tie, 1.0, 0.0).astype(BF16)
        rank_a = tie_count + jnp.dot(lower, tie01[0:tq], preferred_element_type=F32)
        rank_b = rank_a[tq - 1:tq, :] + jnp.dot(lower, tie01[tq:2 * tq], preferred_element_type=F32)
        rank = jnp.concatenate([rank_a, rank_b], axis=0)
        keep_tie = jnp.where(rank <= need, 0.0, NEG)
        bias = jnp.where(sblk > tau, 0.0, jnp.where(tie, keep_tie, NEG))
        s = _qk(k_ref[0, pl.ds(start, 2 * tq), :], qs_scr[...])
        s = jnp.concatenate([s[:, h * tq:(h + 1) * tq] + bias for h in range(B_HEADS)], axis=1)
        _flash_t_update(s, [((0, tq), vt_ref[0, 2 * jj]), ((tq, 2 * tq), vt_ref[0, 2 * jj + 1])],
                        m_scr, acc_scr)
        return rank[2 * tq - 1:2 * tq, :]

    lax.fori_loop(0, npair, p3_body, jnp.zeros((1, tq), F32))
    acc = acc_scr[...]
    o = acc[0:HEAD_DIM] / acc[HEAD_DIM:HEAD_DIM + 1]
    half_cols = (B_HEADS // 2) * tq
    for p in range(B_HEADS // 2):
        pair = jnp.concatenate([o[:, p * tq:(p + 1) * tq],
                                o[:, half_cols + p * tq:half_cols + (p + 1) * tq]], axis=0)
        o_ref[0, :, p * LANES:(p + 1) * LANES] = pair.T.astype(BF16)


def _dsa_attention(rope_slab, plain_slab, tq):
    bsz, s, _ = rope_slab.shape
    topk = min(TOPK_MAX, s // 4)
    nblk = s // tq
    assert tq >= topk and s % tq == 0 and nblk % 2 == 0
    kern = functools.partial(_dsa_kernel, tq=tq, topk=topk)
    idx0 = R_IDX * COL_TILE
    misc0 = P_MISC * COL_TILE
    width = B_HEADS * HEAD_DIM
    vt = _with_ones_rows(
        plain_slab[:, :, misc0:misc0 + HEAD_DIM].reshape(bsz, nblk, tq, HEAD_DIM).transpose(0, 1, 3, 2))
    return pl.pallas_call(
        kern,
        grid=(bsz, s // tq),
        in_specs=[
            pl.BlockSpec((1, tq, COL_TILE), lambda b, i: (b, i, R_BQ)),
            pl.BlockSpec((1, tq, 256), lambda b, i: (b, i, idx0 // 256)),
            pl.BlockSpec((1, tq, LANES), lambda b, i: (b, i, (misc0 + LANES) // LANES)),
            pl.BlockSpec((1, s, LANES), lambda b, i: (b, 0, (idx0 + 256) // LANES)),
            pl.BlockSpec((1, s, LANES), lambda b, i: (b, 0, (idx0 + 384) // LANES)),
            pl.BlockSpec((1, nblk, VT_ROWS, tq), lambda b, i: (b, 0, 0, 0)),
        ],
        out_specs=pl.BlockSpec((1, tq, width), lambda b, i: (b, i, 0)),
        out_shape=jax.ShapeDtypeStruct((bsz, s, width), BF16),
        scratch_shapes=[
            pltpu.VMEM((nblk // 2, 2 * tq, tq), F32),
            pltpu.VMEM((B_HEADS * tq, LANES), BF16),
            pltpu.VMEM((1, B_HEADS * tq), F32),
            pltpu.VMEM((VT_ROWS, B_HEADS * tq), F32),
        ],
        compiler_params=_cparams(("arbitrary", "arbitrary")),
        name="dsa_attn",
    )(rope_slab, rope_slab, plain_slab, rope_slab, rope_slab, vt)


BAND_TQ = 2 * CHUNK
BAND_KEYS = (C_LEFT_CHUNKS + BAND_TQ // CHUNK) * CHUNK
BAND_PAD = C_LEFT_CHUNKS * CHUNK


def _band_kernel(q_ref, k_ref, v_ref, bias_ref, o_ref):
    i = pl.program_id(1)
    tq = BAND_TQ
    start = pl.multiple_of(i * tq, tq)
    col = lax.broadcasted_iota(jnp.int32, (2 * tq, BAND_KEYS), 1)
    key_ok = col + start >= BAND_PAD
    for p in range(C_HEADS // 2):
        lanes = slice(p * LANES, (p + 1) * LANES)
        k = k_ref[0, pl.ds(start, BAND_KEYS), lanes]
        v = v_ref[0, pl.ds(start, BAND_KEYS), lanes]
        qs = jnp.concatenate(_split_pair(q_ref[0, :, lanes], False), axis=0)
        s = _qk(qs, k) + bias_ref[2 * p:2 * p + 2].reshape(2 * tq, BAND_KEYS)
        s = jnp.where(key_ok, s, NEG)
        m = jnp.max(s, axis=-1, keepdims=True)
        e = jnp.exp2(s - m)
        l = jnp.sum(e, axis=-1, keepdims=True)
        eb = e.astype(BF16)
        v_even, v_odd = _split_pair(v, False)
        o = (jnp.dot(eb[:tq], v_even, preferred_element_type=F32) / l[:tq]
             + jnp.dot(eb[tq:], v_odd, preferred_element_type=F32) / l[tq:])
        o_ref[0, :, lanes] = o.astype(BF16)


def _band_bias(rel_bias):
    r = jnp.arange(BAND_TQ)[:, None]
    cidx = jnp.arange(BAND_KEYS)[None, :]
    n_diag = BAND_TQ + BAND_KEYS - 1
    rel = BAND_PAD - (BAND_KEYS - 1) + jnp.arange(n_diag)
    g = rel_bias.astype(F32)[:, jnp.clip(rel, -REL_CLIP, REL_CLIP) + REL_CLIP] * LOG2E
    skew = jnp.tile(g, (1, BAND_TQ + 2))[:, :BAND_TQ * (n_diag + 1)]
    bias = skew.reshape(-1, BAND_TQ, n_diag + 1)[:, :, :BAND_KEYS][:, :, ::-1]
    dchunk = (r // CHUNK + C_LEFT_CHUNKS) - cidx // CHUNK
    in_band = jnp.logical_and(dchunk >= 0, dchunk <= C_LEFT_CHUNKS)
    return jnp.where(in_band[None], bias, NEG)


def _band_attention(plain_slab, kp, vp, bias):
    bsz, s, _ = plain_slab.shape
    tq = BAND_TQ
    sp = kp.shape[1]
    width = C_HEADS * HEAD_DIM
    return pl.pallas_call(
        _band_kernel,
        grid=(bsz, s // tq),
        in_specs=[
            pl.BlockSpec((1, tq, width), lambda b, i: (b, i, P_CQ)),
            pl.BlockSpec((1, sp, width), lambda b, i: (b, 0, 0)),
            pl.BlockSpec((1, sp, width), lambda b, i: (b, 0, 0)),
            pl.BlockSpec((C_HEADS, tq, BAND_KEYS), lambda b, i: (0, 0, 0)),
        ],
        out_specs=pl.BlockSpec((1, tq, width), lambda b, i: (b, i, 0)),
        out_shape=jax.ShapeDtypeStruct((bsz, s, width), BF16),
        compiler_params=_cparams(("arbitrary", "arbitrary")),
        name="band_attn",
    )(plain_slab, kp, vp, bias)


def _merge_kernel(x_ref, g1_ref, ya_ref, yb_ref, yc_ref, ga_ref, gb_ref, gc_ref,
                  wa_ref, wb_ref, wc_ref, wo_ref, o_ref):
    def branch(y_ref, w_ref, gate_ref):
        return gate_ref[0].astype(F32) * jnp.dot(y_ref[0], w_ref[...], preferred_element_type=F32)

    merged = (branch(ya_ref, wa_ref, ga_ref) + branch(yb_ref, wb_ref, gb_ref)
              + branch(yc_ref, wc_ref, gc_ref))
    mixed = jnp.dot(merged.astype(BF16), wo_ref[...], preferred_element_type=F32)
    o_ref[0] = x_ref[0] + g1_ref[0] * mixed


def _merge(x, g1, ya, yb, yc, gate_slab, wa, wb, wc, wo, tm):
    bsz, s, d = x.shape
    tok = lambda b, i: (b, i, 0)
    full = lambda b, i: (0, 0)
    return pl.pallas_call(
        _merge_kernel,
        grid=(bsz, s // tm),
        in_specs=[
            pl.BlockSpec((1, tm, d), tok),
            pl.BlockSpec((1, 1, d), lambda b, i: (b, 0, 0)),
            pl.BlockSpec((1, tm, ya.shape[2]), tok),
            pl.BlockSpec((1, tm, yb.shape[2]), tok),
            pl.BlockSpec((1, tm, yc.shape[2]), tok),
            pl.BlockSpec((1, tm, d), lambda b, i: (b, i, 0)),
            pl.BlockSpec((1, tm, d), lambda b, i: (b, i, 1)),
            pl.BlockSpec((1, tm, d), lambda b, i: (b, i, 2)),
            pl.BlockSpec(wa.shape, full),
            pl.BlockSpec(wb.shape, full),
            pl.BlockSpec(wc.shape, full),
            pl.BlockSpec(wo.shape, full),
        ],
        out_specs=pl.BlockSpec((1, tm, d), tok),
        out_shape=jax.ShapeDtypeStruct((bsz, s, d), F32),
        compiler_params=_cparams(("arbitrary", "arbitrary")),
        name="merge",
    )(x, g1, ya, yb, yc, gate_slab, gate_slab, gate_slab, wa, wb, wc, wo)


def _router_kernel(x_ref, g_ref, sc_ref, sh_ref, rw_ref, rb_ref, u_ref, comb_ref):
    u = _modulated_norm(x_ref[0], g_ref[...], sc_ref[0], sh_ref[0])
    u_ref[0] = u.astype(BF16)
    logits = lax.dot_general(rw_ref[...], u, (((1,), (1,)), ((), ())),
                             preferred_element_type=F32, precision=lax.Precision.HIGHEST)
    aff = jax.nn.sigmoid(logits)
    sel = aff + rb_ref[...]
    rows = [sel[e:e + 1] for e in range(N_EXPERTS)]
    gscore = []
    for g in range(N_GROUPS):
        r = rows[g * EXPERTS_PER_GROUP:(g + 1) * EXPERTS_PER_GROUP]
        best = None
        for a in range(EXPERTS_PER_GROUP):
            for b in range(a + 1, EXPERTS_PER_GROUP):
                pair = r[a] + r[b]
                best = pair if best is None else jnp.maximum(best, pair)
        gscore.append(best)
    gmax = functools.reduce(jnp.maximum, gscore)
    taken = jnp.zeros_like(gmax) > 1.0
    in_best = []
    for g in range(N_GROUPS):
        is_g = jnp.logical_and(gscore[g] == gmax, jnp.logical_not(taken))
        in_best.append(is_g)
        taken = jnp.logical_or(taken, is_g)
    keep = []
    for e in range(N_EXPERTS):
        g = e // EXPERTS_PER_GROUP
        rank = jnp.zeros_like(gmax)
        for o in range(g * EXPERTS_PER_GROUP, (g + 1) * EXPERTS_PER_GROUP):
            if o == e:
                continue
            ahead = rows[o] > rows[e] if o > e else rows[o] >= rows[e]
            rank = rank + jnp.where(ahead, 1.0, 0.0)
        keep.append(jnp.logical_and(in_best[g], rank < 2.0))
    w = [jnp.where(keep[e], aff[e:e + 1], 0.0) for e in range(N_EXPERTS)]
    total = functools.reduce(lambda a, b: a + b, w)
    comb = jnp.concatenate([we / total for we in w]
                           + [jnp.zeros((LANES - N_EXPERTS, total.shape[1]), F32)], axis=0)
    comb_ref[0] = comb.T


def _router(x, g, sc, sh, router_w, router_b, tm):
    bsz, s, d = x.shape
    tok = lambda b, i: (b, i, 0)
    return pl.pallas_call(
        _router_kernel,
        grid=(bsz, s // tm),
        in_specs=[
            pl.BlockSpec((1, tm, d), tok),
            pl.BlockSpec((1, d), lambda b, i: (0, 0)),
            pl.BlockSpec((1, 1, d), lambda b, i: (b, 0, 0)),
            pl.BlockSpec((1, 1, d), lambda b, i: (b, 0, 0)),
            pl.BlockSpec((N_EXPERTS, d), lambda b, i: (0, 0)),
            pl.BlockSpec((N_EXPERTS, 1), lambda b, i: (0, 0)),
        ],
        out_specs=[pl.BlockSpec((1, tm, d), tok), pl.BlockSpec((1, tm, LANES), tok)],
        out_shape=[jax.ShapeDtypeStruct((bsz, s, d), BF16),
                   jax.ShapeDtypeStruct((bsz, s, LANES), F32)],
        compiler_params=_cparams(("arbitrary", "arbitrary")),
        name="router",
    )(x, g.reshape(1, d), sc, sh, router_w.T, router_b.reshape(N_EXPERTS, 1))


MOE_SUB = 512
MOE_CAP = 128


def _expert_ffn(rows, w1_ref, w3_ref, w2_ref):
    h1 = jnp.dot(rows, w1_ref[0], preferred_element_type=F32)
    h3 = jnp.dot(rows, w3_ref[0], preferred_element_type=F32)
    h = (h1 * jax.nn.sigmoid(h1)) * h3
    return jnp.dot(h.astype(BF16), w2_ref[0], preferred_element_type=F32)


def _moe_kernel(x_ref, g2_ref, u_ref, comb_ref, w1_ref, w3_ref, w2_ref, o_ref,
                acc_scr, pos_scr, pos_t_scr, comb_t_scr, scatter_scr, y_scr):
    e = pl.program_id(2)
    tm = u_ref.shape[1]
    sub = min(MOE_SUB, tm)
    subs = [slice(r0, r0 + sub) for r0 in range(0, tm, sub)]

    @pl.when(e == 0)
    def _():
        acc_scr[...] = jnp.zeros(acc_scr.shape, F32)
        before = (lax.broadcasted_iota(jnp.int32, (sub, sub), 1)
                  < lax.broadcasted_iota(jnp.int32, (sub, sub), 0)).astype(BF16)
        for rows in subs:
            comb0 = comb_ref[0, rows, :]
            member = jnp.where(comb0 > 0.0, 1.0, 0.0).astype(BF16)
            pos = jnp.dot(before, member, preferred_element_type=F32)
            pos_scr[rows, :] = pos
            pos_t_scr[:, rows] = pos.T
            comb_t_scr[:, rows] = comb0.T

    comb = comb_ref[0]
    lane = lax.broadcasted_iota(jnp.int32, comb.shape, 1)
    ce = jnp.sum(jnp.where(lane == e, comb, 0.0), axis=-1, keepdims=True)
    pe = jnp.sum(jnp.where(lane == e, pos_scr[...], 0.0), axis=-1, keepdims=True)
    member_e = jnp.where(ce > 0.0, 1.0, 0.0)
    fullest = functools.reduce(jnp.maximum, [jnp.sum(member_e[rows]) for rows in subs])
    routed = fullest <= float(MOE_CAP)
    slots = pl.ds(pl.multiple_of(e * MOE_CAP, MOE_CAP), MOE_CAP)

    @pl.when(routed)
    def _():
        slot_l = lax.broadcasted_iota(jnp.int32, (1, MOE_CAP), 1).astype(F32)
        scatter_scr[e] = jnp.where(jnp.logical_and(ce > 0.0, pe == slot_l), 1.0, 0.0).astype(BF16)
        slot_s = lax.broadcasted_iota(jnp.int32, (MOE_CAP, 1), 0).astype(F32)
        packed, weights = [], []
        for rows in subs:
            ce_t = comb_t_scr[pl.ds(e, 1), rows]
            pe_t = pos_t_scr[pl.ds(e, 1), rows]
            hit = jnp.logical_and(ce_t > 0.0, pe_t == slot_s)
            weights.append(jnp.sum(jnp.where(hit, ce_t, 0.0), axis=-1, keepdims=True))
            packed.append(jnp.dot(jnp.where(hit, 1.0, 0.0).astype(BF16), u_ref[0, rows, :],
                                  preferred_element_type=F32).astype(BF16))
        y = jnp.concatenate(weights, axis=0) * _expert_ffn(jnp.concatenate(packed, axis=0),
                                                           w1_ref, w3_ref, w2_ref)
        for k in range(len(subs)):
            y_scr[k, slots, :] = y[k * MOE_CAP:(k + 1) * MOE_CAP].astype(BF16)

    @pl.when(jnp.logical_not(routed))
    def _():
        scatter_scr[e] = jnp.zeros(scatter_scr.shape[1:], BF16)
        for k in range(len(subs)):
            y_scr[k, slots, :] = jnp.zeros((MOE_CAP, y_scr.shape[2]), BF16)
        acc_scr[...] += ce * _expert_ffn(u_ref[0], w1_ref, w3_ref, w2_ref)

    @pl.when(e == pl.num_programs(2) - 1)
    def _():
        for k, rows in enumerate(subs):
            scatter = jnp.concatenate([scatter_scr[j, rows, :] for j in range(N_EXPERTS)], axis=1)
            y = acc_scr[rows, :] + jnp.dot(scatter, y_scr[k], preferred_element_type=F32)
            o_ref[0, rows, :] = x_ref[0, rows, :] + g2_ref[0] * y


def _moe(x, g2, u, comb, w1, w3, w2, tm):
    bsz, s, d = x.shape
    ne, _, dff = w1.shape
    assert ne == N_EXPERTS
    tok = lambda b, i, e: (b, i, 0)
    return pl.pallas_call(
        _moe_kernel,
        grid=(bsz, s // tm, ne),
        in_specs=[
            pl.BlockSpec((1, tm, d), tok),
            pl.BlockSpec((1, 1, d), lambda b, i, e: (b, 0, 0)),
            pl.BlockSpec((1, tm, d), tok),
            pl.BlockSpec((1, tm, LANES), tok),
            pl.BlockSpec((1, d, dff), lambda b, i, e: (e, 0, 0)),
            pl.BlockSpec((1, d, dff), lambda b, i, e: (e, 0, 0)),
            pl.BlockSpec((1, dff, d), lambda b, i, e: (e, 0, 0)),
        ],
        out_specs=pl.BlockSpec((1, tm, d), tok),
        out_shape=jax.ShapeDtypeStruct((bsz, s, d), F32),
        scratch_shapes=[
            pltpu.VMEM((tm, d), F32),
            pltpu.VMEM((tm, LANES), F32),
            pltpu.VMEM((LANES, tm), F32),
            pltpu.VMEM((LANES, tm), F32),
            pltpu.VMEM((ne, tm, MOE_CAP), BF16),
            pltpu.VMEM((pl.cdiv(tm, MOE_SUB), ne * MOE_CAP, d), BF16),
        ],
        compiler_params=_cparams(("arbitrary", "arbitrary", "arbitrary")),
        name="moe",
    )(x, g2, u, comb, w1, w3, w2)


def _final_norm_kernel(x_ref, g_ref, o_ref):
    x = x_ref[0]
    o_ref[0] = (x * lax.rsqrt(jnp.mean(x * x, axis=-1, keepdims=True) + EPS)) * g_ref[...]


def _final_norm(x, g, tm):
    bsz, s, d = x.shape
    tok = lambda b, i: (b, i, 0)
    return pl.pallas_call(
        _final_norm_kernel,
        grid=(bsz, s // tm),
        in_specs=[pl.BlockSpec((1, tm, d), tok), pl.BlockSpec((1, d), lambda b, i: (0, 0))],
        out_specs=pl.BlockSpec((1, tm, d), tok),
        out_shape=jax.ShapeDtypeStruct((bsz, s, d), F32),
        compiler_params=_cparams(("arbitrary", "arbitrary")),
        name="final_norm",
    )(x, g.reshape(1, d))


def _tile(s, want):
    t = min(want, s)
    assert s % t == 0
    return t


def kernel(x, c, positions, norm1_g, norm2_g, w_mod, b_mod, w_in, lambda_q1, lambda_k1, lambda_q2, lambda_k2, a_norm_g, c_rel_bias, w_branch_a, w_branch_b, w_branch_c, w_out, router_w, router_b, exp_w1, exp_w3, exp_w2, final_g):
    bsz, s, d = x.shape
    depth = w_mod.shape[0]
    tm = _tile(s, 1024)

    mod = _modulation(c, w_mod, b_mod)
    rope_tables = _rope_tables(positions)

    for layer in range(depth):
        lam_init = 0.8 - 0.6 * math.exp(-0.3 * layer)
        sh1, sc1, g1, sh2, sc2, g2 = [m[:, None, :] for m in jnp.split(mod[layer], 6, axis=-1)]
        lam = (jnp.exp(jnp.sum(lambda_q1[layer] * lambda_k1[layer]))
               - jnp.exp(jnp.sum(lambda_q2[layer] * lambda_k2[layer])) + lam_init)

        w_rope, w_gate, w_plain = _build_weights(w_in[layer])
        u = _norm(x, norm1_g[layer], sc1, sh1, tm)
        rope_slab = _project(u, w_rope, rope_tables, "rope", tm)
        gate_slab = _project(u, w_gate, (), "gate", tm)
        plain_slab = _project(u, w_plain, (), "plain", tm)

        ya = _diff_attention(rope_slab, plain_slab, lam, a_norm_g[layer], lam_init, _tile(s, 512))
        yb = _dsa_attention(rope_slab, plain_slab, _tile(s, 256))
        pad = ((0, 0), (BAND_PAD, 0), (0, 0))
        kp = jnp.pad(plain_slab[:, :, P_CK * COL_TILE:(P_CK + 1) * COL_TILE], pad)
        vp = jnp.pad(plain_slab[:, :, P_CV * COL_TILE:(P_CV + 1) * COL_TILE], pad)
        yc = _band_attention(plain_slab, kp, vp, _band_bias(c_rel_bias[layer]))
        x = _merge(x, g1, ya, yb, yc, gate_slab,
                   w_branch_a[layer].astype(BF16), w_branch_b[layer].astype(BF16),
                   w_branch_c[layer].astype(BF16), w_out[layer].astype(BF16), _tile(s, 512))

        u, comb = _router(x, norm2_g[layer], sc2, sh2, router_w, router_b, _tile(s, 512))
        x = _moe(x, g2, u, comb, exp_w1[layer].astype(BF16), exp_w3[layer].astype(BF16),
                 exp_w2[layer].astype(BF16), tm)

    return _final_norm(x, final_g, tm)
```

```python
import functools
import math

import jax
import jax.numpy as jnp
from jax import lax
from jax.experimental import pallas as pl
from jax.experimental.pallas import tpu as pltpu

F32 = jnp.float32
BF16 = jnp.bfloat16

CHUNK = 64
ROPE_THETA = 10000.0
EPS = 1e-6
A_HEADS = 4
HEAD_DIM = 64
B_HEADS = 8
IDX_HEADS = 4
TOPK_MAX = 256
C_HEADS = 8
C_LEFT_CHUNKS = 8
REL_CLIP = 256
N_EXPERTS = 16
N_GROUPS = 4
EXPERTS_PER_GROUP = 4
N_BRANCHES = 3

LANES = 128
NEG = -1e30
LOG2E = math.log2(math.e)
VMEM_LIMIT = 56 * 1024 * 1024

COL_TILE = 512
PROJ_ROWS = 256
R_AQ, R_AK, R_BQ, R_IDX = 0, 1, 2, 3
P_AV, P_CQ, P_CK, P_CV, P_MISC = 0, 1, 2, 3, 4


def _cparams(sem):
    return pltpu.CompilerParams(dimension_semantics=sem, vmem_limit_bytes=VMEM_LIMIT)


def _mod_kernel(c_ref, w_ref, b_ref, o_ref):
    c = c_ref[...]
    ca = c * jax.nn.sigmoid(c)
    o_ref[0] = jnp.dot(ca, w_ref[0], preferred_element_type=F32) + b_ref[0]


def _modulation(c, w_mod, b_mod):
    depth, d, n6 = w_mod.shape
    bsz = c.shape[0]
    tn = 1024
    return pl.pallas_call(
        _mod_kernel,
        grid=(depth, n6 // tn),
        in_specs=[
            pl.BlockSpec((bsz, d), lambda l, j: (0, 0)),
            pl.BlockSpec((1, d, tn), lambda l, j: (l, 0, j)),
            pl.BlockSpec((1, 1, tn), lambda l, j: (l, 0, j)),
        ],
        out_specs=pl.BlockSpec((1, bsz, tn), lambda l, j: (l, 0, j)),
        out_shape=jax.ShapeDtypeStruct((depth, bsz, n6), F32),
        compiler_params=_cparams(("arbitrary", "arbitrary")),
        name="modulation",
    )(c, w_mod, b_mod.reshape(depth, 1, n6))


def _modulated_norm(x, g, sc, sh):
    y = x * lax.rsqrt(jnp.mean(x * x, axis=-1, keepdims=True) + EPS)
    return (y * g) * (1.0 + sc) + sh


def _norm_kernel(x_ref, g_ref, sc_ref, sh_ref, u_ref):
    u_ref[0] = _modulated_norm(x_ref[0], g_ref[...], sc_ref[0], sh_ref[0]).astype(BF16)


def _norm(x, g, sc, sh, tm):
    bsz, s, d = x.shape
    tok = lambda b, i: (b, i, 0)
    per_batch = lambda b, i: (b, 0, 0)
    return pl.pallas_call(
        _norm_kernel,
        grid=(bsz, s // tm),
        in_specs=[pl.BlockSpec((1, tm, d), tok), pl.BlockSpec((1, d), lambda b, i: (0, 0)),
                  pl.BlockSpec((1, 1, d), per_batch), pl.BlockSpec((1, 1, d), per_batch)],
        out_specs=pl.BlockSpec((1, tm, d), tok),
        out_shape=jax.ShapeDtypeStruct((bsz, s, d), BF16),
        compiler_params=_cparams(("arbitrary", "arbitrary")),
        name="norm",
    )(x, g.reshape(1, d), sc, sh)


def _proj_kernel(*refs, mode):
    if mode == "rope":
        u_ref, cos_ref, sin_ref, w_ref, o_ref = refs
    else:
        u_ref, w_ref, o_ref = refs
    tm = u_ref.shape[1]
    for r0 in range(0, tm, PROJ_ROWS):
        rows = slice(r0, r0 + PROJ_ROWS)
        acc = jnp.dot(u_ref[0, rows, :], w_ref[...], preferred_element_type=F32)
        if mode == "rope":
            cos, sin = cos_ref[0, rows, :], sin_ref[0, rows, :]
            groups = [acc[:, c:c + LANES] for c in range(0, COL_TILE, LANES)]
            acc = jnp.concatenate(
                [g * cos + pltpu.roll(g, LANES // 2, axis=1) * sin for g in groups], axis=1)
        elif mode == "gate":
            acc = 1.0 / (1.0 + jnp.exp(-acc))
        o_ref[0, rows, :] = acc.astype(BF16)


def _project(u, w, rope_tables, mode, tm):
    bsz, s, d = u.shape
    ncols = w.shape[1]
    tok = lambda b, i, j: (b, i, 0)
    in_specs = [pl.BlockSpec((1, tm, d), tok)]
    in_specs += [pl.BlockSpec((1, tm, LANES), tok) for _ in rope_tables]
    in_specs += [pl.BlockSpec((d, COL_TILE), lambda b, i, j: (0, j))]
    return pl.pallas_call(
        functools.partial(_proj_kernel, mode=mode),
        grid=(bsz, s // tm, ncols // COL_TILE),
        in_specs=in_specs,
        out_specs=pl.BlockSpec((1, tm, COL_TILE), lambda b, i, j: (b, i, j)),
        out_shape=jax.ShapeDtypeStruct((bsz, s, ncols), BF16),
        compiler_params=_cparams(("arbitrary", "arbitrary", "arbitrary")),
        name="proj_" + mode,
    )(u, *rope_tables, w)


def _pair_layout(w):
    half = HEAD_DIM // 2
    col = jnp.arange(w.shape[1])
    base, r = (col // LANES) * LANES, col % LANES
    src = base + ((r // half) % 2) * HEAD_DIM + (r // HEAD_DIM) * half + r % half
    return w[:, src]


def _build_weights(w_in):
    sizes = (512, 512, 512, 512, 64, 64, 256, 64, 4, 512, 512, 512, 3072)
    parts, start = [], 0
    for sz in sizes:
        parts.append(w_in[:, start:start + sz])
        start += sz
    aq, ak, av, bq, bk, bv, iq, ik, iw, cq, ck, cv, gates = parts
    d = w_in.shape[0]
    qscale = HEAD_DIM ** -0.5 * LOG2E
    iw_scale = IDX_HEADS ** -0.5 * HEAD_DIM ** -0.5
    zeros = lambda n: jnp.zeros((d, n), w_in.dtype)
    w_rope = _pair_layout(jnp.concatenate([aq * qscale, ak, bq * qscale, iq, bk, bk, ik, ik], axis=1))
    w_plain = jnp.concatenate([av, cq * qscale, ck, cv,
                               bv, bv, iw * iw_scale, zeros(LANES - IDX_HEADS), zeros(256)], axis=1)
    return w_rope.astype(BF16), gates.astype(BF16), w_plain.astype(BF16)


def _rope_tables(positions):
    half = HEAD_DIM // 2
    inv = ROPE_THETA ** (-jnp.arange(half, dtype=F32) / half)
    ang = positions.astype(F32)[..., None] * inv
    cos, sin = jnp.cos(ang), jnp.sin(ang)
    cos_t = jnp.tile(cos, (1, 1, LANES // half))
    sin_t = jnp.concatenate([-sin, -sin, sin, sin], axis=-1)
    return cos_t, sin_t


def _split_pair(pair, interleaved):
    lane = lax.broadcasted_iota(jnp.int32, pair.shape, 1)
    first = ((lane // (HEAD_DIM // 2)) % 2 == 0) if interleaved else (lane < HEAD_DIM)
    zero = jnp.zeros_like(pair)
    return jnp.where(first, pair, zero), jnp.where(first, zero, pair)


def _qk(q, k):
    return lax.dot_general(q, k, (((1,), (1,)), ((), ())), preferred_element_type=F32)


FOLD_CHAINS = 4


def _fold_rows(x, op):
    parts = x.reshape(x.shape[0] // 8, 8, x.shape[1])
    chains = [functools.reduce(op, [parts[g] for g in range(c, parts.shape[0], FOLD_CHAINS)])
              for c in range(min(FOLD_CHAINS, parts.shape[0]))]
    return functools.reduce(op, chains)


def _flash_t_update(s, vt_blocks, m_scr, acc_scr):
    m_old = m_scr[...]
    m_new = jnp.maximum(m_old, jnp.max(_fold_rows(s, jnp.maximum), axis=0, keepdims=True))
    alpha = jnp.exp2(m_old - m_new)
    pb = jnp.exp2(s - m_new).astype(BF16)
    m_scr[...] = m_new
    pv = functools.reduce(lambda a, b: a + b, [
        jnp.dot(vt, pb[k0:k1], preferred_element_type=F32) for (k0, k1), vt in vt_blocks])
    acc_scr[...] = alpha * acc_scr[...] + pv


ONES_ROWS = 16


def _with_ones_rows(vt):
    shape = vt.shape[:-2]
    return jnp.concatenate([vt, jnp.ones(shape + (1, vt.shape[-1]), vt.dtype),
                            jnp.zeros(shape + (ONES_ROWS - 1, vt.shape[-1]), vt.dtype)], axis=-2)


def _flash_update(s, v, m_scr, l_scr, acc_scr):
    groups = [s[:, c:c + LANES] for c in range(0, s.shape[1], LANES)]
    m_old = m_scr[...]
    lane_max = functools.reduce(jnp.maximum, groups)
    m_new = jnp.maximum(m_old, jnp.max(lane_max, axis=-1, keepdims=True))
    alpha = jnp.exp2(m_old - m_new)
    p_groups = [jnp.exp2(g - m_new) for g in groups]
    l_scr[...] = alpha * l_scr[...] + functools.reduce(lambda a, b: a + b, p_groups)
    m_scr[...] = m_new
    pb = jnp.concatenate([g.astype(BF16) for g in p_groups], axis=1)
    acc_scr[...] = alpha * acc_scr[...] + jnp.dot(pb, v, preferred_element_type=F32)


def _diff_attn_kernel(lam_ref, q_ref, k_ref, v_ref, ng_ref, o_ref, m_scr, l_scr, acc_scr,
                      *, tq, tk, lam_init):
    i = pl.program_id(2)
    per_tile = tq // tk
    qs = jnp.concatenate(_split_pair(q_ref[0], True), axis=0)
    m_scr[...] = jnp.full(m_scr.shape, NEG, F32)
    l_scr[...] = jnp.zeros(l_scr.shape, F32)
    acc_scr[...] = jnp.zeros(acc_scr.shape, F32)

    def kv_block(j):
        start = pl.multiple_of(j * tk, tk)
        return k_ref[0, pl.ds(start, tk), :], v_ref[0, pl.ds(start, tk), :]

    def body(j, carry):
        k, v = kv_block(j)
        _flash_update(_qk(qs, k), v, m_scr, l_scr, acc_scr)
        return carry

    lax.fori_loop(0, i * per_tile, body, 0)
    for d in range(per_tile):
        k, v = kv_block(i * per_tile + d)
        s = _qk(qs, k)
        row = lax.broadcasted_iota(jnp.int32, s.shape, 0) % tq
        col = lax.broadcasted_iota(jnp.int32, s.shape, 1) + d * tk
        s = jnp.where((col // CHUNK) <= (row // CHUNK), s, NEG)
        _flash_update(s, v, m_scr, l_scr, acc_scr)

    o = acc_scr[...] / jnp.sum(l_scr[...], axis=-1, keepdims=True)
    o = o[:tq] - lam_ref[0] * o[tq:]
    o = o * lax.rsqrt(jnp.mean(o * o, axis=-1, keepdims=True) + EPS)
    o_ref[0] = ((o * ng_ref[...]) * (1.0 - lam_init)).astype(BF16)


def _diff_attention(rope_slab, plain_slab, lam, norm_g, lam_init, tq, tk):
    bsz, s, _ = rope_slab.shape
    assert tq % tk == 0
    kern = functools.partial(_diff_attn_kernel, tq=tq, tk=tk, lam_init=lam_init)
    cb = COL_TILE // LANES
    return pl.pallas_call(
        kern,
        grid=(bsz, A_HEADS, s // tq),
        in_specs=[
            pl.BlockSpec(memory_space=pltpu.SMEM),
            pl.BlockSpec((1, tq, LANES), lambda b, h, i: (b, i, R_AQ * cb + h)),
            pl.BlockSpec((1, s, LANES), lambda b, h, i: (b, 0, R_AK * cb + h)),
            pl.BlockSpec((1, s, LANES), lambda b, h, i: (b, 0, P_AV * cb + h)),
            pl.BlockSpec((1, LANES), lambda b, h, i: (0, 0)),
        ],
        out_specs=pl.BlockSpec((1, tq, LANES), lambda b, h, i: (b, i, h)),
        out_shape=jax.ShapeDtypeStruct((bsz, s, A_HEADS * LANES), BF16),
        scratch_shapes=[pltpu.VMEM((2 * tq, LANES), F32), pltpu.VMEM((2 * tq, LANES), F32),
                        pltpu.VMEM((2 * tq, LANES), F32)],
        compiler_params=_cparams(("arbitrary", "arbitrary", "arbitrary")),
        name="diff_attn",
    )(lam.reshape(1), rope_slab, rope_slab, plain_slab, norm_g.reshape(1, LANES))


KEY_NEG_INF = -2139095040
KEY_POS_INF = 2139095040
SEARCH_MAX_STEPS = 80
NO_TIE_LIMIT = 1e9
VT_ROWS = HEAD_DIM + ONES_ROWS


def _key_to_float(key):
    bits = jnp.where(key >= 0, key, (key - 1) ^ jnp.int32(0x7FFFFFFF))
    return pltpu.bitcast(bits, F32)


def _float_to_key(t):
    bits = pltpu.bitcast(t, jnp.int32)
    return jnp.where(bits >= 0, bits, (bits ^ jnp.int32(0x7FFFFFFF)) + 1)


def _dsa_kernel(q_ref, iq_ref, iw_ref, k_ref, ik_ref, vt_ref, o_ref,
                sc_scr, qs_scr, m_scr, acc_scr, *, tq, topk):
    i = pl.program_id(1)
    nblk = i + 1
    npair = (nblk + 1) // 2
    nq = B_HEADS * tq

    iq = iq_ref[0]
    parts = []
    for p in range(IDX_HEADS // 2):
        parts += list(_split_pair(iq[:, p * LANES:(p + 1) * LANES], True))
    iqs = jnp.concatenate(parts, axis=0)
    iw_t = iw_ref[0].astype(F32).T

    def index_block(j):
        start = pl.multiple_of(j * tq, tq)
        ik = ik_ref[0, pl.ds(start, tq), :]
        score = None
        for h in range(IDX_HEADS):
            hs = jnp.maximum(_qk(ik, iqs[h * tq:(h + 1) * tq]), 0.0)
            score = iw_t[h:h + 1] * hs if score is None else score + iw_t[h:h + 1] * hs
        return score

    def score_slot(j):
        return j // 2, pl.ds(pl.multiple_of((j % 2) * tq, tq), tq)

    def p1_body(j, carry):
        pair, rows = score_slot(j)
        sc_scr[pair, rows, :] = index_block(j)
        return carry

    lax.fori_loop(0, i, p1_body, 0)
    score = index_block(i)
    key_i = lax.broadcasted_iota(jnp.int32, score.shape, 0)
    qry_i = lax.broadcasted_iota(jnp.int32, score.shape, 1)
    pair, rows = score_slot(i)
    sc_scr[pair, rows, :] = jnp.where((key_i // CHUNK) <= (qry_i // CHUNK), score, -jnp.inf)

    @pl.when(nblk % 2 == 1)
    def _():
        sc_scr[npair - 1, tq:2 * tq, :] = jnp.full((tq, tq), -jnp.inf, F32)

    kf = float(topk)

    def scan(hit_fn, ext_fn, ext_op, ext_init):
        def body(jj, carry):
            cnt, ext = carry
            sblk = sc_scr[jj]
            cnt = cnt + _fold_rows(jnp.where(hit_fn(sblk), 1.0, 0.0), lambda a, b: a + b)
            if ext_fn is not None:
                ext = ext_op(ext, _fold_rows(ext_fn(sblk), ext_op))
            return cnt, ext
        cnt, ext = lax.fori_loop(0, npair, body, (jnp.zeros((8, tq), F32),
                                                  jnp.full((8, tq), ext_init, F32)))
        return jnp.sum(cnt, axis=0, keepdims=True), ext

    def count_ge(t):
        return scan(lambda sblk: sblk >= t, None, None, 0.0)[0]

    c_ge0, ext = scan(lambda sblk: sblk >= 0.0, lambda sblk: sblk, jnp.maximum, -jnp.inf)
    col_max = jnp.max(ext, axis=0, keepdims=True)
    c_gt0, ext = scan(lambda sblk: sblk > 0.0,
                      lambda sblk: jnp.where(sblk > -jnp.inf, sblk, jnp.inf), jnp.minimum, jnp.inf)
    col_min = jnp.min(ext, axis=0, keepdims=True)
    qpos = i * tq + lax.broadcasted_iota(jnp.int32, (1, tq), 1)
    n_valid = (((qpos // CHUNK) + 1) * CHUNK).astype(F32)
    open_q = n_valid < kf
    above = c_gt0 >= kf
    below = c_ge0 < kf
    ikey = lambda v: jnp.full((1, tq), v, jnp.int32)
    lo_k = jnp.where(below, _float_to_key(col_min), ikey(0))
    hi_k = jnp.where(above, _float_to_key(col_max) + 1, jnp.where(below, ikey(0), ikey(1)))
    c_lo = jnp.where(below, n_valid, c_ge0)
    c_hi = jnp.where(above, 0.0, jnp.where(below, c_ge0, c_gt0))
    done = jnp.logical_or(open_q, jnp.logical_not(jnp.logical_or(above, below)))
    state = (jnp.int32(0), lo_k, hi_k, c_lo, c_hi, done.astype(jnp.int32), ikey(0))

    def search_cond(st):
        return jnp.logical_and(st[0] < SEARCH_MAX_STEPS, jnp.min(st[5]) == 0)

    def search_step(st):
        step, lo_k, hi_k, c_lo, c_hi, done, force_bisect = st
        finite = jnp.logical_and(lo_k > KEY_NEG_INF, hi_k < KEY_POS_INF)
        t_lo, t_hi = _key_to_float(lo_k), _key_to_float(hi_k)
        log_lo = jnp.log(c_lo)
        frac = (log_lo - math.log(kf - 0.5)) / (log_lo - jnp.log(jnp.maximum(c_hi, 0.5)))
        k_interp = _float_to_key(t_lo + (t_hi - t_lo) * frac)
        k_mid = (lo_k & hi_k) + ((lo_k ^ hi_k) >> 1)
        interp = jnp.logical_and(finite, force_bisect == 0)
        k = jnp.where(interp, k_interp, k_mid)
        k = jnp.minimum(jnp.maximum(k, lo_k + 1), hi_k - 1)
        c = count_ge(_key_to_float(k))
        ok = c >= kf
        live = done == 0
        new_lo = jnp.where(jnp.logical_and(live, ok), k, lo_k)
        new_hi = jnp.where(jnp.logical_and(live, jnp.logical_not(ok)), k, hi_k)
        new_c_lo = jnp.where(jnp.logical_and(live, ok), c, c_lo)
        new_c_hi = jnp.where(jnp.logical_and(live, jnp.logical_not(ok)), c, c_hi)
        width = lambda a, b: b.astype(F32) - a.astype(F32)
        slow = width(new_lo, new_hi) > 0.5 * width(lo_k, hi_k)
        new_force = jnp.logical_and(interp, slow).astype(jnp.int32)
        finished = jnp.logical_or(new_c_lo == kf, new_hi - new_lo == 1)
        new_done = jnp.maximum(done, finished.astype(jnp.int32))
        return step + 1, new_lo, new_hi, new_c_lo, new_c_hi, new_done, new_force

    _, lo_k, hi_k, c_lo, c_hi, _, _ = lax.while_loop(search_cond, search_step, state)
    tau = jnp.where(open_q, -jnp.inf, _key_to_float(lo_k))
    need = jnp.where(open_q, 0.0, jnp.where(c_lo == kf, NO_TIE_LIMIT, kf - c_hi))

    q = q_ref[0]
    for p in range(B_HEADS // 2):
        even, odd = _split_pair(q[:, p * LANES:(p + 1) * LANES], True)
        qs_scr[p * tq:(p + 1) * tq, :] = even
        qs_scr[(B_HEADS // 2 + p) * tq:(B_HEADS // 2 + p + 1) * tq, :] = odd
    m_scr[...] = jnp.full(m_scr.shape, NEG, F32)
    acc_scr[...] = jnp.zeros(acc_scr.shape, F32)
    key_i = lax.broadcasted_iota(jnp.int32, (tq, tq), 0)
    lower = (lax.broadcasted_iota(jnp.int32, (tq, tq), 1) <= key_i).astype(BF16)

    def p3_body(jj, tie_count):
        start = pl.multiple_of(jj * 2 * tq, 2 * tq)
        sblk = sc_scr[jj]
        tie = sblk == tau
        tie01 = jnp.where(tie, 1.0, 0.0).astype(BF16)
        rank_a = tie_count + jnp.dot(lower, tie01[0:tq], preferred_element_type=F32)
        rank_b = rank_a[tq - 1:tq, :] + jnp.dot(lower, tie01[tq:2 * tq], preferred_element_type=F32)
        rank = jnp.concatenate([rank_a, rank_b], axis=0)
        keep_tie = jnp.where(rank <= need, 0.0, NEG)
        bias = jnp.where(sblk > tau, 0.0, jnp.where(tie, keep_tie, NEG))
        s = _qk(k_ref[0, pl.ds(start, 2 * tq), :], qs_scr[...])
        s = jnp.concatenate([s[:, h * tq:(h + 1) * tq] + bias for h in range(B_HEADS)], axis=1)
        _flash_t_update(s, [((0, tq), vt_ref[0, 2 * jj]), ((tq, 2 * tq), vt_ref[0, 2 * jj + 1])],
                        m_scr, acc_scr)
        return rank[2 * tq - 1:2 * tq, :]

    lax.fori_loop(0, npair, p3_body, jnp.zeros((1, tq), F32))
    acc = acc_scr[...]
    o = acc[0:HEAD_DIM] / acc[HEAD_DIM:HEAD_DIM + 1]
    half_cols = (B_HEADS // 2) * tq
    for p in range(B_HEADS // 2):
        pair = jnp.concatenate([o[:, p * tq:(p + 1) * tq],
                                o[:, half_cols + p * tq:half_cols + (p + 1) * tq]], axis=0)
        o_ref[0, :, p * LANES:(p + 1) * LANES] = pair.T.astype(BF16)


def _dsa_attention(rope_slab, plain_slab, tq):
    bsz, s, _ = rope_slab.shape
    topk = min(TOPK_MAX, s // 4)
    nblk = s // tq
    assert tq >= topk and s % tq == 0 and nblk % 2 == 0
    kern = functools.partial(_dsa_kernel, tq=tq, topk=topk)
    idx0 = R_IDX * COL_TILE
    misc0 = P_MISC * COL_TILE
    width = B_HEADS * HEAD_DIM
    vt = _with_ones_rows(
        plain_slab[:, :, misc0:misc0 + HEAD_DIM].reshape(bsz, nblk, tq, HEAD_DIM).transpose(0, 1, 3, 2))
    return pl.pallas_call(
        kern,
        grid=(bsz, s // tq),
        in_specs=[
            pl.BlockSpec((1, tq, COL_TILE), lambda b, i: (b, i, R_BQ)),
            pl.BlockSpec((1, tq, 256), lambda b, i: (b, i, idx0 // 256)),
            pl.BlockSpec((1, tq, LANES), lambda b, i: (b, i, (misc0 + LANES) // LANES)),
            pl.BlockSpec((1, s, LANES), lambda b, i: (b, 0, (idx0 + 256) // LANES)),
            pl.BlockSpec((1, s, LANES), lambda b, i: (b, 0, (idx0 + 384) // LANES)),
            pl.BlockSpec((1, nblk, VT_ROWS, tq), lambda b, i: (b, 0, 0, 0)),
        ],
        out_specs=pl.BlockSpec((1, tq, width), lambda b, i: (b, i, 0)),
        out_shape=jax.ShapeDtypeStruct((bsz, s, width), BF16),
        scratch_shapes=[
            pltpu.VMEM((nblk // 2, 2 * tq, tq), F32),
            pltpu.VMEM((B_HEADS * tq, LANES), BF16),
            pltpu.VMEM((1, B_HEADS * tq), F32),
            pltpu.VMEM((VT_ROWS, B_HEADS * tq), F32),
        ],
        compiler_params=_cparams(("arbitrary", "arbitrary")),
        name="dsa_attn",
    )(rope_slab, rope_slab, plain_slab, rope_slab, rope_slab, vt)


BAND_TQ = 2 * CHUNK
BAND_KEYS = (C_LEFT_CHUNKS + BAND_TQ // CHUNK) * CHUNK
BAND_PAD = C_LEFT_CHUNKS * CHUNK


def _band_kernel(q_ref, k_ref, v_ref, bias_ref, o_ref):
    i = pl.program_id(1)
    tq = BAND_TQ
    start = pl.multiple_of(i * tq, tq)
    col = lax.broadcasted_iota(jnp.int32, (2 * tq, BAND_KEYS), 1)
    key_ok = col + start >= BAND_PAD
    for p in range(C_HEADS // 2):
        lanes = slice(p * LANES, (p + 1) * LANES)
        k = k_ref[0, pl.ds(start, BAND_KEYS), lanes]
        v = v_ref[0, pl.ds(start, BAND_KEYS), lanes]
        qs = jnp.concatenate(_split_pair(q_ref[0, :, lanes], False), axis=0)
        s = _qk(qs, k) + bias_ref[2 * p:2 * p + 2].reshape(2 * tq, BAND_KEYS)
        s = jnp.where(key_ok, s, NEG)
        m = jnp.max(s, axis=-1, keepdims=True)
        e = jnp.exp2(s - m)
        l = jnp.sum(e, axis=-1, keepdims=True)
        eb = e.astype(BF16)
        v_even, v_odd = _split_pair(v, False)
        o = (jnp.dot(eb[:tq], v_even, preferred_element_type=F32) / l[:tq]
             + jnp.dot(eb[tq:], v_odd, preferred_element_type=F32) / l[tq:])
        o_ref[0, :, lanes] = o.astype(BF16)


def _band_bias(rel_bias):
    r = jnp.arange(BAND_TQ)[:, None]
    cidx = jnp.arange(BAND_KEYS)[None, :]
    n_diag = BAND_TQ + BAND_KEYS - 1
    rel = BAND_PAD - (BAND_KEYS - 1) + jnp.arange(n_diag)
    g = rel_bias.astype(F32)[:, jnp.clip(rel, -REL_CLIP, REL_CLIP) + REL_CLIP] * LOG2E
    skew = jnp.tile(g, (1, BAND_TQ + 2))[:, :BAND_TQ * (n_diag + 1)]
    bias = skew.reshape(-1, BAND_TQ, n_diag + 1)[:, :, :BAND_KEYS][:, :, ::-1]
    dchunk = (r // CHUNK + C_LEFT_CHUNKS) - cidx // CHUNK
    in_band = jnp.logical_and(dchunk >= 0, dchunk <= C_LEFT_CHUNKS)
    return jnp.where(in_band[None], bias, NEG)


def _band_attention(plain_slab, kp, vp, bias):
    bsz, s, _ = plain_slab.shape
    tq = BAND_TQ
    sp = kp.shape[1]
    width = C_HEADS * HEAD_DIM
    return pl.pallas_call(
        _band_kernel,
        grid=(bsz, s // tq),
        in_specs=[
            pl.BlockSpec((1, tq, width), lambda b, i: (b, i, P_CQ)),
            pl.BlockSpec((1, sp, width), lambda b, i: (b, 0, 0)),
            pl.BlockSpec((1, sp, width), lambda b, i: (b, 0, 0)),
            pl.BlockSpec((C_HEADS, tq, BAND_KEYS), lambda b, i: (0, 0, 0)),
        ],
        out_specs=pl.BlockSpec((1, tq, width), lambda b, i: (b, i, 0)),
        out_shape=jax.ShapeDtypeStruct((bsz, s, width), BF16),
        compiler_params=_cparams(("arbitrary", "arbitrary")),
        name="band_attn",
    )(plain_slab, kp, vp, bias)


def _merge_kernel(x_ref, g1_ref, ya_ref, yb_ref, yc_ref, ga_ref, gb_ref, gc_ref,
                  wa_ref, wb_ref, wc_ref, wo_ref, o_ref):
    for r0 in range(0, x_ref.shape[1], PROJ_ROWS):
        rows = slice(r0, r0 + PROJ_ROWS)

        def branch(y_ref, w_ref, gate_ref):
            return gate_ref[0, rows, :].astype(F32) * jnp.dot(y_ref[0, rows, :], w_ref[...],
                                                             preferred_element_type=F32)

        merged = (branch(ya_ref, wa_ref, ga_ref) + branch(yb_ref, wb_ref, gb_ref)
                  + branch(yc_ref, wc_ref, gc_ref))
        mixed = jnp.dot(merged.astype(BF16), wo_ref[...], preferred_element_type=F32)
        o_ref[0, rows, :] = x_ref[0, rows, :] + g1_ref[0] * mixed


def _merge(x, g1, ya, yb, yc, gate_slab, wa, wb, wc, wo, tm):
    bsz, s, d = x.shape
    tok = lambda b, i: (b, i, 0)
    full = lambda b, i: (0, 0)
    return pl.pallas_call(
        _merge_kernel,
        grid=(bsz, s // tm),
        in_specs=[
            pl.BlockSpec((1, tm, d), tok),
            pl.BlockSpec((1, 1, d), lambda b, i: (b, 0, 0)),
            pl.BlockSpec((1, tm, ya.shape[2]), tok),
            pl.BlockSpec((1, tm, yb.shape[2]), tok),
            pl.BlockSpec((1, tm, yc.shape[2]), tok),
            pl.BlockSpec((1, tm, d), lambda b, i: (b, i, 0)),
            pl.BlockSpec((1, tm, d), lambda b, i: (b, i, 1)),
            pl.BlockSpec((1, tm, d), lambda b, i: (b, i, 2)),
            pl.BlockSpec(wa.shape, full),
            pl.BlockSpec(wb.shape, full),
            pl.BlockSpec(wc.shape, full),
            pl.BlockSpec(wo.shape, full),
        ],
        out_specs=pl.BlockSpec((1, tm, d), tok),
        out_shape=jax.ShapeDtypeStruct((bsz, s, d), F32),
        compiler_params=_cparams(("arbitrary", "arbitrary")),
        name="merge",
    )(x, g1, ya, yb, yc, gate_slab, gate_slab, gate_slab, wa, wb, wc, wo)


def _router_kernel(x_ref, g_ref, sc_ref, sh_ref, rw_ref, rb_ref, u_ref, comb_ref):
    u = _modulated_norm(x_ref[0], g_ref[...], sc_ref[0], sh_ref[0])
    u_ref[0] = u.astype(BF16)
    logits = lax.dot_general(rw_ref[...], u, (((1,), (1,)), ((), ())),
                             preferred_element_type=F32, precision=lax.Precision.HIGHEST)
    aff = jax.nn.sigmoid(logits)
    sel = aff + rb_ref[...]
    rows = [sel[e:e + 1] for e in range(N_EXPERTS)]
    gscore = []
    for g in range(N_GROUPS):
        r = rows[g * EXPERTS_PER_GROUP:(g + 1) * EXPERTS_PER_GROUP]
        best = None
        for a in range(EXPERTS_PER_GROUP):
            for b in range(a + 1, EXPERTS_PER_GROUP):
                pair = r[a] + r[b]
                best = pair if best is None else jnp.maximum(best, pair)
        gscore.append(best)
    gmax = functools.reduce(jnp.maximum, gscore)
    taken = jnp.zeros_like(gmax) > 1.0
    in_best = []
    for g in range(N_GROUPS):
        is_g = jnp.logical_and(gscore[g] == gmax, jnp.logical_not(taken))
        in_best.append(is_g)
        taken = jnp.logical_or(taken, is_g)
    keep = []
    for e in range(N_EXPERTS):
        g = e // EXPERTS_PER_GROUP
        rank = jnp.zeros_like(gmax)
        for o in range(g * EXPERTS_PER_GROUP, (g + 1) * EXPERTS_PER_GROUP):
            if o == e:
                continue
            ahead = rows[o] > rows[e] if o > e else rows[o] >= rows[e]
            rank = rank + jnp.where(ahead, 1.0, 0.0)
        keep.append(jnp.logical_and(in_best[g], rank < 2.0))
    w = [jnp.where(keep[e], aff[e:e + 1], 0.0) for e in range(N_EXPERTS)]
    total = functools.reduce(lambda a, b: a + b, w)
    comb = jnp.concatenate([we / total for we in w]
                           + [jnp.zeros((LANES - N_EXPERTS, total.shape[1]), F32)], axis=0)
    comb_ref[0] = comb.T


def _router(x, g, sc, sh, router_w, router_b, tm):
    bsz, s, d = x.shape
    tok = lambda b, i: (b, i, 0)
    return pl.pallas_call(
        _router_kernel,
        grid=(bsz, s // tm),
        in_specs=[
            pl.BlockSpec((1, tm, d), tok),
            pl.BlockSpec((1, d), lambda b, i: (0, 0)),
            pl.BlockSpec((1, 1, d), lambda b, i: (b, 0, 0)),
            pl.BlockSpec((1, 1, d), lambda b, i: (b, 0, 0)),
            pl.BlockSpec((N_EXPERTS, d), lambda b, i: (0, 0)),
            pl.BlockSpec((N_EXPERTS, 1), lambda b, i: (0, 0)),
        ],
        out_specs=[pl.BlockSpec((1, tm, d), tok), pl.BlockSpec((1, tm, LANES), tok)],
        out_shape=[jax.ShapeDtypeStruct((bsz, s, d), BF16),
                   jax.ShapeDtypeStruct((bsz, s, LANES), F32)],
        compiler_params=_cparams(("arbitrary", "arbitrary")),
        name="router",
    )(x, g.reshape(1, d), sc, sh, router_w.T, router_b.reshape(N_EXPERTS, 1))


MOE_SUB = 512
MOE_CAP = 128


def _expert_ffn(rows, w1_ref, w3_ref, w2_ref):
    h1 = jnp.dot(rows, w1_ref[0], preferred_element_type=F32)
    h3 = jnp.dot(rows, w3_ref[0], preferred_element_type=F32)
    h = (h1 * jax.nn.sigmoid(h1)) * h3
    return jnp.dot(h.astype(BF16), w2_ref[0], preferred_element_type=F32)


def _moe_kernel(x_ref, g2_ref, u_ref, comb_ref, w1_ref, w3_ref, w2_ref, o_ref,
                acc_scr, pos_scr, pos_t_scr, comb_t_scr, scatter_scr, y_scr):
    e = pl.program_id(2)
    tm = u_ref.shape[1]
    sub = min(MOE_SUB, tm)
    subs = [slice(r0, r0 + sub) for r0 in range(0, tm, sub)]

    @pl.when(e == 0)
    def _():
        acc_scr[...] = jnp.zeros(acc_scr.shape, F32)
        before = (lax.broadcasted_iota(jnp.int32, (sub, sub), 1)
                  < lax.broadcasted_iota(jnp.int32, (sub, sub), 0)).astype(BF16)
        for rows in subs:
            comb0 = comb_ref[0, rows, :]
            member = jnp.where(comb0 > 0.0, 1.0, 0.0).astype(BF16)
            pos = jnp.dot(before, member, preferred_element_type=F32)
            pos_scr[rows, :] = pos
            pos_t_scr[:, rows] = pos.T
            comb_t_scr[:, rows] = comb0.T

    comb = comb_ref[0]
    lane = lax.broadcasted_iota(jnp.int32, comb.shape, 1)
    ce = jnp.sum(jnp.where(lane == e, comb, 0.0), axis=-1, keepdims=True)
    pe = jnp.sum(jnp.where(lane == e, pos_scr[...], 0.0), axis=-1, keepdims=True)
    member_e = jnp.where(ce > 0.0, 1.0, 0.0)
    fullest = functools.reduce(jnp.maximum, [jnp.sum(member_e[rows]) for rows in subs])
    routed = fullest <= float(MOE_CAP)
    slots = pl.ds(pl.multiple_of(e * MOE_CAP, MOE_CAP), MOE_CAP)

    @pl.when(routed)
    def _():
        slot_l = lax.broadcasted_iota(jnp.int32, (1, MOE_CAP), 1).astype(F32)
        scatter_scr[e] = jnp.where(jnp.logical_and(ce > 0.0, pe == slot_l), 1.0, 0.0).astype(BF16)
        slot_s = lax.broadcasted_iota(jnp.int32, (MOE_CAP, 1), 0).astype(F32)
        packed, weights = [], []
        for rows in subs:
            ce_t = comb_t_scr[pl.ds(e, 1), rows]
            pe_t = pos_t_scr[pl.ds(e, 1), rows]
            hit = jnp.logical_and(ce_t > 0.0, pe_t == slot_s)
            weights.append(jnp.sum(jnp.where(hit, ce_t, 0.0), axis=-1, keepdims=True))
            packed.append(jnp.dot(jnp.where(hit, 1.0, 0.0).astype(BF16), u_ref[0, rows, :],
                                  preferred_element_type=F32).astype(BF16))
        y = jnp.concatenate(weights, axis=0) * _expert_ffn(jnp.concatenate(packed, axis=0),
                                                           w1_ref, w3_ref, w2_ref)
        for k in range(len(subs)):
            y_scr[k, slots, :] = y[k * MOE_CAP:(k + 1) * MOE_CAP].astype(BF16)

    @pl.when(jnp.logical_not(routed))
    def _():
        scatter_scr[e] = jnp.zeros(scatter_scr.shape[1:], BF16)
        for k in range(len(subs)):
            y_scr[k, slots, :] = jnp.zeros((MOE_CAP, y_scr.shape[2]), BF16)
        acc_scr[...] += ce * _expert_ffn(u_ref[0], w1_ref, w3_ref, w2_ref)

    @pl.when(e == pl.num_programs(2) - 1)
    def _():
        for k, rows in enumerate(subs):
            scatter = jnp.concatenate([scatter_scr[j, rows, :] for j in range(N_EXPERTS)], axis=1)
            y = acc_scr[rows, :] + jnp.dot(scatter, y_scr[k], preferred_element_type=F32)
            o_ref[0, rows, :] = x_ref[0, rows, :] + g2_ref[0] * y


def _moe(x, g2, u, comb, w1, w3, w2, tm):
    bsz, s, d = x.shape
    ne, _, dff = w1.shape
    assert ne == N_EXPERTS
    tok = lambda b, i, e: (b, i, 0)
    return pl.pallas_call(
        _moe_kernel,
        grid=(bsz, s // tm, ne),
        in_specs=[
            pl.BlockSpec((1, tm, d), tok),
            pl.BlockSpec((1, 1, d), lambda b, i, e: (b, 0, 0)),
            pl.BlockSpec((1, tm, d), tok),
            pl.BlockSpec((1, tm, LANES), tok),
            pl.BlockSpec((1, d, dff), lambda b, i, e: (e, 0, 0)),
            pl.BlockSpec((1, d, dff), lambda b, i, e: (e, 0, 0)),
            pl.BlockSpec((1, dff, d), lambda b, i, e: (e, 0, 0)),
        ],
        out_specs=pl.BlockSpec((1, tm, d), tok),
        out_shape=jax.ShapeDtypeStruct((bsz, s, d), F32),
        scratch_shapes=[
            pltpu.VMEM((tm, d), F32),
            pltpu.VMEM((tm, LANES), F32),
            pltpu.VMEM((LANES, tm), F32),
            pltpu.VMEM((LANES, tm), F32),
            pltpu.VMEM((ne, tm, MOE_CAP), BF16),
            pltpu.VMEM((pl.cdiv(tm, MOE_SUB), ne * MOE_CAP, d), BF16),
        ],
        compiler_params=_cparams(("arbitrary", "arbitrary", "arbitrary")),
        name="moe",
    )(x, g2, u, comb, w1, w3, w2)


def _final_norm_kernel(x_ref, g_ref, o_ref):
    x = x_ref[0]
    o_ref[0] = (x * lax.rsqrt(jnp.mean(x * x, axis=-1, keepdims=True) + EPS)) * g_ref[...]


def _final_norm(x, g, tm):
    bsz, s, d = x.shape
    tok = lambda b, i: (b, i, 0)
    return pl.pallas_call(
        _final_norm_kernel,
        grid=(bsz, s // tm),
        in_specs=[pl.BlockSpec((1, tm, d), tok), pl.BlockSpec((1, d), lambda b, i: (0, 0))],
        out_specs=pl.BlockSpec((1, tm, d), tok),
        out_shape=jax.ShapeDtypeStruct((bsz, s, d), F32),
        compiler_params=_cparams(("arbitrary", "arbitrary")),
        name="final_norm",
    )(x, g.reshape(1, d))


def _tile(s, want):
    t = min(want, s)
    assert s % t == 0
    return t


def kernel(x, c, positions, norm1_g, norm2_g, w_mod, b_mod, w_in, lambda_q1, lambda_k1, lambda_q2, lambda_k2, a_norm_g, c_rel_bias, w_branch_a, w_branch_b, w_branch_c, w_out, router_w, router_b, exp_w1, exp_w3, exp_w2, final_g):
    bsz, s, d = x.shape
    depth = w_mod.shape[0]
    tm = _tile(s, 1024)

    mod = _modulation(c, w_mod, b_mod)
    rope_tables = _rope_tables(positions)

    for layer in range(depth):
        lam_init = 0.8 - 0.6 * math.exp(-0.3 * layer)
        sh1, sc1, g1, sh2, sc2, g2 = [m[:, None, :] for m in jnp.split(mod[layer], 6, axis=-1)]
        lam = (jnp.exp(jnp.sum(lambda_q1[layer] * lambda_k1[layer]))
               - jnp.exp(jnp.sum(lambda_q2[layer] * lambda_k2[layer])) + lam_init)

        w_rope, w_gate, w_plain = _build_weights(w_in[layer])
        u = _norm(x, norm1_g[layer], sc1, sh1, tm)
        rope_slab = _project(u, w_rope, rope_tables, "rope", tm)
        gate_slab = _project(u, w_gate, (), "gate", tm)
        plain_slab = _project(u, w_plain, (), "plain", tm)

        ya = _diff_attention(rope_slab, plain_slab, lam, a_norm_g[layer], lam_init,
                             _tile(s, 1024), _tile(s, 512))
        yb = _dsa_attention(rope_slab, plain_slab, _tile(s, 256))
        pad = ((0, 0), (BAND_PAD, 0), (0, 0))
        kp = jnp.pad(plain_slab[:, :, P_CK * COL_TILE:(P_CK + 1) * COL_TILE], pad)
        vp = jnp.pad(plain_slab[:, :, P_CV * COL_TILE:(P_CV + 1) * COL_TILE], pad)
        yc = _band_attention(plain_slab, kp, vp, _band_bias(c_rel_bias[layer]))
        x = _merge(x, g1, ya, yb, yc, gate_slab,
                   w_branch_a[layer].astype(BF16), w_branch_b[layer].astype(BF16),
                   w_branch_c[layer].astype(BF16), w_out[layer].astype(BF16), _tile(s, 512))

        u, comb = _router(x, norm2_g[layer], sc2, sh2, router_w, router_b, _tile(s, 512))
        x = _moe(x, g2, u, comb, exp_w1[layer].astype(BF16), exp_w3[layer].astype(BF16),
                 exp_w2[layer].astype(BF16), tm)

    return _final_norm(x, final_g, tm)
```

```python
import functools
import math

import jax
import jax.numpy as jnp
from jax import lax
from jax.experimental import pallas as pl
from jax.experimental.pallas import tpu as pltpu

F32 = jnp.float32
BF16 = jnp.bfloat16

CHUNK = 64
ROPE_THETA = 10000.0
EPS = 1e-6
A_HEADS = 4
HEAD_DIM = 64
B_HEADS = 8
IDX_HEADS = 4
TOPK_MAX = 256
C_HEADS = 8
C_LEFT_CHUNKS = 8
REL_CLIP = 256
N_EXPERTS = 16
N_GROUPS = 4
EXPERTS_PER_GROUP = 4
N_BRANCHES = 3

LANES = 128
NEG = -1e30
LOG2E = math.log2(math.e)
VMEM_LIMIT = 56 * 1024 * 1024

COL_TILE = 512
PROJ_ROWS = 256
R_AQ, R_AK, R_BQ, R_IDX = 0, 1, 2, 3
P_AV, P_CQ, P_CK, P_CV, P_MISC = 0, 1, 2, 3, 4


def _cparams(sem):
    return pltpu.CompilerParams(dimension_semantics=sem, vmem_limit_bytes=VMEM_LIMIT)


def _mod_kernel(c_ref, w_ref, b_ref, o_ref):
    c = c_ref[...]
    ca = c * jax.nn.sigmoid(c)
    o_ref[0] = jnp.dot(ca, w_ref[0], preferred_element_type=F32) + b_ref[0]


def _modulation(c, w_mod, b_mod):
    depth, d, n6 = w_mod.shape
    bsz = c.shape[0]
    tn = 1024
    return pl.pallas_call(
        _mod_kernel,
        grid=(depth, n6 // tn),
        in_specs=[
            pl.BlockSpec((bsz, d), lambda l, j: (0, 0)),
            pl.BlockSpec((1, d, tn), lambda l, j: (l, 0, j)),
            pl.BlockSpec((1, 1, tn), lambda l, j: (l, 0, j)),
        ],
        out_specs=pl.BlockSpec((1, bsz, tn), lambda l, j: (l, 0, j)),
        out_shape=jax.ShapeDtypeStruct((depth, bsz, n6), F32),
        compiler_params=_cparams(("arbitrary", "arbitrary")),
        name="modulation",
    )(c, w_mod, b_mod.reshape(depth, 1, n6))


def _modulated_norm(x, g, sc, sh):
    y = x * lax.rsqrt(jnp.mean(x * x, axis=-1, keepdims=True) + EPS)
    return (y * g) * (1.0 + sc) + sh


def _norm_kernel(x_ref, g_ref, sc_ref, sh_ref, u_ref):
    u_ref[0] = _modulated_norm(x_ref[0], g_ref[...], sc_ref[0], sh_ref[0]).astype(BF16)


def _norm(x, g, sc, sh, tm):
    bsz, s, d = x.shape
    tok = lambda b, i: (b, i, 0)
    per_batch = lambda b, i: (b, 0, 0)
    return pl.pallas_call(
        _norm_kernel,
        grid=(bsz, s // tm),
        in_specs=[pl.BlockSpec((1, tm, d), tok), pl.BlockSpec((1, d), lambda b, i: (0, 0)),
                  pl.BlockSpec((1, 1, d), per_batch), pl.BlockSpec((1, 1, d), per_batch)],
        out_specs=pl.BlockSpec((1, tm, d), tok),
        out_shape=jax.ShapeDtypeStruct((bsz, s, d), BF16),
        compiler_params=_cparams(("arbitrary", "arbitrary")),
        name="norm",
    )(x, g.reshape(1, d), sc, sh)


def _proj_kernel(*refs, mode):
    if mode == "rope":
        u_ref, cos_ref, sin_ref, w_ref, o_ref = refs
    else:
        u_ref, w_ref, o_ref = refs
    tm = u_ref.shape[1]
    for r0 in range(0, tm, PROJ_ROWS):
        rows = slice(r0, r0 + PROJ_ROWS)
        acc = jnp.dot(u_ref[0, rows, :], w_ref[...], preferred_element_type=F32)
        if mode == "rope":
            cos, sin = cos_ref[0, rows, :], sin_ref[0, rows, :]
            groups = [acc[:, c:c + LANES] for c in range(0, COL_TILE, LANES)]
            acc = jnp.concatenate(
                [g * cos + pltpu.roll(g, LANES // 2, axis=1) * sin for g in groups], axis=1)
        elif mode == "gate":
            acc = 1.0 / (1.0 + jnp.exp(-acc))
        o_ref[0, rows, :] = acc.astype(BF16)


def _project(u, w, rope_tables, mode, tm):
    bsz, s, d = u.shape
    ncols = w.shape[1]
    tok = lambda b, i, j: (b, i, 0)
    in_specs = [pl.BlockSpec((1, tm, d), tok)]
    in_specs += [pl.BlockSpec((1, tm, LANES), tok) for _ in rope_tables]
    in_specs += [pl.BlockSpec((d, COL_TILE), lambda b, i, j: (0, j))]
    return pl.pallas_call(
        functools.partial(_proj_kernel, mode=mode),
        grid=(bsz, s // tm, ncols // COL_TILE),
        in_specs=in_specs,
        out_specs=pl.BlockSpec((1, tm, COL_TILE), lambda b, i, j: (b, i, j)),
        out_shape=jax.ShapeDtypeStruct((bsz, s, ncols), BF16),
        compiler_params=_cparams(("arbitrary", "arbitrary", "arbitrary")),
        name="proj_" + mode,
    )(u, *rope_tables, w)


def _pair_layout(w):
    half = HEAD_DIM // 2
    col = jnp.arange(w.shape[1])
    base, r = (col // LANES) * LANES, col % LANES
    src = base + ((r // half) % 2) * HEAD_DIM + (r // HEAD_DIM) * half + r % half
    return w[:, src]


def _build_weights(w_in):
    sizes = (512, 512, 512, 512, 64, 64, 256, 64, 4, 512, 512, 512, 3072)
    parts, start = [], 0
    for sz in sizes:
        parts.append(w_in[:, start:start + sz])
        start += sz
    aq, ak, av, bq, bk, bv, iq, ik, iw, cq, ck, cv, gates = parts
    d = w_in.shape[0]
    qscale = HEAD_DIM ** -0.5 * LOG2E
    iw_scale = IDX_HEADS ** -0.5 * HEAD_DIM ** -0.5
    zeros = lambda n: jnp.zeros((d, n), w_in.dtype)
    w_rope = _pair_layout(jnp.concatenate([aq * qscale, ak, bq * qscale, iq, bk, bk, ik, ik], axis=1))
    w_plain = jnp.concatenate([av, cq * qscale, ck, cv,
                               bv, bv, iw * iw_scale, zeros(LANES - IDX_HEADS),
                               zeros(COL_TILE - 2 * LANES)], axis=1)
    return w_rope.astype(BF16), gates.astype(BF16), w_plain.astype(BF16)


def _rope_tables(positions):
    half = HEAD_DIM // 2
    inv = ROPE_THETA ** (-jnp.arange(half, dtype=F32) / half)
    ang = positions.astype(F32)[..., None] * inv
    cos, sin = jnp.cos(ang), jnp.sin(ang)
    cos_t = jnp.tile(cos, (1, 1, LANES // half))
    sin_t = jnp.concatenate([-sin, -sin, sin, sin], axis=-1)
    return cos_t, sin_t


def _split_pair(pair, interleaved):
    lane = lax.broadcasted_iota(jnp.int32, pair.shape, 1)
    first = ((lane // (HEAD_DIM // 2)) % 2 == 0) if interleaved else (lane < HEAD_DIM)
    zero = jnp.zeros_like(pair)
    return jnp.where(first, pair, zero), jnp.where(first, zero, pair)


def _qk(q, k):
    return lax.dot_general(q, k, (((1,), (1,)), ((), ())), preferred_element_type=F32)


FOLD_CHAINS = 4


def _fold_rows(x, op):
    parts = x.reshape(x.shape[0] // 8, 8, x.shape[1])
    chains = [functools.reduce(op, [parts[g] for g in range(c, parts.shape[0], FOLD_CHAINS)])
              for c in range(min(FOLD_CHAINS, parts.shape[0]))]
    return functools.reduce(op, chains)


def _flash_t_update(s, vt_blocks, m_scr, acc_scr):
    m_old = m_scr[...]
    m_new = jnp.maximum(m_old, jnp.max(_fold_rows(s, jnp.maximum), axis=0, keepdims=True))
    alpha = jnp.exp2(m_old - m_new)
    pb = jnp.exp2(s - m_new).astype(BF16)
    m_scr[...] = m_new
    pv = functools.reduce(lambda a, b: a + b, [
        jnp.dot(vt, pb[k0:k1], preferred_element_type=F32) for (k0, k1), vt in vt_blocks])
    acc_scr[...] = alpha * acc_scr[...] + pv


ONES_ROWS = 16


def _with_ones_rows(vt):
    shape = vt.shape[:-2]
    return jnp.concatenate([vt, jnp.ones(shape + (1, vt.shape[-1]), vt.dtype),
                            jnp.zeros(shape + (ONES_ROWS - 1, vt.shape[-1]), vt.dtype)], axis=-2)


def _flash_update(s, v, m_scr, l_scr, acc_scr):
    groups = [s[:, c:c + LANES] for c in range(0, s.shape[1], LANES)]
    m_old = m_scr[...]
    lane_max = functools.reduce(jnp.maximum, groups)
    m_new = jnp.maximum(m_old, jnp.max(lane_max, axis=-1, keepdims=True))
    alpha = jnp.exp2(m_old - m_new)
    p_groups = [jnp.exp2(g - m_new) for g in groups]
    l_scr[...] = alpha * l_scr[...] + functools.reduce(lambda a, b: a + b, p_groups)
    m_scr[...] = m_new
    pb = jnp.concatenate([g.astype(BF16) for g in p_groups], axis=1)
    acc_scr[...] = alpha * acc_scr[...] + jnp.dot(pb, v, preferred_element_type=F32)


def _diff_attn_kernel(lam_ref, q_ref, k_ref, v_ref, ng_ref, o_ref, m_scr, l_scr, acc_scr,
                      *, tq, tk, lam_init):
    i = pl.program_id(2)
    per_tile = tq // tk
    qs = jnp.concatenate(_split_pair(q_ref[0], True), axis=0)
    m_scr[...] = jnp.full(m_scr.shape, NEG, F32)
    l_scr[...] = jnp.zeros(l_scr.shape, F32)
    acc_scr[...] = jnp.zeros(acc_scr.shape, F32)

    def kv_block(j):
        start = pl.multiple_of(j * tk, tk)
        return k_ref[0, pl.ds(start, tk), :], v_ref[0, pl.ds(start, tk), :]

    def body(j, carry):
        k, v = kv_block(j)
        _flash_update(_qk(qs, k), v, m_scr, l_scr, acc_scr)
        return carry

    lax.fori_loop(0, i * per_tile, body, 0)
    for d in range(per_tile):
        k, v = kv_block(i * per_tile + d)
        s = _qk(qs, k)
        row = lax.broadcasted_iota(jnp.int32, s.shape, 0) % tq
        col = lax.broadcasted_iota(jnp.int32, s.shape, 1) + d * tk
        s = jnp.where((col // CHUNK) <= (row // CHUNK), s, NEG)
        _flash_update(s, v, m_scr, l_scr, acc_scr)

    o = acc_scr[...] / jnp.sum(l_scr[...], axis=-1, keepdims=True)
    o = o[:tq] - lam_ref[0] * o[tq:]
    o = o * lax.rsqrt(jnp.mean(o * o, axis=-1, keepdims=True) + EPS)
    o_ref[0] = ((o * ng_ref[...]) * (1.0 - lam_init)).astype(BF16)


def _diff_attention(rope_slab, plain_slab, lam, norm_g, lam_init, tq, tk):
    bsz, s, _ = rope_slab.shape
    assert tq % tk == 0
    kern = functools.partial(_diff_attn_kernel, tq=tq, tk=tk, lam_init=lam_init)
    cb = COL_TILE // LANES
    return pl.pallas_call(
        kern,
        grid=(bsz, A_HEADS, s // tq),
        in_specs=[
            pl.BlockSpec(memory_space=pltpu.SMEM),
            pl.BlockSpec((1, tq, LANES), lambda b, h, i: (b, i, R_AQ * cb + h)),
            pl.BlockSpec((1, s, LANES), lambda b, h, i: (b, 0, R_AK * cb + h)),
            pl.BlockSpec((1, s, LANES), lambda b, h, i: (b, 0, P_AV * cb + h)),
            pl.BlockSpec((1, LANES), lambda b, h, i: (0, 0)),
        ],
        out_specs=pl.BlockSpec((1, tq, LANES), lambda b, h, i: (b, i, h)),
        out_shape=jax.ShapeDtypeStruct((bsz, s, A_HEADS * LANES), BF16),
        scratch_shapes=[pltpu.VMEM((2 * tq, LANES), F32), pltpu.VMEM((2 * tq, LANES), F32),
                        pltpu.VMEM((2 * tq, LANES), F32)],
        compiler_params=_cparams(("arbitrary", "arbitrary", "arbitrary")),
        name="diff_attn",
    )(lam.reshape(1), rope_slab, rope_slab, plain_slab, norm_g.reshape(1, LANES))


KEY_NEG_INF = -2139095040
KEY_POS_INF = 2139095040
SEARCH_MAX_STEPS = 80
NO_TIE_LIMIT = 1e9
P3_PAIRS = 2
VT_ROWS = HEAD_DIM + ONES_ROWS


def _key_to_float(key):
    bits = jnp.where(key >= 0, key, (key - 1) ^ jnp.int32(0x7FFFFFFF))
    return pltpu.bitcast(bits, F32)


def _float_to_key(t):
    bits = pltpu.bitcast(t, jnp.int32)
    return jnp.where(bits >= 0, bits, (bits ^ jnp.int32(0x7FFFFFFF)) + 1)


def _dsa_kernel(q_ref, iq_ref, iw_ref, k_ref, ik_ref, vt_ref, o_ref,
                sc_scr, qs_scr, m_scr, acc_scr, *, tq, topk):
    i = pl.program_id(1)
    nblk = i + 1
    npair = (nblk + 1) // 2
    nq = B_HEADS * tq

    iq = iq_ref[0]
    parts = []
    for p in range(IDX_HEADS // 2):
        parts += list(_split_pair(iq[:, p * LANES:(p + 1) * LANES], True))
    iqs = jnp.concatenate(parts, axis=0)
    iw_t = iw_ref[0].astype(F32).T

    def index_block(j):
        start = pl.multiple_of(j * tq, tq)
        ik = ik_ref[0, pl.ds(start, tq), :]
        score = None
        for h in range(IDX_HEADS):
            hs = jnp.maximum(_qk(ik, iqs[h * tq:(h + 1) * tq]), 0.0)
            score = iw_t[h:h + 1] * hs if score is None else score + iw_t[h:h + 1] * hs
        return score

    def score_slot(j):
        return j // 2, pl.ds(pl.multiple_of((j % 2) * tq, tq), tq)

    def p1_body(j, carry):
        pair, rows = score_slot(j)
        sc_scr[pair, rows, :] = index_block(j)
        return carry

    lax.fori_loop(0, i, p1_body, 0)
    score = index_block(i)
    key_i = lax.broadcasted_iota(jnp.int32, score.shape, 0)
    qry_i = lax.broadcasted_iota(jnp.int32, score.shape, 1)
    pair, rows = score_slot(i)
    sc_scr[pair, rows, :] = jnp.where((key_i // CHUNK) <= (qry_i // CHUNK), score, -jnp.inf)

    @pl.when(nblk % 2 == 1)
    def _():
        sc_scr[npair - 1, tq:2 * tq, :] = jnp.full((tq, tq), -jnp.inf, F32)

    kf = float(topk)

    def scan(hit_fn, ext_fn, ext_op, ext_init):
        def body(jj, carry):
            cnt, ext = carry
            sblk = sc_scr[jj]
            cnt = cnt + _fold_rows(jnp.where(hit_fn(sblk), 1.0, 0.0), lambda a, b: a + b)
            if ext_fn is not None:
                ext = ext_op(ext, _fold_rows(ext_fn(sblk), ext_op))
            return cnt, ext
        cnt, ext = lax.fori_loop(0, npair, body, (jnp.zeros((8, tq), F32),
                                                  jnp.full((8, tq), ext_init, F32)))
        return jnp.sum(cnt, axis=0, keepdims=True), ext

    def count_ge(t):
        return scan(lambda sblk: sblk >= t, None, None, 0.0)[0]

    c_ge0, ext = scan(lambda sblk: sblk >= 0.0, lambda sblk: sblk, jnp.maximum, -jnp.inf)
    col_max = jnp.max(ext, axis=0, keepdims=True)
    c_gt0, ext = scan(lambda sblk: sblk > 0.0,
                      lambda sblk: jnp.where(sblk > -jnp.inf, sblk, jnp.inf), jnp.minimum, jnp.inf)
    col_min = jnp.min(ext, axis=0, keepdims=True)
    qpos = i * tq + lax.broadcasted_iota(jnp.int32, (1, tq), 1)
    n_valid = (((qpos // CHUNK) + 1) * CHUNK).astype(F32)
    open_q = n_valid < kf
    above = c_gt0 >= kf
    below = c_ge0 < kf
    ikey = lambda v: jnp.full((1, tq), v, jnp.int32)
    lo_k = jnp.where(below, _float_to_key(col_min), ikey(0))
    hi_k = jnp.where(above, _float_to_key(col_max) + 1, jnp.where(below, ikey(0), ikey(1)))
    c_lo = jnp.where(below, n_valid, c_ge0)
    c_hi = jnp.where(above, 0.0, jnp.where(below, c_ge0, c_gt0))
    done = jnp.logical_or(open_q, jnp.logical_not(jnp.logical_or(above, below)))
    state = (jnp.int32(0), lo_k, hi_k, c_lo, c_hi, done.astype(jnp.int32), ikey(0))

    def search_cond(st):
        return jnp.logical_and(st[0] < SEARCH_MAX_STEPS, jnp.min(st[5]) == 0)

    def search_step(st):
        step, lo_k, hi_k, c_lo, c_hi, done, force_bisect = st
        finite = jnp.logical_and(lo_k > KEY_NEG_INF, hi_k < KEY_POS_INF)
        t_lo, t_hi = _key_to_float(lo_k), _key_to_float(hi_k)
        log_lo = jnp.log(c_lo)
        frac = (log_lo - math.log(kf - 0.5)) / (log_lo - jnp.log(jnp.maximum(c_hi, 0.5)))
        k_interp = _float_to_key(t_lo + (t_hi - t_lo) * frac)
        k_mid = (lo_k & hi_k) + ((lo_k ^ hi_k) >> 1)
        interp = jnp.logical_and(finite, force_bisect == 0)
        k = jnp.where(interp, k_interp, k_mid)
        k = jnp.minimum(jnp.maximum(k, lo_k + 1), hi_k - 1)
        c = count_ge(_key_to_float(k))
        ok = c >= kf
        live = done == 0
        new_lo = jnp.where(jnp.logical_and(live, ok), k, lo_k)
        new_hi = jnp.where(jnp.logical_and(live, jnp.logical_not(ok)), k, hi_k)
        new_c_lo = jnp.where(jnp.logical_and(live, ok), c, c_lo)
        new_c_hi = jnp.where(jnp.logical_and(live, jnp.logical_not(ok)), c, c_hi)
        width = lambda a, b: b.astype(F32) - a.astype(F32)
        slow = width(new_lo, new_hi) > 0.5 * width(lo_k, hi_k)
        new_force = jnp.logical_and(interp, slow).astype(jnp.int32)
        finished = jnp.logical_or(new_c_lo == kf, new_hi - new_lo == 1)
        new_done = jnp.maximum(done, finished.astype(jnp.int32))
        return step + 1, new_lo, new_hi, new_c_lo, new_c_hi, new_done, new_force

    _, lo_k, hi_k, c_lo, c_hi, _, _ = lax.while_loop(search_cond, search_step, state)
    tau = jnp.where(open_q, -jnp.inf, _key_to_float(lo_k))
    need = jnp.where(open_q, 0.0, jnp.where(c_lo == kf, NO_TIE_LIMIT, kf - c_hi))

    q = q_ref[0]
    for p in range(B_HEADS // 2):
        even, odd = _split_pair(q[:, p * LANES:(p + 1) * LANES], True)
        qs_scr[p * tq:(p + 1) * tq, :] = even
        qs_scr[(B_HEADS // 2 + p) * tq:(B_HEADS // 2 + p + 1) * tq, :] = odd
    m_scr[...] = jnp.full(m_scr.shape, NEG, F32)
    acc_scr[...] = jnp.zeros(acc_scr.shape, F32)
    key_i = lax.broadcasted_iota(jnp.int32, (tq, tq), 0)
    lower = (lax.broadcasted_iota(jnp.int32, (tq, tq), 1) <= key_i).astype(BF16)

    def select_bias(jj, tie_count):
        sblk = sc_scr[jj]
        tie = sblk == tau
        tie01 = jnp.where(tie, 1.0, 0.0).astype(BF16)
        rank_a = tie_count + jnp.dot(lower, tie01[0:tq], preferred_element_type=F32)
        rank_b = rank_a[tq - 1:tq, :] + jnp.dot(lower, tie01[tq:2 * tq], preferred_element_type=F32)
        rank = jnp.concatenate([rank_a, rank_b], axis=0)
        keep_tie = jnp.where(rank <= need, 0.0, NEG)
        return jnp.where(sblk > tau, 0.0, jnp.where(tie, keep_tie, NEG)), rank[2 * tq - 1:2 * tq, :]

    def p3_step(jj, pairs, tie_count):
        nkeys = pairs * 2 * tq
        start = pl.multiple_of(jj * 2 * tq, 2 * tq)
        biases = []
        for t in range(pairs):
            bias, tie_count = select_bias(jj + t, tie_count)
            biases.append(bias)
        bias = jnp.concatenate(biases, axis=0)
        s = _qk(k_ref[0, pl.ds(start, nkeys), :], qs_scr[...])
        s = jnp.concatenate([s[:, h * tq:(h + 1) * tq] + bias for h in range(B_HEADS)], axis=1)
        _flash_t_update(s, [((t * tq, (t + 1) * tq), vt_ref[0, 2 * jj + t]) for t in range(2 * pairs)],
                        m_scr, acc_scr)
        return tie_count

    tie_count = lax.fori_loop(0, npair // P3_PAIRS,
                              lambda g, tc: p3_step(g * P3_PAIRS, P3_PAIRS, tc),
                              jnp.zeros((1, tq), F32))
    for r in range(1, P3_PAIRS):
        @pl.when(npair % P3_PAIRS == r)
        def _():
            p3_step(npair - r, r, tie_count)

    acc = acc_scr[...]
    o = acc[0:HEAD_DIM] / acc[HEAD_DIM:HEAD_DIM + 1]
    half_cols = (B_HEADS // 2) * tq
    for p in range(B_HEADS // 2):
        pair = jnp.concatenate([o[:, p * tq:(p + 1) * tq],
                                o[:, half_cols + p * tq:half_cols + (p + 1) * tq]], axis=0)
        o_ref[0, :, p * LANES:(p + 1) * LANES] = pair.T.astype(BF16)


def _dsa_attention(rope_slab, plain_slab, tq):
    bsz, s, _ = rope_slab.shape
    topk = min(TOPK_MAX, s // 4)
    nblk = s // tq
    assert tq >= topk and s % tq == 0 and nblk % 2 == 0
    kern = functools.partial(_dsa_kernel, tq=tq, topk=topk)
    idx0 = R_IDX * COL_TILE
    misc0 = P_MISC * COL_TILE
    iq_w = IDX_HEADS * HEAD_DIM
    width = B_HEADS * HEAD_DIM
    vt = _with_ones_rows(
        plain_slab[:, :, misc0:misc0 + HEAD_DIM].reshape(bsz, nblk, tq, HEAD_DIM).transpose(0, 1, 3, 2))
    return pl.pallas_call(
        kern,
        grid=(bsz, s // tq),
        in_specs=[
            pl.BlockSpec((1, tq, COL_TILE), lambda b, i: (b, i, R_BQ)),
            pl.BlockSpec((1, tq, iq_w), lambda b, i: (b, i, idx0 // iq_w)),
            pl.BlockSpec((1, tq, LANES), lambda b, i: (b, i, (misc0 + LANES) // LANES)),
            pl.BlockSpec((1, s, LANES), lambda b, i: (b, 0, (idx0 + iq_w) // LANES)),
            pl.BlockSpec((1, s, LANES), lambda b, i: (b, 0, (idx0 + iq_w + LANES) // LANES)),
            pl.BlockSpec((1, nblk, VT_ROWS, tq), lambda b, i: (b, 0, 0, 0)),
        ],
        out_specs=pl.BlockSpec((1, tq, width), lambda b, i: (b, i, 0)),
        out_shape=jax.ShapeDtypeStruct((bsz, s, width), BF16),
        scratch_shapes=[
            pltpu.VMEM((nblk // 2, 2 * tq, tq), F32),
            pltpu.VMEM((B_HEADS * tq, LANES), BF16),
            pltpu.VMEM((1, B_HEADS * tq), F32),
            pltpu.VMEM((VT_ROWS, B_HEADS * tq), F32),
        ],
        compiler_params=_cparams(("arbitrary", "arbitrary")),
        name="dsa_attn",
    )(rope_slab, rope_slab, plain_slab, rope_slab, rope_slab, vt)


BAND_TQ = 2 * CHUNK
BAND_KEYS = (C_LEFT_CHUNKS + BAND_TQ // CHUNK) * CHUNK
BAND_PAD = C_LEFT_CHUNKS * CHUNK


def _band_kernel(q_ref, k_ref, v_ref, bias_ref, o_ref):
    i = pl.program_id(1)
    tq = BAND_TQ
    start = pl.multiple_of(i * tq, tq)
    col = lax.broadcasted_iota(jnp.int32, (2 * tq, BAND_KEYS), 1)
    key_ok = col + start >= BAND_PAD
    for p in range(C_HEADS // 2):
        lanes = slice(p * LANES, (p + 1) * LANES)
        k = k_ref[0, pl.ds(start, BAND_KEYS), lanes]
        v = v_ref[0, pl.ds(start, BAND_KEYS), lanes]
        qs = jnp.concatenate(_split_pair(q_ref[0, :, lanes], False), axis=0)
        s = _qk(qs, k) + bias_ref[2 * p:2 * p + 2].reshape(2 * tq, BAND_KEYS)
        s = jnp.where(key_ok, s, NEG)
        m = jnp.max(s, axis=-1, keepdims=True)
        e = jnp.exp2(s - m)
        l = jnp.sum(e, axis=-1, keepdims=True)
        eb = e.astype(BF16)
        v_even, v_odd = _split_pair(v, False)
        o = (jnp.dot(eb[:tq], v_even, preferred_element_type=F32) / l[:tq]
             + jnp.dot(eb[tq:], v_odd, preferred_element_type=F32) / l[tq:])
        o_ref[0, :, lanes] = o.astype(BF16)


def _band_bias(rel_bias):
    r = jnp.arange(BAND_TQ)[:, None]
    cidx = jnp.arange(BAND_KEYS)[None, :]
    n_diag = BAND_TQ + BAND_KEYS - 1
    rel = BAND_PAD - (BAND_KEYS - 1) + jnp.arange(n_diag)
    g = rel_bias.astype(F32)[:, jnp.clip(rel, -REL_CLIP, REL_CLIP) + REL_CLIP] * LOG2E
    skew = jnp.tile(g, (1, BAND_TQ + 2))[:, :BAND_TQ * (n_diag + 1)]
    bias = skew.reshape(-1, BAND_TQ, n_diag + 1)[:, :, :BAND_KEYS][:, :, ::-1]
    dchunk = (r // CHUNK + C_LEFT_CHUNKS) - cidx // CHUNK
    in_band = jnp.logical_and(dchunk >= 0, dchunk <= C_LEFT_CHUNKS)
    return jnp.where(in_band[None], bias, NEG)


def _band_attention(plain_slab, kp, vp, bias):
    bsz, s, _ = plain_slab.shape
    tq = BAND_TQ
    sp = kp.shape[1]
    width = C_HEADS * HEAD_DIM
    return pl.pallas_call(
        _band_kernel,
        grid=(bsz, s // tq),
        in_specs=[
            pl.BlockSpec((1, tq, width), lambda b, i: (b, i, P_CQ)),
            pl.BlockSpec((1, sp, width), lambda b, i: (b, 0, 0)),
            pl.BlockSpec((1, sp, width), lambda b, i: (b, 0, 0)),
            pl.BlockSpec((C_HEADS, tq, BAND_KEYS), lambda b, i: (0, 0, 0)),
        ],
        out_specs=pl.BlockSpec((1, tq, width), lambda b, i: (b, i, 0)),
        out_shape=jax.ShapeDtypeStruct((bsz, s, width), BF16),
        compiler_params=_cparams(("arbitrary", "arbitrary")),
        name="band_attn",
    )(plain_slab, kp, vp, bias)


def _merge_kernel(x_ref, g1_ref, ya_ref, yb_ref, yc_ref, ga_ref, gb_ref, gc_ref,
                  wa_ref, wb_ref, wc_ref, wo_ref, o_ref):
    for r0 in range(0, x_ref.shape[1], PROJ_ROWS):
        rows = slice(r0, r0 + PROJ_ROWS)

        def branch(y_ref, w_ref, gate_ref):
            return gate_ref[0, rows, :].astype(F32) * jnp.dot(y_ref[0, rows, :], w_ref[...],
                                                             preferred_element_type=F32)

        merged = (branch(ya_ref, wa_ref, ga_ref) + branch(yb_ref, wb_ref, gb_ref)
                  + branch(yc_ref, wc_ref, gc_ref))
        mixed = jnp.dot(merged.astype(BF16), wo_ref[...], preferred_element_type=F32)
        o_ref[0, rows, :] = x_ref[0, rows, :] + g1_ref[0] * mixed


def _merge(x, g1, ya, yb, yc, gate_slab, wa, wb, wc, wo, tm):
    bsz, s, d = x.shape
    tok = lambda b, i: (b, i, 0)
    full = lambda b, i: (0, 0)
    return pl.pallas_call(
        _merge_kernel,
        grid=(bsz, s // tm),
        in_specs=[
            pl.BlockSpec((1, tm, d), tok),
            pl.BlockSpec((1, 1, d), lambda b, i: (b, 0, 0)),
            pl.BlockSpec((1, tm, ya.shape[2]), tok),
            pl.BlockSpec((1, tm, yb.shape[2]), tok),
            pl.BlockSpec((1, tm, yc.shape[2]), tok),
            pl.BlockSpec((1, tm, d), lambda b, i: (b, i, 0)),
            pl.BlockSpec((1, tm, d), lambda b, i: (b, i, 1)),
            pl.BlockSpec((1, tm, d), lambda b, i: (b, i, 2)),
            pl.BlockSpec(wa.shape, full),
            pl.BlockSpec(wb.shape, full),
            pl.BlockSpec(wc.shape, full),
            pl.BlockSpec(wo.shape, full),
        ],
        out_specs=pl.BlockSpec((1, tm, d), tok),
        out_shape=jax.ShapeDtypeStruct((bsz, s, d), F32),
        compiler_params=_cparams(("arbitrary", "arbitrary")),
        name="merge",
    )(x, g1, ya, yb, yc, gate_slab, gate_slab, gate_slab, wa, wb, wc, wo)


def _router_kernel(x_ref, g_ref, sc_ref, sh_ref, rw_ref, rb_ref, u_ref, comb_ref):
    u = _modulated_norm(x_ref[0], g_ref[...], sc_ref[0], sh_ref[0])
    u_ref[0] = u.astype(BF16)
    logits = lax.dot_general(rw_ref[...], u, (((1,), (1,)), ((), ())),
                             preferred_element_type=F32, precision=lax.Precision.HIGHEST)
    aff = jax.nn.sigmoid(logits)
    sel = aff + rb_ref[...]
    rows = [sel[e:e + 1] for e in range(N_EXPERTS)]
    gscore = []
    for g in range(N_GROUPS):
        r = rows[g * EXPERTS_PER_GROUP:(g + 1) * EXPERTS_PER_GROUP]
        best = None
        for a in range(EXPERTS_PER_GROUP):
            for b in range(a + 1, EXPERTS_PER_GROUP):
                pair = r[a] + r[b]
                best = pair if best is None else jnp.maximum(best, pair)
        gscore.append(best)
    gmax = functools.reduce(jnp.maximum, gscore)
    taken = jnp.zeros_like(gmax) > 1.0
    in_best = []
    for g in range(N_GROUPS):
        is_g = jnp.logical_and(gscore[g] == gmax, jnp.logical_not(taken))
        in_best.append(is_g)
        taken = jnp.logical_or(taken, is_g)
    keep = []
    for e in range(N_EXPERTS):
        g = e // EXPERTS_PER_GROUP
        rank = jnp.zeros_like(gmax)
        for o in range(g * EXPERTS_PER_GROUP, (g + 1) * EXPERTS_PER_GROUP):
            if o == e:
                continue
            ahead = rows[o] > rows[e] if o > e else rows[o] >= rows[e]
            rank = rank + jnp.where(ahead, 1.0, 0.0)
        keep.append(jnp.logical_and(in_best[g], rank < 2.0))
    w = [jnp.where(keep[e], aff[e:e + 1], 0.0) for e in range(N_EXPERTS)]
    total = functools.reduce(lambda a, b: a + b, w)
    comb = jnp.concatenate([we / total for we in w]
                           + [jnp.zeros((LANES - N_EXPERTS, total.shape[1]), F32)], axis=0)
    comb_ref[0] = comb.T


def _router(x, g, sc, sh, router_w, router_b, tm):
    bsz, s, d = x.shape
    tok = lambda b, i: (b, i, 0)
    return pl.pallas_call(
        _router_kernel,
        grid=(bsz, s // tm),
        in_specs=[
            pl.BlockSpec((1, tm, d), tok),
            pl.BlockSpec((1, d), lambda b, i: (0, 0)),
            pl.BlockSpec((1, 1, d), lambda b, i: (b, 0, 0)),
            pl.BlockSpec((1, 1, d), lambda b, i: (b, 0, 0)),
            pl.BlockSpec((N_EXPERTS, d), lambda b, i: (0, 0)),
            pl.BlockSpec((N_EXPERTS, 1), lambda b, i: (0, 0)),
        ],
        out_specs=[pl.BlockSpec((1, tm, d), tok), pl.BlockSpec((1, tm, LANES), tok)],
        out_shape=[jax.ShapeDtypeStruct((bsz, s, d), BF16),
                   jax.ShapeDtypeStruct((bsz, s, LANES), F32)],
        compiler_params=_cparams(("arbitrary", "arbitrary")),
        name="router",
    )(x, g.reshape(1, d), sc, sh, router_w.T, router_b.reshape(N_EXPERTS, 1))


MOE_SUB = 512
MOE_CAP = 128


def _expert_ffn(rows, w1_ref, w3_ref, w2_ref):
    h1 = jnp.dot(rows, w1_ref[0], preferred_element_type=F32)
    h3 = jnp.dot(rows, w3_ref[0], preferred_element_type=F32)
    h = (h1 * jax.nn.sigmoid(h1)) * h3
    return jnp.dot(h.astype(BF16), w2_ref[0], preferred_element_type=F32)


def _moe_kernel(x_ref, g2_ref, u_ref, comb_ref, w1_ref, w3_ref, w2_ref, o_ref,
                acc_scr, pos_scr, pos_t_scr, comb_t_scr, scatter_scr, y_scr):
    e = pl.program_id(2)
    tm = u_ref.shape[1]
    sub = min(MOE_SUB, tm)
    subs = [slice(r0, r0 + sub) for r0 in range(0, tm, sub)]

    @pl.when(e == 0)
    def _():
        acc_scr[...] = jnp.zeros(acc_scr.shape, F32)
        before = (lax.broadcasted_iota(jnp.int32, (sub, sub), 1)
                  < lax.broadcasted_iota(jnp.int32, (sub, sub), 0)).astype(BF16)
        for rows in subs:
            comb0 = comb_ref[0, rows, :]
            member = jnp.where(comb0 > 0.0, 1.0, 0.0).astype(BF16)
            pos = jnp.dot(before, member, preferred_element_type=F32)
            pos_scr[rows, :] = pos
            pos_t_scr[:, rows] = pos.T
            comb_t_scr[:, rows] = comb0.T

    comb = comb_ref[0]
    lane = lax.broadcasted_iota(jnp.int32, comb.shape, 1)
    ce = jnp.sum(jnp.where(lane == e, comb, 0.0), axis=-1, keepdims=True)
    pe = jnp.sum(jnp.where(lane == e, pos_scr[...], 0.0), axis=-1, keepdims=True)
    member_e = jnp.where(ce > 0.0, 1.0, 0.0)
    fullest = functools.reduce(jnp.maximum, [jnp.sum(member_e[rows]) for rows in subs])
    routed = fullest <= float(MOE_CAP)
    slots = pl.ds(pl.multiple_of(e * MOE_CAP, MOE_CAP), MOE_CAP)

    @pl.when(routed)
    def _():
        slot_l = lax.broadcasted_iota(jnp.int32, (1, MOE_CAP), 1).astype(F32)
        scatter_scr[e] = jnp.where(jnp.logical_and(ce > 0.0, pe == slot_l), 1.0, 0.0).astype(BF16)
        slot_s = lax.broadcasted_iota(jnp.int32, (MOE_CAP, 1), 0).astype(F32)
        packed, weights = [], []
        for rows in subs:
            ce_t = comb_t_scr[pl.ds(e, 1), rows]
            pe_t = pos_t_scr[pl.ds(e, 1), rows]
            hit = jnp.logical_and(ce_t > 0.0, pe_t == slot_s)
            weights.append(jnp.sum(jnp.where(hit, ce_t, 0.0), axis=-1, keepdims=True))
            packed.append(jnp.dot(jnp.where(hit, 1.0, 0.0).astype(BF16), u_ref[0, rows, :],
                                  preferred_element_type=F32).astype(BF16))
        y = jnp.concatenate(weights, axis=0) * _expert_ffn(jnp.concatenate(packed, axis=0),
                                                           w1_ref, w3_ref, w2_ref)
        for k in range(len(subs)):
            y_scr[k, slots, :] = y[k * MOE_CAP:(k + 1) * MOE_CAP].astype(BF16)

    @pl.when(jnp.logical_not(routed))
    def _():
        scatter_scr[e] = jnp.zeros(scatter_scr.shape[1:], BF16)
        for k in range(len(subs)):
            y_scr[k, slots, :] = jnp.zeros((MOE_CAP, y_scr.shape[2]), BF16)
        acc_scr[...] += ce * _expert_ffn(u_ref[0], w1_ref, w3_ref, w2_ref)

    @pl.when(e == pl.num_programs(2) - 1)
    def _():
        for k, rows in enumerate(subs):
            scatter = jnp.concatenate([scatter_scr[j, rows, :] for j in range(N_EXPERTS)], axis=1)
            y = acc_scr[rows, :] + jnp.dot(scatter, y_scr[k], preferred_element_type=F32)
            o_ref[0, rows, :] = x_ref[0, rows, :] + g2_ref[0] * y


def _moe(x, g2, u, comb, w1, w3, w2, tm):
    bsz, s, d = x.shape
    ne, _, dff = w1.shape
    assert ne == N_EXPERTS
    tok = lambda b, i, e: (b, i, 0)
    return pl.pallas_call(
        _moe_kernel,
        grid=(bsz, s // tm, ne),
        in_specs=[
            pl.BlockSpec((1, tm, d), tok),
            pl.BlockSpec((1, 1, d), lambda b, i, e: (b, 0, 0)),
            pl.BlockSpec((1, tm, d), tok),
            pl.BlockSpec((1, tm, LANES), tok),
            pl.BlockSpec((1, d, dff), lambda b, i, e: (e, 0, 0)),
            pl.BlockSpec((1, d, dff), lambda b, i, e: (e, 0, 0)),
            pl.BlockSpec((1, dff, d), lambda b, i, e: (e, 0, 0)),
        ],
        out_specs=pl.BlockSpec((1, tm, d), tok),
        out_shape=jax.ShapeDtypeStruct((bsz, s, d), F32),
        scratch_shapes=[
            pltpu.VMEM((tm, d), F32),
            pltpu.VMEM((tm, LANES), F32),
            pltpu.VMEM((LANES, tm), F32),
            pltpu.VMEM((LANES, tm), F32),
            pltpu.VMEM((ne, tm, MOE_CAP), BF16),
            pltpu.VMEM((pl.cdiv(tm, MOE_SUB), ne * MOE_CAP, d), BF16),
        ],
        compiler_params=_cparams(("arbitrary", "arbitrary", "arbitrary")),
        name="moe",
    )(x, g2, u, comb, w1, w3, w2)


def _final_norm_kernel(x_ref, g_ref, o_ref):
    x = x_ref[0]
    o_ref[0] = (x * lax.rsqrt(jnp.mean(x * x, axis=-1, keepdims=True) + EPS)) * g_ref[...]


def _final_norm(x, g, tm):
    bsz, s, d = x.shape
    tok = lambda b, i: (b, i, 0)
    return pl.pallas_call(
        _final_norm_kernel,
        grid=(bsz, s // tm),
        in_specs=[pl.BlockSpec((1, tm, d), tok), pl.BlockSpec((1, d), lambda b, i: (0, 0))],
        out_specs=pl.BlockSpec((1, tm, d), tok),
        out_shape=jax.ShapeDtypeStruct((bsz, s, d), F32),
        compiler_params=_cparams(("arbitrary", "arbitrary")),
        name="final_norm",
    )(x, g.reshape(1, d))


def _tile(s, want):
    t = min(want, s)
    assert s % t == 0
    return t


def _tiles(s):
    return dict(
        rows=_tile(s, 1024),
        diff_q=_tile(s, 1024),
        diff_k=_tile(s, 512),
        dsa_q=_tile(s, 256),
        merge=_tile(s, 512),
    )


def kernel(x, c, positions, norm1_g, norm2_g, w_mod, b_mod, w_in, lambda_q1, lambda_k1, lambda_q2, lambda_k2, a_norm_g, c_rel_bias, w_branch_a, w_branch_b, w_branch_c, w_out, router_w, router_b, exp_w1, exp_w3, exp_w2, final_g):
    bsz, s, d = x.shape
    depth = w_mod.shape[0]
    t = _tiles(s)
    tm = t["rows"]

    mod = _modulation(c, w_mod, b_mod)
    rope_tables = _rope_tables(positions)

    for layer in range(depth):
        lam_init = 0.8 - 0.6 * math.exp(-0.3 * layer)
        sh1, sc1, g1, sh2, sc2, g2 = [m[:, None, :] for m in jnp.split(mod[layer], 6, axis=-1)]
        lam = (jnp.exp(jnp.sum(lambda_q1[layer] * lambda_k1[layer]))
               - jnp.exp(jnp.sum(lambda_q2[layer] * lambda_k2[layer])) + lam_init)

        w_rope, w_gate, w_plain = _build_weights(w_in[layer])
        u = _norm(x, norm1_g[layer], sc1, sh1, tm)
        rope_slab = _project(u, w_rope, rope_tables, "rope", tm)
        gate_slab = _project(u, w_gate, (), "gate", tm)
        plain_slab = _project(u, w_plain, (), "plain", tm)

        ya = _diff_attention(rope_slab, plain_slab, lam, a_norm_g[layer], lam_init,
                             t["diff_q"], t["diff_k"])
        yb = _dsa_attention(rope_slab, plain_slab, t["dsa_q"])
        pad = ((0, 0), (BAND_PAD, 0), (0, 0))
        kp = jnp.pad(plain_slab[:, :, P_CK * COL_TILE:(P_CK + 1) * COL_TILE], pad)
        vp = jnp.pad(plain_slab[:, :, P_CV * COL_TILE:(P_CV + 1) * COL_TILE], pad)
        yc = _band_attention(plain_slab, kp, vp, _band_bias(c_rel_bias[layer]))
        x = _merge(x, g1, ya, yb, yc, gate_slab,
                   w_branch_a[layer].astype(BF16), w_branch_b[layer].astype(BF16),
                   w_branch_c[layer].astype(BF16), w_out[layer].astype(BF16), t["merge"])

        u, comb = _router(x, norm2_g[layer], sc2, sh2, router_w, router_b, t["merge"])
        x = _moe(x, g2, u, comb, exp_w1[layer].astype(BF16), exp_w3[layer].astype(BF16),
                 exp_w2[layer].astype(BF16), tm)

    return _final_norm(x, final_g, tm)
```

```python
import functools
import math

import jax
import jax.numpy as jnp
from jax import lax
from jax.experimental import pallas as pl
from jax.experimental.pallas import tpu as pltpu

F32 = jnp.float32
BF16 = jnp.bfloat16

CHUNK = 64
ROPE_THETA = 10000.0
EPS = 1e-6
A_HEADS = 4
HEAD_DIM = 64
B_HEADS = 8
IDX_HEADS = 4
TOPK_MAX = 256
C_HEADS = 8
C_LEFT_CHUNKS = 8
REL_CLIP = 256
N_EXPERTS = 16
N_GROUPS = 4
EXPERTS_PER_GROUP = 4
N_BRANCHES = 3

LANES = 128
NEG = -1e30
LOG2E = math.log2(math.e)
VMEM_LIMIT = 56 * 1024 * 1024

COL_TILE = 512
PROJ_ROWS = 256
R_AQ, R_AK, R_BQ, R_IDX = 0, 1, 2, 3
P_AV, P_CQ, P_CK, P_CV, P_MISC = 0, 1, 2, 3, 4


def _cparams(sem):
    return pltpu.CompilerParams(dimension_semantics=sem, vmem_limit_bytes=VMEM_LIMIT)


def _mod_kernel(c_ref, w_ref, b_ref, o_ref):
    c = c_ref[...]
    ca = c * jax.nn.sigmoid(c)
    o_ref[0] = jnp.dot(ca, w_ref[0], preferred_element_type=F32) + b_ref[0]


def _modulation(c, w_mod, b_mod):
    depth, d, n6 = w_mod.shape
    bsz = c.shape[0]
    tn = 1024
    return pl.pallas_call(
        _mod_kernel,
        grid=(depth, n6 // tn),
        in_specs=[
            pl.BlockSpec((bsz, d), lambda l, j: (0, 0)),
            pl.BlockSpec((1, d, tn), lambda l, j: (l, 0, j)),
            pl.BlockSpec((1, 1, tn), lambda l, j: (l, 0, j)),
        ],
        out_specs=pl.BlockSpec((1, bsz, tn), lambda l, j: (l, 0, j)),
        out_shape=jax.ShapeDtypeStruct((depth, bsz, n6), F32),
        compiler_params=_cparams(("arbitrary", "arbitrary")),
        name="modulation",
    )(c, w_mod, b_mod.reshape(depth, 1, n6))


def _modulated_norm(x, g, sc, sh):
    y = x * lax.rsqrt(jnp.mean(x * x, axis=-1, keepdims=True) + EPS)
    return (y * g) * (1.0 + sc) + sh


def _norm_kernel(x_ref, g_ref, sc_ref, sh_ref, u_ref):
    u_ref[0] = _modulated_norm(x_ref[0], g_ref[...], sc_ref[0], sh_ref[0]).astype(BF16)


def _norm(x, g, sc, sh, tm):
    bsz, s, d = x.shape
    tok = lambda b, i: (b, i, 0)
    per_batch = lambda b, i: (b, 0, 0)
    return pl.pallas_call(
        _norm_kernel,
        grid=(bsz, s // tm),
        in_specs=[pl.BlockSpec((1, tm, d), tok), pl.BlockSpec((1, d), lambda b, i: (0, 0)),
                  pl.BlockSpec((1, 1, d), per_batch), pl.BlockSpec((1, 1, d), per_batch)],
        out_specs=pl.BlockSpec((1, tm, d), tok),
        out_shape=jax.ShapeDtypeStruct((bsz, s, d), BF16),
        compiler_params=_cparams(("arbitrary", "arbitrary")),
        name="norm",
    )(x, g.reshape(1, d), sc, sh)


def _proj_kernel(*refs, mode):
    if mode == "rope":
        u_ref, cos_ref, sin_ref, w_ref, o_ref = refs
    else:
        u_ref, w_ref, o_ref = refs
    tm = u_ref.shape[1]
    for r0 in range(0, tm, PROJ_ROWS):
        rows = slice(r0, r0 + PROJ_ROWS)
        acc = jnp.dot(u_ref[0, rows, :], w_ref[...], preferred_element_type=F32)
        if mode == "rope":
            cos, sin = cos_ref[0, rows, :], sin_ref[0, rows, :]
            groups = [acc[:, c:c + LANES] for c in range(0, COL_TILE, LANES)]
            acc = jnp.concatenate(
                [g * cos + pltpu.roll(g, LANES // 2, axis=1) * sin for g in groups], axis=1)
        elif mode == "gate":
            acc = 1.0 / (1.0 + jnp.exp(-acc))
        o_ref[0, rows, :] = acc.astype(BF16)


def _project(u, w, rope_tables, mode, tm):
    bsz, s, d = u.shape
    ncols = w.shape[1]
    tok = lambda b, i, j: (b, i, 0)
    in_specs = [pl.BlockSpec((1, tm, d), tok)]
    in_specs += [pl.BlockSpec((1, tm, LANES), tok) for _ in rope_tables]
    in_specs += [pl.BlockSpec((d, COL_TILE), lambda b, i, j: (0, j))]
    return pl.pallas_call(
        functools.partial(_proj_kernel, mode=mode),
        grid=(bsz, s // tm, ncols // COL_TILE),
        in_specs=in_specs,
        out_specs=pl.BlockSpec((1, tm, COL_TILE), lambda b, i, j: (b, i, j)),
        out_shape=jax.ShapeDtypeStruct((bsz, s, ncols), BF16),
        compiler_params=_cparams(("arbitrary", "arbitrary", "arbitrary")),
        name="proj_" + mode,
    )(u, *rope_tables, w)


def _pair_layout(w):
    half = HEAD_DIM // 2
    col = jnp.arange(w.shape[1])
    base, r = (col // LANES) * LANES, col % LANES
    src = base + ((r // half) % 2) * HEAD_DIM + (r // HEAD_DIM) * half + r % half
    return w[:, src]


def _build_weights(w_in):
    sizes = (512, 512, 512, 512, 64, 64, 256, 64, 4, 512, 512, 512, 3072)
    parts, start = [], 0
    for sz in sizes:
        parts.append(w_in[:, start:start + sz])
        start += sz
    aq, ak, av, bq, bk, bv, iq, ik, iw, cq, ck, cv, gates = parts
    d = w_in.shape[0]
    qscale = HEAD_DIM ** -0.5 * LOG2E
    iw_scale = IDX_HEADS ** -0.5 * HEAD_DIM ** -0.5
    zeros = lambda n: jnp.zeros((d, n), w_in.dtype)
    w_rope = _pair_layout(jnp.concatenate([aq * qscale, ak, bq * qscale, iq, bk, bk, ik, ik], axis=1))
    w_plain = jnp.concatenate([av, cq * qscale, ck, cv,
                               bv, bv, iw * iw_scale, zeros(LANES - IDX_HEADS),
                               zeros(COL_TILE - 2 * LANES)], axis=1)
    return w_rope.astype(BF16), gates.astype(BF16), w_plain.astype(BF16)


def _rope_tables(positions):
    half = HEAD_DIM // 2
    inv = ROPE_THETA ** (-jnp.arange(half, dtype=F32) / half)
    ang = positions.astype(F32)[..., None] * inv
    cos, sin = jnp.cos(ang), jnp.sin(ang)
    cos_t = jnp.tile(cos, (1, 1, LANES // half))
    sin_t = jnp.concatenate([-sin, -sin, sin, sin], axis=-1)
    return cos_t, sin_t


def _split_pair(pair, interleaved):
    lane = lax.broadcasted_iota(jnp.int32, pair.shape, 1)
    first = ((lane // (HEAD_DIM // 2)) % 2 == 0) if interleaved else (lane < HEAD_DIM)
    zero = jnp.zeros_like(pair)
    return jnp.where(first, pair, zero), jnp.where(first, zero, pair)


def _qk(q, k):
    return lax.dot_general(q, k, (((1,), (1,)), ((), ())), preferred_element_type=F32)


FOLD_CHAINS = 4


def _fold_rows(x, op, group=8):
    parts = x.reshape(x.shape[0] // group, group, x.shape[1])
    chains = [functools.reduce(op, [parts[g] for g in range(c, parts.shape[0], FOLD_CHAINS)])
              for c in range(min(FOLD_CHAINS, parts.shape[0]))]
    return functools.reduce(op, chains)


def _flash_t_update(s, vt_blocks, m_scr, acc_scr):
    m_old = m_scr[...]
    m_new = jnp.maximum(m_old, jnp.max(_fold_rows(s, jnp.maximum), axis=0, keepdims=True))
    alpha = jnp.exp2(m_old - m_new)
    pb = jnp.exp2(s - m_new).astype(BF16)
    m_scr[...] = m_new
    pv = functools.reduce(lambda a, b: a + b, [
        jnp.dot(vt, pb[k0:k1], preferred_element_type=F32) for (k0, k1), vt in vt_blocks])
    acc_scr[...] = alpha * acc_scr[...] + pv


ONES_ROWS = 16


def _with_ones_rows(vt):
    shape = vt.shape[:-2]
    return jnp.concatenate([vt, jnp.ones(shape + (1, vt.shape[-1]), vt.dtype),
                            jnp.zeros(shape + (ONES_ROWS - 1, vt.shape[-1]), vt.dtype)], axis=-2)


def _flash_update(s, v, m_scr, l_scr, acc_scr):
    groups = [s[:, c:c + LANES] for c in range(0, s.shape[1], LANES)]
    m_old = m_scr[...]
    lane_max = functools.reduce(jnp.maximum, groups)
    m_new = jnp.maximum(m_old, jnp.max(lane_max, axis=-1, keepdims=True))
    alpha = jnp.exp2(m_old - m_new)
    p_groups = [jnp.exp2(g - m_new) for g in groups]
    l_scr[...] = alpha * l_scr[...] + functools.reduce(lambda a, b: a + b, p_groups)
    m_scr[...] = m_new
    pb = jnp.concatenate([g.astype(BF16) for g in p_groups], axis=1)
    acc_scr[...] = alpha * acc_scr[...] + jnp.dot(pb, v, preferred_element_type=F32)


def _diff_attn_kernel(lam_ref, q_ref, k_ref, v_ref, ng_ref, o_ref, m_scr, l_scr, acc_scr,
                      *, tq, tk, lam_init):
    i = pl.program_id(2)
    per_tile = tq // tk
    qs = jnp.concatenate(_split_pair(q_ref[0], True), axis=0)
    m_scr[...] = jnp.full(m_scr.shape, NEG, F32)
    l_scr[...] = jnp.zeros(l_scr.shape, F32)
    acc_scr[...] = jnp.zeros(acc_scr.shape, F32)

    def kv_block(j):
        start = pl.multiple_of(j * tk, tk)
        return k_ref[0, pl.ds(start, tk), :], v_ref[0, pl.ds(start, tk), :]

    def body(j, carry):
        k, v = kv_block(j)
        _flash_update(_qk(qs, k), v, m_scr, l_scr, acc_scr)
        return carry

    lax.fori_loop(0, i * per_tile, body, 0)
    for d in range(per_tile):
        k, v = kv_block(i * per_tile + d)
        s = _qk(qs, k)
        row = lax.broadcasted_iota(jnp.int32, s.shape, 0) % tq
        col = lax.broadcasted_iota(jnp.int32, s.shape, 1) + d * tk
        s = jnp.where((col // CHUNK) <= (row // CHUNK), s, NEG)
        _flash_update(s, v, m_scr, l_scr, acc_scr)

    o = acc_scr[...] / jnp.sum(l_scr[...], axis=-1, keepdims=True)
    o = o[:tq] - lam_ref[0] * o[tq:]
    o = o * lax.rsqrt(jnp.mean(o * o, axis=-1, keepdims=True) + EPS)
    o_ref[0] = ((o * ng_ref[...]) * (1.0 - lam_init)).astype(BF16)


def _diff_attention(rope_slab, plain_slab, lam, norm_g, lam_init, tq, tk):
    bsz, s, _ = rope_slab.shape
    assert tq % tk == 0
    kern = functools.partial(_diff_attn_kernel, tq=tq, tk=tk, lam_init=lam_init)
    cb = COL_TILE // LANES
    return pl.pallas_call(
        kern,
        grid=(bsz, A_HEADS, s // tq),
        in_specs=[
            pl.BlockSpec(memory_space=pltpu.SMEM),
            pl.BlockSpec((1, tq, LANES), lambda b, h, i: (b, i, R_AQ * cb + h)),
            pl.BlockSpec((1, s, LANES), lambda b, h, i: (b, 0, R_AK * cb + h)),
            pl.BlockSpec((1, s, LANES), lambda b, h, i: (b, 0, P_AV * cb + h)),
            pl.BlockSpec((1, LANES), lambda b, h, i: (0, 0)),
        ],
        out_specs=pl.BlockSpec((1, tq, LANES), lambda b, h, i: (b, i, h)),
        out_shape=jax.ShapeDtypeStruct((bsz, s, A_HEADS * LANES), BF16),
        scratch_shapes=[pltpu.VMEM((2 * tq, LANES), F32), pltpu.VMEM((2 * tq, LANES), F32),
                        pltpu.VMEM((2 * tq, LANES), F32)],
        compiler_params=_cparams(("arbitrary", "arbitrary", "arbitrary")),
        name="diff_attn",
    )(lam.reshape(1), rope_slab, rope_slab, plain_slab, norm_g.reshape(1, LANES))


KEY_NEG_INF = -2139095040
KEY_POS_INF = 2139095040
SEARCH_MAX_STEPS = 80
NO_TIE_LIMIT = 1e9
P3_PAIRS = 2
VT_ROWS = HEAD_DIM + ONES_ROWS


def _key_to_float(key):
    bits = jnp.where(key >= 0, key, (key - 1) ^ jnp.int32(0x7FFFFFFF))
    return pltpu.bitcast(bits, F32)


def _float_to_key(t):
    bits = pltpu.bitcast(t, jnp.int32)
    return jnp.where(bits >= 0, bits, (bits ^ jnp.int32(0x7FFFFFFF)) + 1)


def _dsa_kernel(q_ref, iq_ref, iw_ref, k_ref, ik_ref, vt_ref, o_ref,
                sc_scr, sc16_scr, qs_scr, m_scr, acc_scr, *, tq, topk):
    i = pl.program_id(1)
    nblk = i + 1
    npair = (nblk + 1) // 2
    nq = B_HEADS * tq

    iq = iq_ref[0]
    parts = []
    for p in range(IDX_HEADS // 2):
        parts += list(_split_pair(iq[:, p * LANES:(p + 1) * LANES], True))
    iqs = jnp.concatenate(parts, axis=0)
    iw_t = iw_ref[0].astype(F32).T

    def index_block(j):
        start = pl.multiple_of(j * tq, tq)
        ik = ik_ref[0, pl.ds(start, tq), :]
        score = None
        for h in range(IDX_HEADS):
            hs = jnp.maximum(_qk(ik, iqs[h * tq:(h + 1) * tq]), 0.0)
            score = iw_t[h:h + 1] * hs if score is None else score + iw_t[h:h + 1] * hs
        return score

    def store_scores(pair, rows, score):
        sc_scr[pair, rows, :] = score
        bits = pltpu.bitcast(score, jnp.int32)
        bits = (bits + ((bits >> 31) & 0xFFFF)) & jnp.int32(-65536)
        sc16_scr[pair, rows, :] = pltpu.bitcast(bits, F32).astype(BF16)

    def score_slot(j):
        return j // 2, pl.ds(pl.multiple_of((j % 2) * tq, tq), tq)

    def p1_body(j, carry):
        store_scores(*score_slot(j), index_block(j))
        return carry

    lax.fori_loop(0, i, p1_body, 0)
    score = index_block(i)
    key_i = lax.broadcasted_iota(jnp.int32, score.shape, 0)
    qry_i = lax.broadcasted_iota(jnp.int32, score.shape, 1)
    store_scores(*score_slot(i), jnp.where((key_i // CHUNK) <= (qry_i // CHUNK), score, -jnp.inf))

    @pl.when(nblk % 2 == 1)
    def _():
        store_scores(npair - 1, slice(tq, 2 * tq), jnp.full((tq, tq), -jnp.inf, F32))

    kf = float(topk)
    add = lambda a, b: a + b

    def make_scan(ref, group):
        dtype = ref.dtype
        one, zero = jnp.ones((), dtype), jnp.zeros((), dtype)

        def scan(hit_fn, ext_fn=None, ext_op=None, ext_init=0.0):
            def body(jj, carry):
                cnt, ext = carry
                blk = ref[jj]
                cnt = cnt + _fold_rows(jnp.where(hit_fn(blk), one, zero), add, group).astype(F32)
                if ext_fn is not None:
                    ext = ext_op(ext, _fold_rows(ext_fn(blk), ext_op, group))
                return cnt, ext
            cnt, ext = lax.fori_loop(0, npair, body, (jnp.zeros((group, tq), F32),
                                                      jnp.full((group, tq), ext_init, dtype)))
            return jnp.sum(cnt, axis=0, keepdims=True), ext
        return scan

    scan16, scan32 = make_scan(sc16_scr, 16), make_scan(sc_scr, 8)
    key16 = lambda t: _float_to_key(t) >> 16
    thr16 = lambda k: _key_to_float(k << 16)

    c_ge0, ext = scan16(lambda blk: blk >= 0.0, lambda blk: blk, jnp.maximum, -jnp.inf)
    col_max = jnp.max(ext, axis=0, keepdims=True).astype(F32)
    c_gt0, ext = scan16(lambda blk: blk > 0.0,
                        lambda blk: jnp.where(blk > -jnp.inf, blk, jnp.inf), jnp.minimum, jnp.inf)
    col_min = jnp.min(ext, axis=0, keepdims=True).astype(F32)
    qpos = i * tq + lax.broadcasted_iota(jnp.int32, (1, tq), 1)
    n_valid = (((qpos // CHUNK) + 1) * CHUNK).astype(F32)
    open_q = n_valid < kf
    above = c_gt0 >= kf
    below = c_ge0 < kf
    zero_tie = jnp.logical_not(jnp.logical_or(above, below))
    ikey = lambda v: jnp.full((1, tq), v, jnp.int32)
    lo_k = jnp.where(below, key16(col_min), ikey(0))
    hi_k = jnp.where(above, key16(col_max) + 1, jnp.where(below, ikey(0), ikey(1)))
    c_lo = jnp.where(below, n_valid, c_ge0)
    c_hi = jnp.where(above, 0.0, jnp.where(below, c_ge0, c_gt0))

    def search(lo_k, hi_k, c_lo, c_hi, done, to_thr, to_key, count, key_min, key_max):
        def cond(st):
            return jnp.logical_and(st[0] < SEARCH_MAX_STEPS, jnp.min(st[5]) == 0)

        def step(st):
            n, lo_k, hi_k, c_lo, c_hi, done, force_bisect = st
            finite = jnp.logical_and(lo_k > key_min, hi_k < key_max)
            t_lo, t_hi = to_thr(lo_k), to_thr(hi_k)
            log_lo = jnp.log(c_lo)
            frac = (log_lo - math.log(kf - 0.5)) / (log_lo - jnp.log(jnp.maximum(c_hi, 0.5)))
            k_interp = to_key(t_lo + (t_hi - t_lo) * frac)
            k_mid = (lo_k & hi_k) + ((lo_k ^ hi_k) >> 1)
            interp = jnp.logical_and(finite, force_bisect == 0)
            k = jnp.where(interp, k_interp, k_mid)
            k = jnp.minimum(jnp.maximum(k, lo_k + 1), hi_k - 1)
            c = count(to_thr(k))
            ok = c >= kf
            live = done == 0
            new_lo = jnp.where(jnp.logical_and(live, ok), k, lo_k)
            new_hi = jnp.where(jnp.logical_and(live, jnp.logical_not(ok)), k, hi_k)
            new_c_lo = jnp.where(jnp.logical_and(live, ok), c, c_lo)
            new_c_hi = jnp.where(jnp.logical_and(live, jnp.logical_not(ok)), c, c_hi)
            width = lambda a, b: b.astype(F32) - a.astype(F32)
            slow = width(new_lo, new_hi) > 0.5 * width(lo_k, hi_k)
            new_force = jnp.logical_and(interp, slow).astype(jnp.int32)
            finished = jnp.logical_or(new_c_lo == kf, new_hi - new_lo == 1)
            new_done = jnp.maximum(done, finished.astype(jnp.int32))
            return n + 1, new_lo, new_hi, new_c_lo, new_c_hi, new_done, new_force

        state = (jnp.int32(0), lo_k, hi_k, c_lo, c_hi, done.astype(jnp.int32), ikey(0))
        return lax.while_loop(cond, step, state)[1:5]

    settled = jnp.logical_or(open_q, zero_tie)
    lo_k, hi_k, c_lo, c_hi = search(
        lo_k, hi_k, c_lo, c_hi, settled, thr16, key16,
        lambda t: scan16(lambda blk: blk >= t.astype(BF16))[0], KEY_NEG_INF >> 16, KEY_POS_INF >> 16)
    settled = jnp.logical_or(settled, c_lo == kf)
    lo_k, hi_k = lo_k << 16, jnp.where(zero_tie, ikey(1), hi_k << 16)
    lo_k, hi_k, c_lo, c_hi = search(
        lo_k, hi_k, c_lo, c_hi, settled, _key_to_float, _float_to_key,
        lambda t: scan32(lambda blk: blk >= t)[0], KEY_NEG_INF, KEY_POS_INF)
    tau = jnp.where(open_q, -jnp.inf, _key_to_float(lo_k))
    need = jnp.where(open_q, 0.0, jnp.where(c_lo == kf, NO_TIE_LIMIT, kf - c_hi))

    q = q_ref[0]
    for p in range(B_HEADS // 2):
        even, odd = _split_pair(q[:, p * LANES:(p + 1) * LANES], True)
        qs_scr[p * tq:(p + 1) * tq, :] = even
        qs_scr[(B_HEADS // 2 + p) * tq:(B_HEADS // 2 + p + 1) * tq, :] = odd
    m_scr[...] = jnp.full(m_scr.shape, NEG, F32)
    acc_scr[...] = jnp.zeros(acc_scr.shape, F32)
    key_i = lax.broadcasted_iota(jnp.int32, (tq, tq), 0)
    lower = (lax.broadcasted_iota(jnp.int32, (tq, tq), 1) <= key_i).astype(BF16)

    def select_bias(jj, tie_count):
        sblk = sc_scr[jj]
        tie = sblk == tau
        tie01 = jnp.where(tie, 1.0, 0.0).astype(BF16)
        rank_a = tie_count + jnp.dot(lower, tie01[0:tq], preferred_element_type=F32)
        rank_b = rank_a[tq - 1:tq, :] + jnp.dot(lower, tie01[tq:2 * tq], preferred_element_type=F32)
        rank = jnp.concatenate([rank_a, rank_b], axis=0)
        keep_tie = jnp.where(rank <= need, 0.0, NEG)
        return jnp.where(sblk > tau, 0.0, jnp.where(tie, keep_tie, NEG)), rank[2 * tq - 1:2 * tq, :]

    def p3_step(jj, pairs, tie_count):
        nkeys = pairs * 2 * tq
        start = pl.multiple_of(jj * 2 * tq, 2 * tq)
        biases = []
        for t in range(pairs):
            bias, tie_count = select_bias(jj + t, tie_count)
            biases.append(bias)
        bias = jnp.concatenate(biases, axis=0)
        s = _qk(k_ref[0, pl.ds(start, nkeys), :], qs_scr[...])
        s = jnp.concatenate([s[:, h * tq:(h + 1) * tq] + bias for h in range(B_HEADS)], axis=1)
        _flash_t_update(s, [((t * tq, (t + 1) * tq), vt_ref[0, 2 * jj + t]) for t in range(2 * pairs)],
                        m_scr, acc_scr)
        return tie_count

    tie_count = lax.fori_loop(0, npair // P3_PAIRS,
                              lambda g, tc: p3_step(g * P3_PAIRS, P3_PAIRS, tc),
                              jnp.zeros((1, tq), F32))
    for r in range(1, P3_PAIRS):
        @pl.when(npair % P3_PAIRS == r)
        def _():
            p3_step(npair - r, r, tie_count)

    acc = acc_scr[...]
    o = acc[0:HEAD_DIM] / acc[HEAD_DIM:HEAD_DIM + 1]
    half_cols = (B_HEADS // 2) * tq
    for p in range(B_HEADS // 2):
        pair = jnp.concatenate([o[:, p * tq:(p + 1) * tq],
                                o[:, half_cols + p * tq:half_cols + (p + 1) * tq]], axis=0)
        o_ref[0, :, p * LANES:(p + 1) * LANES] = pair.T.astype(BF16)


def _dsa_attention(rope_slab, plain_slab, tq):
    bsz, s, _ = rope_slab.shape
    topk = min(TOPK_MAX, s // 4)
    nblk = s // tq
    assert tq >= topk and s % tq == 0 and nblk % 2 == 0
    kern = functools.partial(_dsa_kernel, tq=tq, topk=topk)
    idx0 = R_IDX * COL_TILE
    misc0 = P_MISC * COL_TILE
    iq_w = IDX_HEADS * HEAD_DIM
    width = B_HEADS * HEAD_DIM
    vt = _with_ones_rows(
        plain_slab[:, :, misc0:misc0 + HEAD_DIM].reshape(bsz, nblk, tq, HEAD_DIM).transpose(0, 1, 3, 2))
    return pl.pallas_call(
        kern,
        grid=(bsz, s // tq),
        in_specs=[
            pl.BlockSpec((1, tq, COL_TILE), lambda b, i: (b, i, R_BQ)),
            pl.BlockSpec((1, tq, iq_w), lambda b, i: (b, i, idx0 // iq_w)),
            pl.BlockSpec((1, tq, LANES), lambda b, i: (b, i, (misc0 + LANES) // LANES)),
            pl.BlockSpec((1, s, LANES), lambda b, i: (b, 0, (idx0 + iq_w) // LANES)),
            pl.BlockSpec((1, s, LANES), lambda b, i: (b, 0, (idx0 + iq_w + LANES) // LANES)),
            pl.BlockSpec((1, nblk, VT_ROWS, tq), lambda b, i: (b, 0, 0, 0)),
        ],
        out_specs=pl.BlockSpec((1, tq, width), lambda b, i: (b, i, 0)),
        out_shape=jax.ShapeDtypeStruct((bsz, s, width), BF16),
        scratch_shapes=[
            pltpu.VMEM((nblk // 2, 2 * tq, tq), F32),
            pltpu.VMEM((nblk // 2, 2 * tq, tq), BF16),
            pltpu.VMEM((B_HEADS * tq, LANES), BF16),
            pltpu.VMEM((1, B_HEADS * tq), F32),
            pltpu.VMEM((VT_ROWS, B_HEADS * tq), F32),
        ],
        compiler_params=_cparams(("arbitrary", "arbitrary")),
        name="dsa_attn",
    )(rope_slab, rope_slab, plain_slab, rope_slab, rope_slab, vt)


BAND_TQ = 2 * CHUNK
BAND_KEYS = (C_LEFT_CHUNKS + BAND_TQ // CHUNK) * CHUNK
BAND_PAD = C_LEFT_CHUNKS * CHUNK


def _band_kernel(q_ref, k_ref, v_ref, bias_ref, o_ref):
    i = pl.program_id(1)
    tq = BAND_TQ
    start = pl.multiple_of(i * tq, tq)
    col = lax.broadcasted_iota(jnp.int32, (2 * tq, BAND_KEYS), 1)
    key_ok = col + start >= BAND_PAD
    for p in range(C_HEADS // 2):
        lanes = slice(p * LANES, (p + 1) * LANES)
        k = k_ref[0, pl.ds(start, BAND_KEYS), lanes]
        v = v_ref[0, pl.ds(start, BAND_KEYS), lanes]
        qs = jnp.concatenate(_split_pair(q_ref[0, :, lanes], False), axis=0)
        s = _qk(qs, k) + bias_ref[2 * p:2 * p + 2].reshape(2 * tq, BAND_KEYS)
        s = jnp.where(key_ok, s, NEG)
        m = jnp.max(s, axis=-1, keepdims=True)
        e = jnp.exp2(s - m)
        l = jnp.sum(e, axis=-1, keepdims=True)
        eb = e.astype(BF16)
        v_even, v_odd = _split_pair(v, False)
        o = (jnp.dot(eb[:tq], v_even, preferred_element_type=F32) / l[:tq]
             + jnp.dot(eb[tq:], v_odd, preferred_element_type=F32) / l[tq:])
        o_ref[0, :, lanes] = o.astype(BF16)


def _band_bias(rel_bias):
    r = jnp.arange(BAND_TQ)[:, None]
    cidx = jnp.arange(BAND_KEYS)[None, :]
    n_diag = BAND_TQ + BAND_KEYS - 1
    rel = BAND_PAD - (BAND_KEYS - 1) + jnp.arange(n_diag)
    g = rel_bias.astype(F32)[:, jnp.clip(rel, -REL_CLIP, REL_CLIP) + REL_CLIP] * LOG2E
    skew = jnp.tile(g, (1, BAND_TQ + 2))[:, :BAND_TQ * (n_diag + 1)]
    bias = skew.reshape(-1, BAND_TQ, n_diag + 1)[:, :, :BAND_KEYS][:, :, ::-1]
    dchunk = (r // CHUNK + C_LEFT_CHUNKS) - cidx // CHUNK
    in_band = jnp.logical_and(dchunk >= 0, dchunk <= C_LEFT_CHUNKS)
    return jnp.where(in_band[None], bias, NEG)


def _band_attention(plain_slab, kp, vp, bias):
    bsz, s, _ = plain_slab.shape
    tq = BAND_TQ
    sp = kp.shape[1]
    width = C_HEADS * HEAD_DIM
    return pl.pallas_call(
        _band_kernel,
        grid=(bsz, s // tq),
        in_specs=[
            pl.BlockSpec((1, tq, width), lambda b, i: (b, i, P_CQ)),
            pl.BlockSpec((1, sp, width), lambda b, i: (b, 0, 0)),
            pl.BlockSpec((1, sp, width), lambda b, i: (b, 0, 0)),
            pl.BlockSpec((C_HEADS, tq, BAND_KEYS), lambda b, i: (0, 0, 0)),
        ],
        out_specs=pl.BlockSpec((1, tq, width), lambda b, i: (b, i, 0)),
        out_shape=jax.ShapeDtypeStruct((bsz, s, width), BF16),
        compiler_params=_cparams(("arbitrary", "arbitrary")),
        name="band_attn",
    )(plain_slab, kp, vp, bias)


def _merge_kernel(x_ref, g1_ref, ya_ref, yb_ref, yc_ref, ga_ref, gb_ref, gc_ref,
                  wa_ref, wb_ref, wc_ref, wo_ref, o_ref):
    for r0 in range(0, x_ref.shape[1], PROJ_ROWS):
        rows = slice(r0, r0 + PROJ_ROWS)

        def branch(y_ref, w_ref, gate_ref):
            return gate_ref[0, rows, :].astype(F32) * jnp.dot(y_ref[0, rows, :], w_ref[...],
                                                             preferred_element_type=F32)

        merged = (branch(ya_ref, wa_ref, ga_ref) + branch(yb_ref, wb_ref, gb_ref)
                  + branch(yc_ref, wc_ref, gc_ref))
        mixed = jnp.dot(merged.astype(BF16), wo_ref[...], preferred_element_type=F32)
        o_ref[0, rows, :] = x_ref[0, rows, :] + g1_ref[0] * mixed


def _merge(x, g1, ya, yb, yc, gate_slab, wa, wb, wc, wo, tm):
    bsz, s, d = x.shape
    tok = lambda b, i: (b, i, 0)
    full = lambda b, i: (0, 0)
    return pl.pallas_call(
        _merge_kernel,
        grid=(bsz, s // tm),
        in_specs=[
            pl.BlockSpec((1, tm, d), tok),
            pl.BlockSpec((1, 1, d), lambda b, i: (b, 0, 0)),
            pl.BlockSpec((1, tm, ya.shape[2]), tok),
            pl.BlockSpec((1, tm, yb.shape[2]), tok),
            pl.BlockSpec((1, tm, yc.shape[2]), tok),
            pl.BlockSpec((1, tm, d), lambda b, i: (b, i, 0)),
            pl.BlockSpec((1, tm, d), lambda b, i: (b, i, 1)),
            pl.BlockSpec((1, tm, d), lambda b, i: (b, i, 2)),
            pl.BlockSpec(wa.shape, full),
            pl.BlockSpec(wb.shape, full),
            pl.BlockSpec(wc.shape, full),
            pl.BlockSpec(wo.shape, full),
        ],
        out_specs=pl.BlockSpec((1, tm, d), tok),
        out_shape=jax.ShapeDtypeStruct((bsz, s, d), F32),
        compiler_params=_cparams(("arbitrary", "arbitrary")),
        name="merge",
    )(x, g1, ya, yb, yc, gate_slab, gate_slab, gate_slab, wa, wb, wc, wo)


def _router_kernel(x_ref, g_ref, sc_ref, sh_ref, rw_ref, rb_ref, u_ref, comb_ref):
    u = _modulated_norm(x_ref[0], g_ref[...], sc_ref[0], sh_ref[0])
    u_ref[0] = u.astype(BF16)
    logits = lax.dot_general(rw_ref[...], u, (((1,), (1,)), ((), ())),
                             preferred_element_type=F32, precision=lax.Precision.HIGHEST)
    aff = jax.nn.sigmoid(logits)
    sel = aff + rb_ref[...]
    rows = [sel[e:e + 1] for e in range(N_EXPERTS)]
    gscore = []
    for g in range(N_GROUPS):
        r = rows[g * EXPERTS_PER_GROUP:(g + 1) * EXPERTS_PER_GROUP]
        best = None
        for a in range(EXPERTS_PER_GROUP):
            for b in range(a + 1, EXPERTS_PER_GROUP):
                pair = r[a] + r[b]
                best = pair if best is None else jnp.maximum(best, pair)
        gscore.append(best)
    gmax = functools.reduce(jnp.maximum, gscore)
    taken = jnp.zeros_like(gmax) > 1.0
    in_best = []
    for g in range(N_GROUPS):
        is_g = jnp.logical_and(gscore[g] == gmax, jnp.logical_not(taken))
        in_best.append(is_g)
        taken = jnp.logical_or(taken, is_g)
    keep = []
    for e in range(N_EXPERTS):
        g = e // EXPERTS_PER_GROUP
        rank = jnp.zeros_like(gmax)
        for o in range(g * EXPERTS_PER_GROUP, (g + 1) * EXPERTS_PER_GROUP):
            if o == e:
                continue
            ahead = rows[o] > rows[e] if o > e else rows[o] >= rows[e]
            rank = rank + jnp.where(ahead, 1.0, 0.0)
        keep.append(jnp.logical_and(in_best[g], rank < 2.0))
    w = [jnp.where(keep[e], aff[e:e + 1], 0.0) for e in range(N_EXPERTS)]
    total = functools.reduce(lambda a, b: a + b, w)
    comb = jnp.concatenate([we / total for we in w]
                           + [jnp.zeros((LANES - N_EXPERTS, total.shape[1]), F32)], axis=0)
    comb_ref[0] = comb.T


def _router(x, g, sc, sh, router_w, router_b, tm):
    bsz, s, d = x.shape
    tok = lambda b, i: (b, i, 0)
    return pl.pallas_call(
        _router_kernel,
        grid=(bsz, s // tm),
        in_specs=[
            pl.BlockSpec((1, tm, d), tok),
            pl.BlockSpec((1, d), lambda b, i: (0, 0)),
            pl.BlockSpec((1, 1, d), lambda b, i: (b, 0, 0)),
            pl.BlockSpec((1, 1, d), lambda b, i: (b, 0, 0)),
            pl.BlockSpec((N_EXPERTS, d), lambda b, i: (0, 0)),
            pl.BlockSpec((N_EXPERTS, 1), lambda b, i: (0, 0)),
        ],
        out_specs=[pl.BlockSpec((1, tm, d), tok), pl.BlockSpec((1, tm, LANES), tok)],
        out_shape=[jax.ShapeDtypeStruct((bsz, s, d), BF16),
                   jax.ShapeDtypeStruct((bsz, s, LANES), F32)],
        compiler_params=_cparams(("arbitrary", "arbitrary")),
        name="router",
    )(x, g.reshape(1, d), sc, sh, router_w.T, router_b.reshape(N_EXPERTS, 1))


MOE_SUB = 512
MOE_CAP = 128


def _expert_ffn(rows, w1_ref, w3_ref, w2_ref):
    h1 = jnp.dot(rows, w1_ref[0], preferred_element_type=F32)
    h3 = jnp.dot(rows, w3_ref[0], preferred_element_type=F32)
    h = (h1 * jax.nn.sigmoid(h1)) * h3
    return jnp.dot(h.astype(BF16), w2_ref[0], preferred_element_type=F32)


def _moe_kernel(x_ref, g2_ref, u_ref, comb_ref, w1_ref, w3_ref, w2_ref, o_ref,
                acc_scr, pos_scr, pos_t_scr, comb_t_scr, scatter_scr, y_scr):
    e = pl.program_id(2)
    tm = u_ref.shape[1]
    sub = min(MOE_SUB, tm)
    subs = [slice(r0, r0 + sub) for r0 in range(0, tm, sub)]

    @pl.when(e == 0)
    def _():
        acc_scr[...] = jnp.zeros(acc_scr.shape, F32)
        before = (lax.broadcasted_iota(jnp.int32, (sub, sub), 1)
                  < lax.broadcasted_iota(jnp.int32, (sub, sub), 0)).astype(BF16)
        for rows in subs:
            comb0 = comb_ref[0, rows, :]
            member = jnp.where(comb0 > 0.0, 1.0, 0.0).astype(BF16)
            pos = jnp.dot(before, member, preferred_element_type=F32)
            pos_scr[rows, :] = pos
            pos_t_scr[:, rows] = pos.T
            comb_t_scr[:, rows] = comb0.T

    comb = comb_ref[0]
    lane = lax.broadcasted_iota(jnp.int32, comb.shape, 1)
    ce = jnp.sum(jnp.where(lane == e, comb, 0.0), axis=-1, keepdims=True)
    pe = jnp.sum(jnp.where(lane == e, pos_scr[...], 0.0), axis=-1, keepdims=True)
    member_e = jnp.where(ce > 0.0, 1.0, 0.0)
    fullest = functools.reduce(jnp.maximum, [jnp.sum(member_e[rows]) for rows in subs])
    routed = fullest <= float(MOE_CAP)
    slots = pl.ds(pl.multiple_of(e * MOE_CAP, MOE_CAP), MOE_CAP)

    @pl.when(routed)
    def _():
        slot_l = lax.broadcasted_iota(jnp.int32, (1, MOE_CAP), 1).astype(F32)
        scatter_scr[e] = jnp.where(jnp.logical_and(ce > 0.0, pe == slot_l), 1.0, 0.0).astype(BF16)
        slot_s = lax.broadcasted_iota(jnp.int32, (MOE_CAP, 1), 0).astype(F32)
        packed, weights = [], []
        for rows in subs:
            ce_t = comb_t_scr[pl.ds(e, 1), rows]
            pe_t = pos_t_scr[pl.ds(e, 1), rows]
            hit = jnp.logical_and(ce_t > 0.0, pe_t == slot_s)
            weights.append(jnp.sum(jnp.where(hit, ce_t, 0.0), axis=-1, keepdims=True))
            packed.append(jnp.dot(jnp.where(hit, 1.0, 0.0).astype(BF16), u_ref[0, rows, :],
                                  preferred_element_type=F32).astype(BF16))
        y = jnp.concatenate(weights, axis=0) * _expert_ffn(jnp.concatenate(packed, axis=0),
                                                           w1_ref, w3_ref, w2_ref)
        for k in range(len(subs)):
            y_scr[k, slots, :] = y[k * MOE_CAP:(k + 1) * MOE_CAP].astype(BF16)

    @pl.when(jnp.logical_not(routed))
    def _():
        scatter_scr[e] = jnp.zeros(scatter_scr.shape[1:], BF16)
        for k in range(len(subs)):
            y_scr[k, slots, :] = jnp.zeros((MOE_CAP, y_scr.shape[2]), BF16)
        acc_scr[...] += ce * _expert_ffn(u_ref[0], w1_ref, w3_ref, w2_ref)

    @pl.when(e == pl.num_programs(2) - 1)
    def _():
        for k, rows in enumerate(subs):
            scatter = jnp.concatenate([scatter_scr[j, rows, :] for j in range(N_EXPERTS)], axis=1)
            y = acc_scr[rows, :] + jnp.dot(scatter, y_scr[k], preferred_element_type=F32)
            o_ref[0, rows, :] = x_ref[0, rows, :] + g2_ref[0] * y


def _moe(x, g2, u, comb, w1, w3, w2, tm):
    bsz, s, d = x.shape
    ne, _, dff = w1.shape
    assert ne == N_EXPERTS
    tok = lambda b, i, e: (b, i, 0)
    return pl.pallas_call(
        _moe_kernel,
        grid=(bsz, s // tm, ne),
        in_specs=[
            pl.BlockSpec((1, tm, d), tok),
            pl.BlockSpec((1, 1, d), lambda b, i, e: (b, 0, 0)),
            pl.BlockSpec((1, tm, d), tok),
            pl.BlockSpec((1, tm, LANES), tok),
            pl.BlockSpec((1, d, dff), lambda b, i, e: (e, 0, 0)),
            pl.BlockSpec((1, d, dff), lambda b, i, e: (e, 0, 0)),
            pl.BlockSpec((1, dff, d), lambda b, i, e: (e, 0, 0)),
        ],
        out_specs=pl.BlockSpec((1, tm, d), tok),
        out_shape=jax.ShapeDtypeStruct((bsz, s, d), F32),
        scratch_shapes=[
            pltpu.VMEM((tm, d), F32),
            pltpu.VMEM((tm, LANES), F32),
            pltpu.VMEM((LANES, tm), F32),
            pltpu.VMEM((LANES, tm), F32),
            pltpu.VMEM((ne, tm, MOE_CAP), BF16),
            pltpu.VMEM((pl.cdiv(tm, MOE_SUB), ne * MOE_CAP, d), BF16),
        ],
        compiler_params=_cparams(("arbitrary", "arbitrary", "arbitrary")),
        name="moe",
    )(x, g2, u, comb, w1, w3, w2)


def _final_norm_kernel(x_ref, g_ref, o_ref):
    x = x_ref[0]
    o_ref[0] = (x * lax.rsqrt(jnp.mean(x * x, axis=-1, keepdims=True) + EPS)) * g_ref[...]


def _final_norm(x, g, tm):
    bsz, s, d = x.shape
    tok = lambda b, i: (b, i, 0)
    return pl.pallas_call(
        _final_norm_kernel,
        grid=(bsz, s // tm),
        in_specs=[pl.BlockSpec((1, tm, d), tok), pl.BlockSpec((1, d), lambda b, i: (0, 0))],
        out_specs=pl.BlockSpec((1, tm, d), tok),
        out_shape=jax.ShapeDtypeStruct((bsz, s, d), F32),
        compiler_params=_cparams(("arbitrary", "arbitrary")),
        name="final_norm",
    )(x, g.reshape(1, d))


def _tile(s, want):
    t = min(want, s)
    assert s % t == 0
    return t


def _tiles(s):
    return dict(
        rows=_tile(s, 1024),
        diff_q=_tile(s, 1024),
        diff_k=_tile(s, 512),
        dsa_q=_tile(s, 256),
        merge=_tile(s, 512),
    )


def kernel(x, c, positions, norm1_g, norm2_g, w_mod, b_mod, w_in, lambda_q1, lambda_k1, lambda_q2, lambda_k2, a_norm_g, c_rel_bias, w_branch_a, w_branch_b, w_branch_c, w_out, router_w, router_b, exp_w1, exp_w3, exp_w2, final_g):
    bsz, s, d = x.shape
    depth = w_mod.shape[0]
    t = _tiles(s)
    tm = t["rows"]

    mod = _modulation(c, w_mod, b_mod)
    rope_tables = _rope_tables(positions)

    for layer in range(depth):
        lam_init = 0.8 - 0.6 * math.exp(-0.3 * layer)
        sh1, sc1, g1, sh2, sc2, g2 = [m[:, None, :] for m in jnp.split(mod[layer], 6, axis=-1)]
        lam = (jnp.exp(jnp.sum(lambda_q1[layer] * lambda_k1[layer]))
               - jnp.exp(jnp.sum(lambda_q2[layer] * lambda_k2[layer])) + lam_init)

        w_rope, w_gate, w_plain = _build_weights(w_in[layer])
        u = _norm(x, norm1_g[layer], sc1, sh1, tm)
        rope_slab = _project(u, w_rope, rope_tables, "rope", tm)
        gate_slab = _project(u, w_gate, (), "gate", tm)
        plain_slab = _project(u, w_plain, (), "plain", tm)

        ya = _diff_attention(rope_slab, plain_slab, lam, a_norm_g[layer], lam_init,
                             t["diff_q"], t["diff_k"])
        yb = _dsa_attention(rope_slab, plain_slab, t["dsa_q"])
        pad = ((0, 0), (BAND_PAD, 0), (0, 0))
        kp = jnp.pad(plain_slab[:, :, P_CK * COL_TILE:(P_CK + 1) * COL_TILE], pad)
        vp = jnp.pad(plain_slab[:, :, P_CV * COL_TILE:(P_CV + 1) * COL_TILE], pad)
        yc = _band_attention(plain_slab, kp, vp, _band_bias(c_rel_bias[layer]))
        x = _merge(x, g1, ya, yb, yc, gate_slab,
                   w_branch_a[layer].astype(BF16), w_branch_b[layer].astype(BF16),
                   w_branch_c[layer].astype(BF16), w_out[layer].astype(BF16), t["merge"])

        u, comb = _router(x, norm2_g[layer], sc2, sh2, router_w, router_b, t["merge"])
        x = _moe(x, g2, u, comb, exp_w1[layer].astype(BF16), exp_w3[layer].astype(BF16),
                 exp_w2[layer].astype(BF16), tm)

    return _final_norm(x, final_g, tm)
```

```python
import functools
import math

import jax
import jax.numpy as jnp
from jax import lax
from jax.experimental import pallas as pl
from jax.experimental.pallas import tpu as pltpu

F32 = jnp.float32
BF16 = jnp.bfloat16

CHUNK = 64
ROPE_THETA = 10000.0
EPS = 1e-6
A_HEADS = 4
HEAD_DIM = 64
B_HEADS = 8
IDX_HEADS = 4
TOPK_MAX = 256
C_HEADS = 8
C_LEFT_CHUNKS = 8
REL_CLIP = 256
N_EXPERTS = 16
N_GROUPS = 4
EXPERTS_PER_GROUP = 4
N_BRANCHES = 3

LANES = 128
NEG = -1e30
LOG2E = math.log2(math.e)
VMEM_LIMIT = 56 * 1024 * 1024

COL_TILE = 512
PROJ_ROWS = 256
R_AQ, R_AK, R_BQ, R_IDX = 0, 1, 2, 3
P_AV, P_CQ, P_CK, P_CV, P_MISC = 0, 1, 2, 3, 4


def _cparams(sem):
    return pltpu.CompilerParams(dimension_semantics=sem, vmem_limit_bytes=VMEM_LIMIT)


def _mod_kernel(c_ref, w_ref, b_ref, o_ref):
    c = c_ref[...]
    ca = c * jax.nn.sigmoid(c)
    o_ref[0] = jnp.dot(ca, w_ref[0], preferred_element_type=F32) + b_ref[0]


def _modulation(c, w_mod, b_mod):
    depth, d, n6 = w_mod.shape
    bsz = c.shape[0]
    tn = 1024
    return pl.pallas_call(
        _mod_kernel,
        grid=(depth, n6 // tn),
        in_specs=[
            pl.BlockSpec((bsz, d), lambda l, j: (0, 0)),
            pl.BlockSpec((1, d, tn), lambda l, j: (l, 0, j)),
            pl.BlockSpec((1, 1, tn), lambda l, j: (l, 0, j)),
        ],
        out_specs=pl.BlockSpec((1, bsz, tn), lambda l, j: (l, 0, j)),
        out_shape=jax.ShapeDtypeStruct((depth, bsz, n6), F32),
        compiler_params=_cparams(("arbitrary", "arbitrary")),
        name="modulation",
    )(c, w_mod, b_mod.reshape(depth, 1, n6))


def _modulated_norm(x, g, sc, sh):
    y = x * lax.rsqrt(jnp.mean(x * x, axis=-1, keepdims=True) + EPS)
    return (y * g) * (1.0 + sc) + sh


def _norm_kernel(x_ref, g_ref, sc_ref, sh_ref, u_ref):
    u_ref[0] = _modulated_norm(x_ref[0], g_ref[...], sc_ref[0], sh_ref[0]).astype(BF16)


def _norm(x, g, sc, sh, tm):
    bsz, s, d = x.shape
    tok = lambda b, i: (b, i, 0)
    per_batch = lambda b, i: (b, 0, 0)
    return pl.pallas_call(
        _norm_kernel,
        grid=(bsz, s // tm),
        in_specs=[pl.BlockSpec((1, tm, d), tok), pl.BlockSpec((1, d), lambda b, i: (0, 0)),
                  pl.BlockSpec((1, 1, d), per_batch), pl.BlockSpec((1, 1, d), per_batch)],
        out_specs=pl.BlockSpec((1, tm, d), tok),
        out_shape=jax.ShapeDtypeStruct((bsz, s, d), BF16),
        compiler_params=_cparams(("arbitrary", "arbitrary")),
        name="norm",
    )(x, g.reshape(1, d), sc, sh)


def _proj_kernel(*refs, mode):
    if mode == "rope":
        u_ref, cos_ref, sin_ref, w_ref, o_ref = refs
    else:
        u_ref, w_ref, o_ref = refs
    tm = u_ref.shape[1]
    for r0 in range(0, tm, PROJ_ROWS):
        rows = slice(r0, r0 + PROJ_ROWS)
        acc = jnp.dot(u_ref[0, rows, :], w_ref[...], preferred_element_type=F32)
        if mode == "rope":
            cos, sin = cos_ref[0, rows, :], sin_ref[0, rows, :]
            groups = [acc[:, c:c + LANES] for c in range(0, COL_TILE, LANES)]
            acc = jnp.concatenate(
                [g * cos + pltpu.roll(g, LANES // 2, axis=1) * sin for g in groups], axis=1)
        elif mode == "gate":
            acc = 1.0 / (1.0 + jnp.exp(-acc))
        o_ref[0, rows, :] = acc.astype(BF16)


def _project(u, w, rope_tables, mode, tm):
    bsz, s, d = u.shape
    ncols = w.shape[1]
    tok = lambda b, i, j: (b, i, 0)
    in_specs = [pl.BlockSpec((1, tm, d), tok)]
    in_specs += [pl.BlockSpec((1, tm, LANES), tok) for _ in rope_tables]
    in_specs += [pl.BlockSpec((d, COL_TILE), lambda b, i, j: (0, j))]
    return pl.pallas_call(
        functools.partial(_proj_kernel, mode=mode),
        grid=(bsz, s // tm, ncols // COL_TILE),
        in_specs=in_specs,
        out_specs=pl.BlockSpec((1, tm, COL_TILE), lambda b, i, j: (b, i, j)),
        out_shape=jax.ShapeDtypeStruct((bsz, s, ncols), BF16),
        compiler_params=_cparams(("arbitrary", "arbitrary", "arbitrary")),
        name="proj_" + mode,
    )(u, *rope_tables, w)


def _pair_layout(w):
    half = HEAD_DIM // 2
    col = jnp.arange(w.shape[1])
    base, r = (col // LANES) * LANES, col % LANES
    src = base + ((r // half) % 2) * HEAD_DIM + (r // HEAD_DIM) * half + r % half
    return w[:, src]


def _build_weights(w_in):
    sizes = (512, 512, 512, 512, 64, 64, 256, 64, 4, 512, 512, 512, 3072)
    parts, start = [], 0
    for sz in sizes:
        parts.append(w_in[:, start:start + sz])
        start += sz
    aq, ak, av, bq, bk, bv, iq, ik, iw, cq, ck, cv, gates = parts
    d = w_in.shape[0]
    qscale = HEAD_DIM ** -0.5 * LOG2E
    iw_scale = IDX_HEADS ** -0.5 * HEAD_DIM ** -0.5
    zeros = lambda n: jnp.zeros((d, n), w_in.dtype)
    w_rope = _pair_layout(jnp.concatenate([aq * qscale, ak, bq * qscale, iq, bk, bk, ik, ik], axis=1))
    w_plain = jnp.concatenate([av, cq * qscale, ck, cv,
                               bv, bv, iw * iw_scale, zeros(LANES - IDX_HEADS),
                               zeros(COL_TILE - 2 * LANES)], axis=1)
    return w_rope.astype(BF16), gates.astype(BF16), w_plain.astype(BF16)


def _rope_tables(positions):
    half = HEAD_DIM // 2
    inv = ROPE_THETA ** (-jnp.arange(half, dtype=F32) / half)
    ang = positions.astype(F32)[..., None] * inv
    cos, sin = jnp.cos(ang), jnp.sin(ang)
    cos_t = jnp.tile(cos, (1, 1, LANES // half))
    sin_t = jnp.concatenate([-sin, -sin, sin, sin], axis=-1)
    return cos_t, sin_t


def _split_pair(pair, interleaved):
    lane = lax.broadcasted_iota(jnp.int32, pair.shape, 1)
    first = ((lane // (HEAD_DIM // 2)) % 2 == 0) if interleaved else (lane < HEAD_DIM)
    zero = jnp.zeros_like(pair)
    return jnp.where(first, pair, zero), jnp.where(first, zero, pair)


def _qk(q, k):
    return lax.dot_general(q, k, (((1,), (1,)), ((), ())), preferred_element_type=F32)


FOLD_CHAINS = 4


def _fold_rows(x, op):
    parts = x.reshape(x.shape[0] // 8, 8, x.shape[1])
    chains = [functools.reduce(op, [parts[g] for g in range(c, parts.shape[0], FOLD_CHAINS)])
              for c in range(min(FOLD_CHAINS, parts.shape[0]))]
    return functools.reduce(op, chains)


def _flash_t_update(s, vt_blocks, m_scr, acc_scr):
    m_old = m_scr[...]
    m_new = jnp.maximum(m_old, jnp.max(_fold_rows(s, jnp.maximum), axis=0, keepdims=True))
    alpha = jnp.exp2(m_old - m_new)
    pb = jnp.exp2(s - m_new).astype(BF16)
    m_scr[...] = m_new
    pv = functools.reduce(lambda a, b: a + b, [
        jnp.dot(vt, pb[k0:k1], preferred_element_type=F32) for (k0, k1), vt in vt_blocks])
    acc_scr[...] = alpha * acc_scr[...] + pv


ONES_ROWS = 16


def _with_ones_rows(vt):
    shape = vt.shape[:-2]
    return jnp.concatenate([vt, jnp.ones(shape + (1, vt.shape[-1]), vt.dtype),
                            jnp.zeros(shape + (ONES_ROWS - 1, vt.shape[-1]), vt.dtype)], axis=-2)


def _flash_update(s, v, m_scr, l_scr, acc_scr):
    groups = [s[:, c:c + LANES] for c in range(0, s.shape[1], LANES)]
    m_old = m_scr[...]
    lane_max = functools.reduce(jnp.maximum, groups)
    m_new = jnp.maximum(m_old, jnp.max(lane_max, axis=-1, keepdims=True))
    alpha = jnp.exp2(m_old - m_new)
    p_groups = [jnp.exp2(g - m_new) for g in groups]
    l_scr[...] = alpha * l_scr[...] + functools.reduce(lambda a, b: a + b, p_groups)
    m_scr[...] = m_new
    pb = jnp.concatenate([g.astype(BF16) for g in p_groups], axis=1)
    acc_scr[...] = alpha * acc_scr[...] + jnp.dot(pb, v, preferred_element_type=F32)


def _diff_attn_kernel(lam_ref, q_ref, k_ref, v_ref, ng_ref, o_ref, m_scr, l_scr, acc_scr,
                      *, tq, tk, lam_init):
    i = pl.program_id(2)
    per_tile = tq // tk
    qs = jnp.concatenate(_split_pair(q_ref[0], True), axis=0)
    m_scr[...] = jnp.full(m_scr.shape, NEG, F32)
    l_scr[...] = jnp.zeros(l_scr.shape, F32)
    acc_scr[...] = jnp.zeros(acc_scr.shape, F32)

    def kv_block(j):
        start = pl.multiple_of(j * tk, tk)
        return k_ref[0, pl.ds(start, tk), :], v_ref[0, pl.ds(start, tk), :]

    def body(j, carry):
        k, v = kv_block(j)
        _flash_update(_qk(qs, k), v, m_scr, l_scr, acc_scr)
        return carry

    lax.fori_loop(0, i * per_tile, body, 0)
    for d in range(per_tile):
        k, v = kv_block(i * per_tile + d)
        s = _qk(qs, k)
        row = lax.broadcasted_iota(jnp.int32, s.shape, 0) % tq
        col = lax.broadcasted_iota(jnp.int32, s.shape, 1) + d * tk
        s = jnp.where((col // CHUNK) <= (row // CHUNK), s, NEG)
        _flash_update(s, v, m_scr, l_scr, acc_scr)

    o = acc_scr[...] / jnp.sum(l_scr[...], axis=-1, keepdims=True)
    o = o[:tq] - lam_ref[0] * o[tq:]
    o = o * lax.rsqrt(jnp.mean(o * o, axis=-1, keepdims=True) + EPS)
    o_ref[0] = ((o * ng_ref[...]) * (1.0 - lam_init)).astype(BF16)


def _diff_attention(rope_slab, plain_slab, lam, norm_g, lam_init, tq, tk):
    bsz, s, _ = rope_slab.shape
    assert tq % tk == 0
    kern = functools.partial(_diff_attn_kernel, tq=tq, tk=tk, lam_init=lam_init)
    cb = COL_TILE // LANES
    return pl.pallas_call(
        kern,
        grid=(bsz, A_HEADS, s // tq),
        in_specs=[
            pl.BlockSpec(memory_space=pltpu.SMEM),
            pl.BlockSpec((1, tq, LANES), lambda b, h, i: (b, i, R_AQ * cb + h)),
            pl.BlockSpec((1, s, LANES), lambda b, h, i: (b, 0, R_AK * cb + h)),
            pl.BlockSpec((1, s, LANES), lambda b, h, i: (b, 0, P_AV * cb + h)),
            pl.BlockSpec((1, LANES), lambda b, h, i: (0, 0)),
        ],
        out_specs=pl.BlockSpec((1, tq, LANES), lambda b, h, i: (b, i, h)),
        out_shape=jax.ShapeDtypeStruct((bsz, s, A_HEADS * LANES), BF16),
        scratch_shapes=[pltpu.VMEM((2 * tq, LANES), F32), pltpu.VMEM((2 * tq, LANES), F32),
                        pltpu.VMEM((2 * tq, LANES), F32)],
        compiler_params=_cparams(("arbitrary", "arbitrary", "arbitrary")),
        name="diff_attn",
    )(lam.reshape(1), rope_slab, rope_slab, plain_slab, norm_g.reshape(1, LANES))


KEY_NEG_INF = -2139095040
KEY_POS_INF = 2139095040
SEARCH_MAX_STEPS = 80
NO_TIE_LIMIT = 1e9
P3_PAIRS = 2
VT_ROWS = HEAD_DIM + ONES_ROWS


def _key_to_float(key):
    bits = jnp.where(key >= 0, key, (key - 1) ^ jnp.int32(0x7FFFFFFF))
    return pltpu.bitcast(bits, F32)


def _float_to_key(t):
    bits = pltpu.bitcast(t, jnp.int32)
    return jnp.where(bits >= 0, bits, (bits ^ jnp.int32(0x7FFFFFFF)) + 1)


def _dsa_kernel(q_ref, iq_ref, iw_ref, k_ref, ik_ref, vt_ref, o_ref,
                sc_scr, qs_scr, m_scr, acc_scr, *, tq, topk):
    i = pl.program_id(1)
    nblk = i + 1
    npair = (nblk + 1) // 2
    nq = B_HEADS * tq

    iq = iq_ref[0]
    parts = []
    for p in range(IDX_HEADS // 2):
        parts += list(_split_pair(iq[:, p * LANES:(p + 1) * LANES], True))
    iqs = jnp.concatenate(parts, axis=0)
    iw_t = iw_ref[0].astype(F32).T

    def index_block(j):
        start = pl.multiple_of(j * tq, tq)
        ik = ik_ref[0, pl.ds(start, tq), :]
        score = None
        for h in range(IDX_HEADS):
            hs = jnp.maximum(_qk(ik, iqs[h * tq:(h + 1) * tq]), 0.0)
            score = iw_t[h:h + 1] * hs if score is None else score + iw_t[h:h + 1] * hs
        return score

    def p1_body(jj, carry):
        sc_scr[jj, 0:tq, :] = index_block(2 * jj)
        sc_scr[jj, tq:2 * tq, :] = index_block(2 * jj + 1)
        return carry

    lax.fori_loop(0, i // 2, p1_body, 0)
    score = index_block(i)
    key_i = lax.broadcasted_iota(jnp.int32, score.shape, 0)
    qry_i = lax.broadcasted_iota(jnp.int32, score.shape, 1)
    diagonal = jnp.where((key_i // CHUNK) <= (qry_i // CHUNK), score, -jnp.inf)

    @pl.when(i % 2 == 1)
    def _():
        sc_scr[npair - 1, 0:tq, :] = index_block(i - 1)
        sc_scr[npair - 1, tq:2 * tq, :] = diagonal

    @pl.when(i % 2 == 0)
    def _():
        sc_scr[npair - 1, 0:tq, :] = diagonal
        sc_scr[npair - 1, tq:2 * tq, :] = jnp.full((tq, tq), -jnp.inf, F32)

    kf = float(topk)

    def scan(hit_fn, ext_fn, ext_op, ext_init):
        def body(jj, carry):
            cnt, ext = carry
            sblk = sc_scr[jj]
            cnt = cnt + _fold_rows(jnp.where(hit_fn(sblk), 1.0, 0.0), lambda a, b: a + b)
            if ext_fn is not None:
                ext = ext_op(ext, _fold_rows(ext_fn(sblk), ext_op))
            return cnt, ext
        cnt, ext = lax.fori_loop(0, npair, body, (jnp.zeros((8, tq), F32),
                                                  jnp.full((8, tq), ext_init, F32)))
        return jnp.sum(cnt, axis=0, keepdims=True), ext

    def count_ge(t):
        return scan(lambda sblk: sblk >= t, None, None, 0.0)[0]

    c_ge0, ext = scan(lambda sblk: sblk >= 0.0, lambda sblk: sblk, jnp.maximum, -jnp.inf)
    col_max = jnp.max(ext, axis=0, keepdims=True)
    c_gt0, ext = scan(lambda sblk: sblk > 0.0,
                      lambda sblk: jnp.where(sblk > -jnp.inf, sblk, jnp.inf), jnp.minimum, jnp.inf)
    col_min = jnp.min(ext, axis=0, keepdims=True)
    qpos = i * tq + lax.broadcasted_iota(jnp.int32, (1, tq), 1)
    n_valid = (((qpos // CHUNK) + 1) * CHUNK).astype(F32)
    open_q = n_valid < kf
    above = c_gt0 >= kf
    below = c_ge0 < kf
    ikey = lambda v: jnp.full((1, tq), v, jnp.int32)
    lo_k = jnp.where(below, _float_to_key(col_min), ikey(0))
    hi_k = jnp.where(above, _float_to_key(col_max) + 1, jnp.where(below, ikey(0), ikey(1)))
    c_lo = jnp.where(below, n_valid, c_ge0)
    c_hi = jnp.where(above, 0.0, jnp.where(below, c_ge0, c_gt0))
    done = jnp.logical_or(open_q, jnp.logical_not(jnp.logical_or(above, below)))

    def next_probe(lo_k, hi_k, c_lo, c_hi, force_bisect):
        finite = jnp.logical_and(lo_k > KEY_NEG_INF, hi_k < KEY_POS_INF)
        t_lo, t_hi = _key_to_float(lo_k), _key_to_float(hi_k)
        log_lo = jnp.log(c_lo)
        frac = (log_lo - math.log(kf - 0.5)) / (log_lo - jnp.log(jnp.maximum(c_hi, 0.5)))
        k_interp = _float_to_key(t_lo + (t_hi - t_lo) * frac)
        k_mid = (lo_k & hi_k) + ((lo_k ^ hi_k) >> 1)
        interp = jnp.logical_and(finite, force_bisect == 0)
        k = jnp.where(interp, k_interp, k_mid)
        return jnp.minimum(jnp.maximum(k, lo_k + 1), hi_k - 1), interp

    def search_cond(st):
        return jnp.logical_and(st[0] < SEARCH_MAX_STEPS, jnp.min(st[5]) == 0)

    def search_step(st):
        step, lo_k, hi_k, c_lo, c_hi, done, k, interp = st
        c = count_ge(_key_to_float(k))
        ok = c >= kf
        live = done == 0
        new_lo = jnp.where(jnp.logical_and(live, ok), k, lo_k)
        new_hi = jnp.where(jnp.logical_and(live, jnp.logical_not(ok)), k, hi_k)
        new_c_lo = jnp.where(jnp.logical_and(live, ok), c, c_lo)
        new_c_hi = jnp.where(jnp.logical_and(live, jnp.logical_not(ok)), c, c_hi)
        width = lambda a, b: b.astype(F32) - a.astype(F32)
        slow = width(new_lo, new_hi) > 0.5 * width(lo_k, hi_k)
        force = jnp.logical_and(interp != 0, slow).astype(jnp.int32)
        finished = jnp.logical_or(new_c_lo == kf, new_hi - new_lo == 1)
        new_done = jnp.maximum(done, finished.astype(jnp.int32))
        new_k, new_interp = next_probe(new_lo, new_hi, new_c_lo, new_c_hi, force)
        return (step + 1, new_lo, new_hi, new_c_lo, new_c_hi, new_done, new_k,
                new_interp.astype(jnp.int32))

    k0, interp0 = next_probe(lo_k, hi_k, c_lo, c_hi, ikey(0))
    state = (jnp.int32(0), lo_k, hi_k, c_lo, c_hi, done.astype(jnp.int32), k0,
             interp0.astype(jnp.int32))
    _, lo_k, hi_k, c_lo, c_hi, _, _, _ = lax.while_loop(search_cond, search_step, state)
    tau = jnp.where(open_q, -jnp.inf, _key_to_float(lo_k))
    need = jnp.where(open_q, 0.0, jnp.where(c_lo == kf, NO_TIE_LIMIT, kf - c_hi))

    q = q_ref[0]
    for p in range(B_HEADS // 2):
        even, odd = _split_pair(q[:, p * LANES:(p + 1) * LANES], True)
        qs_scr[p * tq:(p + 1) * tq, :] = even
        qs_scr[(B_HEADS // 2 + p) * tq:(B_HEADS // 2 + p + 1) * tq, :] = odd
    m_scr[...] = jnp.full(m_scr.shape, NEG, F32)
    acc_scr[...] = jnp.zeros(acc_scr.shape, F32)
    key_i = lax.broadcasted_iota(jnp.int32, (tq, tq), 0)
    lower = (lax.broadcasted_iota(jnp.int32, (tq, tq), 1) <= key_i).astype(BF16)

    def select_bias(jj, tie_count):
        sblk = sc_scr[jj]
        tie = sblk == tau
        tie01 = jnp.where(tie, 1.0, 0.0).astype(BF16)
        rank_a = tie_count + jnp.dot(lower, tie01[0:tq], preferred_element_type=F32)
        rank_b = rank_a[tq - 1:tq, :] + jnp.dot(lower, tie01[tq:2 * tq], preferred_element_type=F32)
        rank = jnp.concatenate([rank_a, rank_b], axis=0)
        keep_tie = jnp.where(rank <= need, 0.0, NEG)
        return jnp.where(sblk > tau, 0.0, jnp.where(tie, keep_tie, NEG)), rank[2 * tq - 1:2 * tq, :]

    def p3_step(jj, pairs, tie_count):
        nkeys = pairs * 2 * tq
        start = pl.multiple_of(jj * 2 * tq, 2 * tq)
        biases = []
        for t in range(pairs):
            bias, tie_count = select_bias(jj + t, tie_count)
            biases.append(bias)
        bias = jnp.concatenate(biases, axis=0)
        s = _qk(k_ref[0, pl.ds(start, nkeys), :], qs_scr[...])
        s = jnp.concatenate([s[:, h * tq:(h + 1) * tq] + bias for h in range(B_HEADS)], axis=1)
        _flash_t_update(s, [((t * tq, (t + 1) * tq), vt_ref[0, 2 * jj + t]) for t in range(2 * pairs)],
                        m_scr, acc_scr)
        return tie_count

    tie_count = lax.fori_loop(0, npair // P3_PAIRS,
                              lambda g, tc: p3_step(g * P3_PAIRS, P3_PAIRS, tc),
                              jnp.zeros((1, tq), F32))
    for r in range(1, P3_PAIRS):
        @pl.when(npair % P3_PAIRS == r)
        def _():
            p3_step(npair - r, r, tie_count)

    acc = acc_scr[...]
    o = acc[0:HEAD_DIM] / acc[HEAD_DIM:HEAD_DIM + 1]
    half_cols = (B_HEADS // 2) * tq
    for p in range(B_HEADS // 2):
        pair = jnp.concatenate([o[:, p * tq:(p + 1) * tq],
                                o[:, half_cols + p * tq:half_cols + (p + 1) * tq]], axis=0)
        o_ref[0, :, p * LANES:(p + 1) * LANES] = pair.T.astype(BF16)


def _dsa_attention(rope_slab, plain_slab, tq):
    bsz, s, _ = rope_slab.shape
    topk = min(TOPK_MAX, s // 4)
    nblk = s // tq
    assert tq >= topk and s % tq == 0 and nblk % 2 == 0
    kern = functools.partial(_dsa_kernel, tq=tq, topk=topk)
    idx0 = R_IDX * COL_TILE
    misc0 = P_MISC * COL_TILE
    iq_w = IDX_HEADS * HEAD_DIM
    width = B_HEADS * HEAD_DIM
    vt = _with_ones_rows(
        plain_slab[:, :, misc0:misc0 + HEAD_DIM].reshape(bsz, nblk, tq, HEAD_DIM).transpose(0, 1, 3, 2))
    return pl.pallas_call(
        kern,
        grid=(bsz, s // tq),
        in_specs=[
            pl.BlockSpec((1, tq, COL_TILE), lambda b, i: (b, i, R_BQ)),
            pl.BlockSpec((1, tq, iq_w), lambda b, i: (b, i, idx0 // iq_w)),
            pl.BlockSpec((1, tq, LANES), lambda b, i: (b, i, (misc0 + LANES) // LANES)),
            pl.BlockSpec((1, s, LANES), lambda b, i: (b, 0, (idx0 + iq_w) // LANES)),
            pl.BlockSpec((1, s, LANES), lambda b, i: (b, 0, (idx0 + iq_w + LANES) // LANES)),
            pl.BlockSpec((1, nblk, VT_ROWS, tq), lambda b, i: (b, 0, 0, 0)),
        ],
        out_specs=pl.BlockSpec((1, tq, width), lambda b, i: (b, i, 0)),
        out_shape=jax.ShapeDtypeStruct((bsz, s, width), BF16),
        scratch_shapes=[
            pltpu.VMEM((nblk // 2, 2 * tq, tq), F32),
            pltpu.VMEM((B_HEADS * tq, LANES), BF16),
            pltpu.VMEM((1, B_HEADS * tq), F32),
            pltpu.VMEM((VT_ROWS, B_HEADS * tq), F32),
        ],
        compiler_params=_cparams(("arbitrary", "arbitrary")),
        name="dsa_attn",
    )(rope_slab, rope_slab, plain_slab, rope_slab, rope_slab, vt)


BAND_TQ = 2 * CHUNK
BAND_KEYS = (C_LEFT_CHUNKS + BAND_TQ // CHUNK) * CHUNK
BAND_PAD = C_LEFT_CHUNKS * CHUNK


def _band_kernel(q_ref, k_ref, v_ref, bias_ref, o_ref):
    i = pl.program_id(1)
    tq = BAND_TQ
    start = pl.multiple_of(i * tq, tq)
    col = lax.broadcasted_iota(jnp.int32, (2 * tq, BAND_KEYS), 1)
    key_ok = col + start >= BAND_PAD
    for p in range(C_HEADS // 2):
        lanes = slice(p * LANES, (p + 1) * LANES)
        k = k_ref[0, pl.ds(start, BAND_KEYS), lanes]
        v = v_ref[0, pl.ds(start, BAND_KEYS), lanes]
        qs = jnp.concatenate(_split_pair(q_ref[0, :, lanes], False), axis=0)
        s = _qk(qs, k) + bias_ref[2 * p:2 * p + 2].reshape(2 * tq, BAND_KEYS)
        s = jnp.where(key_ok, s, NEG)
        m = jnp.max(s, axis=-1, keepdims=True)
        e = jnp.exp2(s - m)
        l = jnp.sum(e, axis=-1, keepdims=True)
        eb = e.astype(BF16)
        v_even, v_odd = _split_pair(v, False)
        o = (jnp.dot(eb[:tq], v_even, preferred_element_type=F32) / l[:tq]
             + jnp.dot(eb[tq:], v_odd, preferred_element_type=F32) / l[tq:])
        o_ref[0, :, lanes] = o.astype(BF16)


def _band_bias(rel_bias):
    r = jnp.arange(BAND_TQ)[:, None]
    cidx = jnp.arange(BAND_KEYS)[None, :]
    n_diag = BAND_TQ + BAND_KEYS - 1
    rel = BAND_PAD - (BAND_KEYS - 1) + jnp.arange(n_diag)
    g = rel_bias.astype(F32)[:, jnp.clip(rel, -REL_CLIP, REL_CLIP) + REL_CLIP] * LOG2E
    skew = jnp.tile(g, (1, BAND_TQ + 2))[:, :BAND_TQ * (n_diag + 1)]
    bias = skew.reshape(-1, BAND_TQ, n_diag + 1)[:, :, :BAND_KEYS][:, :, ::-1]
    dchunk = (r // CHUNK + C_LEFT_CHUNKS) - cidx // CHUNK
    in_band = jnp.logical_and(dchunk >= 0, dchunk <= C_LEFT_CHUNKS)
    return jnp.where(in_band[None], bias, NEG)


def _band_attention(plain_slab, kp, vp, bias):
    bsz, s, _ = plain_slab.shape
    tq = BAND_TQ
    sp = kp.shape[1]
    width = C_HEADS * HEAD_DIM
    return pl.pallas_call(
        _band_kernel,
        grid=(bsz, s // tq),
        in_specs=[
            pl.BlockSpec((1, tq, width), lambda b, i: (b, i, P_CQ)),
            pl.BlockSpec((1, sp, width), lambda b, i: (b, 0, 0)),
            pl.BlockSpec((1, sp, width), lambda b, i: (b, 0, 0)),
            pl.BlockSpec((C_HEADS, tq, BAND_KEYS), lambda b, i: (0, 0, 0)),
        ],
        out_specs=pl.BlockSpec((1, tq, width), lambda b, i: (b, i, 0)),
        out_shape=jax.ShapeDtypeStruct((bsz, s, width), BF16),
        compiler_params=_cparams(("arbitrary", "arbitrary")),
        name="band_attn",
    )(plain_slab, kp, vp, bias)


def _merge_kernel(x_ref, g1_ref, ya_ref, yb_ref, yc_ref, ga_ref, gb_ref, gc_ref,
                  wa_ref, wb_ref, wc_ref, wo_ref, o_ref):
    for r0 in range(0, x_ref.shape[1], PROJ_ROWS):
        rows = slice(r0, r0 + PROJ_ROWS)

        def branch(y_ref, w_ref, gate_ref):
            return gate_ref[0, rows, :].astype(F32) * jnp.dot(y_ref[0, rows, :], w_ref[...],
                                                             preferred_element_type=F32)

        merged = (branch(ya_ref, wa_ref, ga_ref) + branch(yb_ref, wb_ref, gb_ref)
                  + branch(yc_ref, wc_ref, gc_ref))
        mixed = jnp.dot(merged.astype(BF16), wo_ref[...], preferred_element_type=F32)
        o_ref[0, rows, :] = x_ref[0, rows, :] + g1_ref[0] * mixed


def _merge(x, g1, ya, yb, yc, gate_slab, wa, wb, wc, wo, tm):
    bsz, s, d = x.shape
    tok = lambda b, i: (b, i, 0)
    full = lambda b, i: (0, 0)
    return pl.pallas_call(
        _merge_kernel,
        grid=(bsz, s // tm),
        in_specs=[
            pl.BlockSpec((1, tm, d), tok),
            pl.BlockSpec((1, 1, d), lambda b, i: (b, 0, 0)),
            pl.BlockSpec((1, tm, ya.shape[2]), tok),
            pl.BlockSpec((1, tm, yb.shape[2]), tok),
            pl.BlockSpec((1, tm, yc.shape[2]), tok),
            pl.BlockSpec((1, tm, d), lambda b, i: (b, i, 0)),
            pl.BlockSpec((1, tm, d), lambda b, i: (b, i, 1)),
            pl.BlockSpec((1, tm, d), lambda b, i: (b, i, 2)),
            pl.BlockSpec(wa.shape, full),
            pl.BlockSpec(wb.shape, full),
            pl.BlockSpec(wc.shape, full),
            pl.BlockSpec(wo.shape, full),
        ],
        out_specs=pl.BlockSpec((1, tm, d), tok),
        out_shape=jax.ShapeDtypeStruct((bsz, s, d), F32),
        compiler_params=_cparams(("arbitrary", "arbitrary")),
        name="merge",
    )(x, g1, ya, yb, yc, gate_slab, gate_slab, gate_slab, wa, wb, wc, wo)


def _router_kernel(x_ref, g_ref, sc_ref, sh_ref, rw_ref, rb_ref, u_ref, comb_ref):
    u = _modulated_norm(x_ref[0], g_ref[...], sc_ref[0], sh_ref[0])
    u_ref[0] = u.astype(BF16)
    logits = lax.dot_general(rw_ref[...], u, (((1,), (1,)), ((), ())),
                             preferred_element_type=F32, precision=lax.Precision.HIGHEST)
    aff = jax.nn.sigmoid(logits)
    sel = aff + rb_ref[...]
    rows = [sel[e:e + 1] for e in range(N_EXPERTS)]
    gscore = []
    for g in range(N_GROUPS):
        r = rows[g * EXPERTS_PER_GROUP:(g + 1) * EXPERTS_PER_GROUP]
        best = None
        for a in range(EXPERTS_PER_GROUP):
            for b in range(a + 1, EXPERTS_PER_GROUP):
                pair = r[a] + r[b]
                best = pair if best is None else jnp.maximum(best, pair)
        gscore.append(best)
    gmax = functools.reduce(jnp.maximum, gscore)
    taken = jnp.zeros_like(gmax) > 1.0
    in_best = []
    for g in range(N_GROUPS):
        is_g = jnp.logical_and(gscore[g] == gmax, jnp.logical_not(taken))
        in_best.append(is_g)
        taken = jnp.logical_or(taken, is_g)
    keep = []
    for e in range(N_EXPERTS):
        g = e // EXPERTS_PER_GROUP
        rank = jnp.zeros_like(gmax)
        for o in range(g * EXPERTS_PER_GROUP, (g + 1) * EXPERTS_PER_GROUP):
            if o == e:
                continue
            ahead = rows[o] > rows[e] if o > e else rows[o] >= rows[e]
            rank = rank + jnp.where(ahead, 1.0, 0.0)
        keep.append(jnp.logical_and(in_best[g], rank < 2.0))
    w = [jnp.where(keep[e], aff[e:e + 1], 0.0) for e in range(N_EXPERTS)]
    total = functools.reduce(lambda a, b: a + b, w)
    comb = jnp.concatenate([we / total for we in w]
                           + [jnp.zeros((LANES - N_EXPERTS, total.shape[1]), F32)], axis=0)
    comb_ref[0] = comb.T


def _router(x, g, sc, sh, router_w, router_b, tm):
    bsz, s, d = x.shape
    tok = lambda b, i: (b, i, 0)
    return pl.pallas_call(
        _router_kernel,
        grid=(bsz, s // tm),
        in_specs=[
            pl.BlockSpec((1, tm, d), tok),
            pl.BlockSpec((1, d), lambda b, i: (0, 0)),
            pl.BlockSpec((1, 1, d), lambda b, i: (b, 0, 0)),
            pl.BlockSpec((1, 1, d), lambda b, i: (b, 0, 0)),
            pl.BlockSpec((N_EXPERTS, d), lambda b, i: (0, 0)),
            pl.BlockSpec((N_EXPERTS, 1), lambda b, i: (0, 0)),
        ],
        out_specs=[pl.BlockSpec((1, tm, d), tok), pl.BlockSpec((1, tm, LANES), tok)],
        out_shape=[jax.ShapeDtypeStruct((bsz, s, d), BF16),
                   jax.ShapeDtypeStruct((bsz, s, LANES), F32)],
        compiler_params=_cparams(("arbitrary", "arbitrary")),
        name="router",
    )(x, g.reshape(1, d), sc, sh, router_w.T, router_b.reshape(N_EXPERTS, 1))


MOE_SUB = 512
MOE_CAP = 128


def _expert_ffn(rows, w1_ref, w3_ref, w2_ref):
    h1 = jnp.dot(rows, w1_ref[0], preferred_element_type=F32)
    h3 = jnp.dot(rows, w3_ref[0], preferred_element_type=F32)
    h = (h1 * jax.nn.sigmoid(h1)) * h3
    return jnp.dot(h.astype(BF16), w2_ref[0], preferred_element_type=F32)


def _moe_kernel(x_ref, g2_ref, u_ref, comb_ref, w1_ref, w3_ref, w2_ref, o_ref,
                acc_scr, pos_scr, pos_t_scr, comb_t_scr, scatter_scr, y_scr):
    e = pl.program_id(2)
    tm = u_ref.shape[1]
    sub = min(MOE_SUB, tm)
    subs = [slice(r0, r0 + sub) for r0 in range(0, tm, sub)]

    @pl.when(e == 0)
    def _():
        acc_scr[...] = jnp.zeros(acc_scr.shape, F32)
        before = (lax.broadcasted_iota(jnp.int32, (sub, sub), 1)
                  < lax.broadcasted_iota(jnp.int32, (sub, sub), 0)).astype(BF16)
        for rows in subs:
            comb0 = comb_ref[0, rows, :]
            member = jnp.where(comb0 > 0.0, 1.0, 0.0).astype(BF16)
            pos = jnp.dot(before, member, preferred_element_type=F32)
            pos_scr[rows, :] = pos
            pos_t_scr[:, rows] = pos.T
            comb_t_scr[:, rows] = comb0.T

    comb = comb_ref[0]
    lane = lax.broadcasted_iota(jnp.int32, comb.shape, 1)
    ce = jnp.sum(jnp.where(lane == e, comb, 0.0), axis=-1, keepdims=True)
    pe = jnp.sum(jnp.where(lane == e, pos_scr[...], 0.0), axis=-1, keepdims=True)
    member_e = jnp.where(ce > 0.0, 1.0, 0.0)
    fullest = functools.reduce(jnp.maximum, [jnp.sum(member_e[rows]) for rows in subs])
    routed = fullest <= float(MOE_CAP)
    slots = pl.ds(pl.multiple_of(e * MOE_CAP, MOE_CAP), MOE_CAP)

    @pl.when(routed)
    def _():
        slot_l = lax.broadcasted_iota(jnp.int32, (1, MOE_CAP), 1).astype(F32)
        scatter_scr[e] = jnp.where(jnp.logical_and(ce > 0.0, pe == slot_l), 1.0, 0.0).astype(BF16)
        slot_s = lax.broadcasted_iota(jnp.int32, (MOE_CAP, 1), 0).astype(F32)
        packed, weights = [], []
        for rows in subs:
            ce_t = comb_t_scr[pl.ds(e, 1), rows]
            pe_t = pos_t_scr[pl.ds(e, 1), rows]
            hit = jnp.logical_and(ce_t > 0.0, pe_t == slot_s)
            weights.append(jnp.sum(jnp.where(hit, ce_t, 0.0), axis=-1, keepdims=True))
            packed.append(jnp.dot(jnp.where(hit, 1.0, 0.0).astype(BF16), u_ref[0, rows, :],
                                  preferred_element_type=F32).astype(BF16))
        y = jnp.concatenate(weights, axis=0) * _expert_ffn(jnp.concatenate(packed, axis=0),
                                                           w1_ref, w3_ref, w2_ref)
        for k in range(len(subs)):
            y_scr[k, slots, :] = y[k * MOE_CAP:(k + 1) * MOE_CAP].astype(BF16)

    @pl.when(jnp.logical_not(routed))
    def _():
        scatter_scr[e] = jnp.zeros(scatter_scr.shape[1:], BF16)
        for k in range(len(subs)):
            y_scr[k, slots, :] = jnp.zeros((MOE_CAP, y_scr.shape[2]), BF16)
        acc_scr[...] += ce * _expert_ffn(u_ref[0], w1_ref, w3_ref, w2_ref)

    @pl.when(e == pl.num_programs(2) - 1)
    def _():
        for k, rows in enumerate(subs):
            scatter = jnp.concatenate([scatter_scr[j, rows, :] for j in range(N_EXPERTS)], axis=1)
            y = acc_scr[rows, :] + jnp.dot(scatter, y_scr[k], preferred_element_type=F32)
            o_ref[0, rows, :] = x_ref[0, rows, :] + g2_ref[0] * y


def _moe(x, g2, u, comb, w1, w3, w2, tm):
    bsz, s, d = x.shape
    ne, _, dff = w1.shape
    assert ne == N_EXPERTS
    tok = lambda b, i, e: (b, i, 0)
    return pl.pallas_call(
        _moe_kernel,
        grid=(bsz, s // tm, ne),
        in_specs=[
            pl.BlockSpec((1, tm, d), tok),
            pl.BlockSpec((1, 1, d), lambda b, i, e: (b, 0, 0)),
            pl.BlockSpec((1, tm, d), tok),
            pl.BlockSpec((1, tm, LANES), tok),
            pl.BlockSpec((1, d, dff), lambda b, i, e: (e, 0, 0)),
            pl.BlockSpec((1, d, dff), lambda b, i, e: (e, 0, 0)),
            pl.BlockSpec((1, dff, d), lambda b, i, e: (e, 0, 0)),
        ],
        out_specs=pl.BlockSpec((1, tm, d), tok),
        out_shape=jax.ShapeDtypeStruct((bsz, s, d), F32),
        scratch_shapes=[
            pltpu.VMEM((tm, d), F32),
            pltpu.VMEM((tm, LANES), F32),
            pltpu.VMEM((LANES, tm), F32),
            pltpu.VMEM((LANES, tm), F32),
            pltpu.VMEM((ne, tm, MOE_CAP), BF16),
            pltpu.VMEM((pl.cdiv(tm, MOE_SUB), ne * MOE_CAP, d), BF16),
        ],
        compiler_params=_cparams(("arbitrary", "arbitrary", "arbitrary")),
        name="moe",
    )(x, g2, u, comb, w1, w3, w2)


def _final_norm_kernel(x_ref, g_ref, o_ref):
    x = x_ref[0]
    o_ref[0] = (x * lax.rsqrt(jnp.mean(x * x, axis=-1, keepdims=True) + EPS)) * g_ref[...]


def _final_norm(x, g, tm):
    bsz, s, d = x.shape
    tok = lambda b, i: (b, i, 0)
    return pl.pallas_call(
        _final_norm_kernel,
        grid=(bsz, s // tm),
        in_specs=[pl.BlockSpec((1, tm, d), tok), pl.BlockSpec((1, d), lambda b, i: (0, 0))],
        out_specs=pl.BlockSpec((1, tm, d), tok),
        out_shape=jax.ShapeDtypeStruct((bsz, s, d), F32),
        compiler_params=_cparams(("arbitrary", "arbitrary")),
        name="final_norm",
    )(x, g.reshape(1, d))


def _tile(s, want):
    t = min(want, s)
    assert s % t == 0
    return t


def _tiles(s):
    return dict(
        rows=_tile(s, 1024),
        diff_q=_tile(s, 1024),
        diff_k=_tile(s, 512),
        dsa_q=_tile(s, 256),
        merge=_tile(s, 512),
    )


def kernel(x, c, positions, norm1_g, norm2_g, w_mod, b_mod, w_in, lambda_q1, lambda_k1, lambda_q2, lambda_k2, a_norm_g, c_rel_bias, w_branch_a, w_branch_b, w_branch_c, w_out, router_w, router_b, exp_w1, exp_w3, exp_w2, final_g):
    bsz, s, d = x.shape
    depth = w_mod.shape[0]
    t = _tiles(s)
    tm = t["rows"]

    mod = _modulation(c, w_mod, b_mod)
    rope_tables = _rope_tables(positions)

    for layer in range(depth):
        lam_init = 0.8 - 0.6 * math.exp(-0.3 * layer)
        sh1, sc1, g1, sh2, sc2, g2 = [m[:, None, :] for m in jnp.split(mod[layer], 6, axis=-1)]
        lam = (jnp.exp(jnp.sum(lambda_q1[layer] * lambda_k1[layer]))
               - jnp.exp(jnp.sum(lambda_q2[layer] * lambda_k2[layer])) + lam_init)

        w_rope, w_gate, w_plain = _build_weights(w_in[layer])
        u = _norm(x, norm1_g[layer], sc1, sh1, tm)
        rope_slab = _project(u, w_rope, rope_tables, "rope", tm)
        gate_slab = _project(u, w_gate, (), "gate", tm)
        plain_slab = _project(u, w_plain, (), "plain", tm)

        ya = _diff_attention(rope_slab, plain_slab, lam, a_norm_g[layer], lam_init,
                             t["diff_q"], t["diff_k"])
        yb = _dsa_attention(rope_slab, plain_slab, t["dsa_q"])
        pad = ((0, 0), (BAND_PAD, 0), (0, 0))
        kp = jnp.pad(plain_slab[:, :, P_CK * COL_TILE:(P_CK + 1) * COL_TILE], pad)
        vp = jnp.pad(plain_slab[:, :, P_CV * COL_TILE:(P_CV + 1) * COL_TILE], pad)
        yc = _band_attention(plain_slab, kp, vp, _band_bias(c_rel_bias[layer]))
        x = _merge(x, g1, ya, yb, yc, gate_slab,
                   w_branch_a[layer].astype(BF16), w_branch_b[layer].astype(BF16),
                   w_branch_c[layer].astype(BF16), w_out[layer].astype(BF16), t["merge"])

        u, comb = _router(x, norm2_g[layer], sc2, sh2, router_w, router_b, t["merge"])
        x = _moe(x, g2, u, comb, exp_w1[layer].astype(BF16), exp_w3[layer].astype(BF16),
                 exp_w2[layer].astype(BF16), tm)

    return _final_norm(x, final_g, tm)
```

```python
import functools
import math

import jax
import jax.numpy as jnp
from jax import lax
from jax.experimental import pallas as pl
from jax.experimental.pallas import tpu as pltpu

F32 = jnp.float32
BF16 = jnp.bfloat16

CHUNK = 64
ROPE_THETA = 10000.0
EPS = 1e-6
A_HEADS = 4
HEAD_DIM = 64
B_HEADS = 8
IDX_HEADS = 4
TOPK_MAX = 256
C_HEADS = 8
C_LEFT_CHUNKS = 8
REL_CLIP = 256
N_EXPERTS = 16
N_GROUPS = 4
EXPERTS_PER_GROUP = 4
N_BRANCHES = 3

LANES = 128
NEG = -1e30
LOG2E = math.log2(math.e)
VMEM_LIMIT = 56 * 1024 * 1024

COL_TILE = 512
PROJ_ROWS = 256
R_AQ, R_AK, R_BQ, R_IDX = 0, 1, 2, 3
P_AV, P_CQ, P_CK, P_CV, P_MISC = 0, 1, 2, 3, 4


def _cparams(sem):
    return pltpu.CompilerParams(dimension_semantics=sem, vmem_limit_bytes=VMEM_LIMIT)


def _mod_kernel(c_ref, w_ref, b_ref, o_ref):
    c = c_ref[...]
    ca = c * jax.nn.sigmoid(c)
    o_ref[0] = jnp.dot(ca, w_ref[0], preferred_element_type=F32) + b_ref[0]


def _modulation(c, w_mod, b_mod):
    depth, d, n6 = w_mod.shape
    bsz = c.shape[0]
    tn = 1024
    return pl.pallas_call(
        _mod_kernel,
        grid=(depth, n6 // tn),
        in_specs=[
            pl.BlockSpec((bsz, d), lambda l, j: (0, 0)),
            pl.BlockSpec((1, d, tn), lambda l, j: (l, 0, j)),
            pl.BlockSpec((1, 1, tn), lambda l, j: (l, 0, j)),
        ],
        out_specs=pl.BlockSpec((1, bsz, tn), lambda l, j: (l, 0, j)),
        out_shape=jax.ShapeDtypeStruct((depth, bsz, n6), F32),
        compiler_params=_cparams(("arbitrary", "arbitrary")),
        name="modulation",
    )(c, w_mod, b_mod.reshape(depth, 1, n6))


def _modulated_norm(x, g, sc, sh):
    y = x * lax.rsqrt(jnp.mean(x * x, axis=-1, keepdims=True) + EPS)
    return (y * g) * (1.0 + sc) + sh


def _norm_kernel(x_ref, g_ref, sc_ref, sh_ref, u_ref):
    u_ref[0] = _modulated_norm(x_ref[0], g_ref[...], sc_ref[0], sh_ref[0]).astype(BF16)


def _norm(x, g, sc, sh, tm):
    bsz, s, d = x.shape
    tok = lambda b, i: (b, i, 0)
    per_batch = lambda b, i: (b, 0, 0)
    return pl.pallas_call(
        _norm_kernel,
        grid=(bsz, s // tm),
        in_specs=[pl.BlockSpec((1, tm, d), tok), pl.BlockSpec((1, d), lambda b, i: (0, 0)),
                  pl.BlockSpec((1, 1, d), per_batch), pl.BlockSpec((1, 1, d), per_batch)],
        out_specs=pl.BlockSpec((1, tm, d), tok),
        out_shape=jax.ShapeDtypeStruct((bsz, s, d), BF16),
        compiler_params=_cparams(("arbitrary", "arbitrary")),
        name="norm",
    )(x, g.reshape(1, d), sc, sh)


def _proj_kernel(*refs, mode):
    if mode == "rope":
        u_ref, cos_ref, sin_ref, w_ref, o_ref = refs
    else:
        u_ref, w_ref, o_ref = refs
    tm = u_ref.shape[1]
    for r0 in range(0, tm, PROJ_ROWS):
        rows = slice(r0, r0 + PROJ_ROWS)
        acc = jnp.dot(u_ref[0, rows, :], w_ref[...], preferred_element_type=F32)
        if mode == "rope":
            cos, sin = cos_ref[0, rows, :], sin_ref[0, rows, :]
            groups = [acc[:, c:c + LANES] for c in range(0, COL_TILE, LANES)]
            acc = jnp.concatenate(
                [g * cos + pltpu.roll(g, LANES // 2, axis=1) * sin for g in groups], axis=1)
        elif mode == "gate":
            acc = 1.0 / (1.0 + jnp.exp(-acc))
        o_ref[0, rows, :] = acc.astype(BF16)


def _project(u, w, rope_tables, mode, tm):
    bsz, s, d = u.shape
    ncols = w.shape[1]
    tok = lambda b, i, j: (b, i, 0)
    in_specs = [pl.BlockSpec((1, tm, d), tok)]
    in_specs += [pl.BlockSpec((1, tm, LANES), tok) for _ in rope_tables]
    in_specs += [pl.BlockSpec((d, COL_TILE), lambda b, i, j: (0, j))]
    return pl.pallas_call(
        functools.partial(_proj_kernel, mode=mode),
        grid=(bsz, s // tm, ncols // COL_TILE),
        in_specs=in_specs,
        out_specs=pl.BlockSpec((1, tm, COL_TILE), lambda b, i, j: (b, i, j)),
        out_shape=jax.ShapeDtypeStruct((bsz, s, ncols), BF16),
        compiler_params=_cparams(("arbitrary", "arbitrary", "arbitrary")),
        name="proj_" + mode,
    )(u, *rope_tables, w)


def _pair_layout(w):
    half = HEAD_DIM // 2
    col = jnp.arange(w.shape[1])
    base, r = (col // LANES) * LANES, col % LANES
    src = base + ((r // half) % 2) * HEAD_DIM + (r // HEAD_DIM) * half + r % half
    return w[:, src]


def _build_weights(w_in):
    sizes = (512, 512, 512, 512, 64, 64, 256, 64, 4, 512, 512, 512, 3072)
    parts, start = [], 0
    for sz in sizes:
        parts.append(w_in[:, start:start + sz])
        start += sz
    aq, ak, av, bq, bk, bv, iq, ik, iw, cq, ck, cv, gates = parts
    d = w_in.shape[0]
    qscale = HEAD_DIM ** -0.5 * LOG2E
    iw_scale = IDX_HEADS ** -0.5 * HEAD_DIM ** -0.5
    zeros = lambda n: jnp.zeros((d, n), w_in.dtype)
    w_rope = _pair_layout(jnp.concatenate([aq * qscale, ak, bq * qscale, iq, bk, bk, ik, ik], axis=1))
    w_plain = jnp.concatenate([av, cq * qscale, ck, cv,
                               bv, bv, iw * iw_scale, zeros(LANES - IDX_HEADS),
                               zeros(COL_TILE - 2 * LANES)], axis=1)
    return w_rope.astype(BF16), gates.astype(BF16), w_plain.astype(BF16)


def _rope_tables(positions):
    half = HEAD_DIM // 2
    inv = ROPE_THETA ** (-jnp.arange(half, dtype=F32) / half)
    ang = positions.astype(F32)[..., None] * inv
    cos, sin = jnp.cos(ang), jnp.sin(ang)
    cos_t = jnp.tile(cos, (1, 1, LANES // half))
    sin_t = jnp.concatenate([-sin, -sin, sin, sin], axis=-1)
    return cos_t, sin_t


def _split_pair(pair, interleaved):
    lane = lax.broadcasted_iota(jnp.int32, pair.shape, 1)
    first = ((lane // (HEAD_DIM // 2)) % 2 == 0) if interleaved else (lane < HEAD_DIM)
    zero = jnp.zeros_like(pair)
    return jnp.where(first, pair, zero), jnp.where(first, zero, pair)


def _qk(q, k):
    return lax.dot_general(q, k, (((1,), (1,)), ((), ())), preferred_element_type=F32)


FOLD_CHAINS = 4


def _fold_rows(x, op):
    parts = x.reshape(x.shape[0] // 8, 8, x.shape[1])
    chains = [functools.reduce(op, [parts[g] for g in range(c, parts.shape[0], FOLD_CHAINS)])
              for c in range(min(FOLD_CHAINS, parts.shape[0]))]
    return functools.reduce(op, chains)


def _flash_t_update(s, vt_blocks, m_scr, acc_scr):
    m_old = m_scr[...]
    m_new = jnp.maximum(m_old, jnp.max(_fold_rows(s, jnp.maximum), axis=0, keepdims=True))
    alpha = jnp.exp2(m_old - m_new)
    pb = jnp.exp2(s - m_new).astype(BF16)
    m_scr[...] = m_new
    pv = functools.reduce(lambda a, b: a + b, [
        jnp.dot(vt, pb[k0:k1], preferred_element_type=F32) for (k0, k1), vt in vt_blocks])
    acc_scr[...] = alpha * acc_scr[...] + pv


ONES_ROWS = 16


def _with_ones_rows(vt):
    shape = vt.shape[:-2]
    return jnp.concatenate([vt, jnp.ones(shape + (1, vt.shape[-1]), vt.dtype),
                            jnp.zeros(shape + (ONES_ROWS - 1, vt.shape[-1]), vt.dtype)], axis=-2)


def _flash_update(s, v, m_scr, l_scr, acc_scr):
    groups = [s[:, c:c + LANES] for c in range(0, s.shape[1], LANES)]
    m_old = m_scr[...]
    lane_max = functools.reduce(jnp.maximum, groups)
    m_new = jnp.maximum(m_old, jnp.max(lane_max, axis=-1, keepdims=True))
    alpha = jnp.exp2(m_old - m_new)
    p_groups = [jnp.exp2(g - m_new) for g in groups]
    l_scr[...] = alpha * l_scr[...] + functools.reduce(lambda a, b: a + b, p_groups)
    m_scr[...] = m_new
    pb = jnp.concatenate([g.astype(BF16) for g in p_groups], axis=1)
    acc_scr[...] = alpha * acc_scr[...] + jnp.dot(pb, v, preferred_element_type=F32)


def _diff_attn_kernel(lam_ref, q_ref, k_ref, v_ref, ng_ref, o_ref, m_scr, l_scr, acc_scr,
                      *, tq, tk, lam_init):
    i = pl.program_id(2)
    per_tile = tq // tk
    qs = jnp.concatenate(_split_pair(q_ref[0], True), axis=0)
    m_scr[...] = jnp.full(m_scr.shape, NEG, F32)
    l_scr[...] = jnp.zeros(l_scr.shape, F32)
    acc_scr[...] = jnp.zeros(acc_scr.shape, F32)

    def kv_block(j):
        start = pl.multiple_of(j * tk, tk)
        return k_ref[0, pl.ds(start, tk), :], v_ref[0, pl.ds(start, tk), :]

    def body(j, carry):
        k, v = kv_block(j)
        _flash_update(_qk(qs, k), v, m_scr, l_scr, acc_scr)
        return carry

    lax.fori_loop(0, i * per_tile, body, 0)
    for d in range(per_tile):
        k, v = kv_block(i * per_tile + d)
        s = _qk(qs, k)
        row = lax.broadcasted_iota(jnp.int32, s.shape, 0) % tq
        col = lax.broadcasted_iota(jnp.int32, s.shape, 1) + d * tk
        s = jnp.where((col // CHUNK) <= (row // CHUNK), s, NEG)
        _flash_update(s, v, m_scr, l_scr, acc_scr)

    o = acc_scr[...] / jnp.sum(l_scr[...], axis=-1, keepdims=True)
    o = o[:tq] - lam_ref[0] * o[tq:]
    o = o * lax.rsqrt(jnp.mean(o * o, axis=-1, keepdims=True) + EPS)
    o_ref[0] = ((o * ng_ref[...]) * (1.0 - lam_init)).astype(BF16)


def _diff_attention(rope_slab, plain_slab, lam, norm_g, lam_init, tq, tk):
    bsz, s, _ = rope_slab.shape
    assert tq % tk == 0
    kern = functools.partial(_diff_attn_kernel, tq=tq, tk=tk, lam_init=lam_init)
    cb = COL_TILE // LANES
    return pl.pallas_call(
        kern,
        grid=(bsz, A_HEADS, s // tq),
        in_specs=[
            pl.BlockSpec(memory_space=pltpu.SMEM),
            pl.BlockSpec((1, tq, LANES), lambda b, h, i: (b, i, R_AQ * cb + h)),
            pl.BlockSpec((1, s, LANES), lambda b, h, i: (b, 0, R_AK * cb + h)),
            pl.BlockSpec((1, s, LANES), lambda b, h, i: (b, 0, P_AV * cb + h)),
            pl.BlockSpec((1, LANES), lambda b, h, i: (0, 0)),
        ],
        out_specs=pl.BlockSpec((1, tq, LANES), lambda b, h, i: (b, i, h)),
        out_shape=jax.ShapeDtypeStruct((bsz, s, A_HEADS * LANES), BF16),
        scratch_shapes=[pltpu.VMEM((2 * tq, LANES), F32), pltpu.VMEM((2 * tq, LANES), F32),
                        pltpu.VMEM((2 * tq, LANES), F32)],
        compiler_params=_cparams(("arbitrary", "arbitrary", "arbitrary")),
        name="diff_attn",
    )(lam.reshape(1), rope_slab, rope_slab, plain_slab, norm_g.reshape(1, LANES))


KEY_NEG_INF = -2139095040
KEY_POS_INF = 2139095040
SEARCH_MAX_STEPS = 80
NO_TIE_LIMIT = 1e9
P3_PAIRS = 2
VT_ROWS = HEAD_DIM + ONES_ROWS


def _key_to_float(key):
    bits = jnp.where(key >= 0, key, (key - 1) ^ jnp.int32(0x7FFFFFFF))
    return pltpu.bitcast(bits, F32)


def _float_to_key(t):
    bits = pltpu.bitcast(t, jnp.int32)
    return jnp.where(bits >= 0, bits, (bits ^ jnp.int32(0x7FFFFFFF)) + 1)


def _dsa_kernel(q_ref, iq_ref, iw_ref, k_ref, ik_ref, vt_ref, o_ref,
                sc_scr, qs_scr, m_scr, acc_scr, *, tq, topk):
    i = pl.program_id(1)
    nblk = i + 1
    npair = (nblk + 1) // 2
    nq = B_HEADS * tq

    iq = iq_ref[0]
    parts = []
    for p in range(IDX_HEADS // 2):
        parts += list(_split_pair(iq[:, p * LANES:(p + 1) * LANES], True))
    iqs = jnp.concatenate(parts, axis=0)
    iw_t = iw_ref[0].astype(F32).T

    def index_block(j):
        start = pl.multiple_of(j * tq, tq)
        ik = ik_ref[0, pl.ds(start, tq), :]
        score = None
        for h in range(IDX_HEADS):
            hs = jnp.maximum(_qk(ik, iqs[h * tq:(h + 1) * tq]), 0.0)
            score = iw_t[h:h + 1] * hs if score is None else score + iw_t[h:h + 1] * hs
        return score

    def p1_body(jj, carry):
        sc_scr[jj, 0:tq, :] = index_block(2 * jj)
        sc_scr[jj, tq:2 * tq, :] = index_block(2 * jj + 1)
        return carry

    lax.fori_loop(0, i // 2, p1_body, 0)
    score = index_block(i)
    key_i = lax.broadcasted_iota(jnp.int32, score.shape, 0)
    qry_i = lax.broadcasted_iota(jnp.int32, score.shape, 1)
    diagonal = jnp.where((key_i // CHUNK) <= (qry_i // CHUNK), score, -jnp.inf)

    @pl.when(i % 2 == 1)
    def _():
        sc_scr[npair - 1, 0:tq, :] = index_block(i - 1)
        sc_scr[npair - 1, tq:2 * tq, :] = diagonal

    @pl.when(i % 2 == 0)
    def _():
        sc_scr[npair - 1, 0:tq, :] = diagonal
        sc_scr[npair - 1, tq:2 * tq, :] = jnp.full((tq, tq), -jnp.inf, F32)

    kf = float(topk)

    def scan(hit_fn, ext_fn, ext_op, ext_init):
        def body(jj, carry):
            cnt, ext = carry
            sblk = sc_scr[jj]
            cnt = cnt + _fold_rows(jnp.where(hit_fn(sblk), 1.0, 0.0), lambda a, b: a + b)
            if ext_fn is not None:
                ext = ext_op(ext, _fold_rows(ext_fn(sblk), ext_op))
            return cnt, ext
        cnt, ext = lax.fori_loop(0, npair, body, (jnp.zeros((8, tq), F32),
                                                  jnp.full((8, tq), ext_init, F32)))
        return jnp.sum(cnt, axis=0, keepdims=True), ext

    def count_ge(t):
        return scan(lambda sblk: sblk >= t, None, None, 0.0)[0]

    c_ge0, ext = scan(lambda sblk: sblk >= 0.0, lambda sblk: sblk, jnp.maximum, -jnp.inf)
    col_max = jnp.max(ext, axis=0, keepdims=True)
    c_gt0, ext = scan(lambda sblk: sblk > 0.0,
                      lambda sblk: jnp.where(sblk > -jnp.inf, sblk, jnp.inf), jnp.minimum, jnp.inf)
    col_min = jnp.min(ext, axis=0, keepdims=True)
    qpos = i * tq + lax.broadcasted_iota(jnp.int32, (1, tq), 1)
    n_valid = (((qpos // CHUNK) + 1) * CHUNK).astype(F32)
    open_q = n_valid < kf
    above = c_gt0 >= kf
    below = c_ge0 < kf
    ikey = lambda v: jnp.full((1, tq), v, jnp.int32)
    lo_k = jnp.where(below, _float_to_key(col_min), ikey(0))
    hi_k = jnp.where(above, _float_to_key(col_max) + 1, jnp.where(below, ikey(0), ikey(1)))
    c_lo = jnp.where(below, n_valid, c_ge0)
    c_hi = jnp.where(above, 0.0, jnp.where(below, c_ge0, c_gt0))
    done = jnp.logical_or(open_q, jnp.logical_not(jnp.logical_or(above, below)))

    def next_probe(lo_k, hi_k, c_lo, c_hi, force_bisect):
        finite = jnp.logical_and(lo_k > KEY_NEG_INF, hi_k < KEY_POS_INF)
        t_lo, t_hi = _key_to_float(lo_k), _key_to_float(hi_k)
        log_lo = jnp.log(c_lo)
        frac = (log_lo - math.log(kf - 0.5)) / (log_lo - jnp.log(jnp.maximum(c_hi, 0.5)))
        k_interp = _float_to_key(t_lo + (t_hi - t_lo) * frac)
        k_mid = (lo_k & hi_k) + ((lo_k ^ hi_k) >> 1)
        interp = jnp.logical_and(finite, force_bisect == 0)
        k = jnp.where(interp, k_interp, k_mid)
        return jnp.minimum(jnp.maximum(k, lo_k + 1), hi_k - 1), interp

    def search_cond(st):
        return jnp.logical_and(st[0] < SEARCH_MAX_STEPS, jnp.min(st[5]) == 0)

    def search_step(st):
        step, lo_k, hi_k, c_lo, c_hi, done, k, interp = st
        c = count_ge(_key_to_float(k))
        ok = c >= kf
        live = done == 0
        new_lo = jnp.where(jnp.logical_and(live, ok), k, lo_k)
        new_hi = jnp.where(jnp.logical_and(live, jnp.logical_not(ok)), k, hi_k)
        new_c_lo = jnp.where(jnp.logical_and(live, ok), c, c_lo)
        new_c_hi = jnp.where(jnp.logical_and(live, jnp.logical_not(ok)), c, c_hi)
        width = lambda a, b: b.astype(F32) - a.astype(F32)
        slow = width(new_lo, new_hi) > 0.5 * width(lo_k, hi_k)
        force = jnp.logical_and(interp != 0, slow).astype(jnp.int32)
        finished = jnp.logical_or(new_c_lo == kf, new_hi - new_lo == 1)
        new_done = jnp.maximum(done, finished.astype(jnp.int32))
        new_k, new_interp = next_probe(new_lo, new_hi, new_c_lo, new_c_hi, force)
        return (step + 1, new_lo, new_hi, new_c_lo, new_c_hi, new_done, new_k,
                new_interp.astype(jnp.int32))

    k0, interp0 = next_probe(lo_k, hi_k, c_lo, c_hi, ikey(0))
    state = (jnp.int32(0), lo_k, hi_k, c_lo, c_hi, done.astype(jnp.int32), k0,
             interp0.astype(jnp.int32))
    _, lo_k, hi_k, c_lo, c_hi, _, _, _ = lax.while_loop(search_cond, search_step, state)
    tau = jnp.where(open_q, -jnp.inf, _key_to_float(lo_k))
    need = jnp.where(open_q, 0.0, jnp.where(c_lo == kf, NO_TIE_LIMIT, kf - c_hi))

    q = q_ref[0]
    for p in range(B_HEADS // 2):
        even, odd = _split_pair(q[:, p * LANES:(p + 1) * LANES], True)
        qs_scr[p * tq:(p + 1) * tq, :] = even
        qs_scr[(B_HEADS // 2 + p) * tq:(B_HEADS // 2 + p + 1) * tq, :] = odd
    m_scr[...] = jnp.full(m_scr.shape, NEG, F32)
    acc_scr[...] = jnp.zeros(acc_scr.shape, F32)
    key_i = lax.broadcasted_iota(jnp.int32, (tq, tq), 0)
    lower = (lax.broadcasted_iota(jnp.int32, (tq, tq), 1) <= key_i).astype(BF16)

    def select_bias(jj, tie_count):
        sblk = sc_scr[jj]
        tie = sblk == tau
        tie01 = jnp.where(tie, 1.0, 0.0).astype(BF16)
        rank_a = tie_count + jnp.dot(lower, tie01[0:tq], preferred_element_type=F32)
        rank_b = rank_a[tq - 1:tq, :] + jnp.dot(lower, tie01[tq:2 * tq], preferred_element_type=F32)
        rank = jnp.concatenate([rank_a, rank_b], axis=0)
        keep_tie = jnp.where(rank <= need, 0.0, NEG)
        return jnp.where(sblk > tau, 0.0, jnp.where(tie, keep_tie, NEG)), rank[2 * tq - 1:2 * tq, :]

    def p3_step(jj, pairs, tie_count):
        nkeys = pairs * 2 * tq
        start = pl.multiple_of(jj * 2 * tq, 2 * tq)
        biases = []
        for t in range(pairs):
            bias, tie_count = select_bias(jj + t, tie_count)
            biases.append(bias)
        bias = jnp.concatenate(biases, axis=0)
        s = _qk(k_ref[0, pl.ds(start, nkeys), :], qs_scr[...])
        s = jnp.concatenate([s[:, h * tq:(h + 1) * tq] + bias for h in range(B_HEADS)], axis=1)
        _flash_t_update(s, [((t * tq, (t + 1) * tq), vt_ref[0, 2 * jj + t]) for t in range(2 * pairs)],
                        m_scr, acc_scr)
        return tie_count

    tie_count = lax.fori_loop(0, npair // P3_PAIRS,
                              lambda g, tc: p3_step(g * P3_PAIRS, P3_PAIRS, tc),
                              jnp.zeros((1, tq), F32))
    for r in range(1, P3_PAIRS):
        @pl.when(npair % P3_PAIRS == r)
        def _():
            p3_step(npair - r, r, tie_count)

    acc = acc_scr[...]
    o = acc[0:HEAD_DIM] / acc[HEAD_DIM:HEAD_DIM + 1]
    half_cols = (B_HEADS // 2) * tq
    for p in range(B_HEADS // 2):
        pair = jnp.concatenate([o[:, p * tq:(p + 1) * tq],
                                o[:, half_cols + p * tq:half_cols + (p + 1) * tq]], axis=0)
        o_ref[0, :, p * LANES:(p + 1) * LANES] = pair.T.astype(BF16)


def _dsa_attention(rope_slab, plain_slab, tq):
    bsz, s, _ = rope_slab.shape
    topk = min(TOPK_MAX, s // 4)
    nblk = s // tq
    assert tq >= topk and s % tq == 0 and nblk % 2 == 0
    kern = functools.partial(_dsa_kernel, tq=tq, topk=topk)
    idx0 = R_IDX * COL_TILE
    misc0 = P_MISC * COL_TILE
    iq_w = IDX_HEADS * HEAD_DIM
    width = B_HEADS * HEAD_DIM
    vt = _with_ones_rows(
        plain_slab[:, :, misc0:misc0 + HEAD_DIM].reshape(bsz, nblk, tq, HEAD_DIM).transpose(0, 1, 3, 2))
    return pl.pallas_call(
        kern,
        grid=(bsz, s // tq),
        in_specs=[
            pl.BlockSpec((1, tq, COL_TILE), lambda b, i: (b, i, R_BQ)),
            pl.BlockSpec((1, tq, iq_w), lambda b, i: (b, i, idx0 // iq_w)),
            pl.BlockSpec((1, tq, LANES), lambda b, i: (b, i, (misc0 + LANES) // LANES)),
            pl.BlockSpec((1, s, LANES), lambda b, i: (b, 0, (idx0 + iq_w) // LANES)),
            pl.BlockSpec((1, s, LANES), lambda b, i: (b, 0, (idx0 + iq_w + LANES) // LANES)),
            pl.BlockSpec((1, nblk, VT_ROWS, tq), lambda b, i: (b, 0, 0, 0)),
        ],
        out_specs=pl.BlockSpec((1, tq, width), lambda b, i: (b, i, 0)),
        out_shape=jax.ShapeDtypeStruct((bsz, s, width), BF16),
        scratch_shapes=[
            pltpu.VMEM((nblk // 2, 2 * tq, tq), F32),
            pltpu.VMEM((B_HEADS * tq, LANES), BF16),
            pltpu.VMEM((1, B_HEADS * tq), F32),
            pltpu.VMEM((VT_ROWS, B_HEADS * tq), F32),
        ],
        compiler_params=_cparams(("arbitrary", "arbitrary")),
        name="dsa_attn",
    )(rope_slab, rope_slab, plain_slab, rope_slab, rope_slab, vt)


BAND_TQ = 2 * CHUNK
BAND_KEYS = (C_LEFT_CHUNKS + BAND_TQ // CHUNK) * CHUNK
BAND_PAD = C_LEFT_CHUNKS * CHUNK


def _band_kernel(q_ref, k_ref, v_ref, bias_ref, o_ref):
    i = pl.program_id(1)
    tq = BAND_TQ
    start = pl.multiple_of(i * tq, tq)
    col = lax.broadcasted_iota(jnp.int32, (2 * tq, BAND_KEYS), 1)
    key_ok = col + start >= BAND_PAD
    for p in range(C_HEADS // 2):
        lanes = slice(p * LANES, (p + 1) * LANES)
        k = k_ref[0, pl.ds(start, BAND_KEYS), lanes]
        v = v_ref[0, pl.ds(start, BAND_KEYS), lanes]
        qs = jnp.concatenate(_split_pair(q_ref[0, :, lanes], False), axis=0)
        s = _qk(qs, k) + bias_ref[2 * p:2 * p + 2].reshape(2 * tq, BAND_KEYS)
        s = jnp.where(key_ok, s, NEG)
        m = jnp.max(s, axis=-1, keepdims=True)
        e = jnp.exp2(s - m)
        l = jnp.sum(e, axis=-1, keepdims=True)
        eb = e.astype(BF16)
        v_even, v_odd = _split_pair(v, False)
        o = (jnp.dot(eb[:tq], v_even, preferred_element_type=F32) / l[:tq]
             + jnp.dot(eb[tq:], v_odd, preferred_element_type=F32) / l[tq:])
        o_ref[0, :, lanes] = o.astype(BF16)


def _band_bias(rel_bias):
    r = jnp.arange(BAND_TQ)[:, None]
    cidx = jnp.arange(BAND_KEYS)[None, :]
    n_diag = BAND_TQ + BAND_KEYS - 1
    rel = BAND_PAD - (BAND_KEYS - 1) + jnp.arange(n_diag)
    g = rel_bias.astype(F32)[:, jnp.clip(rel, -REL_CLIP, REL_CLIP) + REL_CLIP] * LOG2E
    skew = jnp.tile(g, (1, BAND_TQ + 2))[:, :BAND_TQ * (n_diag + 1)]
    bias = skew.reshape(-1, BAND_TQ, n_diag + 1)[:, :, :BAND_KEYS][:, :, ::-1]
    dchunk = (r // CHUNK + C_LEFT_CHUNKS) - cidx // CHUNK
    in_band = jnp.logical_and(dchunk >= 0, dchunk <= C_LEFT_CHUNKS)
    return jnp.where(in_band[None], bias, NEG)


def _band_attention(plain_slab, kp, vp, bias):
    bsz, s, _ = plain_slab.shape
    tq = BAND_TQ
    sp = kp.shape[1]
    width = C_HEADS * HEAD_DIM
    return pl.pallas_call(
        _band_kernel,
        grid=(bsz, s // tq),
        in_specs=[
            pl.BlockSpec((1, tq, width), lambda b, i: (b, i, P_CQ)),
            pl.BlockSpec((1, sp, width), lambda b, i: (b, 0, 0)),
            pl.BlockSpec((1, sp, width), lambda b, i: (b, 0, 0)),
            pl.BlockSpec((C_HEADS, tq, BAND_KEYS), lambda b, i: (0, 0, 0)),
        ],
        out_specs=pl.BlockSpec((1, tq, width), lambda b, i: (b, i, 0)),
        out_shape=jax.ShapeDtypeStruct((bsz, s, width), BF16),
        compiler_params=_cparams(("arbitrary", "arbitrary")),
        name="band_attn",
    )(plain_slab, kp, vp, bias)


def _merge_kernel(x_ref, g1_ref, ya_ref, yb_ref, yc_ref, ga_ref, gb_ref, gc_ref,
                  wa_ref, wb_ref, wc_ref, wo_ref, o_ref):
    for r0 in range(0, x_ref.shape[1], PROJ_ROWS):
        rows = slice(r0, r0 + PROJ_ROWS)

        def branch(y_ref, w_ref, gate_ref):
            return gate_ref[0, rows, :].astype(F32) * jnp.dot(y_ref[0, rows, :], w_ref[...],
                                                             preferred_element_type=F32)

        merged = (branch(ya_ref, wa_ref, ga_ref) + branch(yb_ref, wb_ref, gb_ref)
                  + branch(yc_ref, wc_ref, gc_ref))
        mixed = jnp.dot(merged.astype(BF16), wo_ref[...], preferred_element_type=F32)
        o_ref[0, rows, :] = x_ref[0, rows, :] + g1_ref[0] * mixed


def _merge(x, g1, ya, yb, yc, gate_slab, wa, wb, wc, wo, tm):
    bsz, s, d = x.shape
    tok = lambda b, i: (b, i, 0)
    full = lambda b, i: (0, 0)
    return pl.pallas_call(
        _merge_kernel,
        grid=(bsz, s // tm),
        in_specs=[
            pl.BlockSpec((1, tm, d), tok),
            pl.BlockSpec((1, 1, d), lambda b, i: (b, 0, 0)),
            pl.BlockSpec((1, tm, ya.shape[2]), tok),
            pl.BlockSpec((1, tm, yb.shape[2]), tok),
            pl.BlockSpec((1, tm, yc.shape[2]), tok),
            pl.BlockSpec((1, tm, d), lambda b, i: (b, i, 0)),
            pl.BlockSpec((1, tm, d), lambda b, i: (b, i, 1)),
            pl.BlockSpec((1, tm, d), lambda b, i: (b, i, 2)),
            pl.BlockSpec(wa.shape, full),
            pl.BlockSpec(wb.shape, full),
            pl.BlockSpec(wc.shape, full),
            pl.BlockSpec(wo.shape, full),
        ],
        out_specs=pl.BlockSpec((1, tm, d), tok),
        out_shape=jax.ShapeDtypeStruct((bsz, s, d), F32),
        compiler_params=_cparams(("arbitrary", "arbitrary")),
        name="merge",
    )(x, g1, ya, yb, yc, gate_slab, gate_slab, gate_slab, wa, wb, wc, wo)


def _router_kernel(x_ref, g_ref, sc_ref, sh_ref, rw_ref, rb_ref, u_ref, comb_ref):
    u = _modulated_norm(x_ref[0], g_ref[...], sc_ref[0], sh_ref[0])
    u_ref[0] = u.astype(BF16)
    logits = lax.dot_general(rw_ref[...], u, (((1,), (1,)), ((), ())),
                             preferred_element_type=F32, precision=lax.Precision.HIGHEST)
    aff = jax.nn.sigmoid(logits)
    sel = aff + rb_ref[...]
    rows = [sel[e:e + 1] for e in range(N_EXPERTS)]
    gscore = []
    for g in range(N_GROUPS):
        r = rows[g * EXPERTS_PER_GROUP:(g + 1) * EXPERTS_PER_GROUP]
        best = None
        for a in range(EXPERTS_PER_GROUP):
            for b in range(a + 1, EXPERTS_PER_GROUP):
                pair = r[a] + r[b]
                best = pair if best is None else jnp.maximum(best, pair)
        gscore.append(best)
    gmax = functools.reduce(jnp.maximum, gscore)
    taken = jnp.zeros_like(gmax) > 1.0
    in_best = []
    for g in range(N_GROUPS):
        is_g = jnp.logical_and(gscore[g] == gmax, jnp.logical_not(taken))
        in_best.append(is_g)
        taken = jnp.logical_or(taken, is_g)
    keep = []
    for e in range(N_EXPERTS):
        g = e // EXPERTS_PER_GROUP
        rank = jnp.zeros_like(gmax)
        for o in range(g * EXPERTS_PER_GROUP, (g + 1) * EXPERTS_PER_GROUP):
            if o == e:
                continue
            ahead = rows[o] > rows[e] if o > e else rows[o] >= rows[e]
            rank = rank + jnp.where(ahead, 1.0, 0.0)
        keep.append(jnp.logical_and(in_best[g], rank < 2.0))
    w = [jnp.where(keep[e], aff[e:e + 1], 0.0) for e in range(N_EXPERTS)]
    total = functools.reduce(lambda a, b: a + b, w)
    comb = jnp.concatenate([we / total for we in w]
                           + [jnp.zeros((LANES - N_EXPERTS, total.shape[1]), F32)], axis=0)
    comb_ref[0] = comb.T


def _router(x, g, sc, sh, router_w, router_b, tm):
    bsz, s, d = x.shape
    tok = lambda b, i: (b, i, 0)
    return pl.pallas_call(
        _router_kernel,
        grid=(bsz, s // tm),
        in_specs=[
            pl.BlockSpec((1, tm, d), tok),
            pl.BlockSpec((1, d), lambda b, i: (0, 0)),
            pl.BlockSpec((1, 1, d), lambda b, i: (b, 0, 0)),
            pl.BlockSpec((1, 1, d), lambda b, i: (b, 0, 0)),
            pl.BlockSpec((N_EXPERTS, d), lambda b, i: (0, 0)),
            pl.BlockSpec((N_EXPERTS, 1), lambda b, i: (0, 0)),
        ],
        out_specs=[pl.BlockSpec((1, tm, d), tok), pl.BlockSpec((1, tm, LANES), tok)],
        out_shape=[jax.ShapeDtypeStruct((bsz, s, d), BF16),
                   jax.ShapeDtypeStruct((bsz, s, LANES), F32)],
        compiler_params=_cparams(("arbitrary", "arbitrary")),
        name="router",
    )(x, g.reshape(1, d), sc, sh, router_w.T, router_b.reshape(N_EXPERTS, 1))


MOE_SUB = 512
MOE_CAP = 128


def _expert_ffn(rows, w1_ref, w3_ref, w2_ref):
    h1 = jnp.dot(rows, w1_ref[0], preferred_element_type=F32)
    h3 = jnp.dot(rows, w3_ref[0], preferred_element_type=F32)
    h = (h1 * jax.nn.sigmoid(h1)) * h3
    return jnp.dot(h.astype(BF16), w2_ref[0], preferred_element_type=F32)


def _moe_kernel(x_ref, g2_ref, fg_ref, u_ref, comb_ref, w1_ref, w3_ref, w2_ref, o_ref,
                acc_scr, pos_scr, pos_t_scr, comb_t_scr, scatter_scr, y_scr, *, final_norm):
    e = pl.program_id(2)
    tm = u_ref.shape[1]
    sub = min(MOE_SUB, tm)
    subs = [slice(r0, r0 + sub) for r0 in range(0, tm, sub)]

    @pl.when(e == 0)
    def _():
        acc_scr[...] = jnp.zeros(acc_scr.shape, F32)
        before = (lax.broadcasted_iota(jnp.int32, (sub, sub), 1)
                  < lax.broadcasted_iota(jnp.int32, (sub, sub), 0)).astype(BF16)
        for rows in subs:
            comb0 = comb_ref[0, rows, :]
            member = jnp.where(comb0 > 0.0, 1.0, 0.0).astype(BF16)
            pos = jnp.dot(before, member, preferred_element_type=F32)
            pos_scr[rows, :] = pos
            pos_t_scr[:, rows] = pos.T
            comb_t_scr[:, rows] = comb0.T

    comb = comb_ref[0]
    lane = lax.broadcasted_iota(jnp.int32, comb.shape, 1)
    ce = jnp.sum(jnp.where(lane == e, comb, 0.0), axis=-1, keepdims=True)
    pe = jnp.sum(jnp.where(lane == e, pos_scr[...], 0.0), axis=-1, keepdims=True)
    member_e = jnp.where(ce > 0.0, 1.0, 0.0)
    fullest = functools.reduce(jnp.maximum, [jnp.sum(member_e[rows]) for rows in subs])
    routed = fullest <= float(MOE_CAP)
    slots = pl.ds(pl.multiple_of(e * MOE_CAP, MOE_CAP), MOE_CAP)

    @pl.when(routed)
    def _():
        slot_l = lax.broadcasted_iota(jnp.int32, (1, MOE_CAP), 1).astype(F32)
        scatter_scr[e] = jnp.where(jnp.logical_and(ce > 0.0, pe == slot_l), 1.0, 0.0).astype(BF16)
        slot_s = lax.broadcasted_iota(jnp.int32, (MOE_CAP, 1), 0).astype(F32)
        packed, weights = [], []
        for rows in subs:
            ce_t = comb_t_scr[pl.ds(e, 1), rows]
            pe_t = pos_t_scr[pl.ds(e, 1), rows]
            hit = jnp.logical_and(ce_t > 0.0, pe_t == slot_s)
            weights.append(jnp.sum(jnp.where(hit, ce_t, 0.0), axis=-1, keepdims=True))
            packed.append(jnp.dot(jnp.where(hit, 1.0, 0.0).astype(BF16), u_ref[0, rows, :],
                                  preferred_element_type=F32).astype(BF16))
        y = jnp.concatenate(weights, axis=0) * _expert_ffn(jnp.concatenate(packed, axis=0),
                                                           w1_ref, w3_ref, w2_ref)
        for k in range(len(subs)):
            y_scr[k, slots, :] = y[k * MOE_CAP:(k + 1) * MOE_CAP].astype(BF16)

    @pl.when(jnp.logical_not(routed))
    def _():
        scatter_scr[e] = jnp.zeros(scatter_scr.shape[1:], BF16)
        for k in range(len(subs)):
            y_scr[k, slots, :] = jnp.zeros((MOE_CAP, y_scr.shape[2]), BF16)
        acc_scr[...] += ce * _expert_ffn(u_ref[0], w1_ref, w3_ref, w2_ref)

    @pl.when(e == pl.num_programs(2) - 1)
    def _():
        for k, rows in enumerate(subs):
            scatter = jnp.concatenate([scatter_scr[j, rows, :] for j in range(N_EXPERTS)], axis=1)
            y = acc_scr[rows, :] + jnp.dot(scatter, y_scr[k], preferred_element_type=F32)
            out = x_ref[0, rows, :] + g2_ref[0] * y
            if final_norm:
                out = (out * lax.rsqrt(jnp.mean(out * out, axis=-1, keepdims=True) + EPS)) * fg_ref[...]
            o_ref[0, rows, :] = out


def _moe(x, g2, final_g, u, comb, w1, w3, w2, tm, final_norm):
    bsz, s, d = x.shape
    ne, _, dff = w1.shape
    assert ne == N_EXPERTS
    tok = lambda b, i, e: (b, i, 0)
    return pl.pallas_call(
        functools.partial(_moe_kernel, final_norm=final_norm),
        grid=(bsz, s // tm, ne),
        in_specs=[
            pl.BlockSpec((1, tm, d), tok),
            pl.BlockSpec((1, 1, d), lambda b, i, e: (b, 0, 0)),
            pl.BlockSpec((1, d), lambda b, i, e: (0, 0)),
            pl.BlockSpec((1, tm, d), tok),
            pl.BlockSpec((1, tm, LANES), tok),
            pl.BlockSpec((1, d, dff), lambda b, i, e: (e, 0, 0)),
            pl.BlockSpec((1, d, dff), lambda b, i, e: (e, 0, 0)),
            pl.BlockSpec((1, dff, d), lambda b, i, e: (e, 0, 0)),
        ],
        out_specs=pl.BlockSpec((1, tm, d), tok),
        out_shape=jax.ShapeDtypeStruct((bsz, s, d), F32),
        scratch_shapes=[
            pltpu.VMEM((tm, d), F32),
            pltpu.VMEM((tm, LANES), F32),
            pltpu.VMEM((LANES, tm), F32),
            pltpu.VMEM((LANES, tm), F32),
            pltpu.VMEM((ne, tm, MOE_CAP), BF16),
            pltpu.VMEM((pl.cdiv(tm, MOE_SUB), ne * MOE_CAP, d), BF16),
        ],
        compiler_params=_cparams(("arbitrary", "arbitrary", "arbitrary")),
        name="moe",
    )(x, g2, final_g.reshape(1, d), u, comb, w1, w3, w2)


def _tile(s, want):
    t = min(want, s)
    assert s % t == 0
    return t


def _tiles(s):
    return dict(
        rows=_tile(s, 1024),
        diff_q=_tile(s, 1024),
        diff_k=_tile(s, 512),
        dsa_q=_tile(s, 256),
        merge=_tile(s, 512),
    )


def kernel(x, c, positions, norm1_g, norm2_g, w_mod, b_mod, w_in, lambda_q1, lambda_k1, lambda_q2, lambda_k2, a_norm_g, c_rel_bias, w_branch_a, w_branch_b, w_branch_c, w_out, router_w, router_b, exp_w1, exp_w3, exp_w2, final_g):
    bsz, s, d = x.shape
    depth = w_mod.shape[0]
    t = _tiles(s)
    tm = t["rows"]

    mod = _modulation(c, w_mod, b_mod)
    rope_tables = _rope_tables(positions)

    for layer in range(depth):
        lam_init = 0.8 - 0.6 * math.exp(-0.3 * layer)
        sh1, sc1, g1, sh2, sc2, g2 = [m[:, None, :] for m in jnp.split(mod[layer], 6, axis=-1)]
        lam = (jnp.exp(jnp.sum(lambda_q1[layer] * lambda_k1[layer]))
               - jnp.exp(jnp.sum(lambda_q2[layer] * lambda_k2[layer])) + lam_init)

        w_rope, w_gate, w_plain = _build_weights(w_in[layer])
        u = _norm(x, norm1_g[layer], sc1, sh1, tm)
        rope_slab = _project(u, w_rope, rope_tables, "rope", tm)
        gate_slab = _project(u, w_gate, (), "gate", tm)
        plain_slab = _project(u, w_plain, (), "plain", tm)

        ya = _diff_attention(rope_slab, plain_slab, lam, a_norm_g[layer], lam_init,
                             t["diff_q"], t["diff_k"])
        yb = _dsa_attention(rope_slab, plain_slab, t["dsa_q"])
        pad = ((0, 0), (BAND_PAD, 0), (0, 0))
        kp = jnp.pad(plain_slab[:, :, P_CK * COL_TILE:(P_CK + 1) * COL_TILE], pad)
        vp = jnp.pad(plain_slab[:, :, P_CV * COL_TILE:(P_CV + 1) * COL_TILE], pad)
        yc = _band_attention(plain_slab, kp, vp, _band_bias(c_rel_bias[layer]))
        x = _merge(x, g1, ya, yb, yc, gate_slab,
                   w_branch_a[layer].astype(BF16), w_branch_b[layer].astype(BF16),
                   w_branch_c[layer].astype(BF16), w_out[layer].astype(BF16), t["merge"])

        u, comb = _router(x, norm2_g[layer], sc2, sh2, router_w, router_b, t["merge"])
        x = _moe(x, g2, final_g, u, comb, exp_w1[layer].astype(BF16), exp_w3[layer].astype(BF16),
                 exp_w2[layer].astype(BF16), tm, final_norm=(layer == depth - 1))

    return x
```

```python
import functools
import math

import jax
import jax.numpy as jnp
from jax import lax
from jax.experimental import pallas as pl
from jax.experimental.pallas import tpu as pltpu

F32 = jnp.float32
BF16 = jnp.bfloat16

CHUNK = 64
ROPE_THETA = 10000.0
EPS = 1e-6
A_HEADS = 4
HEAD_DIM = 64
B_HEADS = 8
IDX_HEADS = 4
TOPK_MAX = 256
C_HEADS = 8
C_LEFT_CHUNKS = 8
REL_CLIP = 256
N_EXPERTS = 16
N_GROUPS = 4
EXPERTS_PER_GROUP = 4
N_BRANCHES = 3

LANES = 128
NEG = -1e30
LOG2E = math.log2(math.e)
VMEM_LIMIT = 56 * 1024 * 1024

COL_TILE = 512
PROJ_ROWS = 256
R_AQ, R_AK, R_BQ, R_IDX = 0, 1, 2, 3
P_AV, P_CQ, P_CK, P_CV, P_MISC = 0, 1, 2, 3, 4


def _cparams(sem):
    return pltpu.CompilerParams(dimension_semantics=sem, vmem_limit_bytes=VMEM_LIMIT)


def _mod_kernel(c_ref, w_ref, b_ref, o_ref):
    c = c_ref[...]
    ca = c * jax.nn.sigmoid(c)
    o_ref[0] = jnp.dot(ca, w_ref[0], preferred_element_type=F32) + b_ref[0]


def _modulation(c, w_mod, b_mod):
    depth, d, n6 = w_mod.shape
    bsz = c.shape[0]
    tn = 1024
    return pl.pallas_call(
        _mod_kernel,
        grid=(depth, n6 // tn),
        in_specs=[
            pl.BlockSpec((bsz, d), lambda l, j: (0, 0)),
            pl.BlockSpec((1, d, tn), lambda l, j: (l, 0, j)),
            pl.BlockSpec((1, 1, tn), lambda l, j: (l, 0, j)),
        ],
        out_specs=pl.BlockSpec((1, bsz, tn), lambda l, j: (l, 0, j)),
        out_shape=jax.ShapeDtypeStruct((depth, bsz, n6), F32),
        compiler_params=_cparams(("arbitrary", "arbitrary")),
        name="modulation",
    )(c, w_mod, b_mod.reshape(depth, 1, n6))


def _modulated_norm(x, g, sc, sh):
    y = x * lax.rsqrt(jnp.mean(x * x, axis=-1, keepdims=True) + EPS)
    return (y * g) * (1.0 + sc) + sh


def _norm_kernel(x_ref, g_ref, sc_ref, sh_ref, u_ref):
    u_ref[0] = _modulated_norm(x_ref[0], g_ref[...], sc_ref[0], sh_ref[0]).astype(BF16)


def _norm(x, g, sc, sh, tm):
    bsz, s, d = x.shape
    tok = lambda b, i: (b, i, 0)
    per_batch = lambda b, i: (b, 0, 0)
    return pl.pallas_call(
        _norm_kernel,
        grid=(bsz, s // tm),
        in_specs=[pl.BlockSpec((1, tm, d), tok), pl.BlockSpec((1, d), lambda b, i: (0, 0)),
                  pl.BlockSpec((1, 1, d), per_batch), pl.BlockSpec((1, 1, d), per_batch)],
        out_specs=pl.BlockSpec((1, tm, d), tok),
        out_shape=jax.ShapeDtypeStruct((bsz, s, d), BF16),
        compiler_params=_cparams(("arbitrary", "arbitrary")),
        name="norm",
    )(x, g.reshape(1, d), sc, sh)


def _proj_kernel(*refs, mode):
    if mode == "rope":
        u_ref, cos_ref, sin_ref, w_ref, o_ref = refs
    else:
        u_ref, w_ref, o_ref = refs
    tm = u_ref.shape[1]
    for r0 in range(0, tm, PROJ_ROWS):
        rows = slice(r0, r0 + PROJ_ROWS)
        acc = jnp.dot(u_ref[0, rows, :], w_ref[...], preferred_element_type=F32)
        if mode == "rope":
            cos, sin = cos_ref[0, rows, :], sin_ref[0, rows, :]
            groups = [acc[:, c:c + LANES] for c in range(0, COL_TILE, LANES)]
            acc = jnp.concatenate(
                [g * cos + pltpu.roll(g, LANES // 2, axis=1) * sin for g in groups], axis=1)
        elif mode == "gate":
            acc = 1.0 / (1.0 + jnp.exp(-acc))
        o_ref[0, rows, :] = acc.astype(BF16)


def _project(u, w, rope_tables, mode, tm):
    bsz, s, d = u.shape
    ncols = w.shape[1]
    tok = lambda b, i, j: (b, i, 0)
    in_specs = [pl.BlockSpec((1, tm, d), tok)]
    in_specs += [pl.BlockSpec((1, tm, LANES), tok) for _ in rope_tables]
    in_specs += [pl.BlockSpec((d, COL_TILE), lambda b, i, j: (0, j))]
    return pl.pallas_call(
        functools.partial(_proj_kernel, mode=mode),
        grid=(bsz, s // tm, ncols // COL_TILE),
        in_specs=in_specs,
        out_specs=pl.BlockSpec((1, tm, COL_TILE), lambda b, i, j: (b, i, j)),
        out_shape=jax.ShapeDtypeStruct((bsz, s, ncols), BF16),
        compiler_params=_cparams(("arbitrary", "arbitrary", "arbitrary")),
        name="proj_" + mode,
    )(u, *rope_tables, w)


def _pair_layout(w):
    half = HEAD_DIM // 2
    col = jnp.arange(w.shape[1])
    base, r = (col // LANES) * LANES, col % LANES
    src = base + ((r // half) % 2) * HEAD_DIM + (r // HEAD_DIM) * half + r % half
    return w[:, src]


def _build_weights(w_in):
    sizes = (512, 512, 512, 512, 64, 64, 256, 64, 4, 512, 512, 512, 3072)
    parts, start = [], 0
    for sz in sizes:
        parts.append(w_in[:, start:start + sz])
        start += sz
    aq, ak, av, bq, bk, bv, iq, ik, iw, cq, ck, cv, gates = parts
    d = w_in.shape[0]
    qscale = HEAD_DIM ** -0.5 * LOG2E
    iw_scale = IDX_HEADS ** -0.5 * HEAD_DIM ** -0.5
    zeros = lambda n: jnp.zeros((d, n), w_in.dtype)
    w_rope = _pair_layout(jnp.concatenate([aq * qscale, ak, bq * qscale, iq, bk, bk, ik, ik], axis=1))
    w_plain = jnp.concatenate([av, cq * qscale, ck, cv,
                               bv, bv, iw * iw_scale, zeros(LANES - IDX_HEADS),
                               zeros(COL_TILE - 2 * LANES)], axis=1)
    return w_rope.astype(BF16), gates.astype(BF16), w_plain.astype(BF16)


def _rope_tables(positions):
    half = HEAD_DIM // 2
    inv = ROPE_THETA ** (-jnp.arange(half, dtype=F32) / half)
    ang = positions.astype(F32)[..., None] * inv
    cos, sin = jnp.cos(ang), jnp.sin(ang)
    cos_t = jnp.tile(cos, (1, 1, LANES // half))
    sin_t = jnp.concatenate([-sin, -sin, sin, sin], axis=-1)
    return cos_t, sin_t


def _split_pair(pair, interleaved):
    lane = lax.broadcasted_iota(jnp.int32, pair.shape, 1)
    first = ((lane // (HEAD_DIM // 2)) % 2 == 0) if interleaved else (lane < HEAD_DIM)
    zero = jnp.zeros_like(pair)
    return jnp.where(first, pair, zero), jnp.where(first, zero, pair)


def _qk(q, k):
    return lax.dot_general(q, k, (((1,), (1,)), ((), ())), preferred_element_type=F32)


FOLD_CHAINS = 4


def _fold_rows(x, op):
    parts = x.reshape(x.shape[0] // 8, 8, x.shape[1])
    chains = [functools.reduce(op, [parts[g] for g in range(c, parts.shape[0], FOLD_CHAINS)])
              for c in range(min(FOLD_CHAINS, parts.shape[0]))]
    return functools.reduce(op, chains)


def _flash_t_update(s, vt_blocks, m_scr, acc_scr):
    m_old = m_scr[...]
    m_new = jnp.maximum(m_old, jnp.max(_fold_rows(s, jnp.maximum), axis=0, keepdims=True))
    alpha = jnp.exp2(m_old - m_new)
    pb = jnp.exp2(s - m_new).astype(BF16)
    m_scr[...] = m_new
    pv = functools.reduce(lambda a, b: a + b, [
        jnp.dot(vt, pb[k0:k1], preferred_element_type=F32) for (k0, k1), vt in vt_blocks])
    acc_scr[...] = alpha * acc_scr[...] + pv


ONES_ROWS = 16


def _with_ones_rows(vt):
    shape = vt.shape[:-2]
    return jnp.concatenate([vt, jnp.ones(shape + (1, vt.shape[-1]), vt.dtype),
                            jnp.zeros(shape + (ONES_ROWS - 1, vt.shape[-1]), vt.dtype)], axis=-2)


def _flash_update(s, v, m_scr, l_scr, acc_scr, rows=slice(None)):
    groups = [s[:, c:c + LANES] for c in range(0, s.shape[1], LANES)]
    m_old = m_scr[rows, :]
    lane_max = functools.reduce(jnp.maximum, groups)
    m_new = jnp.maximum(m_old, jnp.max(lane_max, axis=-1, keepdims=True))
    alpha = jnp.exp2(m_old - m_new)
    p_groups = [jnp.exp2(g - m_new) for g in groups]
    l_scr[rows, :] = alpha * l_scr[rows, :] + functools.reduce(lambda a, b: a + b, p_groups)
    m_scr[rows, :] = m_new
    pb = jnp.concatenate([g.astype(BF16) for g in p_groups], axis=1)
    acc_scr[rows, :] = alpha * acc_scr[rows, :] + jnp.dot(pb, v, preferred_element_type=F32)


def _diff_attn_kernel(lam_ref, q_ref, k_ref, v_ref, ng_ref, o_ref, m_scr, l_scr, acc_scr,
                      *, tq, tk, lam_init):
    i = pl.program_id(2)
    per_tile = tq // tk
    qs = jnp.concatenate(_split_pair(q_ref[0], True), axis=0)
    m_scr[...] = jnp.full(m_scr.shape, NEG, F32)
    l_scr[...] = jnp.zeros(l_scr.shape, F32)
    acc_scr[...] = jnp.zeros(acc_scr.shape, F32)

    def kv_block(j):
        start = pl.multiple_of(j * tk, tk)
        return k_ref[0, pl.ds(start, tk), :], v_ref[0, pl.ds(start, tk), :]

    def body(j, carry):
        k, v = kv_block(j)
        _flash_update(_qk(qs, k), v, m_scr, l_scr, acc_scr)
        return carry

    lax.fori_loop(0, i * per_tile, body, 0)
    for d in range(per_tile):
        k, v = kv_block(i * per_tile + d)
        for base in (0, tq):
            rows = slice(base + d * tk, base + tq)
            s = _qk(qs[rows], k)
            row = lax.broadcasted_iota(jnp.int32, s.shape, 0) + d * tk
            col = lax.broadcasted_iota(jnp.int32, s.shape, 1) + d * tk
            s = jnp.where((col // CHUNK) <= (row // CHUNK), s, NEG)
            _flash_update(s, v, m_scr, l_scr, acc_scr, rows)

    o = acc_scr[...] / jnp.sum(l_scr[...], axis=-1, keepdims=True)
    o = o[:tq] - lam_ref[0] * o[tq:]
    o = o * lax.rsqrt(jnp.mean(o * o, axis=-1, keepdims=True) + EPS)
    o_ref[0] = ((o * ng_ref[...]) * (1.0 - lam_init)).astype(BF16)


def _diff_attention(rope_slab, plain_slab, lam, norm_g, lam_init, tq, tk):
    bsz, s, _ = rope_slab.shape
    assert tq % tk == 0
    kern = functools.partial(_diff_attn_kernel, tq=tq, tk=tk, lam_init=lam_init)
    cb = COL_TILE // LANES
    return pl.pallas_call(
        kern,
        grid=(bsz, A_HEADS, s // tq),
        in_specs=[
            pl.BlockSpec(memory_space=pltpu.SMEM),
            pl.BlockSpec((1, tq, LANES), lambda b, h, i: (b, i, R_AQ * cb + h)),
            pl.BlockSpec((1, s, LANES), lambda b, h, i: (b, 0, R_AK * cb + h)),
            pl.BlockSpec((1, s, LANES), lambda b, h, i: (b, 0, P_AV * cb + h)),
            pl.BlockSpec((1, LANES), lambda b, h, i: (0, 0)),
        ],
        out_specs=pl.BlockSpec((1, tq, LANES), lambda b, h, i: (b, i, h)),
        out_shape=jax.ShapeDtypeStruct((bsz, s, A_HEADS * LANES), BF16),
        scratch_shapes=[pltpu.VMEM((2 * tq, LANES), F32), pltpu.VMEM((2 * tq, LANES), F32),
                        pltpu.VMEM((2 * tq, LANES), F32)],
        compiler_params=_cparams(("arbitrary", "arbitrary", "arbitrary")),
        name="diff_attn",
    )(lam.reshape(1), rope_slab, rope_slab, plain_slab, norm_g.reshape(1, LANES))


KEY_NEG_INF = -2139095040
KEY_POS_INF = 2139095040
SEARCH_MAX_STEPS = 80
NO_TIE_LIMIT = 1e9
P3_PAIRS = 2
VT_ROWS = HEAD_DIM + ONES_ROWS


def _key_to_float(key):
    bits = jnp.where(key >= 0, key, (key - 1) ^ jnp.int32(0x7FFFFFFF))
    return pltpu.bitcast(bits, F32)


def _float_to_key(t):
    bits = pltpu.bitcast(t, jnp.int32)
    return jnp.where(bits >= 0, bits, (bits ^ jnp.int32(0x7FFFFFFF)) + 1)


def _dsa_kernel(q_ref, iq_ref, iw_ref, k_ref, ik_ref, vt_ref, o_ref,
                sc_scr, qs_scr, m_scr, acc_scr, *, tq, topk):
    i = pl.program_id(1)
    nblk = i + 1
    npair = (nblk + 1) // 2
    nq = B_HEADS * tq

    iq = iq_ref[0]
    parts = []
    for p in range(IDX_HEADS // 2):
        parts += list(_split_pair(iq[:, p * LANES:(p + 1) * LANES], True))
    iqs = jnp.concatenate(parts, axis=0)
    iw_t = iw_ref[0].astype(F32).T

    def index_block(j):
        start = pl.multiple_of(j * tq, tq)
        ik = ik_ref[0, pl.ds(start, tq), :]
        score = None
        for h in range(IDX_HEADS):
            hs = jnp.maximum(_qk(ik, iqs[h * tq:(h + 1) * tq]), 0.0)
            score = iw_t[h:h + 1] * hs if score is None else score + iw_t[h:h + 1] * hs
        return score

    def p1_body(jj, carry):
        sc_scr[jj, 0:tq, :] = index_block(2 * jj)
        sc_scr[jj, tq:2 * tq, :] = index_block(2 * jj + 1)
        return carry

    lax.fori_loop(0, i // 2, p1_body, 0)
    score = index_block(i)
    key_i = lax.broadcasted_iota(jnp.int32, score.shape, 0)
    qry_i = lax.broadcasted_iota(jnp.int32, score.shape, 1)
    diagonal = jnp.where((key_i // CHUNK) <= (qry_i // CHUNK), score, -jnp.inf)

    @pl.when(i % 2 == 1)
    def _():
        sc_scr[npair - 1, 0:tq, :] = index_block(i - 1)
        sc_scr[npair - 1, tq:2 * tq, :] = diagonal

    @pl.when(i % 2 == 0)
    def _():
        sc_scr[npair - 1, 0:tq, :] = diagonal
        sc_scr[npair - 1, tq:2 * tq, :] = jnp.full((tq, tq), -jnp.inf, F32)

    kf = float(topk)

    def scan(hit_fn, ext_fn, ext_op, ext_init):
        def one_pair(jj, carry):
            cnt, ext = carry
            sblk = sc_scr[jj]
            cnt = cnt + _fold_rows(jnp.where(hit_fn(sblk), 1.0, 0.0), lambda a, b: a + b)
            if ext_fn is not None:
                ext = ext_op(ext, _fold_rows(ext_fn(sblk), ext_op))
            return cnt, ext

        def two_pairs(g, carry):
            return one_pair(2 * g + 1, one_pair(2 * g, carry))

        carry = lax.fori_loop(0, npair // 2, two_pairs, (jnp.zeros((8, tq), F32),
                                                         jnp.full((8, tq), ext_init, F32)))
        cnt, ext = lax.cond(npair % 2 == 1, lambda c: one_pair(npair - 1, c), lambda c: c, carry)
        return jnp.sum(cnt, axis=0, keepdims=True), ext

    def count_ge(t):
        return scan(lambda sblk: sblk >= t, None, None, 0.0)[0]

    c_ge0, ext = scan(lambda sblk: sblk >= 0.0, lambda sblk: sblk, jnp.maximum, -jnp.inf)
    col_max = jnp.max(ext, axis=0, keepdims=True)
    c_gt0, ext = scan(lambda sblk: sblk > 0.0,
                      lambda sblk: jnp.where(sblk > -jnp.inf, sblk, jnp.inf), jnp.minimum, jnp.inf)
    col_min = jnp.min(ext, axis=0, keepdims=True)
    qpos = i * tq + lax.broadcasted_iota(jnp.int32, (1, tq), 1)
    n_valid = (((qpos // CHUNK) + 1) * CHUNK).astype(F32)
    open_q = n_valid < kf
    above = c_gt0 >= kf
    below = c_ge0 < kf
    ikey = lambda v: jnp.full((1, tq), v, jnp.int32)
    lo_k = jnp.where(below, _float_to_key(col_min), ikey(0))
    hi_k = jnp.where(above, _float_to_key(col_max) + 1, jnp.where(below, ikey(0), ikey(1)))
    c_lo = jnp.where(below, n_valid, c_ge0)
    c_hi = jnp.where(above, 0.0, jnp.where(below, c_ge0, c_gt0))
    done = jnp.logical_or(open_q, jnp.logical_not(jnp.logical_or(above, below)))

    def next_probe(lo_k, hi_k, c_lo, c_hi, force_bisect):
        finite = jnp.logical_and(lo_k > KEY_NEG_INF, hi_k < KEY_POS_INF)
        t_lo, t_hi = _key_to_float(lo_k), _key_to_float(hi_k)
        log_lo = jnp.log(c_lo)
        frac = (log_lo - math.log(kf - 0.5)) / (log_lo - jnp.log(jnp.maximum(c_hi, 0.5)))
        k_interp = _float_to_key(t_lo + (t_hi - t_lo) * frac)
        k_mid = (lo_k & hi_k) + ((lo_k ^ hi_k) >> 1)
        interp = jnp.logical_and(finite, force_bisect == 0)
        k = jnp.where(interp, k_interp, k_mid)
        return jnp.minimum(jnp.maximum(k, lo_k + 1), hi_k - 1), interp

    def search_cond(st):
        return jnp.logical_and(st[0] < SEARCH_MAX_STEPS, jnp.min(st[5]) == 0)

    def search_step(st):
        step, lo_k, hi_k, c_lo, c_hi, done, k, interp = st
        c = count_ge(_key_to_float(k))
        ok = c >= kf
        live = done == 0
        new_lo = jnp.where(jnp.logical_and(live, ok), k, lo_k)
        new_hi = jnp.where(jnp.logical_and(live, jnp.logical_not(ok)), k, hi_k)
        new_c_lo = jnp.where(jnp.logical_and(live, ok), c, c_lo)
        new_c_hi = jnp.where(jnp.logical_and(live, jnp.logical_not(ok)), c, c_hi)
        width = lambda a, b: b.astype(F32) - a.astype(F32)
        slow = width(new_lo, new_hi) > 0.5 * width(lo_k, hi_k)
        force = jnp.logical_and(interp != 0, slow).astype(jnp.int32)
        finished = jnp.logical_or(new_c_lo == kf, new_hi - new_lo == 1)
        new_done = jnp.maximum(done, finished.astype(jnp.int32))
        new_k, new_interp = next_probe(new_lo, new_hi, new_c_lo, new_c_hi, force)
        return (step + 1, new_lo, new_hi, new_c_lo, new_c_hi, new_done, new_k,
                new_interp.astype(jnp.int32))

    k0, interp0 = next_probe(lo_k, hi_k, c_lo, c_hi, ikey(0))
    state = (jnp.int32(0), lo_k, hi_k, c_lo, c_hi, done.astype(jnp.int32), k0,
             interp0.astype(jnp.int32))
    _, lo_k, hi_k, c_lo, c_hi, _, _, _ = lax.while_loop(search_cond, search_step, state)
    tau = jnp.where(open_q, -jnp.inf, _key_to_float(lo_k))
    need = jnp.where(open_q, 0.0, jnp.where(c_lo == kf, NO_TIE_LIMIT, kf - c_hi))

    q = q_ref[0]
    for p in range(B_HEADS // 2):
        even, odd = _split_pair(q[:, p * LANES:(p + 1) * LANES], True)
        qs_scr[p * tq:(p + 1) * tq, :] = even
        qs_scr[(B_HEADS // 2 + p) * tq:(B_HEADS // 2 + p + 1) * tq, :] = odd
    m_scr[...] = jnp.full(m_scr.shape, NEG, F32)
    acc_scr[...] = jnp.zeros(acc_scr.shape, F32)
    key_i = lax.broadcasted_iota(jnp.int32, (tq, tq), 0)
    lower = (lax.broadcasted_iota(jnp.int32, (tq, tq), 1) <= key_i).astype(BF16)

    def select_bias(jj, tie_count):
        sblk = sc_scr[jj]
        tie = sblk == tau
        tie01 = jnp.where(tie, 1.0, 0.0).astype(BF16)
        rank_a = tie_count + jnp.dot(lower, tie01[0:tq], preferred_element_type=F32)
        rank_b = rank_a[tq - 1:tq, :] + jnp.dot(lower, tie01[tq:2 * tq], preferred_element_type=F32)
        rank = jnp.concatenate([rank_a, rank_b], axis=0)
        keep_tie = jnp.where(rank <= need, 0.0, NEG)
        return jnp.where(sblk > tau, 0.0, jnp.where(tie, keep_tie, NEG)), rank[2 * tq - 1:2 * tq, :]

    def p3_step(jj, pairs, tie_count):
        nkeys = pairs * 2 * tq
        start = pl.multiple_of(jj * 2 * tq, 2 * tq)
        biases = []
        for t in range(pairs):
            bias, tie_count = select_bias(jj + t, tie_count)
            biases.append(bias)
        bias = jnp.concatenate(biases, axis=0)
        s = _qk(k_ref[0, pl.ds(start, nkeys), :], qs_scr[...])
        s = jnp.concatenate([s[:, h * tq:(h + 1) * tq] + bias for h in range(B_HEADS)], axis=1)
        _flash_t_update(s, [((t * tq, (t + 1) * tq), vt_ref[0, 2 * jj + t]) for t in range(2 * pairs)],
                        m_scr, acc_scr)
        return tie_count

    tie_count = lax.fori_loop(0, npair // P3_PAIRS,
                              lambda g, tc: p3_step(g * P3_PAIRS, P3_PAIRS, tc),
                              jnp.zeros((1, tq), F32))
    for r in range(1, P3_PAIRS):
        @pl.when(npair % P3_PAIRS == r)
        def _():
            p3_step(npair - r, r, tie_count)

    acc = acc_scr[...]
    o = acc[0:HEAD_DIM] / acc[HEAD_DIM:HEAD_DIM + 1]
    half_cols = (B_HEADS // 2) * tq
    for p in range(B_HEADS // 2):
        pair = jnp.concatenate([o[:, p * tq:(p + 1) * tq],
                                o[:, half_cols + p * tq:half_cols + (p + 1) * tq]], axis=0)
        o_ref[0, :, p * LANES:(p + 1) * LANES] = pair.T.astype(BF16)


def _dsa_attention(rope_slab, plain_slab, tq):
    bsz, s, _ = rope_slab.shape
    topk = min(TOPK_MAX, s // 4)
    nblk = s // tq
    assert tq >= topk and s % tq == 0 and nblk % 2 == 0
    kern = functools.partial(_dsa_kernel, tq=tq, topk=topk)
    idx0 = R_IDX * COL_TILE
    misc0 = P_MISC * COL_TILE
    iq_w = IDX_HEADS * HEAD_DIM
    width = B_HEADS * HEAD_DIM
    vt = _with_ones_rows(
        plain_slab[:, :, misc0:misc0 + HEAD_DIM].reshape(bsz, nblk, tq, HEAD_DIM).transpose(0, 1, 3, 2))
    return pl.pallas_call(
        kern,
        grid=(bsz, s // tq),
        in_specs=[
            pl.BlockSpec((1, tq, COL_TILE), lambda b, i: (b, i, R_BQ)),
            pl.BlockSpec((1, tq, iq_w), lambda b, i: (b, i, idx0 // iq_w)),
            pl.BlockSpec((1, tq, LANES), lambda b, i: (b, i, (misc0 + LANES) // LANES)),
            pl.BlockSpec((1, s, LANES), lambda b, i: (b, 0, (idx0 + iq_w) // LANES)),
            pl.BlockSpec((1, s, LANES), lambda b, i: (b, 0, (idx0 + iq_w + LANES) // LANES)),
            pl.BlockSpec((1, nblk, VT_ROWS, tq), lambda b, i: (b, 0, 0, 0)),
        ],
        out_specs=pl.BlockSpec((1, tq, width), lambda b, i: (b, i, 0)),
        out_shape=jax.ShapeDtypeStruct((bsz, s, width), BF16),
        scratch_shapes=[
            pltpu.VMEM((nblk // 2, 2 * tq, tq), F32),
            pltpu.VMEM((B_HEADS * tq, LANES), BF16),
            pltpu.VMEM((1, B_HEADS * tq), F32),
            pltpu.VMEM((VT_ROWS, B_HEADS * tq), F32),
        ],
        compiler_params=_cparams(("arbitrary", "arbitrary")),
        name="dsa_attn",
    )(rope_slab, rope_slab, plain_slab, rope_slab, rope_slab, vt)


BAND_TQ = 2 * CHUNK
BAND_KEYS = (C_LEFT_CHUNKS + BAND_TQ // CHUNK) * CHUNK
BAND_PAD = C_LEFT_CHUNKS * CHUNK


def _band_kernel(q_ref, k_ref, v_ref, bias_ref, o_ref):
    i = pl.program_id(1)
    tq = BAND_TQ
    start = pl.multiple_of(i * tq, tq)
    col = lax.broadcasted_iota(jnp.int32, (2 * tq, BAND_KEYS), 1)
    key_ok = col + start >= BAND_PAD
    for p in range(C_HEADS // 2):
        lanes = slice(p * LANES, (p + 1) * LANES)
        k = k_ref[0, pl.ds(start, BAND_KEYS), lanes]
        v = v_ref[0, pl.ds(start, BAND_KEYS), lanes]
        qs = jnp.concatenate(_split_pair(q_ref[0, :, lanes], False), axis=0)
        s = _qk(qs, k) + bias_ref[2 * p:2 * p + 2].reshape(2 * tq, BAND_KEYS)
        s = jnp.where(key_ok, s, NEG)
        m = jnp.max(s, axis=-1, keepdims=True)
        e = jnp.exp2(s - m)
        l = jnp.sum(e, axis=-1, keepdims=True)
        eb = e.astype(BF16)
        v_even, v_odd = _split_pair(v, False)
        o = (jnp.dot(eb[:tq], v_even, preferred_element_type=F32) / l[:tq]
             + jnp.dot(eb[tq:], v_odd, preferred_element_type=F32) / l[tq:])
        o_ref[0, :, lanes] = o.astype(BF16)


def _band_bias(rel_bias):
    r = jnp.arange(BAND_TQ)[:, None]
    cidx = jnp.arange(BAND_KEYS)[None, :]
    n_diag = BAND_TQ + BAND_KEYS - 1
    rel = BAND_PAD - (BAND_KEYS - 1) + jnp.arange(n_diag)
    g = rel_bias.astype(F32)[:, jnp.clip(rel, -REL_CLIP, REL_CLIP) + REL_CLIP] * LOG2E
    skew = jnp.tile(g, (1, BAND_TQ + 2))[:, :BAND_TQ * (n_diag + 1)]
    bias = skew.reshape(-1, BAND_TQ, n_diag + 1)[:, :, :BAND_KEYS][:, :, ::-1]
    dchunk = (r // CHUNK + C_LEFT_CHUNKS) - cidx // CHUNK
    in_band = jnp.logical_and(dchunk >= 0, dchunk <= C_LEFT_CHUNKS)
    return jnp.where(in_band[None], bias, NEG)


def _band_attention(plain_slab, kp, vp, bias):
    bsz, s, _ = plain_slab.shape
    tq = BAND_TQ
    sp = kp.shape[1]
    width = C_HEADS * HEAD_DIM
    return pl.pallas_call(
        _band_kernel,
        grid=(bsz, s // tq),
        in_specs=[
            pl.BlockSpec((1, tq, width), lambda b, i: (b, i, P_CQ)),
            pl.BlockSpec((1, sp, width), lambda b, i: (b, 0, 0)),
            pl.BlockSpec((1, sp, width), lambda b, i: (b, 0, 0)),
            pl.BlockSpec((C_HEADS, tq, BAND_KEYS), lambda b, i: (0, 0, 0)),
        ],
        out_specs=pl.BlockSpec((1, tq, width), lambda b, i: (b, i, 0)),
        out_shape=jax.ShapeDtypeStruct((bsz, s, width), BF16),
        compiler_params=_cparams(("arbitrary", "arbitrary")),
        name="band_attn",
    )(plain_slab, kp, vp, bias)


def _merge_kernel(x_ref, g1_ref, ya_ref, yb_ref, yc_ref, ga_ref, gb_ref, gc_ref,
                  wa_ref, wb_ref, wc_ref, wo_ref, o_ref):
    for r0 in range(0, x_ref.shape[1], PROJ_ROWS):
        rows = slice(r0, r0 + PROJ_ROWS)

        def branch(y_ref, w_ref, gate_ref):
            return gate_ref[0, rows, :].astype(F32) * jnp.dot(y_ref[0, rows, :], w_ref[...],
                                                             preferred_element_type=F32)

        merged = (branch(ya_ref, wa_ref, ga_ref) + branch(yb_ref, wb_ref, gb_ref)
                  + branch(yc_ref, wc_ref, gc_ref))
        mixed = jnp.dot(merged.astype(BF16), wo_ref[...], preferred_element_type=F32)
        o_ref[0, rows, :] = x_ref[0, rows, :] + g1_ref[0] * mixed


def _merge(x, g1, ya, yb, yc, gate_slab, wa, wb, wc, wo, tm):
    bsz, s, d = x.shape
    tok = lambda b, i: (b, i, 0)
    full = lambda b, i: (0, 0)
    return pl.pallas_call(
        _merge_kernel,
        grid=(bsz, s // tm),
        in_specs=[
            pl.BlockSpec((1, tm, d), tok),
            pl.BlockSpec((1, 1, d), lambda b, i: (b, 0, 0)),
            pl.BlockSpec((1, tm, ya.shape[2]), tok),
            pl.BlockSpec((1, tm, yb.shape[2]), tok),
            pl.BlockSpec((1, tm, yc.shape[2]), tok),
            pl.BlockSpec((1, tm, d), lambda b, i: (b, i, 0)),
            pl.BlockSpec((1, tm, d), lambda b, i: (b, i, 1)),
            pl.BlockSpec((1, tm, d), lambda b, i: (b, i, 2)),
            pl.BlockSpec(wa.shape, full),
            pl.BlockSpec(wb.shape, full),
            pl.BlockSpec(wc.shape, full),
            pl.BlockSpec(wo.shape, full),
        ],
        out_specs=pl.BlockSpec((1, tm, d), tok),
        out_shape=jax.ShapeDtypeStruct((bsz, s, d), F32),
        compiler_params=_cparams(("arbitrary", "arbitrary")),
        name="merge",
    )(x, g1, ya, yb, yc, gate_slab, gate_slab, gate_slab, wa, wb, wc, wo)


def _router_kernel(x_ref, g_ref, sc_ref, sh_ref, rw_ref, rb_ref, u_ref, comb_ref):
    u = _modulated_norm(x_ref[0], g_ref[...], sc_ref[0], sh_ref[0])
    u_ref[0] = u.astype(BF16)
    logits = lax.dot_general(rw_ref[...], u, (((1,), (1,)), ((), ())),
                             preferred_element_type=F32, precision=lax.Precision.HIGHEST)
    aff = jax.nn.sigmoid(logits)
    sel = aff + rb_ref[...]
    rows = [sel[e:e + 1] for e in range(N_EXPERTS)]
    gscore = []
    for g in range(N_GROUPS):
        r = rows[g * EXPERTS_PER_GROUP:(g + 1) * EXPERTS_PER_GROUP]
        best = None
        for a in range(EXPERTS_PER_GROUP):
            for b in range(a + 1, EXPERTS_PER_GROUP):
                pair = r[a] + r[b]
                best = pair if best is None else jnp.maximum(best, pair)
        gscore.append(best)
    gmax = functools.reduce(jnp.maximum, gscore)
    taken = jnp.zeros_like(gmax) > 1.0
    in_best = []
    for g in range(N_GROUPS):
        is_g = jnp.logical_and(gscore[g] == gmax, jnp.logical_not(taken))
        in_best.append(is_g)
        taken = jnp.logical_or(taken, is_g)
    keep = []
    for e in range(N_EXPERTS):
        g = e // EXPERTS_PER_GROUP
        rank = jnp.zeros_like(gmax)
        for o in range(g * EXPERTS_PER_GROUP, (g + 1) * EXPERTS_PER_GROUP):
            if o == e:
                continue
            ahead = rows[o] > rows[e] if o > e else rows[o] >= rows[e]
            rank = rank + jnp.where(ahead, 1.0, 0.0)
        keep.append(jnp.logical_and(in_best[g], rank < 2.0))
    w = [jnp.where(keep[e], aff[e:e + 1], 0.0) for e in range(N_EXPERTS)]
    total = functools.reduce(lambda a, b: a + b, w)
    comb = jnp.concatenate([we / total for we in w]
                           + [jnp.zeros((LANES - N_EXPERTS, total.shape[1]), F32)], axis=0)
    comb_ref[0] = comb.T


def _router(x, g, sc, sh, router_w, router_b, tm):
    bsz, s, d = x.shape
    tok = lambda b, i: (b, i, 0)
    return pl.pallas_call(
        _router_kernel,
        grid=(bsz, s // tm),
        in_specs=[
            pl.BlockSpec((1, tm, d), tok),
            pl.BlockSpec((1, d), lambda b, i: (0, 0)),
            pl.BlockSpec((1, 1, d), lambda b, i: (b, 0, 0)),
            pl.BlockSpec((1, 1, d), lambda b, i: (b, 0, 0)),
            pl.BlockSpec((N_EXPERTS, d), lambda b, i: (0, 0)),
            pl.BlockSpec((N_EXPERTS, 1), lambda b, i: (0, 0)),
        ],
        out_specs=[pl.BlockSpec((1, tm, d), tok), pl.BlockSpec((1, tm, LANES), tok)],
        out_shape=[jax.ShapeDtypeStruct((bsz, s, d), BF16),
                   jax.ShapeDtypeStruct((bsz, s, LANES), F32)],
        compiler_params=_cparams(("arbitrary", "arbitrary")),
        name="router",
    )(x, g.reshape(1, d), sc, sh, router_w.T, router_b.reshape(N_EXPERTS, 1))


MOE_SUB = 512
MOE_CAP = 128


def _expert_ffn(rows, w1_ref, w3_ref, w2_ref):
    h1 = jnp.dot(rows, w1_ref[0], preferred_element_type=F32)
    h3 = jnp.dot(rows, w3_ref[0], preferred_element_type=F32)
    h = (h1 * jax.nn.sigmoid(h1)) * h3
    return jnp.dot(h.astype(BF16), w2_ref[0], preferred_element_type=F32)


def _moe_kernel(x_ref, g2_ref, fg_ref, u_ref, comb_ref, w1_ref, w3_ref, w2_ref, o_ref,
                acc_scr, pos_scr, pos_t_scr, comb_t_scr, scatter_scr, y_scr, *, final_norm):
    e = pl.program_id(2)
    tm = u_ref.shape[1]
    sub = min(MOE_SUB, tm)
    subs = [slice(r0, r0 + sub) for r0 in range(0, tm, sub)]

    @pl.when(e == 0)
    def _():
        acc_scr[...] = jnp.zeros(acc_scr.shape, F32)
        before = (lax.broadcasted_iota(jnp.int32, (sub, sub), 1)
                  < lax.broadcasted_iota(jnp.int32, (sub, sub), 0)).astype(BF16)
        for rows in subs:
            comb0 = comb_ref[0, rows, :]
            member = jnp.where(comb0 > 0.0, 1.0, 0.0).astype(BF16)
            pos = jnp.dot(before, member, preferred_element_type=F32)
            pos_scr[rows, :] = pos
            pos_t_scr[:, rows] = pos.T
            comb_t_scr[:, rows] = comb0.T

    comb = comb_ref[0]
    lane = lax.broadcasted_iota(jnp.int32, comb.shape, 1)
    ce = jnp.sum(jnp.where(lane == e, comb, 0.0), axis=-1, keepdims=True)
    pe = jnp.sum(jnp.where(lane == e, pos_scr[...], 0.0), axis=-1, keepdims=True)
    member_e = jnp.where(ce > 0.0, 1.0, 0.0)
    fullest = functools.reduce(jnp.maximum, [jnp.sum(member_e[rows]) for rows in subs])
    routed = fullest <= float(MOE_CAP)
    slots = pl.ds(pl.multiple_of(e * MOE_CAP, MOE_CAP), MOE_CAP)

    @pl.when(routed)
    def _():
        slot_l = lax.broadcasted_iota(jnp.int32, (1, MOE_CAP), 1).astype(F32)
        scatter_scr[e] = jnp.where(jnp.logical_and(ce > 0.0, pe == slot_l), 1.0, 0.0).astype(BF16)
        slot_s = lax.broadcasted_iota(jnp.int32, (MOE_CAP, 1), 0).astype(F32)
        packed, weights = [], []
        for rows in subs:
            ce_t = comb_t_scr[pl.ds(e, 1), rows]
            pe_t = pos_t_scr[pl.ds(e, 1), rows]
            hit = jnp.logical_and(ce_t > 0.0, pe_t == slot_s)
            weights.append(jnp.sum(jnp.where(hit, ce_t, 0.0), axis=-1, keepdims=True))
            packed.append(jnp.dot(jnp.where(hit, 1.0, 0.0).astype(BF16), u_ref[0, rows, :],
                                  preferred_element_type=F32).astype(BF16))
        y = jnp.concatenate(weights, axis=0) * _expert_ffn(jnp.concatenate(packed, axis=0),
                                                           w1_ref, w3_ref, w2_ref)
        for k in range(len(subs)):
            y_scr[k, slots, :] = y[k * MOE_CAP:(k + 1) * MOE_CAP].astype(BF16)

    @pl.when(jnp.logical_not(routed))
    def _():
        scatter_scr[e] = jnp.zeros(scatter_scr.shape[1:], BF16)
        for k in range(len(subs)):
            y_scr[k, slots, :] = jnp.zeros((MOE_CAP, y_scr.shape[2]), BF16)
        acc_scr[...] += ce * _expert_ffn(u_ref[0], w1_ref, w3_ref, w2_ref)

    @pl.when(e == pl.num_programs(2) - 1)
    def _():
        for k, rows in enumerate(subs):
            scatter = jnp.concatenate([scatter_scr[j, rows, :] for j in range(N_EXPERTS)], axis=1)
            y = acc_scr[rows, :] + jnp.dot(scatter, y_scr[k], preferred_element_type=F32)
            out = x_ref[0, rows, :] + g2_ref[0] * y
            if final_norm:
                out = (out * lax.rsqrt(jnp.mean(out * out, axis=-1, keepdims=True) + EPS)) * fg_ref[...]
            o_ref[0, rows, :] = out


def _moe(x, g2, final_g, u, comb, w1, w3, w2, tm, final_norm):
    bsz, s, d = x.shape
    ne, _, dff = w1.shape
    assert ne == N_EXPERTS
    tok = lambda b, i, e: (b, i, 0)
    return pl.pallas_call(
        functools.partial(_moe_kernel, final_norm=final_norm),
        grid=(bsz, s // tm, ne),
        in_specs=[
            pl.BlockSpec((1, tm, d), tok),
            pl.BlockSpec((1, 1, d), lambda b, i, e: (b, 0, 0)),
            pl.BlockSpec((1, d), lambda b, i, e: (0, 0)),
            pl.BlockSpec((1, tm, d), tok),
            pl.BlockSpec((1, tm, LANES), tok),
            pl.BlockSpec((1, d, dff), lambda b, i, e: (e, 0, 0)),
            pl.BlockSpec((1, d, dff), lambda b, i, e: (e, 0, 0)),
            pl.BlockSpec((1, dff, d), lambda b, i, e: (e, 0, 0)),
        ],
        out_specs=pl.BlockSpec((1, tm, d), tok),
        out_shape=jax.ShapeDtypeStruct((bsz, s, d), F32),
        scratch_shapes=[
            pltpu.VMEM((tm, d), F32),
            pltpu.VMEM((tm, LANES), F32),
            pltpu.VMEM((LANES, tm), F32),
            pltpu.VMEM((LANES, tm), F32),
            pltpu.VMEM((ne, tm, MOE_CAP), BF16),
            pltpu.VMEM((pl.cdiv(tm, MOE_SUB), ne * MOE_CAP, d), BF16),
        ],
        compiler_params=_cparams(("arbitrary", "arbitrary", "arbitrary")),
        name="moe",
    )(x, g2, final_g.reshape(1, d), u, comb, w1, w3, w2)


def _tile(s, want):
    t = min(want, s)
    assert s % t == 0
    return t


def _tiles(s):
    return dict(
        rows=_tile(s, 1024),
        diff_q=_tile(s, 1024),
        diff_k=_tile(s, 512),
        dsa_q=_tile(s, 256),
        merge=_tile(s, 512),
    )


def kernel(x, c, positions, norm1_g, norm2_g, w_mod, b_mod, w_in, lambda_q1, lambda_k1, lambda_q2, lambda_k2, a_norm_g, c_rel_bias, w_branch_a, w_branch_b, w_branch_c, w_out, router_w, router_b, exp_w1, exp_w3, exp_w2, final_g):
    bsz, s, d = x.shape
    depth = w_mod.shape[0]
    t = _tiles(s)
    tm = t["rows"]

    mod = _modulation(c, w_mod, b_mod)
    rope_tables = _rope_tables(positions)

    for layer in range(depth):
        lam_init = 0.8 - 0.6 * math.exp(-0.3 * layer)
        sh1, sc1, g1, sh2, sc2, g2 = [m[:, None, :] for m in jnp.split(mod[layer], 6, axis=-1)]
        lam = (jnp.exp(jnp.sum(lambda_q1[layer] * lambda_k1[layer]))
               - jnp.exp(jnp.sum(lambda_q2[layer] * lambda_k2[layer])) + lam_init)

        w_rope, w_gate, w_plain = _build_weights(w_in[layer])
        u = _norm(x, norm1_g[layer], sc1, sh1, tm)
        rope_slab = _project(u, w_rope, rope_tables, "rope", tm)
        gate_slab = _project(u, w_gate, (), "gate", tm)
        plain_slab = _project(u, w_plain, (), "plain", tm)

        ya = _diff_attention(rope_slab, plain_slab, lam, a_norm_g[layer], lam_init,
                             t["diff_q"], t["diff_k"])
        yb = _dsa_attention(rope_slab, plain_slab, t["dsa_q"])
        pad = ((0, 0), (BAND_PAD, 0), (0, 0))
        kp = jnp.pad(plain_slab[:, :, P_CK * COL_TILE:(P_CK + 1) * COL_TILE], pad)
        vp = jnp.pad(plain_slab[:, :, P_CV * COL_TILE:(P_CV + 1) * COL_TILE], pad)
        yc = _band_attention(plain_slab, kp, vp, _band_bias(c_rel_bias[layer]))
        x = _merge(x, g1, ya, yb, yc, gate_slab,
                   w_branch_a[layer].astype(BF16), w_branch_b[layer].astype(BF16),
                   w_branch_c[layer].astype(BF16), w_out[layer].astype(BF16), t["merge"])

        u, comb = _router(x, norm2_g[layer], sc2, sh2, router_w, router_b, t["merge"])
        x = _moe(x, g2, final_g, u, comb, exp_w1[layer].astype(BF16), exp_w3[layer].astype(BF16),
                 exp_w2[layer].astype(BF16), tm, final_norm=(layer == depth - 1))

    return x
```

```python
import functools
import math

import jax
import jax.numpy as jnp
from jax import lax
from jax.experimental import pallas as pl
from jax.experimental.pallas import tpu as pltpu

F32 = jnp.float32
BF16 = jnp.bfloat16

CHUNK = 64
ROPE_THETA = 10000.0
EPS = 1e-6
A_HEADS = 4
HEAD_DIM = 64
B_HEADS = 8
IDX_HEADS = 4
TOPK_MAX = 256
C_HEADS = 8
C_LEFT_CHUNKS = 8
REL_CLIP = 256
N_EXPERTS = 16
N_GROUPS = 4
EXPERTS_PER_GROUP = 4
N_BRANCHES = 3

LANES = 128
NEG = -1e30
LOG2E = math.log2(math.e)
VMEM_LIMIT = 56 * 1024 * 1024

COL_TILE = 512
PROJ_ROWS = 256
R_AQ, R_AK, R_BQ, R_IDX = 0, 1, 2, 3
P_AV, P_CQ, P_CK, P_CV, P_MISC = 0, 1, 2, 3, 4


def _cparams(sem):
    return pltpu.CompilerParams(dimension_semantics=sem, vmem_limit_bytes=VMEM_LIMIT)


def _mod_kernel(c_ref, w_ref, b_ref, o_ref):
    c = c_ref[...]
    ca = c * jax.nn.sigmoid(c)
    o_ref[0] = jnp.dot(ca, w_ref[0], preferred_element_type=F32) + b_ref[0]


def _modulation(c, w_mod, b_mod):
    depth, d, n6 = w_mod.shape
    bsz = c.shape[0]
    tn = 1024
    return pl.pallas_call(
        _mod_kernel,
        grid=(depth, n6 // tn),
        in_specs=[
            pl.BlockSpec((bsz, d), lambda l, j: (0, 0)),
            pl.BlockSpec((1, d, tn), lambda l, j: (l, 0, j)),
            pl.BlockSpec((1, 1, tn), lambda l, j: (l, 0, j)),
        ],
        out_specs=pl.BlockSpec((1, bsz, tn), lambda l, j: (l, 0, j)),
        out_shape=jax.ShapeDtypeStruct((depth, bsz, n6), F32),
        compiler_params=_cparams(("arbitrary", "arbitrary")),
        name="modulation",
    )(c, w_mod, b_mod.reshape(depth, 1, n6))


def _modulated_norm(x, g, sc, sh):
    y = x * lax.rsqrt(jnp.mean(x * x, axis=-1, keepdims=True) + EPS)
    return (y * g) * (1.0 + sc) + sh


def _norm_kernel(x_ref, g_ref, sc_ref, sh_ref, u_ref):
    u_ref[0] = _modulated_norm(x_ref[0], g_ref[...], sc_ref[0], sh_ref[0]).astype(BF16)


def _norm(x, g, sc, sh, tm):
    bsz, s, d = x.shape
    tok = lambda b, i: (b, i, 0)
    per_batch = lambda b, i: (b, 0, 0)
    return pl.pallas_call(
        _norm_kernel,
        grid=(bsz, s // tm),
        in_specs=[pl.BlockSpec((1, tm, d), tok), pl.BlockSpec((1, d), lambda b, i: (0, 0)),
                  pl.BlockSpec((1, 1, d), per_batch), pl.BlockSpec((1, 1, d), per_batch)],
        out_specs=pl.BlockSpec((1, tm, d), tok),
        out_shape=jax.ShapeDtypeStruct((bsz, s, d), BF16),
        compiler_params=_cparams(("arbitrary", "arbitrary")),
        name="norm",
    )(x, g.reshape(1, d), sc, sh)


def _proj_kernel(*refs, mode):
    if mode == "rope":
        u_ref, cos_ref, sin_ref, w_ref, o_ref = refs
    else:
        u_ref, w_ref, o_ref = refs
    tm = u_ref.shape[1]
    for r0 in range(0, tm, PROJ_ROWS):
        rows = slice(r0, r0 + PROJ_ROWS)
        acc = jnp.dot(u_ref[0, rows, :], w_ref[...], preferred_element_type=F32)
        if mode == "rope":
            cos, sin = cos_ref[0, rows, :], sin_ref[0, rows, :]
            groups = [acc[:, c:c + LANES] for c in range(0, COL_TILE, LANES)]
            acc = jnp.concatenate(
                [g * cos + pltpu.roll(g, LANES // 2, axis=1) * sin for g in groups], axis=1)
        elif mode == "gate":
            acc = 1.0 / (1.0 + jnp.exp(-acc))
        o_ref[0, rows, :] = acc.astype(BF16)


def _project(u, w, rope_tables, mode, tm):
    bsz, s, d = u.shape
    ncols = w.shape[1]
    tok = lambda b, i, j: (b, i, 0)
    in_specs = [pl.BlockSpec((1, tm, d), tok)]
    in_specs += [pl.BlockSpec((1, tm, LANES), tok) for _ in rope_tables]
    in_specs += [pl.BlockSpec((d, COL_TILE), lambda b, i, j: (0, j))]
    return pl.pallas_call(
        functools.partial(_proj_kernel, mode=mode),
        grid=(bsz, s // tm, ncols // COL_TILE),
        in_specs=in_specs,
        out_specs=pl.BlockSpec((1, tm, COL_TILE), lambda b, i, j: (b, i, j)),
        out_shape=jax.ShapeDtypeStruct((bsz, s, ncols), BF16),
        compiler_params=_cparams(("arbitrary", "arbitrary", "arbitrary")),
        name="proj_" + mode,
    )(u, *rope_tables, w)


def _pair_layout(w):
    half = HEAD_DIM // 2
    col = jnp.arange(w.shape[1])
    base, r = (col // LANES) * LANES, col % LANES
    src = base + ((r // half) % 2) * HEAD_DIM + (r // HEAD_DIM) * half + r % half
    return w[:, src]


def _build_weights(w_in):
    sizes = (512, 512, 512, 512, 64, 64, 256, 64, 4, 512, 512, 512, 3072)
    parts, start = [], 0
    for sz in sizes:
        parts.append(w_in[:, start:start + sz])
        start += sz
    aq, ak, av, bq, bk, bv, iq, ik, iw, cq, ck, cv, gates = parts
    d = w_in.shape[0]
    qscale = HEAD_DIM ** -0.5 * LOG2E
    iw_scale = IDX_HEADS ** -0.5 * HEAD_DIM ** -0.5
    zeros = lambda n: jnp.zeros((d, n), w_in.dtype)
    w_rope = _pair_layout(jnp.concatenate([aq * qscale, ak, bq * qscale, iq, bk, bk, ik, ik], axis=1))
    w_plain = jnp.concatenate([av, cq * qscale, ck, cv,
                               bv, bv, iw * iw_scale, zeros(LANES - IDX_HEADS),
                               zeros(COL_TILE - 2 * LANES)], axis=1)
    return w_rope.astype(BF16), gates.astype(BF16), w_plain.astype(BF16)


def _rope_tables(positions):
    half = HEAD_DIM // 2
    inv = ROPE_THETA ** (-jnp.arange(half, dtype=F32) / half)
    ang = positions.astype(F32)[..., None] * inv
    cos, sin = jnp.cos(ang), jnp.sin(ang)
    cos_t = jnp.tile(cos, (1, 1, LANES // half))
    sin_t = jnp.concatenate([-sin, -sin, sin, sin], axis=-1)
    return cos_t, sin_t


def _split_pair(pair, interleaved):
    lane = lax.broadcasted_iota(jnp.int32, pair.shape, 1)
    first = ((lane // (HEAD_DIM // 2)) % 2 == 0) if interleaved else (lane < HEAD_DIM)
    zero = jnp.zeros_like(pair)
    return jnp.where(first, pair, zero), jnp.where(first, zero, pair)


def _qk(q, k):
    return lax.dot_general(q, k, (((1,), (1,)), ((), ())), preferred_element_type=F32)


FOLD_CHAINS = 4


def _fold_rows(x, op):
    parts = x.reshape(x.shape[0] // 8, 8, x.shape[1])
    chains = [functools.reduce(op, [parts[g] for g in range(c, parts.shape[0], FOLD_CHAINS)])
              for c in range(min(FOLD_CHAINS, parts.shape[0]))]
    return functools.reduce(op, chains)


def _flash_t_update(s, vt_blocks, m_scr, acc_scr):
    m_old = m_scr[...]
    m_new = jnp.maximum(m_old, jnp.max(_fold_rows(s, jnp.maximum), axis=0, keepdims=True))
    alpha = jnp.exp2(m_old - m_new)
    pb = jnp.exp2(s - m_new).astype(BF16)
    m_scr[...] = m_new
    pv = functools.reduce(lambda a, b: a + b, [
        jnp.dot(vt, pb[k0:k1], preferred_element_type=F32) for (k0, k1), vt in vt_blocks])
    acc_scr[...] = alpha * acc_scr[...] + pv


ONES_ROWS = 16


def _with_ones_rows(vt):
    shape = vt.shape[:-2]
    return jnp.concatenate([vt, jnp.ones(shape + (1, vt.shape[-1]), vt.dtype),
                            jnp.zeros(shape + (ONES_ROWS - 1, vt.shape[-1]), vt.dtype)], axis=-2)


def _flash_update(s, v, m_scr, l_scr, acc_scr, rows=slice(None)):
    groups = [s[:, c:c + LANES] for c in range(0, s.shape[1], LANES)]
    m_old = m_scr[rows, :]
    lane_max = functools.reduce(jnp.maximum, groups)
    m_new = jnp.maximum(m_old, jnp.max(lane_max, axis=-1, keepdims=True))
    alpha = jnp.exp2(m_old - m_new)
    p_groups = [jnp.exp2(g - m_new) for g in groups]
    l_scr[rows, :] = alpha * l_scr[rows, :] + functools.reduce(lambda a, b: a + b, p_groups)
    m_scr[rows, :] = m_new
    pb = jnp.concatenate([g.astype(BF16) for g in p_groups], axis=1)
    acc_scr[rows, :] = alpha * acc_scr[rows, :] + jnp.dot(pb, v, preferred_element_type=F32)


def _diff_attn_kernel(lam_ref, q_ref, k_ref, v_ref, ng_ref, o_ref, m_scr, l_scr, acc_scr,
                      *, tq, tk, lam_init):
    i = pl.program_id(2)
    per_tile = tq // tk
    qs = jnp.concatenate(_split_pair(q_ref[0], True), axis=0)
    m_scr[...] = jnp.full(m_scr.shape, NEG, F32)
    l_scr[...] = jnp.zeros(l_scr.shape, F32)
    acc_scr[...] = jnp.zeros(acc_scr.shape, F32)

    def kv_block(j):
        start = pl.multiple_of(j * tk, tk)
        return k_ref[0, pl.ds(start, tk), :], v_ref[0, pl.ds(start, tk), :]

    def body(j, carry):
        k, v = kv_block(j)
        _flash_update(_qk(qs, k), v, m_scr, l_scr, acc_scr)
        return carry

    lax.fori_loop(0, i * per_tile, body, 0)
    for d in range(per_tile):
        k, v = kv_block(i * per_tile + d)
        for base in (0, tq):
            rows = slice(base + d * tk, base + tq)
            s = _qk(qs[rows], k)
            row = lax.broadcasted_iota(jnp.int32, s.shape, 0) + d * tk
            col = lax.broadcasted_iota(jnp.int32, s.shape, 1) + d * tk
            s = jnp.where((col // CHUNK) <= (row // CHUNK), s, NEG)
            _flash_update(s, v, m_scr, l_scr, acc_scr, rows)

    o = acc_scr[...] / jnp.sum(l_scr[...], axis=-1, keepdims=True)
    o = o[:tq] - lam_ref[0] * o[tq:]
    o = o * lax.rsqrt(jnp.mean(o * o, axis=-1, keepdims=True) + EPS)
    o_ref[0] = ((o * ng_ref[...]) * (1.0 - lam_init)).astype(BF16)


def _diff_attention(rope_slab, plain_slab, lam, norm_g, lam_init, tq, tk):
    bsz, s, _ = rope_slab.shape
    assert tq % tk == 0
    kern = functools.partial(_diff_attn_kernel, tq=tq, tk=tk, lam_init=lam_init)
    cb = COL_TILE // LANES
    return pl.pallas_call(
        kern,
        grid=(bsz, A_HEADS, s // tq),
        in_specs=[
            pl.BlockSpec(memory_space=pltpu.SMEM),
            pl.BlockSpec((1, tq, LANES), lambda b, h, i: (b, i, R_AQ * cb + h)),
            pl.BlockSpec((1, s, LANES), lambda b, h, i: (b, 0, R_AK * cb + h)),
            pl.BlockSpec((1, s, LANES), lambda b, h, i: (b, 0, P_AV * cb + h)),
            pl.BlockSpec((1, LANES), lambda b, h, i: (0, 0)),
        ],
        out_specs=pl.BlockSpec((1, tq, LANES), lambda b, h, i: (b, i, h)),
        out_shape=jax.ShapeDtypeStruct((bsz, s, A_HEADS * LANES), BF16),
        scratch_shapes=[pltpu.VMEM((2 * tq, LANES), F32), pltpu.VMEM((2 * tq, LANES), F32),
                        pltpu.VMEM((2 * tq, LANES), F32)],
        compiler_params=_cparams(("arbitrary", "arbitrary", "arbitrary")),
        name="diff_attn",
    )(lam.reshape(1), rope_slab, rope_slab, plain_slab, norm_g.reshape(1, LANES))


KEY_NEG_INF = -2139095040
KEY_POS_INF = 2139095040
SEARCH_MAX_STEPS = 80
NO_TIE_LIMIT = 1e9
P3_PAIRS = 2
SCAN_PAIRS = 4
VT_ROWS = HEAD_DIM + ONES_ROWS


def _key_to_float(key):
    bits = jnp.where(key >= 0, key, (key - 1) ^ jnp.int32(0x7FFFFFFF))
    return pltpu.bitcast(bits, F32)


def _float_to_key(t):
    bits = pltpu.bitcast(t, jnp.int32)
    return jnp.where(bits >= 0, bits, (bits ^ jnp.int32(0x7FFFFFFF)) + 1)


def _dsa_kernel(q_ref, iq_ref, iw_ref, k_ref, ik_ref, vt_ref, o_ref,
                sc_scr, qs_scr, m_scr, acc_scr, *, tq, topk):
    i = pl.program_id(1)
    nblk = i + 1
    npair = (nblk + 1) // 2
    nq = B_HEADS * tq

    iq = iq_ref[0]
    parts = []
    for p in range(IDX_HEADS // 2):
        parts += list(_split_pair(iq[:, p * LANES:(p + 1) * LANES], True))
    iqs = jnp.concatenate(parts, axis=0)
    iw_t = iw_ref[0].astype(F32).T

    def index_block(j):
        start = pl.multiple_of(j * tq, tq)
        ik = ik_ref[0, pl.ds(start, tq), :]
        score = None
        for h in range(IDX_HEADS):
            hs = jnp.maximum(_qk(ik, iqs[h * tq:(h + 1) * tq]), 0.0)
            score = iw_t[h:h + 1] * hs if score is None else score + iw_t[h:h + 1] * hs
        return score

    def p1_pair(jj, carry):
        sc_scr[jj, 0:tq, :] = index_block(2 * jj)
        sc_scr[jj, tq:2 * tq, :] = index_block(2 * jj + 1)
        return carry

    def p1_two_pairs(g, carry):
        return p1_pair(2 * g + 1, p1_pair(2 * g, carry))

    full_pairs = i // 2
    lax.fori_loop(0, full_pairs // 2, p1_two_pairs, 0)
    lax.fori_loop((full_pairs // 2) * 2, full_pairs, p1_pair, 0)
    score = index_block(i)
    key_i = lax.broadcasted_iota(jnp.int32, score.shape, 0)
    qry_i = lax.broadcasted_iota(jnp.int32, score.shape, 1)
    diagonal = jnp.where((key_i // CHUNK) <= (qry_i // CHUNK), score, -jnp.inf)

    @pl.when(i % 2 == 1)
    def _():
        sc_scr[npair - 1, 0:tq, :] = index_block(i - 1)
        sc_scr[npair - 1, tq:2 * tq, :] = diagonal

    @pl.when(i % 2 == 0)
    def _():
        sc_scr[npair - 1, 0:tq, :] = diagonal
        sc_scr[npair - 1, tq:2 * tq, :] = jnp.full((tq, tq), -jnp.inf, F32)

    kf = float(topk)

    def scan(hit_fn, ext_fn, ext_op, ext_init):
        def one_pair(jj, carry):
            cnt, ext = carry
            sblk = sc_scr[jj]
            cnt = cnt + _fold_rows(jnp.where(hit_fn(sblk), 1.0, 0.0), lambda a, b: a + b)
            if ext_fn is not None:
                ext = ext_op(ext, _fold_rows(ext_fn(sblk), ext_op))
            return cnt, ext

        def several_pairs(g, carry):
            for t in range(SCAN_PAIRS):
                carry = one_pair(SCAN_PAIRS * g + t, carry)
            return carry

        carry = lax.fori_loop(0, npair // SCAN_PAIRS, several_pairs,
                              (jnp.zeros((8, tq), F32), jnp.full((8, tq), ext_init, F32)))
        cnt, ext = lax.fori_loop((npair // SCAN_PAIRS) * SCAN_PAIRS, npair, one_pair, carry)
        return jnp.sum(cnt, axis=0, keepdims=True), ext

    def count_ge(t):
        return scan(lambda sblk: sblk >= t, None, None, 0.0)[0]

    c_ge0, ext = scan(lambda sblk: sblk >= 0.0, lambda sblk: sblk, jnp.maximum, -jnp.inf)
    col_max = jnp.max(ext, axis=0, keepdims=True)
    c_gt0, ext = scan(lambda sblk: sblk > 0.0,
                      lambda sblk: jnp.where(sblk > -jnp.inf, sblk, jnp.inf), jnp.minimum, jnp.inf)
    col_min = jnp.min(ext, axis=0, keepdims=True)
    qpos = i * tq + lax.broadcasted_iota(jnp.int32, (1, tq), 1)
    n_valid = (((qpos // CHUNK) + 1) * CHUNK).astype(F32)
    open_q = n_valid < kf
    above = c_gt0 >= kf
    below = c_ge0 < kf
    ikey = lambda v: jnp.full((1, tq), v, jnp.int32)
    lo_k = jnp.where(below, _float_to_key(col_min), ikey(0))
    hi_k = jnp.where(above, _float_to_key(col_max) + 1, jnp.where(below, ikey(0), ikey(1)))
    c_lo = jnp.where(below, n_valid, c_ge0)
    c_hi = jnp.where(above, 0.0, jnp.where(below, c_ge0, c_gt0))
    done = jnp.logical_or(open_q, jnp.logical_not(jnp.logical_or(above, below)))

    def next_probe(lo_k, hi_k, c_lo, c_hi, force_bisect):
        finite = jnp.logical_and(lo_k > KEY_NEG_INF, hi_k < KEY_POS_INF)
        t_lo, t_hi = _key_to_float(lo_k), _key_to_float(hi_k)
        log_lo = jnp.log(c_lo)
        frac = (log_lo - math.log(kf - 0.5)) / (log_lo - jnp.log(jnp.maximum(c_hi, 0.5)))
        k_interp = _float_to_key(t_lo + (t_hi - t_lo) * frac)
        k_mid = (lo_k & hi_k) + ((lo_k ^ hi_k) >> 1)
        interp = jnp.logical_and(finite, force_bisect == 0)
        k = jnp.where(interp, k_interp, k_mid)
        return jnp.minimum(jnp.maximum(k, lo_k + 1), hi_k - 1), interp

    def search_cond(st):
        return jnp.logical_and(st[0] < SEARCH_MAX_STEPS, jnp.min(st[5]) == 0)

    def search_step(st):
        step, lo_k, hi_k, c_lo, c_hi, done, k, interp = st
        c = count_ge(_key_to_float(k))
        ok = c >= kf
        live = done == 0
        new_lo = jnp.where(jnp.logical_and(live, ok), k, lo_k)
        new_hi = jnp.where(jnp.logical_and(live, jnp.logical_not(ok)), k, hi_k)
        new_c_lo = jnp.where(jnp.logical_and(live, ok), c, c_lo)
        new_c_hi = jnp.where(jnp.logical_and(live, jnp.logical_not(ok)), c, c_hi)
        width = lambda a, b: b.astype(F32) - a.astype(F32)
        slow = width(new_lo, new_hi) > 0.5 * width(lo_k, hi_k)
        force = jnp.logical_and(interp != 0, slow).astype(jnp.int32)
        finished = jnp.logical_or(new_c_lo == kf, new_hi - new_lo == 1)
        new_done = jnp.maximum(done, finished.astype(jnp.int32))
        new_k, new_interp = next_probe(new_lo, new_hi, new_c_lo, new_c_hi, force)
        return (step + 1, new_lo, new_hi, new_c_lo, new_c_hi, new_done, new_k,
                new_interp.astype(jnp.int32))

    k0, interp0 = next_probe(lo_k, hi_k, c_lo, c_hi, ikey(0))
    state = (jnp.int32(0), lo_k, hi_k, c_lo, c_hi, done.astype(jnp.int32), k0,
             interp0.astype(jnp.int32))
    _, lo_k, hi_k, c_lo, c_hi, _, _, _ = lax.while_loop(search_cond, search_step, state)
    tau = jnp.where(open_q, -jnp.inf, _key_to_float(lo_k))
    need = jnp.where(open_q, 0.0, jnp.where(c_lo == kf, NO_TIE_LIMIT, kf - c_hi))

    q = q_ref[0]
    for p in range(B_HEADS // 2):
        even, odd = _split_pair(q[:, p * LANES:(p + 1) * LANES], True)
        qs_scr[p * tq:(p + 1) * tq, :] = even
        qs_scr[(B_HEADS // 2 + p) * tq:(B_HEADS // 2 + p + 1) * tq, :] = odd
    m_scr[...] = jnp.full(m_scr.shape, NEG, F32)
    acc_scr[...] = jnp.zeros(acc_scr.shape, F32)
    key_i = lax.broadcasted_iota(jnp.int32, (tq, tq), 0)
    lower = (lax.broadcasted_iota(jnp.int32, (tq, tq), 1) <= key_i).astype(BF16)

    def select_bias(jj, tie_count):
        sblk = sc_scr[jj]
        tie = sblk == tau
        tie01 = jnp.where(tie, 1.0, 0.0).astype(BF16)
        rank_a = tie_count + jnp.dot(lower, tie01[0:tq], preferred_element_type=F32)
        rank_b = rank_a[tq - 1:tq, :] + jnp.dot(lower, tie01[tq:2 * tq], preferred_element_type=F32)
        rank = jnp.concatenate([rank_a, rank_b], axis=0)
        keep_tie = jnp.where(rank <= need, 0.0, NEG)
        return jnp.where(sblk > tau, 0.0, jnp.where(tie, keep_tie, NEG)), rank[2 * tq - 1:2 * tq, :]

    def p3_step(jj, pairs, tie_count):
        nkeys = pairs * 2 * tq
        start = pl.multiple_of(jj * 2 * tq, 2 * tq)
        biases = []
        for t in range(pairs):
            bias, tie_count = select_bias(jj + t, tie_count)
            biases.append(bias)
        bias = jnp.concatenate(biases, axis=0)
        s = _qk(k_ref[0, pl.ds(start, nkeys), :], qs_scr[...])
        s = jnp.concatenate([s[:, h * tq:(h + 1) * tq] + bias for h in range(B_HEADS)], axis=1)
        _flash_t_update(s, [((t * tq, (t + 1) * tq), vt_ref[0, 2 * jj + t]) for t in range(2 * pairs)],
                        m_scr, acc_scr)
        return tie_count

    tie_count = lax.fori_loop(0, npair // P3_PAIRS,
                              lambda g, tc: p3_step(g * P3_PAIRS, P3_PAIRS, tc),
                              jnp.zeros((1, tq), F32))
    for r in range(1, P3_PAIRS):
        @pl.when(npair % P3_PAIRS == r)
        def _():
            p3_step(npair - r, r, tie_count)

    acc = acc_scr[...]
    o = acc[0:HEAD_DIM] / acc[HEAD_DIM:HEAD_DIM + 1]
    half_cols = (B_HEADS // 2) * tq
    for p in range(B_HEADS // 2):
        pair = jnp.concatenate([o[:, p * tq:(p + 1) * tq],
                                o[:, half_cols + p * tq:half_cols + (p + 1) * tq]], axis=0)
        o_ref[0, :, p * LANES:(p + 1) * LANES] = pair.T.astype(BF16)


def _dsa_attention(rope_slab, plain_slab, tq):
    bsz, s, _ = rope_slab.shape
    topk = min(TOPK_MAX, s // 4)
    nblk = s // tq
    assert tq >= topk and s % tq == 0 and nblk % 2 == 0
    kern = functools.partial(_dsa_kernel, tq=tq, topk=topk)
    idx0 = R_IDX * COL_TILE
    misc0 = P_MISC * COL_TILE
    iq_w = IDX_HEADS * HEAD_DIM
    width = B_HEADS * HEAD_DIM
    vt = _with_ones_rows(
        plain_slab[:, :, misc0:misc0 + HEAD_DIM].reshape(bsz, nblk, tq, HEAD_DIM).transpose(0, 1, 3, 2))
    return pl.pallas_call(
        kern,
        grid=(bsz, s // tq),
        in_specs=[
            pl.BlockSpec((1, tq, COL_TILE), lambda b, i: (b, i, R_BQ)),
            pl.BlockSpec((1, tq, iq_w), lambda b, i: (b, i, idx0 // iq_w)),
            pl.BlockSpec((1, tq, LANES), lambda b, i: (b, i, (misc0 + LANES) // LANES)),
            pl.BlockSpec((1, s, LANES), lambda b, i: (b, 0, (idx0 + iq_w) // LANES)),
            pl.BlockSpec((1, s, LANES), lambda b, i: (b, 0, (idx0 + iq_w + LANES) // LANES)),
            pl.BlockSpec((1, nblk, VT_ROWS, tq), lambda b, i: (b, 0, 0, 0)),
        ],
        out_specs=pl.BlockSpec((1, tq, width), lambda b, i: (b, i, 0)),
        out_shape=jax.ShapeDtypeStruct((bsz, s, width), BF16),
        scratch_shapes=[
            pltpu.VMEM((nblk // 2, 2 * tq, tq), F32),
            pltpu.VMEM((B_HEADS * tq, LANES), BF16),
            pltpu.VMEM((1, B_HEADS * tq), F32),
            pltpu.VMEM((VT_ROWS, B_HEADS * tq), F32),
        ],
        compiler_params=_cparams(("arbitrary", "arbitrary")),
        name="dsa_attn",
    )(rope_slab, rope_slab, plain_slab, rope_slab, rope_slab, vt)


BAND_TQ = 2 * CHUNK
BAND_KEYS = (C_LEFT_CHUNKS + BAND_TQ // CHUNK) * CHUNK
BAND_PAD = C_LEFT_CHUNKS * CHUNK


def _band_kernel(q_ref, k_ref, v_ref, bias_ref, o_ref):
    i = pl.program_id(1)
    tq = BAND_TQ
    start = pl.multiple_of(i * tq, tq)
    col = lax.broadcasted_iota(jnp.int32, (2 * tq, BAND_KEYS), 1)
    key_ok = col + start >= BAND_PAD
    for p in range(C_HEADS // 2):
        lanes = slice(p * LANES, (p + 1) * LANES)
        k = k_ref[0, pl.ds(start, BAND_KEYS), lanes]
        v = v_ref[0, pl.ds(start, BAND_KEYS), lanes]
        qs = jnp.concatenate(_split_pair(q_ref[0, :, lanes], False), axis=0)
        s = _qk(qs, k) + bias_ref[2 * p:2 * p + 2].reshape(2 * tq, BAND_KEYS)
        s = jnp.where(key_ok, s, NEG)
        m = jnp.max(s, axis=-1, keepdims=True)
        e = jnp.exp2(s - m)
        l = jnp.sum(e, axis=-1, keepdims=True)
        eb = e.astype(BF16)
        v_even, v_odd = _split_pair(v, False)
        o = (jnp.dot(eb[:tq], v_even, preferred_element_type=F32) / l[:tq]
             + jnp.dot(eb[tq:], v_odd, preferred_element_type=F32) / l[tq:])
        o_ref[0, :, lanes] = o.astype(BF16)


def _band_bias(rel_bias):
    r = jnp.arange(BAND_TQ)[:, None]
    cidx = jnp.arange(BAND_KEYS)[None, :]
    n_diag = BAND_TQ + BAND_KEYS - 1
    rel = BAND_PAD - (BAND_KEYS - 1) + jnp.arange(n_diag)
    g = rel_bias.astype(F32)[:, jnp.clip(rel, -REL_CLIP, REL_CLIP) + REL_CLIP] * LOG2E
    skew = jnp.tile(g, (1, BAND_TQ + 2))[:, :BAND_TQ * (n_diag + 1)]
    bias = skew.reshape(-1, BAND_TQ, n_diag + 1)[:, :, :BAND_KEYS][:, :, ::-1]
    dchunk = (r // CHUNK + C_LEFT_CHUNKS) - cidx // CHUNK
    in_band = jnp.logical_and(dchunk >= 0, dchunk <= C_LEFT_CHUNKS)
    return jnp.where(in_band[None], bias, NEG)


def _band_attention(plain_slab, kp, vp, bias):
    bsz, s, _ = plain_slab.shape
    tq = BAND_TQ
    sp = kp.shape[1]
    width = C_HEADS * HEAD_DIM
    return pl.pallas_call(
        _band_kernel,
        grid=(bsz, s // tq),
        in_specs=[
            pl.BlockSpec((1, tq, width), lambda b, i: (b, i, P_CQ)),
            pl.BlockSpec((1, sp, width), lambda b, i: (b, 0, 0)),
            pl.BlockSpec((1, sp, width), lambda b, i: (b, 0, 0)),
            pl.BlockSpec((C_HEADS, tq, BAND_KEYS), lambda b, i: (0, 0, 0)),
        ],
        out_specs=pl.BlockSpec((1, tq, width), lambda b, i: (b, i, 0)),
        out_shape=jax.ShapeDtypeStruct((bsz, s, width), BF16),
        compiler_params=_cparams(("arbitrary", "arbitrary")),
        name="band_attn",
    )(plain_slab, kp, vp, bias)


def _merge_kernel(x_ref, g1_ref, ya_ref, yb_ref, yc_ref, ga_ref, gb_ref, gc_ref,
                  wa_ref, wb_ref, wc_ref, wo_ref, o_ref):
    for r0 in range(0, x_ref.shape[1], PROJ_ROWS):
        rows = slice(r0, r0 + PROJ_ROWS)

        def branch(y_ref, w_ref, gate_ref):
            return gate_ref[0, rows, :].astype(F32) * jnp.dot(y_ref[0, rows, :], w_ref[...],
                                                             preferred_element_type=F32)

        merged = (branch(ya_ref, wa_ref, ga_ref) + branch(yb_ref, wb_ref, gb_ref)
                  + branch(yc_ref, wc_ref, gc_ref))
        mixed = jnp.dot(merged.astype(BF16), wo_ref[...], preferred_element_type=F32)
        o_ref[0, rows, :] = x_ref[0, rows, :] + g1_ref[0] * mixed


def _merge(x, g1, ya, yb, yc, gate_slab, wa, wb, wc, wo, tm):
    bsz, s, d = x.shape
    tok = lambda b, i: (b, i, 0)
    full = lambda b, i: (0, 0)
    return pl.pallas_call(
        _merge_kernel,
        grid=(bsz, s // tm),
        in_specs=[
            pl.BlockSpec((1, tm, d), tok),
            pl.BlockSpec((1, 1, d), lambda b, i: (b, 0, 0)),
            pl.BlockSpec((1, tm, ya.shape[2]), tok),
            pl.BlockSpec((1, tm, yb.shape[2]), tok),
            pl.BlockSpec((1, tm, yc.shape[2]), tok),
            pl.BlockSpec((1, tm, d), lambda b, i: (b, i, 0)),
            pl.BlockSpec((1, tm, d), lambda b, i: (b, i, 1)),
            pl.BlockSpec((1, tm, d), lambda b, i: (b, i, 2)),
            pl.BlockSpec(wa.shape, full),
            pl.BlockSpec(wb.shape, full),
            pl.BlockSpec(wc.shape, full),
            pl.BlockSpec(wo.shape, full),
        ],
        out_specs=pl.BlockSpec((1, tm, d), tok),
        out_shape=jax.ShapeDtypeStruct((bsz, s, d), F32),
        compiler_params=_cparams(("arbitrary", "arbitrary")),
        name="merge",
    )(x, g1, ya, yb, yc, gate_slab, gate_slab, gate_slab, wa, wb, wc, wo)


def _router_kernel(x_ref, g_ref, sc_ref, sh_ref, rw_ref, rb_ref, u_ref, comb_ref):
    u = _modulated_norm(x_ref[0], g_ref[...], sc_ref[0], sh_ref[0])
    u_ref[0] = u.astype(BF16)
    logits = lax.dot_general(rw_ref[...], u, (((1,), (1,)), ((), ())),
                             preferred_element_type=F32, precision=lax.Precision.HIGHEST)
    aff = jax.nn.sigmoid(logits)
    sel = aff + rb_ref[...]
    rows = [sel[e:e + 1] for e in range(N_EXPERTS)]
    gscore = []
    for g in range(N_GROUPS):
        r = rows[g * EXPERTS_PER_GROUP:(g + 1) * EXPERTS_PER_GROUP]
        best = None
        for a in range(EXPERTS_PER_GROUP):
            for b in range(a + 1, EXPERTS_PER_GROUP):
                pair = r[a] + r[b]
                best = pair if best is None else jnp.maximum(best, pair)
        gscore.append(best)
    gmax = functools.reduce(jnp.maximum, gscore)
    taken = jnp.zeros_like(gmax) > 1.0
    in_best = []
    for g in range(N_GROUPS):
        is_g = jnp.logical_and(gscore[g] == gmax, jnp.logical_not(taken))
        in_best.append(is_g)
        taken = jnp.logical_or(taken, is_g)
    keep = []
    for e in range(N_EXPERTS):
        g = e // EXPERTS_PER_GROUP
        rank = jnp.zeros_like(gmax)
        for o in range(g * EXPERTS_PER_GROUP, (g + 1) * EXPERTS_PER_GROUP):
            if o == e:
                continue
            ahead = rows[o] > rows[e] if o > e else rows[o] >= rows[e]
            rank = rank + jnp.where(ahead, 1.0, 0.0)
        keep.append(jnp.logical_and(in_best[g], rank < 2.0))
    w = [jnp.where(keep[e], aff[e:e + 1], 0.0) for e in range(N_EXPERTS)]
    total = functools.reduce(lambda a, b: a + b, w)
    comb = jnp.concatenate([we / total for we in w]
                           + [jnp.zeros((LANES - N_EXPERTS, total.shape[1]), F32)], axis=0)
    comb_ref[0] = comb.T


def _router(x, g, sc, sh, router_w, router_b, tm):
    bsz, s, d = x.shape
    tok = lambda b, i: (b, i, 0)
    return pl.pallas_call(
        _router_kernel,
        grid=(bsz, s // tm),
        in_specs=[
            pl.BlockSpec((1, tm, d), tok),
            pl.BlockSpec((1, d), lambda b, i: (0, 0)),
            pl.BlockSpec((1, 1, d), lambda b, i: (b, 0, 0)),
            pl.BlockSpec((1, 1, d), lambda b, i: (b, 0, 0)),
            pl.BlockSpec((N_EXPERTS, d), lambda b, i: (0, 0)),
            pl.BlockSpec((N_EXPERTS, 1), lambda b, i: (0, 0)),
        ],
        out_specs=[pl.BlockSpec((1, tm, d), tok), pl.BlockSpec((1, tm, LANES), tok)],
        out_shape=[jax.ShapeDtypeStruct((bsz, s, d), BF16),
                   jax.ShapeDtypeStruct((bsz, s, LANES), F32)],
        compiler_params=_cparams(("arbitrary", "arbitrary")),
        name="router",
    )(x, g.reshape(1, d), sc, sh, router_w.T, router_b.reshape(N_EXPERTS, 1))


MOE_SUB = 512
MOE_CAP = 128


def _expert_ffn(rows, w1_ref, w3_ref, w2_ref):
    h1 = jnp.dot(rows, w1_ref[0], preferred_element_type=F32)
    h3 = jnp.dot(rows, w3_ref[0], preferred_element_type=F32)
    h = (h1 * jax.nn.sigmoid(h1)) * h3
    return jnp.dot(h.astype(BF16), w2_ref[0], preferred_element_type=F32)


def _moe_kernel(x_ref, g2_ref, fg_ref, u_ref, comb_ref, w1_ref, w3_ref, w2_ref, o_ref,
                acc_scr, pos_scr, pos_t_scr, comb_t_scr, scatter_scr, y_scr, *, final_norm):
    e = pl.program_id(2)
    tm = u_ref.shape[1]
    sub = min(MOE_SUB, tm)
    subs = [slice(r0, r0 + sub) for r0 in range(0, tm, sub)]

    @pl.when(e == 0)
    def _():
        acc_scr[...] = jnp.zeros(acc_scr.shape, F32)
        before = (lax.broadcasted_iota(jnp.int32, (sub, sub), 1)
                  < lax.broadcasted_iota(jnp.int32, (sub, sub), 0)).astype(BF16)
        for rows in subs:
            comb0 = comb_ref[0, rows, :]
            member = jnp.where(comb0 > 0.0, 1.0, 0.0).astype(BF16)
            pos = jnp.dot(before, member, preferred_element_type=F32)
            pos_scr[rows, :] = pos
            pos_t_scr[:, rows] = pos.T
            comb_t_scr[:, rows] = comb0.T

    comb = comb_ref[0]
    lane = lax.broadcasted_iota(jnp.int32, comb.shape, 1)
    ce = jnp.sum(jnp.where(lane == e, comb, 0.0), axis=-1, keepdims=True)
    pe = jnp.sum(jnp.where(lane == e, pos_scr[...], 0.0), axis=-1, keepdims=True)
    member_e = jnp.where(ce > 0.0, 1.0, 0.0)
    fullest = functools.reduce(jnp.maximum, [jnp.sum(member_e[rows]) for rows in subs])
    routed = fullest <= float(MOE_CAP)
    slots = pl.ds(pl.multiple_of(e * MOE_CAP, MOE_CAP), MOE_CAP)

    @pl.when(routed)
    def _():
        slot_l = lax.broadcasted_iota(jnp.int32, (1, MOE_CAP), 1).astype(F32)
        scatter_scr[e] = jnp.where(jnp.logical_and(ce > 0.0, pe == slot_l), 1.0, 0.0).astype(BF16)
        slot_s = lax.broadcasted_iota(jnp.int32, (MOE_CAP, 1), 0).astype(F32)
        packed, weights = [], []
        for rows in subs:
            ce_t = comb_t_scr[pl.ds(e, 1), rows]
            pe_t = pos_t_scr[pl.ds(e, 1), rows]
            hit = jnp.logical_and(ce_t > 0.0, pe_t == slot_s)
            weights.append(jnp.sum(jnp.where(hit, ce_t, 0.0), axis=-1, keepdims=True))
            packed.append(jnp.dot(jnp.where(hit, 1.0, 0.0).astype(BF16), u_ref[0, rows, :],
                                  preferred_element_type=F32).astype(BF16))
        y = jnp.concatenate(weights, axis=0) * _expert_ffn(jnp.concatenate(packed, axis=0),
                                                           w1_ref, w3_ref, w2_ref)
        for k in range(len(subs)):
            y_scr[k, slots, :] = y[k * MOE_CAP:(k + 1) * MOE_CAP].astype(BF16)

    @pl.when(jnp.logical_not(routed))
    def _():
        scatter_scr[e] = jnp.zeros(scatter_scr.shape[1:], BF16)
        for k in range(len(subs)):
            y_scr[k, slots, :] = jnp.zeros((MOE_CAP, y_scr.shape[2]), BF16)
        acc_scr[...] += ce * _expert_ffn(u_ref[0], w1_ref, w3_ref, w2_ref)

    @pl.when(e == pl.num_programs(2) - 1)
    def _():
        for k, rows in enumerate(subs):
            scatter = jnp.concatenate([scatter_scr[j, rows, :] for j in range(N_EXPERTS)], axis=1)
            y = acc_scr[rows, :] + jnp.dot(scatter, y_scr[k], preferred_element_type=F32)
            out = x_ref[0, rows, :] + g2_ref[0] * y
            if final_norm:
                out = (out * lax.rsqrt(jnp.mean(out * out, axis=-1, keepdims=True) + EPS)) * fg_ref[...]
            o_ref[0, rows, :] = out


def _moe(x, g2, final_g, u, comb, w1, w3, w2, tm, final_norm):
    bsz, s, d = x.shape
    ne, _, dff = w1.shape
    assert ne == N_EXPERTS
    tok = lambda b, i, e: (b, i, 0)
    return pl.pallas_call(
        functools.partial(_moe_kernel, final_norm=final_norm),
        grid=(bsz, s // tm, ne),
        in_specs=[
            pl.BlockSpec((1, tm, d), tok),
            pl.BlockSpec((1, 1, d), lambda b, i, e: (b, 0, 0)),
            pl.BlockSpec((1, d), lambda b, i, e: (0, 0)),
            pl.BlockSpec((1, tm, d), tok),
            pl.BlockSpec((1, tm, LANES), tok),
            pl.BlockSpec((1, d, dff), lambda b, i, e: (e, 0, 0)),
            pl.BlockSpec((1, d, dff), lambda b, i, e: (e, 0, 0)),
            pl.BlockSpec((1, dff, d), lambda b, i, e: (e, 0, 0)),
        ],
        out_specs=pl.BlockSpec((1, tm, d), tok),
        out_shape=jax.ShapeDtypeStruct((bsz, s, d), F32),
        scratch_shapes=[
            pltpu.VMEM((tm, d), F32),
            pltpu.VMEM((tm, LANES), F32),
            pltpu.VMEM((LANES, tm), F32),
            pltpu.VMEM((LANES, tm), F32),
            pltpu.VMEM((ne, tm, MOE_CAP), BF16),
            pltpu.VMEM((pl.cdiv(tm, MOE_SUB), ne * MOE_CAP, d), BF16),
        ],
        compiler_params=_cparams(("arbitrary", "arbitrary", "arbitrary")),
        name="moe",
    )(x, g2, final_g.reshape(1, d), u, comb, w1, w3, w2)


def _tile(s, want):
    t = min(want, s)
    assert s % t == 0
    return t


def _tiles(s):
    return dict(
        rows=_tile(s, 1024),
        diff_q=_tile(s, 1024),
        diff_k=_tile(s, 512),
        dsa_q=_tile(s, 256),
        merge=_tile(s, 512),
    )


def kernel(x, c, positions, norm1_g, norm2_g, w_mod, b_mod, w_in, lambda_q1, lambda_k1, lambda_q2, lambda_k2, a_norm_g, c_rel_bias, w_branch_a, w_branch_b, w_branch_c, w_out, router_w, router_b, exp_w1, exp_w3, exp_w2, final_g):
    bsz, s, d = x.shape
    depth = w_mod.shape[0]
    t = _tiles(s)
    tm = t["rows"]

    mod = _modulation(c, w_mod, b_mod)
    rope_tables = _rope_tables(positions)

    for layer in range(depth):
        lam_init = 0.8 - 0.6 * math.exp(-0.3 * layer)
        sh1, sc1, g1, sh2, sc2, g2 = [m[:, None, :] for m in jnp.split(mod[layer], 6, axis=-1)]
        lam = (jnp.exp(jnp.sum(lambda_q1[layer] * lambda_k1[layer]))
               - jnp.exp(jnp.sum(lambda_q2[layer] * lambda_k2[layer])) + lam_init)

        w_rope, w_gate, w_plain = _build_weights(w_in[layer])
        u = _norm(x, norm1_g[layer], sc1, sh1, tm)
        rope_slab = _project(u, w_rope, rope_tables, "rope", tm)
        gate_slab = _project(u, w_gate, (), "gate", tm)
        plain_slab = _project(u, w_plain, (), "plain", tm)

        ya = _diff_attention(rope_slab, plain_slab, lam, a_norm_g[layer], lam_init,
                             t["diff_q"], t["diff_k"])
        yb = _dsa_attention(rope_slab, plain_slab, t["dsa_q"])
        pad = ((0, 0), (BAND_PAD, 0), (0, 0))
        kp = jnp.pad(plain_slab[:, :, P_CK * COL_TILE:(P_CK + 1) * COL_TILE], pad)
        vp = jnp.pad(plain_slab[:, :, P_CV * COL_TILE:(P_CV + 1) * COL_TILE], pad)
        yc = _band_attention(plain_slab, kp, vp, _band_bias(c_rel_bias[layer]))
        x = _merge(x, g1, ya, yb, yc, gate_slab,
                   w_branch_a[layer].astype(BF16), w_branch_b[layer].astype(BF16),
                   w_branch_c[layer].astype(BF16), w_out[layer].astype(BF16), t["merge"])

        u, comb = _router(x, norm2_g[layer], sc2, sh2, router_w, router_b, t["merge"])
        x = _moe(x, g2, final_g, u, comb, exp_w1[layer].astype(BF16), exp_w3[layer].astype(BF16),
                 exp_w2[layer].astype(BF16), tm, final_norm=(layer == depth - 1))

    return x
```

```python
import functools
import math

import jax
import jax.numpy as jnp
from jax import lax
from jax.experimental import pallas as pl
from jax.experimental.pallas import tpu as pltpu

F32 = jnp.float32
BF16 = jnp.bfloat16

CHUNK = 64
ROPE_THETA = 10000.0
EPS = 1e-6
A_HEADS = 4
HEAD_DIM = 64
B_HEADS = 8
IDX_HEADS = 4
TOPK_MAX = 256
C_HEADS = 8
C_LEFT_CHUNKS = 8
REL_CLIP = 256
N_EXPERTS = 16
N_GROUPS = 4
EXPERTS_PER_GROUP = 4
N_BRANCHES = 3

LANES = 128
NEG = -1e30
LOG2E = math.log2(math.e)
VMEM_LIMIT = 56 * 1024 * 1024

COL_TILE = 512
PROJ_ROWS = 256
R_AQ, R_AK, R_BQ, R_IDX = 0, 1, 2, 3
P_AV, P_CQ, P_CK, P_CV, P_MISC = 0, 1, 2, 3, 4


def _cparams(sem):
    return pltpu.CompilerParams(dimension_semantics=sem, vmem_limit_bytes=VMEM_LIMIT)


def _mod_kernel(c_ref, w_ref, b_ref, o_ref):
    c = c_ref[...]
    ca = c * jax.nn.sigmoid(c)
    o_ref[0] = jnp.dot(ca, w_ref[0], preferred_element_type=F32) + b_ref[0]


def _modulation(c, w_mod, b_mod):
    depth, d, n6 = w_mod.shape
    bsz = c.shape[0]
    tn = 1024
    return pl.pallas_call(
        _mod_kernel,
        grid=(depth, n6 // tn),
        in_specs=[
            pl.BlockSpec((bsz, d), lambda l, j: (0, 0)),
            pl.BlockSpec((1, d, tn), lambda l, j: (l, 0, j)),
            pl.BlockSpec((1, 1, tn), lambda l, j: (l, 0, j)),
        ],
        out_specs=pl.BlockSpec((1, bsz, tn), lambda l, j: (l, 0, j)),
        out_shape=jax.ShapeDtypeStruct((depth, bsz, n6), F32),
        compiler_params=_cparams(("arbitrary", "arbitrary")),
        name="modulation",
    )(c, w_mod, b_mod.reshape(depth, 1, n6))


def _modulated_norm(x, g, sc, sh):
    y = x * lax.rsqrt(jnp.mean(x * x, axis=-1, keepdims=True) + EPS)
    return (y * g) * (1.0 + sc) + sh


def _norm_kernel(x_ref, g_ref, sc_ref, sh_ref, u_ref):
    u_ref[0] = _modulated_norm(x_ref[0], g_ref[...], sc_ref[0], sh_ref[0]).astype(BF16)


def _norm(x, g, sc, sh, tm):
    bsz, s, d = x.shape
    tok = lambda b, i: (b, i, 0)
    per_batch = lambda b, i: (b, 0, 0)
    return pl.pallas_call(
        _norm_kernel,
        grid=(bsz, s // tm),
        in_specs=[pl.BlockSpec((1, tm, d), tok), pl.BlockSpec((1, d), lambda b, i: (0, 0)),
                  pl.BlockSpec((1, 1, d), per_batch), pl.BlockSpec((1, 1, d), per_batch)],
        out_specs=pl.BlockSpec((1, tm, d), tok),
        out_shape=jax.ShapeDtypeStruct((bsz, s, d), BF16),
        compiler_params=_cparams(("arbitrary", "arbitrary")),
        name="norm",
    )(x, g.reshape(1, d), sc, sh)


def _proj_kernel(*refs, mode):
    if mode == "rope":
        u_ref, cos_ref, sin_ref, w_ref, o_ref = refs
    else:
        u_ref, w_ref, o_ref = refs
    tm = u_ref.shape[1]
    for r0 in range(0, tm, PROJ_ROWS):
        rows = slice(r0, r0 + PROJ_ROWS)
        acc = jnp.dot(u_ref[0, rows, :], w_ref[...], preferred_element_type=F32)
        if mode == "rope":
            cos, sin = cos_ref[0, rows, :], sin_ref[0, rows, :]
            groups = [acc[:, c:c + LANES] for c in range(0, COL_TILE, LANES)]
            acc = jnp.concatenate(
                [g * cos + pltpu.roll(g, LANES // 2, axis=1) * sin for g in groups], axis=1)
        elif mode == "gate":
            acc = 1.0 / (1.0 + jnp.exp(-acc))
        o_ref[0, rows, :] = acc.astype(BF16)


def _project(u, w, rope_tables, mode, tm):
    bsz, s, d = u.shape
    ncols = w.shape[1]
    tok = lambda b, i, j: (b, i, 0)
    in_specs = [pl.BlockSpec((1, tm, d), tok)]
    in_specs += [pl.BlockSpec((1, tm, LANES), tok) for _ in rope_tables]
    in_specs += [pl.BlockSpec((d, COL_TILE), lambda b, i, j: (0, j))]
    return pl.pallas_call(
        functools.partial(_proj_kernel, mode=mode),
        grid=(bsz, s // tm, ncols // COL_TILE),
        in_specs=in_specs,
        out_specs=pl.BlockSpec((1, tm, COL_TILE), lambda b, i, j: (b, i, j)),
        out_shape=jax.ShapeDtypeStruct((bsz, s, ncols), BF16),
        compiler_params=_cparams(("arbitrary", "arbitrary", "arbitrary")),
        name="proj_" + mode,
    )(u, *rope_tables, w)


def _pair_layout(w):
    half = HEAD_DIM // 2
    col = jnp.arange(w.shape[1])
    base, r = (col // LANES) * LANES, col % LANES
    src = base + ((r // half) % 2) * HEAD_DIM + (r // HEAD_DIM) * half + r % half
    return w[:, src]


def _build_weights(w_in):
    sizes = (512, 512, 512, 512, 64, 64, 256, 64, 4, 512, 512, 512, 3072)
    parts, start = [], 0
    for sz in sizes:
        parts.append(w_in[:, start:start + sz])
        start += sz
    aq, ak, av, bq, bk, bv, iq, ik, iw, cq, ck, cv, gates = parts
    d = w_in.shape[0]
    qscale = HEAD_DIM ** -0.5 * LOG2E
    iw_scale = IDX_HEADS ** -0.5 * HEAD_DIM ** -0.5
    zeros = lambda n: jnp.zeros((d, n), w_in.dtype)
    w_rope = _pair_layout(jnp.concatenate([aq * qscale, ak, bq * qscale, iq, bk, bk, ik, ik], axis=1))
    w_plain = jnp.concatenate([av, cq * qscale, ck, cv,
                               bv, bv, iw * iw_scale, zeros(LANES - IDX_HEADS),
                               zeros(COL_TILE - 2 * LANES)], axis=1)
    return w_rope.astype(BF16), gates.astype(BF16), w_plain.astype(BF16)


def _rope_tables(positions):
    half = HEAD_DIM // 2
    inv = ROPE_THETA ** (-jnp.arange(half, dtype=F32) / half)
    ang = positions.astype(F32)[..., None] * inv
    cos, sin = jnp.cos(ang), jnp.sin(ang)
    cos_t = jnp.tile(cos, (1, 1, LANES // half))
    sin_t = jnp.concatenate([-sin, -sin, sin, sin], axis=-1)
    return cos_t, sin_t


def _split_pair(pair, interleaved):
    lane = lax.broadcasted_iota(jnp.int32, pair.shape, 1)
    first = ((lane // (HEAD_DIM // 2)) % 2 == 0) if interleaved else (lane < HEAD_DIM)
    zero = jnp.zeros_like(pair)
    return jnp.where(first, pair, zero), jnp.where(first, zero, pair)


def _qk(q, k):
    return lax.dot_general(q, k, (((1,), (1,)), ((), ())), preferred_element_type=F32)


FOLD_CHAINS = 4


def _fold_rows(x, op):
    parts = x.reshape(x.shape[0] // 8, 8, x.shape[1])
    chains = [functools.reduce(op, [parts[g] for g in range(c, parts.shape[0], FOLD_CHAINS)])
              for c in range(min(FOLD_CHAINS, parts.shape[0]))]
    return functools.reduce(op, chains)


def _flash_t_update(s, vt_blocks, m_scr, acc_scr):
    m_old = m_scr[...]
    m_new = jnp.maximum(m_old, jnp.max(_fold_rows(s, jnp.maximum), axis=0, keepdims=True))
    alpha = jnp.exp2(m_old - m_new)
    pb = jnp.exp2(s - m_new).astype(BF16)
    m_scr[...] = m_new
    pv = functools.reduce(lambda a, b: a + b, [
        jnp.dot(vt, pb[k0:k1], preferred_element_type=F32) for (k0, k1), vt in vt_blocks])
    acc_scr[...] = alpha * acc_scr[...] + pv


ONES_ROWS = 16


def _with_ones_rows(vt):
    shape = vt.shape[:-2]
    return jnp.concatenate([vt, jnp.ones(shape + (1, vt.shape[-1]), vt.dtype),
                            jnp.zeros(shape + (ONES_ROWS - 1, vt.shape[-1]), vt.dtype)], axis=-2)


def _flash_update(s, v, m_scr, l_scr, acc_scr, rows=slice(None)):
    groups = [s[:, c:c + LANES] for c in range(0, s.shape[1], LANES)]
    m_old = m_scr[rows, :]
    lane_max = functools.reduce(jnp.maximum, groups)
    m_new = jnp.maximum(m_old, jnp.max(lane_max, axis=-1, keepdims=True))
    alpha = jnp.exp2(m_old - m_new)
    p_groups = [jnp.exp2(g - m_new) for g in groups]
    l_scr[rows, :] = alpha * l_scr[rows, :] + functools.reduce(lambda a, b: a + b, p_groups)
    m_scr[rows, :] = m_new
    pb = jnp.concatenate([g.astype(BF16) for g in p_groups], axis=1)
    acc_scr[rows, :] = alpha * acc_scr[rows, :] + jnp.dot(pb, v, preferred_element_type=F32)


def _diff_attn_kernel(lam_ref, q_ref, k_ref, v_ref, ng_ref, o_ref, m_scr, l_scr, acc_scr,
                      *, tq, tk, lam_init):
    i = pl.program_id(2)
    per_tile = tq // tk
    qs = jnp.concatenate(_split_pair(q_ref[0], True), axis=0)
    m_scr[...] = jnp.full(m_scr.shape, NEG, F32)
    l_scr[...] = jnp.zeros(l_scr.shape, F32)
    acc_scr[...] = jnp.zeros(acc_scr.shape, F32)

    def kv_block(j):
        start = pl.multiple_of(j * tk, tk)
        return k_ref[0, pl.ds(start, tk), :], v_ref[0, pl.ds(start, tk), :]

    def body(j, carry):
        k, v = kv_block(j)
        _flash_update(_qk(qs, k), v, m_scr, l_scr, acc_scr)
        return carry

    lax.fori_loop(0, i * per_tile, body, 0)
    for d in range(per_tile):
        k, v = kv_block(i * per_tile + d)
        for base in (0, tq):
            rows = slice(base + d * tk, base + tq)
            s = _qk(qs[rows], k)
            row = lax.broadcasted_iota(jnp.int32, s.shape, 0) + d * tk
            col = lax.broadcasted_iota(jnp.int32, s.shape, 1) + d * tk
            s = jnp.where((col // CHUNK) <= (row // CHUNK), s, NEG)
            _flash_update(s, v, m_scr, l_scr, acc_scr, rows)

    o = acc_scr[...] / jnp.sum(l_scr[...], axis=-1, keepdims=True)
    o = o[:tq] - lam_ref[0] * o[tq:]
    o = o * lax.rsqrt(jnp.mean(o * o, axis=-1, keepdims=True) + EPS)
    o_ref[0] = ((o * ng_ref[...]) * (1.0 - lam_init)).astype(BF16)


def _diff_attention(rope_slab, plain_slab, lam, norm_g, lam_init, tq, tk):
    bsz, s, _ = rope_slab.shape
    assert tq % tk == 0
    kern = functools.partial(_diff_attn_kernel, tq=tq, tk=tk, lam_init=lam_init)
    cb = COL_TILE // LANES
    return pl.pallas_call(
        kern,
        grid=(bsz, A_HEADS, s // tq),
        in_specs=[
            pl.BlockSpec(memory_space=pltpu.SMEM),
            pl.BlockSpec((1, tq, LANES), lambda b, h, i: (b, i, R_AQ * cb + h)),
            pl.BlockSpec((1, s, LANES), lambda b, h, i: (b, 0, R_AK * cb + h)),
            pl.BlockSpec((1, s, LANES), lambda b, h, i: (b, 0, P_AV * cb + h)),
            pl.BlockSpec((1, LANES), lambda b, h, i: (0, 0)),
        ],
        out_specs=pl.BlockSpec((1, tq, LANES), lambda b, h, i: (b, i, h)),
        out_shape=jax.ShapeDtypeStruct((bsz, s, A_HEADS * LANES), BF16),
        scratch_shapes=[pltpu.VMEM((2 * tq, LANES), F32), pltpu.VMEM((2 * tq, LANES), F32),
                        pltpu.VMEM((2 * tq, LANES), F32)],
        compiler_params=_cparams(("arbitrary", "arbitrary", "arbitrary")),
        name="diff_attn",
    )(lam.reshape(1), rope_slab, rope_slab, plain_slab, norm_g.reshape(1, LANES))


KEY_NEG_INF = -2139095040
KEY_POS_INF = 2139095040
SEARCH_MAX_STEPS = 80
NO_TIE_LIMIT = 1e9
P3_PAIRS = 2
SCAN_PAIRS = 4
VT_ROWS = HEAD_DIM + ONES_ROWS


def _key_to_float(key):
    bits = jnp.where(key >= 0, key, (key - 1) ^ jnp.int32(0x7FFFFFFF))
    return pltpu.bitcast(bits, F32)


def _float_to_key(t):
    bits = pltpu.bitcast(t, jnp.int32)
    return jnp.where(bits >= 0, bits, (bits ^ jnp.int32(0x7FFFFFFF)) + 1)


def _dsa_kernel(q_ref, iq_ref, iw_ref, k_ref, ik_ref, vt_ref, o_ref,
                sc_scr, qs_scr, m_scr, acc_scr, *, tq, topk):
    i = pl.program_id(1)
    nblk = i + 1
    npair = (nblk + 1) // 2
    nq = B_HEADS * tq

    iq = iq_ref[0]
    parts = []
    for p in range(IDX_HEADS // 2):
        parts += list(_split_pair(iq[:, p * LANES:(p + 1) * LANES], True))
    iqs = jnp.concatenate(parts, axis=0)
    iw_t = iw_ref[0].astype(F32).T

    def index_block(j):
        start = pl.multiple_of(j * tq, tq)
        ik = ik_ref[0, pl.ds(start, tq), :]
        score = None
        for h in range(IDX_HEADS):
            hs = jnp.maximum(_qk(ik, iqs[h * tq:(h + 1) * tq]), 0.0)
            score = iw_t[h:h + 1] * hs if score is None else score + iw_t[h:h + 1] * hs
        return score

    def p1_pair(jj, carry):
        sc_scr[jj, 0:tq, :] = index_block(2 * jj)
        sc_scr[jj, tq:2 * tq, :] = index_block(2 * jj + 1)
        return carry

    def p1_two_pairs(g, carry):
        return p1_pair(2 * g + 1, p1_pair(2 * g, carry))

    full_pairs = i // 2
    lax.fori_loop(0, full_pairs // 2, p1_two_pairs, 0)
    lax.fori_loop((full_pairs // 2) * 2, full_pairs, p1_pair, 0)
    score = index_block(i)
    key_i = lax.broadcasted_iota(jnp.int32, score.shape, 0)
    qry_i = lax.broadcasted_iota(jnp.int32, score.shape, 1)
    diagonal = jnp.where((key_i // CHUNK) <= (qry_i // CHUNK), score, -jnp.inf)

    @pl.when(i % 2 == 1)
    def _():
        sc_scr[npair - 1, 0:tq, :] = index_block(i - 1)
        sc_scr[npair - 1, tq:2 * tq, :] = diagonal

    @pl.when(i % 2 == 0)
    def _():
        sc_scr[npair - 1, 0:tq, :] = diagonal
        sc_scr[npair - 1, tq:2 * tq, :] = jnp.full((tq, tq), -jnp.inf, F32)

    kf = float(topk)

    def scan(hit_fn, ext_fn, ext_op, ext_init):
        def one_pair(jj, carry):
            cnt, ext = carry
            sblk = sc_scr[jj]
            cnt = cnt + _fold_rows(jnp.where(hit_fn(sblk), 1.0, 0.0), lambda a, b: a + b)
            if ext_fn is not None:
                ext = ext_op(ext, _fold_rows(ext_fn(sblk), ext_op))
            return cnt, ext

        def several_pairs(g, carry):
            for t in range(SCAN_PAIRS):
                carry = one_pair(SCAN_PAIRS * g + t, carry)
            return carry

        carry = lax.fori_loop(0, npair // SCAN_PAIRS, several_pairs,
                              (jnp.zeros((8, tq), F32), jnp.full((8, tq), ext_init, F32)))
        cnt, ext = lax.fori_loop((npair // SCAN_PAIRS) * SCAN_PAIRS, npair, one_pair, carry)
        return jnp.sum(cnt, axis=0, keepdims=True), ext

    def count_ge(t):
        return scan(lambda sblk: sblk >= t, None, None, 0.0)[0]

    c_ge0, ext = scan(lambda sblk: sblk >= 0.0, lambda sblk: sblk, jnp.maximum, -jnp.inf)
    col_max = jnp.max(ext, axis=0, keepdims=True)
    c_gt0, ext = scan(lambda sblk: sblk > 0.0,
                      lambda sblk: jnp.where(sblk > -jnp.inf, sblk, jnp.inf), jnp.minimum, jnp.inf)
    col_min = jnp.min(ext, axis=0, keepdims=True)
    qpos = i * tq + lax.broadcasted_iota(jnp.int32, (1, tq), 1)
    n_valid = (((qpos // CHUNK) + 1) * CHUNK).astype(F32)
    open_q = n_valid < kf
    above = c_gt0 >= kf
    below = c_ge0 < kf
    ikey = lambda v: jnp.full((1, tq), v, jnp.int32)
    lo_k = jnp.where(below, _float_to_key(col_min), ikey(0))
    hi_k = jnp.where(above, _float_to_key(col_max) + 1, jnp.where(below, ikey(0), ikey(1)))
    c_lo = jnp.where(below, n_valid, c_ge0)
    c_hi = jnp.where(above, 0.0, jnp.where(below, c_ge0, c_gt0))
    done = jnp.logical_or(open_q, jnp.logical_not(jnp.logical_or(above, below)))

    def next_probe(lo_k, hi_k, c_lo, c_hi, force_bisect):
        finite = jnp.logical_and(lo_k > KEY_NEG_INF, hi_k < KEY_POS_INF)
        t_lo, t_hi = _key_to_float(lo_k), _key_to_float(hi_k)
        log_lo = jnp.log(c_lo)
        frac = (log_lo - math.log(kf - 0.5)) / (log_lo - jnp.log(jnp.maximum(c_hi, 0.5)))
        k_interp = _float_to_key(t_lo + (t_hi - t_lo) * frac)
        k_mid = (lo_k & hi_k) + ((lo_k ^ hi_k) >> 1)
        interp = jnp.logical_and(finite, force_bisect == 0)
        k = jnp.where(interp, k_interp, k_mid)
        return jnp.minimum(jnp.maximum(k, lo_k + 1), hi_k - 1), interp

    def search_cond(st):
        return jnp.logical_and(st[0] < SEARCH_MAX_STEPS, jnp.min(st[5]) == 0)

    def search_step(st):
        step, lo_k, hi_k, c_lo, c_hi, done, k, interp = st
        c = count_ge(_key_to_float(k))
        ok = c >= kf
        live = done == 0
        new_lo = jnp.where(jnp.logical_and(live, ok), k, lo_k)
        new_hi = jnp.where(jnp.logical_and(live, jnp.logical_not(ok)), k, hi_k)
        new_c_lo = jnp.where(jnp.logical_and(live, ok), c, c_lo)
        new_c_hi = jnp.where(jnp.logical_and(live, jnp.logical_not(ok)), c, c_hi)
        width = lambda a, b: b.astype(F32) - a.astype(F32)
        slow = width(new_lo, new_hi) > 0.5 * width(lo_k, hi_k)
        force = jnp.logical_and(interp != 0, slow).astype(jnp.int32)
        finished = jnp.logical_or(new_c_lo == kf, new_hi - new_lo == 1)
        new_done = jnp.maximum(done, finished.astype(jnp.int32))
        new_k, new_interp = next_probe(new_lo, new_hi, new_c_lo, new_c_hi, force)
        return (step + 1, new_lo, new_hi, new_c_lo, new_c_hi, new_done, new_k,
                new_interp.astype(jnp.int32))

    k0, interp0 = next_probe(lo_k, hi_k, c_lo, c_hi, ikey(0))
    state = (jnp.int32(0), lo_k, hi_k, c_lo, c_hi, done.astype(jnp.int32), k0,
             interp0.astype(jnp.int32))
    _, lo_k, hi_k, c_lo, c_hi, _, _, _ = lax.while_loop(
        search_cond, lambda st: search_step(search_step(st)), state)
    tau = jnp.where(open_q, -jnp.inf, _key_to_float(lo_k))
    need = jnp.where(open_q, 0.0, jnp.where(c_lo == kf, NO_TIE_LIMIT, kf - c_hi))

    q = q_ref[0]
    for p in range(B_HEADS // 2):
        even, odd = _split_pair(q[:, p * LANES:(p + 1) * LANES], True)
        qs_scr[p * tq:(p + 1) * tq, :] = even
        qs_scr[(B_HEADS // 2 + p) * tq:(B_HEADS // 2 + p + 1) * tq, :] = odd
    m_scr[...] = jnp.full(m_scr.shape, NEG, F32)
    acc_scr[...] = jnp.zeros(acc_scr.shape, F32)
    key_i = lax.broadcasted_iota(jnp.int32, (tq, tq), 0)
    lower = (lax.broadcasted_iota(jnp.int32, (tq, tq), 1) <= key_i).astype(BF16)

    def select_bias(jj, tie_count):
        sblk = sc_scr[jj]
        tie = sblk == tau
        tie01 = jnp.where(tie, 1.0, 0.0).astype(BF16)
        rank_a = tie_count + jnp.dot(lower, tie01[0:tq], preferred_element_type=F32)
        rank_b = rank_a[tq - 1:tq, :] + jnp.dot(lower, tie01[tq:2 * tq], preferred_element_type=F32)
        rank = jnp.concatenate([rank_a, rank_b], axis=0)
        keep_tie = jnp.where(rank <= need, 0.0, NEG)
        return jnp.where(sblk > tau, 0.0, jnp.where(tie, keep_tie, NEG)), rank[2 * tq - 1:2 * tq, :]

    def p3_step(jj, pairs, tie_count):
        nkeys = pairs * 2 * tq
        start = pl.multiple_of(jj * 2 * tq, 2 * tq)
        biases = []
        for t in range(pairs):
            bias, tie_count = select_bias(jj + t, tie_count)
            biases.append(bias)
        bias = jnp.concatenate(biases, axis=0)
        s = _qk(k_ref[0, pl.ds(start, nkeys), :], qs_scr[...])
        s = jnp.concatenate([s[:, h * tq:(h + 1) * tq] + bias for h in range(B_HEADS)], axis=1)
        _flash_t_update(s, [((t * tq, (t + 1) * tq), vt_ref[0, 2 * jj + t]) for t in range(2 * pairs)],
                        m_scr, acc_scr)
        return tie_count

    tie_count = lax.fori_loop(0, npair // P3_PAIRS,
                              lambda g, tc: p3_step(g * P3_PAIRS, P3_PAIRS, tc),
                              jnp.zeros((1, tq), F32))
    for r in range(1, P3_PAIRS):
        @pl.when(npair % P3_PAIRS == r)
        def _():
            p3_step(npair - r, r, tie_count)

    acc = acc_scr[...]
    o = acc[0:HEAD_DIM] / acc[HEAD_DIM:HEAD_DIM + 1]
    half_cols = (B_HEADS // 2) * tq
    for p in range(B_HEADS // 2):
        pair = jnp.concatenate([o[:, p * tq:(p + 1) * tq],
                                o[:, half_cols + p * tq:half_cols + (p + 1) * tq]], axis=0)
        o_ref[0, :, p * LANES:(p + 1) * LANES] = pair.T.astype(BF16)


def _dsa_attention(rope_slab, plain_slab, tq):
    bsz, s, _ = rope_slab.shape
    topk = min(TOPK_MAX, s // 4)
    nblk = s // tq
    assert tq >= topk and s % tq == 0 and nblk % 2 == 0
    kern = functools.partial(_dsa_kernel, tq=tq, topk=topk)
    idx0 = R_IDX * COL_TILE
    misc0 = P_MISC * COL_TILE
    iq_w = IDX_HEADS * HEAD_DIM
    width = B_HEADS * HEAD_DIM
    vt = _with_ones_rows(
        plain_slab[:, :, misc0:misc0 + HEAD_DIM].reshape(bsz, nblk, tq, HEAD_DIM).transpose(0, 1, 3, 2))
    return pl.pallas_call(
        kern,
        grid=(bsz, s // tq),
        in_specs=[
            pl.BlockSpec((1, tq, COL_TILE), lambda b, i: (b, i, R_BQ)),
            pl.BlockSpec((1, tq, iq_w), lambda b, i: (b, i, idx0 // iq_w)),
            pl.BlockSpec((1, tq, LANES), lambda b, i: (b, i, (misc0 + LANES) // LANES)),
            pl.BlockSpec((1, s, LANES), lambda b, i: (b, 0, (idx0 + iq_w) // LANES)),
            pl.BlockSpec((1, s, LANES), lambda b, i: (b, 0, (idx0 + iq_w + LANES) // LANES)),
            pl.BlockSpec((1, nblk, VT_ROWS, tq), lambda b, i: (b, 0, 0, 0)),
        ],
        out_specs=pl.BlockSpec((1, tq, width), lambda b, i: (b, i, 0)),
        out_shape=jax.ShapeDtypeStruct((bsz, s, width), BF16),
        scratch_shapes=[
            pltpu.VMEM((nblk // 2, 2 * tq, tq), F32),
            pltpu.VMEM((B_HEADS * tq, LANES), BF16),
            pltpu.VMEM((1, B_HEADS * tq), F32),
            pltpu.VMEM((VT_ROWS, B_HEADS * tq), F32),
        ],
        compiler_params=_cparams(("arbitrary", "arbitrary")),
        name="dsa_attn",
    )(rope_slab, rope_slab, plain_slab, rope_slab, rope_slab, vt)


BAND_TQ = 2 * CHUNK
BAND_KEYS = (C_LEFT_CHUNKS + BAND_TQ // CHUNK) * CHUNK
BAND_PAD = C_LEFT_CHUNKS * CHUNK


def _band_kernel(q_ref, k_ref, v_ref, bias_ref, o_ref):
    i = pl.program_id(1)
    tq = BAND_TQ
    start = pl.multiple_of(i * tq, tq)
    col = lax.broadcasted_iota(jnp.int32, (2 * tq, BAND_KEYS), 1)
    key_ok = col + start >= BAND_PAD
    for p in range(C_HEADS // 2):
        lanes = slice(p * LANES, (p + 1) * LANES)
        k = k_ref[0, pl.ds(start, BAND_KEYS), lanes]
        v = v_ref[0, pl.ds(start, BAND_KEYS), lanes]
        qs = jnp.concatenate(_split_pair(q_ref[0, :, lanes], False), axis=0)
        s = _qk(qs, k) + bias_ref[2 * p:2 * p + 2].reshape(2 * tq, BAND_KEYS)
        s = jnp.where(key_ok, s, NEG)
        m = jnp.max(s, axis=-1, keepdims=True)
        e = jnp.exp2(s - m)
        l = jnp.sum(e, axis=-1, keepdims=True)
        eb = e.astype(BF16)
        v_even, v_odd = _split_pair(v, False)
        o = (jnp.dot(eb[:tq], v_even, preferred_element_type=F32) / l[:tq]
             + jnp.dot(eb[tq:], v_odd, preferred_element_type=F32) / l[tq:])
        o_ref[0, :, lanes] = o.astype(BF16)


def _band_bias(rel_bias):
    r = jnp.arange(BAND_TQ)[:, None]
    cidx = jnp.arange(BAND_KEYS)[None, :]
    n_diag = BAND_TQ + BAND_KEYS - 1
    rel = BAND_PAD - (BAND_KEYS - 1) + jnp.arange(n_diag)
    g = rel_bias.astype(F32)[:, jnp.clip(rel, -REL_CLIP, REL_CLIP) + REL_CLIP] * LOG2E
    skew = jnp.tile(g, (1, BAND_TQ + 2))[:, :BAND_TQ * (n_diag + 1)]
    bias = skew.reshape(-1, BAND_TQ, n_diag + 1)[:, :, :BAND_KEYS][:, :, ::-1]
    dchunk = (r // CHUNK + C_LEFT_CHUNKS) - cidx // CHUNK
    in_band = jnp.logical_and(dchunk >= 0, dchunk <= C_LEFT_CHUNKS)
    return jnp.where(in_band[None], bias, NEG)


def _band_attention(plain_slab, kp, vp, bias):
    bsz, s, _ = plain_slab.shape
    tq = BAND_TQ
    sp = kp.shape[1]
    width = C_HEADS * HEAD_DIM
    return pl.pallas_call(
        _band_kernel,
        grid=(bsz, s // tq),
        in_specs=[
            pl.BlockSpec((1, tq, width), lambda b, i: (b, i, P_CQ)),
            pl.BlockSpec((1, sp, width), lambda b, i: (b, 0, 0)),
            pl.BlockSpec((1, sp, width), lambda b, i: (b, 0, 0)),
            pl.BlockSpec((C_HEADS, tq, BAND_KEYS), lambda b, i: (0, 0, 0)),
        ],
        out_specs=pl.BlockSpec((1, tq, width), lambda b, i: (b, i, 0)),
        out_shape=jax.ShapeDtypeStruct((bsz, s, width), BF16),
        compiler_params=_cparams(("arbitrary", "arbitrary")),
        name="band_attn",
    )(plain_slab, kp, vp, bias)


def _merge_kernel(x_ref, g1_ref, ya_ref, yb_ref, yc_ref, ga_ref, gb_ref, gc_ref,
                  wa_ref, wb_ref, wc_ref, wo_ref, o_ref):
    for r0 in range(0, x_ref.shape[1], PROJ_ROWS):
        rows = slice(r0, r0 + PROJ_ROWS)

        def branch(y_ref, w_ref, gate_ref):
            return gate_ref[0, rows, :].astype(F32) * jnp.dot(y_ref[0, rows, :], w_ref[...],
                                                             preferred_element_type=F32)

        merged = (branch(ya_ref, wa_ref, ga_ref) + branch(yb_ref, wb_ref, gb_ref)
                  + branch(yc_ref, wc_ref, gc_ref))
        mixed = jnp.dot(merged.astype(BF16), wo_ref[...], preferred_element_type=F32)
        o_ref[0, rows, :] = x_ref[0, rows, :] + g1_ref[0] * mixed


def _merge(x, g1, ya, yb, yc, gate_slab, wa, wb, wc, wo, tm):
    bsz, s, d = x.shape
    tok = lambda b, i: (b, i, 0)
    full = lambda b, i: (0, 0)
    return pl.pallas_call(
        _merge_kernel,
        grid=(bsz, s // tm),
        in_specs=[
            pl.BlockSpec((1, tm, d), tok),
            pl.BlockSpec((1, 1, d), lambda b, i: (b, 0, 0)),
            pl.BlockSpec((1, tm, ya.shape[2]), tok),
            pl.BlockSpec((1, tm, yb.shape[2]), tok),
            pl.BlockSpec((1, tm, yc.shape[2]), tok),
            pl.BlockSpec((1, tm, d), lambda b, i: (b, i, 0)),
            pl.BlockSpec((1, tm, d), lambda b, i: (b, i, 1)),
            pl.BlockSpec((1, tm, d), lambda b, i: (b, i, 2)),
            pl.BlockSpec(wa.shape, full),
            pl.BlockSpec(wb.shape, full),
            pl.BlockSpec(wc.shape, full),
            pl.BlockSpec(wo.shape, full),
        ],
        out_specs=pl.BlockSpec((1, tm, d), tok),
        out_shape=jax.ShapeDtypeStruct((bsz, s, d), F32),
        compiler_params=_cparams(("arbitrary", "arbitrary")),
        name="merge",
    )(x, g1, ya, yb, yc, gate_slab, gate_slab, gate_slab, wa, wb, wc, wo)


def _router_kernel(x_ref, g_ref, sc_ref, sh_ref, rw_ref, rb_ref, u_ref, comb_ref):
    u = _modulated_norm(x_ref[0], g_ref[...], sc_ref[0], sh_ref[0])
    u_ref[0] = u.astype(BF16)
    logits = lax.dot_general(rw_ref[...], u, (((1,), (1,)), ((), ())),
                             preferred_element_type=F32, precision=lax.Precision.HIGHEST)
    aff = jax.nn.sigmoid(logits)
    sel = aff + rb_ref[...]
    rows = [sel[e:e + 1] for e in range(N_EXPERTS)]
    gscore = []
    for g in range(N_GROUPS):
        r = rows[g * EXPERTS_PER_GROUP:(g + 1) * EXPERTS_PER_GROUP]
        best = None
        for a in range(EXPERTS_PER_GROUP):
            for b in range(a + 1, EXPERTS_PER_GROUP):
                pair = r[a] + r[b]
                best = pair if best is None else jnp.maximum(best, pair)
        gscore.append(best)
    gmax = functools.reduce(jnp.maximum, gscore)
    taken = jnp.zeros_like(gmax) > 1.0
    in_best = []
    for g in range(N_GROUPS):
        is_g = jnp.logical_and(gscore[g] == gmax, jnp.logical_not(taken))
        in_best.append(is_g)
        taken = jnp.logical_or(taken, is_g)
    keep = []
    for e in range(N_EXPERTS):
        g = e // EXPERTS_PER_GROUP
        rank = jnp.zeros_like(gmax)
        for o in range(g * EXPERTS_PER_GROUP, (g + 1) * EXPERTS_PER_GROUP):
            if o == e:
                continue
            ahead = rows[o] > rows[e] if o > e else rows[o] >= rows[e]
            rank = rank + jnp.where(ahead, 1.0, 0.0)
        keep.append(jnp.logical_and(in_best[g], rank < 2.0))
    w = [jnp.where(keep[e], aff[e:e + 1], 0.0) for e in range(N_EXPERTS)]
    total = functools.reduce(lambda a, b: a + b, w)
    comb = jnp.concatenate([we / total for we in w]
                           + [jnp.zeros((LANES - N_EXPERTS, total.shape[1]), F32)], axis=0)
    comb_ref[0] = comb.T


def _router(x, g, sc, sh, router_w, router_b, tm):
    bsz, s, d = x.shape
    tok = lambda b, i: (b, i, 0)
    return pl.pallas_call(
        _router_kernel,
        grid=(bsz, s // tm),
        in_specs=[
            pl.BlockSpec((1, tm, d), tok),
            pl.BlockSpec((1, d), lambda b, i: (0, 0)),
            pl.BlockSpec((1, 1, d), lambda b, i: (b, 0, 0)),
            pl.BlockSpec((1, 1, d), lambda b, i: (b, 0, 0)),
            pl.BlockSpec((N_EXPERTS, d), lambda b, i: (0, 0)),
            pl.BlockSpec((N_EXPERTS, 1), lambda b, i: (0, 0)),
        ],
        out_specs=[pl.BlockSpec((1, tm, d), tok), pl.BlockSpec((1, tm, LANES), tok)],
        out_shape=[jax.ShapeDtypeStruct((bsz, s, d), BF16),
                   jax.ShapeDtypeStruct((bsz, s, LANES), F32)],
        compiler_params=_cparams(("arbitrary", "arbitrary")),
        name="router",
    )(x, g.reshape(1, d), sc, sh, router_w.T, router_b.reshape(N_EXPERTS, 1))


MOE_SUB = 512
MOE_CAP = 128


def _expert_ffn(rows, w1_ref, w3_ref, w2_ref):
    h1 = jnp.dot(rows, w1_ref[0], preferred_element_type=F32)
    h3 = jnp.dot(rows, w3_ref[0], preferred_element_type=F32)
    h = (h1 * jax.nn.sigmoid(h1)) * h3
    return jnp.dot(h.astype(BF16), w2_ref[0], preferred_element_type=F32)


def _moe_kernel(x_ref, g2_ref, fg_ref, u_ref, comb_ref, w1_ref, w3_ref, w2_ref, o_ref,
                acc_scr, pos_scr, pos_t_scr, comb_t_scr, scatter_scr, y_scr, *, final_norm):
    e = pl.program_id(2)
    tm = u_ref.shape[1]
    sub = min(MOE_SUB, tm)
    subs = [slice(r0, r0 + sub) for r0 in range(0, tm, sub)]

    @pl.when(e == 0)
    def _():
        acc_scr[...] = jnp.zeros(acc_scr.shape, F32)
        before = (lax.broadcasted_iota(jnp.int32, (sub, sub), 1)
                  < lax.broadcasted_iota(jnp.int32, (sub, sub), 0)).astype(BF16)
        for rows in subs:
            comb0 = comb_ref[0, rows, :]
            member = jnp.where(comb0 > 0.0, 1.0, 0.0).astype(BF16)
            pos = jnp.dot(before, member, preferred_element_type=F32)
            pos_scr[rows, :] = pos
            pos_t_scr[:, rows] = pos.T
            comb_t_scr[:, rows] = comb0.T

    comb = comb_ref[0]
    lane = lax.broadcasted_iota(jnp.int32, comb.shape, 1)
    ce = jnp.sum(jnp.where(lane == e, comb, 0.0), axis=-1, keepdims=True)
    pe = jnp.sum(jnp.where(lane == e, pos_scr[...], 0.0), axis=-1, keepdims=True)
    member_e = jnp.where(ce > 0.0, 1.0, 0.0)
    fullest = functools.reduce(jnp.maximum, [jnp.sum(member_e[rows]) for rows in subs])
    routed = fullest <= float(MOE_CAP)
    slots = pl.ds(pl.multiple_of(e * MOE_CAP, MOE_CAP), MOE_CAP)

    @pl.when(routed)
    def _():
        slot_l = lax.broadcasted_iota(jnp.int32, (1, MOE_CAP), 1).astype(F32)
        scatter_scr[e] = jnp.where(jnp.logical_and(ce > 0.0, pe == slot_l), 1.0, 0.0).astype(BF16)
        slot_s = lax.broadcasted_iota(jnp.int32, (MOE_CAP, 1), 0).astype(F32)
        packed, weights = [], []
        for rows in subs:
            ce_t = comb_t_scr[pl.ds(e, 1), rows]
            pe_t = pos_t_scr[pl.ds(e, 1), rows]
            hit = jnp.logical_and(ce_t > 0.0, pe_t == slot_s)
            weights.append(jnp.sum(jnp.where(hit, ce_t, 0.0), axis=-1, keepdims=True))
            packed.append(jnp.dot(jnp.where(hit, 1.0, 0.0).astype(BF16), u_ref[0, rows, :],
                                  preferred_element_type=F32).astype(BF16))
        y = jnp.concatenate(weights, axis=0) * _expert_ffn(jnp.concatenate(packed, axis=0),
                                                           w1_ref, w3_ref, w2_ref)
        for k in range(len(subs)):
            y_scr[k, slots, :] = y[k * MOE_CAP:(k + 1) * MOE_CAP].astype(BF16)

    @pl.when(jnp.logical_not(routed))
    def _():
        scatter_scr[e] = jnp.zeros(scatter_scr.shape[1:], BF16)
        for k in range(len(subs)):
            y_scr[k, slots, :] = jnp.zeros((MOE_CAP, y_scr.shape[2]), BF16)
        acc_scr[...] += ce * _expert_ffn(u_ref[0], w1_ref, w3_ref, w2_ref)

    @pl.when(e == pl.num_programs(2) - 1)
    def _():
        for k, rows in enumerate(subs):
            scatter = jnp.concatenate([scatter_scr[j, rows, :] for j in range(N_EXPERTS)], axis=1)
            y = acc_scr[rows, :] + jnp.dot(scatter, y_scr[k], preferred_element_type=F32)
            out = x_ref[0, rows, :] + g2_ref[0] * y
            if final_norm:
                out = (out * lax.rsqrt(jnp.mean(out * out, axis=-1, keepdims=True) + EPS)) * fg_ref[...]
            o_ref[0, rows, :] = out


def _moe(x, g2, final_g, u, comb, w1, w3, w2, tm, final_norm):
    bsz, s, d = x.shape
    ne, _, dff = w1.shape
    assert ne == N_EXPERTS
    tok = lambda b, i, e: (b, i, 0)
    return pl.pallas_call(
        functools.partial(_moe_kernel, final_norm=final_norm),
        grid=(bsz, s // tm, ne),
        in_specs=[
            pl.BlockSpec((1, tm, d), tok),
            pl.BlockSpec((1, 1, d), lambda b, i, e: (b, 0, 0)),
            pl.BlockSpec((1, d), lambda b, i, e: (0, 0)),
            pl.BlockSpec((1, tm, d), tok),
            pl.BlockSpec((1, tm, LANES), tok),
            pl.BlockSpec((1, d, dff), lambda b, i, e: (e, 0, 0)),
            pl.BlockSpec((1, d, dff), lambda b, i, e: (e, 0, 0)),
            pl.BlockSpec((1, dff, d), lambda b, i, e: (e, 0, 0)),
        ],
        out_specs=pl.BlockSpec((1, tm, d), tok),
        out_shape=jax.ShapeDtypeStruct((bsz, s, d), F32),
        scratch_shapes=[
            pltpu.VMEM((tm, d), F32),
            pltpu.VMEM((tm, LANES), F32),
            pltpu.VMEM((LANES, tm), F32),
            pltpu.VMEM((LANES, tm), F32),
            pltpu.VMEM((ne, tm, MOE_CAP), BF16),
            pltpu.VMEM((pl.cdiv(tm, MOE_SUB), ne * MOE_CAP, d), BF16),
        ],
        compiler_params=_cparams(("arbitrary", "arbitrary", "arbitrary")),
        name="moe",
    )(x, g2, final_g.reshape(1, d), u, comb, w1, w3, w2)


def _tile(s, want):
    t = min(want, s)
    assert s % t == 0
    return t


def _tiles(s):
    return dict(
        rows=_tile(s, 1024),
        diff_q=_tile(s, 1024),
        diff_k=_tile(s, 512),
        dsa_q=_tile(s, 256),
        merge=_tile(s, 512),
    )


def kernel(x, c, positions, norm1_g, norm2_g, w_mod, b_mod, w_in, lambda_q1, lambda_k1, lambda_q2, lambda_k2, a_norm_g, c_rel_bias, w_branch_a, w_branch_b, w_branch_c, w_out, router_w, router_b, exp_w1, exp_w3, exp_w2, final_g):
    bsz, s, d = x.shape
    depth = w_mod.shape[0]
    t = _tiles(s)
    tm = t["rows"]

    mod = _modulation(c, w_mod, b_mod)
    rope_tables = _rope_tables(positions)

    for layer in range(depth):
        lam_init = 0.8 - 0.6 * math.exp(-0.3 * layer)
        sh1, sc1, g1, sh2, sc2, g2 = [m[:, None, :] for m in jnp.split(mod[layer], 6, axis=-1)]
        lam = (jnp.exp(jnp.sum(lambda_q1[layer] * lambda_k1[layer]))
               - jnp.exp(jnp.sum(lambda_q2[layer] * lambda_k2[layer])) + lam_init)

        w_rope, w_gate, w_plain = _build_weights(w_in[layer])
        u = _norm(x, norm1_g[layer], sc1, sh1, tm)
        rope_slab = _project(u, w_rope, rope_tables, "rope", tm)
        gate_slab = _project(u, w_gate, (), "gate", tm)
        plain_slab = _project(u, w_plain, (), "plain", tm)

        ya = _diff_attention(rope_slab, plain_slab, lam, a_norm_g[layer], lam_init,
                             t["diff_q"], t["diff_k"])
        yb = _dsa_attention(rope_slab, plain_slab, t["dsa_q"])
        pad = ((0, 0), (BAND_PAD, 0), (0, 0))
        kp = jnp.pad(plain_slab[:, :, P_CK * COL_TILE:(P_CK + 1) * COL_TILE], pad)
        vp = jnp.pad(plain_slab[:, :, P_CV * COL_TILE:(P_CV + 1) * COL_TILE], pad)
        yc = _band_attention(plain_slab, kp, vp, _band_bias(c_rel_bias[layer]))
        x = _merge(x, g1, ya, yb, yc, gate_slab,
                   w_branch_a[layer].astype(BF16), w_branch_b[layer].astype(BF16),
                   w_branch_c[layer].astype(BF16), w_out[layer].astype(BF16), t["merge"])

        u, comb = _router(x, norm2_g[layer], sc2, sh2, router_w, router_b, t["merge"])
        x = _moe(x, g2, final_g, u, comb, exp_w1[layer].astype(BF16), exp_w3[layer].astype(BF16),
                 exp_w2[layer].astype(BF16), tm, final_norm=(layer == depth - 1))

    return x
```

```python
import functools
import math

import jax
import jax.numpy as jnp
from jax import lax
from jax.experimental import pallas as pl
from jax.experimental.pallas import tpu as pltpu

F32 = jnp.float32
BF16 = jnp.bfloat16

CHUNK = 64
ROPE_THETA = 10000.0
EPS = 1e-6
A_HEADS = 4
HEAD_DIM = 64
B_HEADS = 8
IDX_HEADS = 4
TOPK_MAX = 256
C_HEADS = 8
C_LEFT_CHUNKS = 8
REL_CLIP = 256
N_EXPERTS = 16
N_GROUPS = 4
EXPERTS_PER_GROUP = 4
N_BRANCHES = 3

LANES = 128
NEG = -1e30
LOG2E = math.log2(math.e)
VMEM_LIMIT = 56 * 1024 * 1024

COL_TILE = 512
PROJ_ROWS = 256
R_AQ, R_AK, R_BQ, R_IDX = 0, 1, 2, 3
P_AV, P_CQ, P_CK, P_CV, P_MISC = 0, 1, 2, 3, 4


def _cparams(sem):
    return pltpu.CompilerParams(dimension_semantics=sem, vmem_limit_bytes=VMEM_LIMIT)


def _mod_kernel(c_ref, w_ref, b_ref, o_ref):
    c = c_ref[...]
    ca = c * jax.nn.sigmoid(c)
    o_ref[0] = jnp.dot(ca, w_ref[0], preferred_element_type=F32) + b_ref[0]


def _modulation(c, w_mod, b_mod):
    depth, d, n6 = w_mod.shape
    bsz = c.shape[0]
    tn = 1024
    return pl.pallas_call(
        _mod_kernel,
        grid=(depth, n6 // tn),
        in_specs=[
            pl.BlockSpec((bsz, d), lambda l, j: (0, 0)),
            pl.BlockSpec((1, d, tn), lambda l, j: (l, 0, j)),
            pl.BlockSpec((1, 1, tn), lambda l, j: (l, 0, j)),
        ],
        out_specs=pl.BlockSpec((1, bsz, tn), lambda l, j: (l, 0, j)),
        out_shape=jax.ShapeDtypeStruct((depth, bsz, n6), F32),
        compiler_params=_cparams(("arbitrary", "arbitrary")),
        name="modulation",
    )(c, w_mod, b_mod.reshape(depth, 1, n6))


def _modulated_norm(x, g, sc, sh):
    y = x * lax.rsqrt(jnp.mean(x * x, axis=-1, keepdims=True) + EPS)
    return (y * g) * (1.0 + sc) + sh


def _norm_kernel(x_ref, g_ref, sc_ref, sh_ref, u_ref):
    u_ref[0] = _modulated_norm(x_ref[0], g_ref[...], sc_ref[0], sh_ref[0]).astype(BF16)


def _norm(x, g, sc, sh, tm):
    bsz, s, d = x.shape
    tok = lambda b, i: (b, i, 0)
    per_batch = lambda b, i: (b, 0, 0)
    return pl.pallas_call(
        _norm_kernel,
        grid=(bsz, s // tm),
        in_specs=[pl.BlockSpec((1, tm, d), tok), pl.BlockSpec((1, d), lambda b, i: (0, 0)),
                  pl.BlockSpec((1, 1, d), per_batch), pl.BlockSpec((1, 1, d), per_batch)],
        out_specs=pl.BlockSpec((1, tm, d), tok),
        out_shape=jax.ShapeDtypeStruct((bsz, s, d), BF16),
        compiler_params=_cparams(("arbitrary", "arbitrary")),
        name="norm",
    )(x, g.reshape(1, d), sc, sh)


def _proj_kernel(*refs, mode):
    if mode == "rope":
        u_ref, cos_ref, sin_ref, w_ref, o_ref = refs
    else:
        u_ref, w_ref, o_ref = refs
    tm = u_ref.shape[1]
    for r0 in range(0, tm, PROJ_ROWS):
        rows = slice(r0, r0 + PROJ_ROWS)
        acc = jnp.dot(u_ref[0, rows, :], w_ref[...], preferred_element_type=F32)
        if mode == "rope":
            cos, sin = cos_ref[0, rows, :], sin_ref[0, rows, :]
            groups = [acc[:, c:c + LANES] for c in range(0, COL_TILE, LANES)]
            acc = jnp.concatenate(
                [g * cos + pltpu.roll(g, LANES // 2, axis=1) * sin for g in groups], axis=1)
        elif mode == "gate":
            acc = 1.0 / (1.0 + jnp.exp(-acc))
        o_ref[0, rows, :] = acc.astype(BF16)


def _project(u, w, rope_tables, mode, tm):
    bsz, s, d = u.shape
    ncols = w.shape[1]
    tok = lambda b, i, j: (b, i, 0)
    in_specs = [pl.BlockSpec((1, tm, d), tok)]
    in_specs += [pl.BlockSpec((1, tm, LANES), tok) for _ in rope_tables]
    in_specs += [pl.BlockSpec((d, COL_TILE), lambda b, i, j: (0, j))]
    return pl.pallas_call(
        functools.partial(_proj_kernel, mode=mode),
        grid=(bsz, s // tm, ncols // COL_TILE),
        in_specs=in_specs,
        out_specs=pl.BlockSpec((1, tm, COL_TILE), lambda b, i, j: (b, i, j)),
        out_shape=jax.ShapeDtypeStruct((bsz, s, ncols), BF16),
        compiler_params=_cparams(("arbitrary", "arbitrary", "arbitrary")),
        name="proj_" + mode,
    )(u, *rope_tables, w)


def _pair_layout(w):
    half = HEAD_DIM // 2
    col = jnp.arange(w.shape[1])
    base, r = (col // LANES) * LANES, col % LANES
    src = base + ((r // half) % 2) * HEAD_DIM + (r // HEAD_DIM) * half + r % half
    return w[:, src]


def _build_weights(w_in):
    sizes = (512, 512, 512, 512, 64, 64, 256, 64, 4, 512, 512, 512, 3072)
    parts, start = [], 0
    for sz in sizes:
        parts.append(w_in[:, start:start + sz])
        start += sz
    aq, ak, av, bq, bk, bv, iq, ik, iw, cq, ck, cv, gates = parts
    d = w_in.shape[0]
    qscale = HEAD_DIM ** -0.5 * LOG2E
    iw_scale = IDX_HEADS ** -0.5 * HEAD_DIM ** -0.5
    zeros = lambda n: jnp.zeros((d, n), w_in.dtype)
    w_rope = _pair_layout(jnp.concatenate([aq * qscale, ak, bq * qscale, iq, bk, bk, ik, ik], axis=1))
    w_plain = jnp.concatenate([av, cq * qscale, ck, cv,
                               bv, bv, iw * iw_scale, zeros(LANES - IDX_HEADS),
                               zeros(COL_TILE - 2 * LANES)], axis=1)
    return w_rope.astype(BF16), gates.astype(BF16), w_plain.astype(BF16)


def _rope_tables(positions):
    half = HEAD_DIM // 2
    inv = ROPE_THETA ** (-jnp.arange(half, dtype=F32) / half)
    ang = positions.astype(F32)[..., None] * inv
    cos, sin = jnp.cos(ang), jnp.sin(ang)
    cos_t = jnp.tile(cos, (1, 1, LANES // half))
    sin_t = jnp.concatenate([-sin, -sin, sin, sin], axis=-1)
    return cos_t, sin_t


def _split_pair(pair, interleaved):
    lane = lax.broadcasted_iota(jnp.int32, pair.shape, 1)
    first = ((lane // (HEAD_DIM // 2)) % 2 == 0) if interleaved else (lane < HEAD_DIM)
    zero = jnp.zeros_like(pair)
    return jnp.where(first, pair, zero), jnp.where(first, zero, pair)


def _qk(q, k):
    return lax.dot_general(q, k, (((1,), (1,)), ((), ())), preferred_element_type=F32)


FOLD_CHAINS = 4


def _fold_rows(x, op):
    parts = x.reshape(x.shape[0] // 8, 8, x.shape[1])
    chains = [functools.reduce(op, [parts[g] for g in range(c, parts.shape[0], FOLD_CHAINS)])
              for c in range(min(FOLD_CHAINS, parts.shape[0]))]
    return functools.reduce(op, chains)


def _flash_t_update(s, vt_blocks, m_scr, acc_scr):
    m_old = m_scr[...]
    m_new = jnp.maximum(m_old, jnp.max(_fold_rows(s, jnp.maximum), axis=0, keepdims=True))
    alpha = jnp.exp2(m_old - m_new)
    pb = jnp.exp2(s - m_new).astype(BF16)
    m_scr[...] = m_new
    pv = functools.reduce(lambda a, b: a + b, [
        jnp.dot(vt, pb[k0:k1], preferred_element_type=F32) for (k0, k1), vt in vt_blocks])
    acc_scr[...] = alpha * acc_scr[...] + pv


ONES_ROWS = 16


def _with_ones_rows(vt):
    shape = vt.shape[:-2]
    return jnp.concatenate([vt, jnp.ones(shape + (1, vt.shape[-1]), vt.dtype),
                            jnp.zeros(shape + (ONES_ROWS - 1, vt.shape[-1]), vt.dtype)], axis=-2)


def _flash_update(s, v, m_scr, l_scr, acc_scr, rows=slice(None)):
    groups = [s[:, c:c + LANES] for c in range(0, s.shape[1], LANES)]
    m_old = m_scr[rows, :]
    lane_max = functools.reduce(jnp.maximum, groups)
    m_new = jnp.maximum(m_old, jnp.max(lane_max, axis=-1, keepdims=True))
    alpha = jnp.exp2(m_old - m_new)
    p_groups = [jnp.exp2(g - m_new) for g in groups]
    l_scr[rows, :] = alpha * l_scr[rows, :] + functools.reduce(lambda a, b: a + b, p_groups)
    m_scr[rows, :] = m_new
    pb = jnp.concatenate([g.astype(BF16) for g in p_groups], axis=1)
    acc_scr[rows, :] = alpha * acc_scr[rows, :] + jnp.dot(pb, v, preferred_element_type=F32)


def _diff_attn_kernel(lam_ref, q_ref, k_ref, v_ref, ng_ref, o_ref, m_scr, l_scr, acc_scr,
                      *, tq, tk, lam_init):
    i = pl.program_id(2)
    per_tile = tq // tk
    qs = jnp.concatenate(_split_pair(q_ref[0], True), axis=0)
    m_scr[...] = jnp.full(m_scr.shape, NEG, F32)
    l_scr[...] = jnp.zeros(l_scr.shape, F32)
    acc_scr[...] = jnp.zeros(acc_scr.shape, F32)

    def kv_block(j):
        start = pl.multiple_of(j * tk, tk)
        return k_ref[0, pl.ds(start, tk), :], v_ref[0, pl.ds(start, tk), :]

    def body(j, carry):
        k, v = kv_block(j)
        _flash_update(_qk(qs, k), v, m_scr, l_scr, acc_scr)
        return carry

    lax.fori_loop(0, i * per_tile, body, 0)
    for d in range(per_tile):
        k, v = kv_block(i * per_tile + d)
        for base in (0, tq):
            rows = slice(base + d * tk, base + tq)
            s = _qk(qs[rows], k)
            row = lax.broadcasted_iota(jnp.int32, s.shape, 0) + d * tk
            col = lax.broadcasted_iota(jnp.int32, s.shape, 1) + d * tk
            s = jnp.where((col // CHUNK) <= (row // CHUNK), s, NEG)
            _flash_update(s, v, m_scr, l_scr, acc_scr, rows)

    o = acc_scr[...] / jnp.sum(l_scr[...], axis=-1, keepdims=True)
    o = o[:tq] - lam_ref[0] * o[tq:]
    o = o * lax.rsqrt(jnp.mean(o * o, axis=-1, keepdims=True) + EPS)
    o_ref[0] = ((o * ng_ref[...]) * (1.0 - lam_init)).astype(BF16)


def _diff_attention(rope_slab, plain_slab, lam, norm_g, lam_init, tq, tk):
    bsz, s, _ = rope_slab.shape
    assert tq % tk == 0
    kern = functools.partial(_diff_attn_kernel, tq=tq, tk=tk, lam_init=lam_init)
    cb = COL_TILE // LANES
    return pl.pallas_call(
        kern,
        grid=(bsz, A_HEADS, s // tq),
        in_specs=[
            pl.BlockSpec(memory_space=pltpu.SMEM),
            pl.BlockSpec((1, tq, LANES), lambda b, h, i: (b, i, R_AQ * cb + h)),
            pl.BlockSpec((1, s, LANES), lambda b, h, i: (b, 0, R_AK * cb + h)),
            pl.BlockSpec((1, s, LANES), lambda b, h, i: (b, 0, P_AV * cb + h)),
            pl.BlockSpec((1, LANES), lambda b, h, i: (0, 0)),
        ],
        out_specs=pl.BlockSpec((1, tq, LANES), lambda b, h, i: (b, i, h)),
        out_shape=jax.ShapeDtypeStruct((bsz, s, A_HEADS * LANES), BF16),
        scratch_shapes=[pltpu.VMEM((2 * tq, LANES), F32), pltpu.VMEM((2 * tq, LANES), F32),
                        pltpu.VMEM((2 * tq, LANES), F32)],
        compiler_params=_cparams(("arbitrary", "arbitrary", "arbitrary")),
        name="diff_attn",
    )(lam.reshape(1), rope_slab, rope_slab, plain_slab, norm_g.reshape(1, LANES))


KEY_NEG_INF = -2139095040
KEY_POS_INF = 2139095040
SEARCH_MAX_STEPS = 80
NO_TIE_LIMIT = 1e9
P3_PAIRS = 2
SCAN_PAIRS = 4
VT_ROWS = HEAD_DIM + ONES_ROWS


def _key_to_float(key):
    bits = jnp.where(key >= 0, key, (key - 1) ^ jnp.int32(0x7FFFFFFF))
    return pltpu.bitcast(bits, F32)


def _float_to_key(t):
    bits = pltpu.bitcast(t, jnp.int32)
    return jnp.where(bits >= 0, bits, (bits ^ jnp.int32(0x7FFFFFFF)) + 1)


def _dsa_kernel(q_ref, iq_ref, iw_ref, k_ref, ik_ref, vt_ref, o_ref,
                sc_scr, qs_scr, m_scr, acc_scr, *, tq, topk):
    i = pl.program_id(1)
    nblk = i + 1
    npair = (nblk + 1) // 2
    nq = B_HEADS * tq

    iq = iq_ref[0]
    parts = []
    for p in range(IDX_HEADS // 2):
        parts += list(_split_pair(iq[:, p * LANES:(p + 1) * LANES], True))
    iqs = jnp.concatenate(parts, axis=0)
    iw_t = iw_ref[0].astype(F32).T

    def index_block(j):
        start = pl.multiple_of(j * tq, tq)
        ik = ik_ref[0, pl.ds(start, tq), :]
        score = None
        for h in range(IDX_HEADS):
            hs = jnp.maximum(_qk(ik, iqs[h * tq:(h + 1) * tq]), 0.0)
            score = iw_t[h:h + 1] * hs if score is None else score + iw_t[h:h + 1] * hs
        return score

    def p1_pair(jj, carry):
        sc_scr[jj, 0:tq, :] = index_block(2 * jj)
        sc_scr[jj, tq:2 * tq, :] = index_block(2 * jj + 1)
        return carry

    def p1_two_pairs(g, carry):
        return p1_pair(2 * g + 1, p1_pair(2 * g, carry))

    full_pairs = i // 2
    lax.fori_loop(0, full_pairs // 2, p1_two_pairs, 0)
    lax.fori_loop((full_pairs // 2) * 2, full_pairs, p1_pair, 0)
    score = index_block(i)
    key_i = lax.broadcasted_iota(jnp.int32, score.shape, 0)
    qry_i = lax.broadcasted_iota(jnp.int32, score.shape, 1)
    diagonal = jnp.where((key_i // CHUNK) <= (qry_i // CHUNK), score, -jnp.inf)

    @pl.when(i % 2 == 1)
    def _():
        sc_scr[npair - 1, 0:tq, :] = index_block(i - 1)
        sc_scr[npair - 1, tq:2 * tq, :] = diagonal

    @pl.when(i % 2 == 0)
    def _():
        sc_scr[npair - 1, 0:tq, :] = diagonal
        sc_scr[npair - 1, tq:2 * tq, :] = jnp.full((tq, tq), -jnp.inf, F32)

    kf = float(topk)

    def scan(hit_fn, ext_fn, ext_op, ext_init):
        def one_pair(jj, carry):
            cnt, ext = carry
            sblk = sc_scr[jj]
            cnt = cnt + _fold_rows(jnp.where(hit_fn(sblk), 1.0, 0.0), lambda a, b: a + b)
            if ext_fn is not None:
                ext = ext_op(ext, _fold_rows(ext_fn(sblk), ext_op))
            return cnt, ext

        def several_pairs(g, carry):
            for t in range(SCAN_PAIRS):
                carry = one_pair(SCAN_PAIRS * g + t, carry)
            return carry

        carry = lax.fori_loop(0, npair // SCAN_PAIRS, several_pairs,
                              (jnp.zeros((8, tq), F32), jnp.full((8, tq), ext_init, F32)))
        cnt, ext = lax.fori_loop((npair // SCAN_PAIRS) * SCAN_PAIRS, npair, one_pair, carry)
        return jnp.sum(cnt, axis=0, keepdims=True), ext

    def count_ge(t):
        return scan(lambda sblk: sblk >= t, None, None, 0.0)[0]

    c_ge0, ext = scan(lambda sblk: sblk >= 0.0, lambda sblk: sblk, jnp.maximum, -jnp.inf)
    col_max = jnp.max(ext, axis=0, keepdims=True)
    c_gt0, ext = scan(lambda sblk: sblk > 0.0,
                      lambda sblk: jnp.where(sblk > -jnp.inf, sblk, jnp.inf), jnp.minimum, jnp.inf)
    col_min = jnp.min(ext, axis=0, keepdims=True)
    qpos = i * tq + lax.broadcasted_iota(jnp.int32, (1, tq), 1)
    n_valid = (((qpos // CHUNK) + 1) * CHUNK).astype(F32)
    open_q = n_valid < kf
    above = c_gt0 >= kf
    below = c_ge0 < kf
    ikey = lambda v: jnp.full((1, tq), v, jnp.int32)
    lo_k = jnp.where(below, _float_to_key(col_min), ikey(0))
    hi_k = jnp.where(above, _float_to_key(col_max) + 1, jnp.where(below, ikey(0), ikey(1)))
    c_lo = jnp.where(below, n_valid, c_ge0)
    c_hi = jnp.where(above, 0.0, jnp.where(below, c_ge0, c_gt0))
    done = jnp.logical_or(open_q, jnp.logical_not(jnp.logical_or(above, below)))

    def next_probe(lo_k, hi_k, c_lo, c_hi, force_bisect):
        finite = jnp.logical_and(lo_k > KEY_NEG_INF, hi_k < KEY_POS_INF)
        t_lo, t_hi = _key_to_float(lo_k), _key_to_float(hi_k)
        log_lo = jnp.log(c_lo)
        frac = (log_lo - math.log(kf - 0.5)) / (log_lo - jnp.log(jnp.maximum(c_hi, 0.5)))
        k_interp = _float_to_key(t_lo + (t_hi - t_lo) * frac)
        k_mid = (lo_k & hi_k) + ((lo_k ^ hi_k) >> 1)
        interp = jnp.logical_and(finite, force_bisect == 0)
        k = jnp.where(interp, k_interp, k_mid)
        return jnp.minimum(jnp.maximum(k, lo_k + 1), hi_k - 1), interp

    def search_cond(st):
        return jnp.logical_and(st[0] < SEARCH_MAX_STEPS, jnp.min(st[5]) == 0)

    def search_step(st):
        step, lo_k, hi_k, c_lo, c_hi, done, k, interp = st
        c = count_ge(_key_to_float(k))
        ok = c >= kf
        live = done == 0
        new_lo = jnp.where(jnp.logical_and(live, ok), k, lo_k)
        new_hi = jnp.where(jnp.logical_and(live, jnp.logical_not(ok)), k, hi_k)
        new_c_lo = jnp.where(jnp.logical_and(live, ok), c, c_lo)
        new_c_hi = jnp.where(jnp.logical_and(live, jnp.logical_not(ok)), c, c_hi)
        width = lambda a, b: b.astype(F32) - a.astype(F32)
        slow = width(new_lo, new_hi) > 0.5 * width(lo_k, hi_k)
        force = jnp.logical_and(interp != 0, slow).astype(jnp.int32)
        finished = jnp.logical_or(new_c_lo == kf, new_hi - new_lo == 1)
        new_done = jnp.maximum(done, finished.astype(jnp.int32))
        new_k, new_interp = next_probe(new_lo, new_hi, new_c_lo, new_c_hi, force)
        return (step + 1, new_lo, new_hi, new_c_lo, new_c_hi, new_done, new_k,
                new_interp.astype(jnp.int32))

    k0, interp0 = next_probe(lo_k, hi_k, c_lo, c_hi, ikey(0))
    state = (jnp.int32(0), lo_k, hi_k, c_lo, c_hi, done.astype(jnp.int32), k0,
             interp0.astype(jnp.int32))
    _, lo_k, hi_k, c_lo, c_hi, _, _, _ = lax.while_loop(
        search_cond, lambda st: search_step(search_step(st)), state)
    tau = jnp.where(open_q, -jnp.inf, _key_to_float(lo_k))
    need = jnp.where(open_q, 0.0, jnp.where(c_lo == kf, NO_TIE_LIMIT, kf - c_hi))

    q = q_ref[0]
    for p in range(B_HEADS // 2):
        even, odd = _split_pair(q[:, p * LANES:(p + 1) * LANES], True)
        qs_scr[p * tq:(p + 1) * tq, :] = even
        qs_scr[(B_HEADS // 2 + p) * tq:(B_HEADS // 2 + p + 1) * tq, :] = odd
    m_scr[...] = jnp.full(m_scr.shape, NEG, F32)
    acc_scr[...] = jnp.zeros(acc_scr.shape, F32)
    key_i = lax.broadcasted_iota(jnp.int32, (tq, tq), 0)
    lower = (lax.broadcasted_iota(jnp.int32, (tq, tq), 1) <= key_i).astype(BF16)

    def select_bias(jj, tie_count):
        sblk = sc_scr[jj]
        tie = sblk == tau
        tie01 = jnp.where(tie, 1.0, 0.0).astype(BF16)
        rank_a = tie_count + jnp.dot(lower, tie01[0:tq], preferred_element_type=F32)
        rank_b = rank_a[tq - 1:tq, :] + jnp.dot(lower, tie01[tq:2 * tq], preferred_element_type=F32)
        rank = jnp.concatenate([rank_a, rank_b], axis=0)
        keep_tie = jnp.where(rank <= need, 0.0, NEG)
        return jnp.where(sblk > tau, 0.0, jnp.where(tie, keep_tie, NEG)), rank[2 * tq - 1:2 * tq, :]

    def p3_step(jj, pairs, tie_count):
        nkeys = pairs * 2 * tq
        start = pl.multiple_of(jj * 2 * tq, 2 * tq)
        biases = []
        for t in range(pairs):
            bias, tie_count = select_bias(jj + t, tie_count)
            biases.append(bias)
        bias = jnp.concatenate(biases, axis=0)
        s = _qk(k_ref[0, pl.ds(start, nkeys), :], qs_scr[...])
        s = jnp.concatenate([s[:, h * tq:(h + 1) * tq] + bias for h in range(B_HEADS)], axis=1)
        _flash_t_update(s, [((t * tq, (t + 1) * tq), vt_ref[0, 2 * jj + t]) for t in range(2 * pairs)],
                        m_scr, acc_scr)
        return tie_count

    tie_count = lax.fori_loop(0, npair // P3_PAIRS,
                              lambda g, tc: p3_step(g * P3_PAIRS, P3_PAIRS, tc),
                              jnp.zeros((1, tq), F32))
    for r in range(1, P3_PAIRS):
        @pl.when(npair % P3_PAIRS == r)
        def _():
            p3_step(npair - r, r, tie_count)

    acc = acc_scr[...]
    o = acc[0:HEAD_DIM] / acc[HEAD_DIM:HEAD_DIM + 1]
    half_cols = (B_HEADS // 2) * tq
    for p in range(B_HEADS // 2):
        pair = jnp.concatenate([o[:, p * tq:(p + 1) * tq],
                                o[:, half_cols + p * tq:half_cols + (p + 1) * tq]], axis=0)
        o_ref[0, :, p * LANES:(p + 1) * LANES] = pair.T.astype(BF16)


def _dsa_attention(rope_slab, plain_slab, tq):
    bsz, s, _ = rope_slab.shape
    topk = min(TOPK_MAX, s // 4)
    nblk = s // tq
    assert tq >= topk and s % tq == 0 and nblk % 2 == 0
    kern = functools.partial(_dsa_kernel, tq=tq, topk=topk)
    idx0 = R_IDX * COL_TILE
    misc0 = P_MISC * COL_TILE
    iq_w = IDX_HEADS * HEAD_DIM
    width = B_HEADS * HEAD_DIM
    vt = _with_ones_rows(
        plain_slab[:, :, misc0:misc0 + HEAD_DIM].reshape(bsz, nblk, tq, HEAD_DIM).transpose(0, 1, 3, 2))
    return pl.pallas_call(
        kern,
        grid=(bsz, s // tq),
        in_specs=[
            pl.BlockSpec((1, tq, COL_TILE), lambda b, i: (b, i, R_BQ)),
            pl.BlockSpec((1, tq, iq_w), lambda b, i: (b, i, idx0 // iq_w)),
            pl.BlockSpec((1, tq, LANES), lambda b, i: (b, i, (misc0 + LANES) // LANES)),
            pl.BlockSpec((1, s, LANES), lambda b, i: (b, 0, (idx0 + iq_w) // LANES)),
            pl.BlockSpec((1, s, LANES), lambda b, i: (b, 0, (idx0 + iq_w + LANES) // LANES)),
            pl.BlockSpec((1, nblk, VT_ROWS, tq), lambda b, i: (b, 0, 0, 0)),
        ],
        out_specs=pl.BlockSpec((1, tq, width), lambda b, i: (b, i, 0)),
        out_shape=jax.ShapeDtypeStruct((bsz, s, width), BF16),
        scratch_shapes=[
            pltpu.VMEM((nblk // 2, 2 * tq, tq), F32),
            pltpu.VMEM((B_HEADS * tq, LANES), BF16),
            pltpu.VMEM((1, B_HEADS * tq), F32),
            pltpu.VMEM((VT_ROWS, B_HEADS * tq), F32),
        ],
        compiler_params=_cparams(("arbitrary", "arbitrary")),
        name="dsa_attn",
    )(rope_slab, rope_slab, plain_slab, rope_slab, rope_slab, vt)


BAND_TQ = 2 * CHUNK
BAND_KEYS = (C_LEFT_CHUNKS + BAND_TQ // CHUNK) * CHUNK
BAND_PAD = C_LEFT_CHUNKS * CHUNK


def _band_kernel(q_ref, k_ref, v_ref, bias_ref, o_ref):
    i = pl.program_id(1)
    tq = BAND_TQ
    start = pl.multiple_of(i * tq, tq)
    col = lax.broadcasted_iota(jnp.int32, (2 * tq, BAND_KEYS), 1)
    key_ok = col + start >= BAND_PAD
    for p in range(C_HEADS // 2):
        lanes = slice(p * LANES, (p + 1) * LANES)
        k = k_ref[0, pl.ds(start, BAND_KEYS), lanes]
        v = v_ref[0, pl.ds(start, BAND_KEYS), lanes]
        qs = jnp.concatenate(_split_pair(q_ref[0, :, lanes], False), axis=0)
        s = _qk(qs, k) + bias_ref[2 * p:2 * p + 2].reshape(2 * tq, BAND_KEYS)
        s = jnp.where(key_ok, s, NEG)
        m = jnp.max(s, axis=-1, keepdims=True)
        e = jnp.exp2(s - m)
        l = jnp.sum(e, axis=-1, keepdims=True)
        eb = e.astype(BF16)
        v_even, v_odd = _split_pair(v, False)
        o = (jnp.dot(eb[:tq], v_even, preferred_element_type=F32) / l[:tq]
             + jnp.dot(eb[tq:], v_odd, preferred_element_type=F32) / l[tq:])
        o_ref[0, :, lanes] = o.astype(BF16)


def _band_bias(rel_bias):
    r = jnp.arange(BAND_TQ)[:, None]
    cidx = jnp.arange(BAND_KEYS)[None, :]
    n_diag = BAND_TQ + BAND_KEYS - 1
    rel = BAND_PAD - (BAND_KEYS - 1) + jnp.arange(n_diag)
    g = rel_bias.astype(F32)[:, jnp.clip(rel, -REL_CLIP, REL_CLIP) + REL_CLIP] * LOG2E
    skew = jnp.tile(g, (1, BAND_TQ + 2))[:, :BAND_TQ * (n_diag + 1)]
    bias = skew.reshape(-1, BAND_TQ, n_diag + 1)[:, :, :BAND_KEYS][:, :, ::-1]
    dchunk = (r // CHUNK + C_LEFT_CHUNKS) - cidx // CHUNK
    in_band = jnp.logical_and(dchunk >= 0, dchunk <= C_LEFT_CHUNKS)
    return jnp.where(in_band[None], bias, NEG)


def _band_attention(plain_slab, kp, vp, bias):
    bsz, s, _ = plain_slab.shape
    tq = BAND_TQ
    sp = kp.shape[1]
    width = C_HEADS * HEAD_DIM
    return pl.pallas_call(
        _band_kernel,
        grid=(bsz, s // tq),
        in_specs=[
            pl.BlockSpec((1, tq, width), lambda b, i: (b, i, P_CQ)),
            pl.BlockSpec((1, sp, width), lambda b, i: (b, 0, 0)),
            pl.BlockSpec((1, sp, width), lambda b, i: (b, 0, 0)),
            pl.BlockSpec((C_HEADS, tq, BAND_KEYS), lambda b, i: (0, 0, 0)),
        ],
        out_specs=pl.BlockSpec((1, tq, width), lambda b, i: (b, i, 0)),
        out_shape=jax.ShapeDtypeStruct((bsz, s, width), BF16),
        compiler_params=_cparams(("arbitrary", "arbitrary")),
        name="band_attn",
    )(plain_slab, kp, vp, bias)


def _merge_kernel(x_ref, g1_ref, ya_ref, yb_ref, yc_ref, ga_ref, gb_ref, gc_ref,
                  wa_ref, wb_ref, wc_ref, wo_ref, o_ref):
    for r0 in range(0, x_ref.shape[1], PROJ_ROWS):
        rows = slice(r0, r0 + PROJ_ROWS)

        def branch(y_ref, w_ref, gate_ref):
            return gate_ref[0, rows, :].astype(F32) * jnp.dot(y_ref[0, rows, :], w_ref[...],
                                                             preferred_element_type=F32)

        merged = (branch(ya_ref, wa_ref, ga_ref) + branch(yb_ref, wb_ref, gb_ref)
                  + branch(yc_ref, wc_ref, gc_ref))
        mixed = jnp.dot(merged.astype(BF16), wo_ref[...], preferred_element_type=F32)
        o_ref[0, rows, :] = x_ref[0, rows, :] + g1_ref[0] * mixed


def _merge(x, g1, ya, yb, yc, gate_slab, wa, wb, wc, wo, tm):
    bsz, s, d = x.shape
    tok = lambda b, i: (b, i, 0)
    full = lambda b, i: (0, 0)
    return pl.pallas_call(
        _merge_kernel,
        grid=(bsz, s // tm),
        in_specs=[
            pl.BlockSpec((1, tm, d), tok),
            pl.BlockSpec((1, 1, d), lambda b, i: (b, 0, 0)),
            pl.BlockSpec((1, tm, ya.shape[2]), tok),
            pl.BlockSpec((1, tm, yb.shape[2]), tok),
            pl.BlockSpec((1, tm, yc.shape[2]), tok),
            pl.BlockSpec((1, tm, d), lambda b, i: (b, i, 0)),
            pl.BlockSpec((1, tm, d), lambda b, i: (b, i, 1)),
            pl.BlockSpec((1, tm, d), lambda b, i: (b, i, 2)),
            pl.BlockSpec(wa.shape, full),
            pl.BlockSpec(wb.shape, full),
            pl.BlockSpec(wc.shape, full),
            pl.BlockSpec(wo.shape, full),
        ],
        out_specs=pl.BlockSpec((1, tm, d), tok),
        out_shape=jax.ShapeDtypeStruct((bsz, s, d), F32),
        compiler_params=_cparams(("arbitrary", "arbitrary")),
        name="merge",
    )(x, g1, ya, yb, yc, gate_slab, gate_slab, gate_slab, wa, wb, wc, wo)


def _router_kernel(x_ref, g_ref, sc_ref, sh_ref, rw_ref, rb_ref, u_ref, comb_ref):
    u = _modulated_norm(x_ref[0], g_ref[...], sc_ref[0], sh_ref[0])
    u_ref[0] = u.astype(BF16)
    logits = lax.dot_general(rw_ref[...], u, (((1,), (1,)), ((), ())),
                             preferred_element_type=F32, precision=lax.Precision.HIGHEST)
    aff = jax.nn.sigmoid(logits)
    sel = aff + rb_ref[...]
    rows = [sel[e:e + 1] for e in range(N_EXPERTS)]
    gscore = []
    for g in range(N_GROUPS):
        r = rows[g * EXPERTS_PER_GROUP:(g + 1) * EXPERTS_PER_GROUP]
        best = None
        for a in range(EXPERTS_PER_GROUP):
            for b in range(a + 1, EXPERTS_PER_GROUP):
                pair = r[a] + r[b]
                best = pair if best is None else jnp.maximum(best, pair)
        gscore.append(best)
    gmax = functools.reduce(jnp.maximum, gscore)
    taken = jnp.zeros_like(gmax) > 1.0
    in_best = []
    for g in range(N_GROUPS):
        is_g = jnp.logical_and(gscore[g] == gmax, jnp.logical_not(taken))
        in_best.append(is_g)
        taken = jnp.logical_or(taken, is_g)
    keep = []
    for e in range(N_EXPERTS):
        g = e // EXPERTS_PER_GROUP
        rank = jnp.zeros_like(gmax)
        for o in range(g * EXPERTS_PER_GROUP, (g + 1) * EXPERTS_PER_GROUP):
            if o == e:
                continue
            ahead = rows[o] > rows[e] if o > e else rows[o] >= rows[e]
            rank = rank + jnp.where(ahead, 1.0, 0.0)
        keep.append(jnp.logical_and(in_best[g], rank < 2.0))
    w = [jnp.where(keep[e], aff[e:e + 1], 0.0) for e in range(N_EXPERTS)]
    total = functools.reduce(lambda a, b: a + b, w)
    comb = jnp.concatenate([we / total for we in w]
                           + [jnp.zeros((LANES - N_EXPERTS, total.shape[1]), F32)], axis=0)
    comb_ref[0] = comb.T


def _router(x, g, sc, sh, router_w, router_b, tm):
    bsz, s, d = x.shape
    tok = lambda b, i: (b, i, 0)
    return pl.pallas_call(
        _router_kernel,
        grid=(bsz, s // tm),
        in_specs=[
            pl.BlockSpec((1, tm, d), tok),
            pl.BlockSpec((1, d), lambda b, i: (0, 0)),
            pl.BlockSpec((1, 1, d), lambda b, i: (b, 0, 0)),
            pl.BlockSpec((1, 1, d), lambda b, i: (b, 0, 0)),
            pl.BlockSpec((N_EXPERTS, d), lambda b, i: (0, 0)),
            pl.BlockSpec((N_EXPERTS, 1), lambda b, i: (0, 0)),
        ],
        out_specs=[pl.BlockSpec((1, tm, d), tok), pl.BlockSpec((1, tm, LANES), tok)],
        out_shape=[jax.ShapeDtypeStruct((bsz, s, d), BF16),
                   jax.ShapeDtypeStruct((bsz, s, LANES), F32)],
        compiler_params=_cparams(("arbitrary", "arbitrary")),
        name="router",
    )(x, g.reshape(1, d), sc, sh, router_w.T, router_b.reshape(N_EXPERTS, 1))


MOE_SUB = 512
MOE_CAP = 128


def _expert_ffn(rows, w1_ref, w3_ref, w2_ref):
    h1 = jnp.dot(rows, w1_ref[0], preferred_element_type=F32)
    h3 = jnp.dot(rows, w3_ref[0], preferred_element_type=F32)
    h = (h1 * jax.nn.sigmoid(h1)) * h3
    return jnp.dot(h.astype(BF16), w2_ref[0], preferred_element_type=F32)


def _moe_kernel(x_ref, g2_ref, fg_ref, u_ref, comb_ref, w1_ref, w3_ref, w2_ref, o_ref,
                acc_scr, pos_scr, pos_t_scr, comb_t_scr, scatter_scr, y_scr, fullest_smem,
                *, final_norm):
    e = pl.program_id(2)
    tm = u_ref.shape[1]
    sub = min(MOE_SUB, tm)
    subs = [slice(r0, r0 + sub) for r0 in range(0, tm, sub)]

    @pl.when(e == 0)
    def _():
        acc_scr[...] = jnp.zeros(acc_scr.shape, F32)
        before = (lax.broadcasted_iota(jnp.int32, (sub, sub), 1)
                  < lax.broadcasted_iota(jnp.int32, (sub, sub), 0)).astype(BF16)
        fullest = jnp.zeros((1, LANES), F32)
        for rows in subs:
            comb0 = comb_ref[0, rows, :]
            member = jnp.where(comb0 > 0.0, 1.0, 0.0)
            pos = jnp.dot(before, member.astype(BF16), preferred_element_type=F32)
            pos_scr[rows, :] = pos
            pos_t_scr[:, rows] = pos.T
            comb_t_scr[:, rows] = comb0.T
            fullest = jnp.maximum(fullest, jnp.sum(member, axis=0, keepdims=True))
        for k in range(N_EXPERTS):
            fullest_smem[k] = fullest[0, k]

    comb = comb_ref[0]
    lane = lax.broadcasted_iota(jnp.int32, comb.shape, 1)
    ce = jnp.sum(jnp.where(lane == e, comb, 0.0), axis=-1, keepdims=True)
    pe = jnp.sum(jnp.where(lane == e, pos_scr[...], 0.0), axis=-1, keepdims=True)
    routed = fullest_smem[e] <= float(MOE_CAP)
    slots = pl.ds(pl.multiple_of(e * MOE_CAP, MOE_CAP), MOE_CAP)

    @pl.when(routed)
    def _():
        slot_l = lax.broadcasted_iota(jnp.int32, (1, MOE_CAP), 1).astype(F32)
        scatter_scr[e] = jnp.where(jnp.logical_and(ce > 0.0, pe == slot_l), 1.0, 0.0).astype(BF16)
        slot_s = lax.broadcasted_iota(jnp.int32, (MOE_CAP, 1), 0).astype(F32)
        packed, weights = [], []
        for rows in subs:
            ce_t = comb_t_scr[pl.ds(e, 1), rows]
            pe_t = pos_t_scr[pl.ds(e, 1), rows]
            hit = jnp.logical_and(ce_t > 0.0, pe_t == slot_s)
            weights.append(jnp.sum(jnp.where(hit, ce_t, 0.0), axis=-1, keepdims=True))
            packed.append(jnp.dot(jnp.where(hit, 1.0, 0.0).astype(BF16), u_ref[0, rows, :],
                                  preferred_element_type=F32).astype(BF16))
        y = jnp.concatenate(weights, axis=0) * _expert_ffn(jnp.concatenate(packed, axis=0),
                                                           w1_ref, w3_ref, w2_ref)
        for k in range(len(subs)):
            y_scr[k, slots, :] = y[k * MOE_CAP:(k + 1) * MOE_CAP].astype(BF16)

    @pl.when(jnp.logical_not(routed))
    def _():
        scatter_scr[e] = jnp.zeros(scatter_scr.shape[1:], BF16)
        for k in range(len(subs)):
            y_scr[k, slots, :] = jnp.zeros((MOE_CAP, y_scr.shape[2]), BF16)
        acc_scr[...] += ce * _expert_ffn(u_ref[0], w1_ref, w3_ref, w2_ref)

    @pl.when(e == pl.num_programs(2) - 1)
    def _():
        for k, rows in enumerate(subs):
            scatter = jnp.concatenate([scatter_scr[j, rows, :] for j in range(N_EXPERTS)], axis=1)
            y = acc_scr[rows, :] + jnp.dot(scatter, y_scr[k], preferred_element_type=F32)
            out = x_ref[0, rows, :] + g2_ref[0] * y
            if final_norm:
                out = (out * lax.rsqrt(jnp.mean(out * out, axis=-1, keepdims=True) + EPS)) * fg_ref[...]
            o_ref[0, rows, :] = out


def _moe(x, g2, final_g, u, comb, w1, w3, w2, tm, final_norm):
    bsz, s, d = x.shape
    ne, _, dff = w1.shape
    assert ne == N_EXPERTS
    tok = lambda b, i, e: (b, i, 0)
    return pl.pallas_call(
        functools.partial(_moe_kernel, final_norm=final_norm),
        grid=(bsz, s // tm, ne),
        in_specs=[
            pl.BlockSpec((1, tm, d), tok),
            pl.BlockSpec((1, 1, d), lambda b, i, e: (b, 0, 0)),
            pl.BlockSpec((1, d), lambda b, i, e: (0, 0)),
            pl.BlockSpec((1, tm, d), tok),
            pl.BlockSpec((1, tm, LANES), tok),
            pl.BlockSpec((1, d, dff), lambda b, i, e: (e, 0, 0)),
            pl.BlockSpec((1, d, dff), lambda b, i, e: (e, 0, 0)),
            pl.BlockSpec((1, dff, d), lambda b, i, e: (e, 0, 0)),
        ],
        out_specs=pl.BlockSpec((1, tm, d), tok),
        out_shape=jax.ShapeDtypeStruct((bsz, s, d), F32),
        scratch_shapes=[
            pltpu.VMEM((tm, d), F32),
            pltpu.VMEM((tm, LANES), F32),
            pltpu.VMEM((LANES, tm), F32),
            pltpu.VMEM((LANES, tm), F32),
            pltpu.VMEM((ne, tm, MOE_CAP), BF16),
            pltpu.VMEM((pl.cdiv(tm, MOE_SUB), ne * MOE_CAP, d), BF16),
            pltpu.SMEM((ne,), F32),
        ],
        compiler_params=_cparams(("arbitrary", "arbitrary", "arbitrary")),
        name="moe",
    )(x, g2, final_g.reshape(1, d), u, comb, w1, w3, w2)


def _tile(s, want):
    t = min(want, s)
    assert s % t == 0
    return t


def _tiles(s):
    return dict(
        rows=_tile(s, 1024),
        diff_q=_tile(s, 1024),
        diff_k=_tile(s, 512),
        dsa_q=_tile(s, 256),
        merge=_tile(s, 512),
    )


def kernel(x, c, positions, norm1_g, norm2_g, w_mod, b_mod, w_in, lambda_q1, lambda_k1, lambda_q2, lambda_k2, a_norm_g, c_rel_bias, w_branch_a, w_branch_b, w_branch_c, w_out, router_w, router_b, exp_w1, exp_w3, exp_w2, final_g):
    bsz, s, d = x.shape
    depth = w_mod.shape[0]
    t = _tiles(s)
    tm = t["rows"]

    mod = _modulation(c, w_mod, b_mod)
    rope_tables = _rope_tables(positions)

    for layer in range(depth):
        lam_init = 0.8 - 0.6 * math.exp(-0.3 * layer)
        sh1, sc1, g1, sh2, sc2, g2 = [m[:, None, :] for m in jnp.split(mod[layer], 6, axis=-1)]
        lam = (jnp.exp(jnp.sum(lambda_q1[layer] * lambda_k1[layer]))
               - jnp.exp(jnp.sum(lambda_q2[layer] * lambda_k2[layer])) + lam_init)

        w_rope, w_gate, w_plain = _build_weights(w_in[layer])
        u = _norm(x, norm1_g[layer], sc1, sh1, tm)
        rope_slab = _project(u, w_rope, rope_tables, "rope", tm)
        gate_slab = _project(u, w_gate, (), "gate", tm)
        plain_slab = _project(u, w_plain, (), "plain", tm)

        ya = _diff_attention(rope_slab, plain_slab, lam, a_norm_g[layer], lam_init,
                             t["diff_q"], t["diff_k"])
        yb = _dsa_attention(rope_slab, plain_slab, t["dsa_q"])
        pad = ((0, 0), (BAND_PAD, 0), (0, 0))
        kp = jnp.pad(plain_slab[:, :, P_CK * COL_TILE:(P_CK + 1) * COL_TILE], pad)
        vp = jnp.pad(plain_slab[:, :, P_CV * COL_TILE:(P_CV + 1) * COL_TILE], pad)
        yc = _band_attention(plain_slab, kp, vp, _band_bias(c_rel_bias[layer]))
        x = _merge(x, g1, ya, yb, yc, gate_slab,
                   w_branch_a[layer].astype(BF16), w_branch_b[layer].astype(BF16),
                   w_branch_c[layer].astype(BF16), w_out[layer].astype(BF16), t["merge"])

        u, comb = _router(x, norm2_g[layer], sc2, sh2, router_w, router_b, t["merge"])
        x = _moe(x, g2, final_g, u, comb, exp_w1[layer].astype(BF16), exp_w3[layer].astype(BF16),
                 exp_w2[layer].astype(BF16), tm, final_norm=(layer == depth - 1))

    return x
```

```python
import functools
import math

import jax
import jax.numpy as jnp
from jax import lax
from jax.experimental import pallas as pl
from jax.experimental.pallas import tpu as pltpu

F32 = jnp.float32
BF16 = jnp.bfloat16

CHUNK = 64
ROPE_THETA = 10000.0
EPS = 1e-6
A_HEADS = 4
HEAD_DIM = 64
B_HEADS = 8
IDX_HEADS = 4
TOPK_MAX = 256
C_HEADS = 8
C_LEFT_CHUNKS = 8
REL_CLIP = 256
N_EXPERTS = 16
N_GROUPS = 4
EXPERTS_PER_GROUP = 4
N_BRANCHES = 3

LANES = 128
NEG = -1e30
LOG2E = math.log2(math.e)
VMEM_LIMIT = 56 * 1024 * 1024

COL_TILE = 512
PROJ_ROWS = 256
R_AQ, R_AK, R_BQ, R_IDX = 0, 1, 2, 3
P_AV, P_CQ, P_CK, P_CV, P_MISC = 0, 1, 2, 3, 4


def _cparams(sem):
    return pltpu.CompilerParams(dimension_semantics=sem, vmem_limit_bytes=VMEM_LIMIT)


def _mod_kernel(c_ref, w_ref, b_ref, o_ref):
    c = c_ref[...]
    ca = c * jax.nn.sigmoid(c)
    o_ref[0] = jnp.dot(ca, w_ref[0], preferred_element_type=F32) + b_ref[0]


def _modulation(c, w_mod, b_mod):
    depth, d, n6 = w_mod.shape
    bsz = c.shape[0]
    tn = 1024
    return pl.pallas_call(
        _mod_kernel,
        grid=(depth, n6 // tn),
        in_specs=[
            pl.BlockSpec((bsz, d), lambda l, j: (0, 0)),
            pl.BlockSpec((1, d, tn), lambda l, j: (l, 0, j)),
            pl.BlockSpec((1, 1, tn), lambda l, j: (l, 0, j)),
        ],
        out_specs=pl.BlockSpec((1, bsz, tn), lambda l, j: (l, 0, j)),
        out_shape=jax.ShapeDtypeStruct((depth, bsz, n6), F32),
        compiler_params=_cparams(("arbitrary", "arbitrary")),
        name="modulation",
    )(c, w_mod, b_mod.reshape(depth, 1, n6))


def _modulated_norm(x, g, sc, sh):
    y = x * lax.rsqrt(jnp.mean(x * x, axis=-1, keepdims=True) + EPS)
    return (y * g) * (1.0 + sc) + sh


def _norm_kernel(x_ref, g_ref, sc_ref, sh_ref, u_ref):
    u_ref[0] = _modulated_norm(x_ref[0], g_ref[...], sc_ref[0], sh_ref[0]).astype(BF16)


def _norm(x, g, sc, sh, tm):
    bsz, s, d = x.shape
    tok = lambda b, i: (b, i, 0)
    per_batch = lambda b, i: (b, 0, 0)
    return pl.pallas_call(
        _norm_kernel,
        grid=(bsz, s // tm),
        in_specs=[pl.BlockSpec((1, tm, d), tok), pl.BlockSpec((1, d), lambda b, i: (0, 0)),
                  pl.BlockSpec((1, 1, d), per_batch), pl.BlockSpec((1, 1, d), per_batch)],
        out_specs=pl.BlockSpec((1, tm, d), tok),
        out_shape=jax.ShapeDtypeStruct((bsz, s, d), BF16),
        compiler_params=_cparams(("arbitrary", "arbitrary")),
        name="norm",
    )(x, g.reshape(1, d), sc, sh)


def _proj_kernel(*refs, mode):
    if mode == "rope":
        u_ref, cos_ref, sin_ref, w_ref, o_ref = refs
    else:
        u_ref, w_ref, o_ref = refs
    tm = u_ref.shape[1]
    for r0 in range(0, tm, PROJ_ROWS):
        rows = slice(r0, r0 + PROJ_ROWS)
        acc = jnp.dot(u_ref[0, rows, :], w_ref[...], preferred_element_type=F32)
        if mode == "rope":
            cos, sin = cos_ref[0, rows, :], sin_ref[0, rows, :]
            groups = [acc[:, c:c + LANES] for c in range(0, COL_TILE, LANES)]
            acc = jnp.concatenate(
                [g * cos + pltpu.roll(g, LANES // 2, axis=1) * sin for g in groups], axis=1)
        elif mode == "gate":
            acc = 1.0 / (1.0 + jnp.exp(-acc))
        o_ref[0, rows, :] = acc.astype(BF16)


def _project(u, w, rope_tables, mode, tm):
    bsz, s, d = u.shape
    ncols = w.shape[1]
    tok = lambda b, i, j: (b, i, 0)
    in_specs = [pl.BlockSpec((1, tm, d), tok)]
    in_specs += [pl.BlockSpec((1, tm, LANES), tok) for _ in rope_tables]
    in_specs += [pl.BlockSpec((d, COL_TILE), lambda b, i, j: (0, j))]
    return pl.pallas_call(
        functools.partial(_proj_kernel, mode=mode),
        grid=(bsz, s // tm, ncols // COL_TILE),
        in_specs=in_specs,
        out_specs=pl.BlockSpec((1, tm, COL_TILE), lambda b, i, j: (b, i, j)),
        out_shape=jax.ShapeDtypeStruct((bsz, s, ncols), BF16),
        compiler_params=_cparams(("arbitrary", "arbitrary", "arbitrary")),
        name="proj_" + mode,
    )(u, *rope_tables, w)


def _pair_layout(w):
    half = HEAD_DIM // 2
    col = jnp.arange(w.shape[1])
    base, r = (col // LANES) * LANES, col % LANES
    src = base + ((r // half) % 2) * HEAD_DIM + (r // HEAD_DIM) * half + r % half
    return w[:, src]


def _build_weights(w_in):
    sizes = (512, 512, 512, 512, 64, 64, 256, 64, 4, 512, 512, 512, 3072)
    parts, start = [], 0
    for sz in sizes:
        parts.append(w_in[:, start:start + sz])
        start += sz
    aq, ak, av, bq, bk, bv, iq, ik, iw, cq, ck, cv, gates = parts
    d = w_in.shape[0]
    qscale = HEAD_DIM ** -0.5 * LOG2E
    iw_scale = IDX_HEADS ** -0.5 * HEAD_DIM ** -0.5
    zeros = lambda n: jnp.zeros((d, n), w_in.dtype)
    w_rope = _pair_layout(jnp.concatenate([aq * qscale, ak, bq * qscale, iq, bk, bk, ik, ik], axis=1))
    w_plain = jnp.concatenate([av, cq * qscale, ck, cv,
                               bv, bv, iw * iw_scale, zeros(LANES - IDX_HEADS),
                               zeros(COL_TILE - 2 * LANES)], axis=1)
    return w_rope.astype(BF16), gates.astype(BF16), w_plain.astype(BF16)


def _rope_tables(positions):
    half = HEAD_DIM // 2
    inv = ROPE_THETA ** (-jnp.arange(half, dtype=F32) / half)
    ang = positions.astype(F32)[..., None] * inv
    cos, sin = jnp.cos(ang), jnp.sin(ang)
    cos_t = jnp.tile(cos, (1, 1, LANES // half))
    sin_t = jnp.concatenate([-sin, -sin, sin, sin], axis=-1)
    return cos_t, sin_t


def _split_pair(pair, interleaved):
    lane = lax.broadcasted_iota(jnp.int32, pair.shape, 1)
    first = ((lane // (HEAD_DIM // 2)) % 2 == 0) if interleaved else (lane < HEAD_DIM)
    zero = jnp.zeros_like(pair)
    return jnp.where(first, pair, zero), jnp.where(first, zero, pair)


def _qk(q, k):
    return lax.dot_general(q, k, (((1,), (1,)), ((), ())), preferred_element_type=F32)


FOLD_CHAINS = 4


def _fold_rows(x, op):
    parts = x.reshape(x.shape[0] // 8, 8, x.shape[1])
    chains = [functools.reduce(op, [parts[g] for g in range(c, parts.shape[0], FOLD_CHAINS)])
              for c in range(min(FOLD_CHAINS, parts.shape[0]))]
    return functools.reduce(op, chains)


def _flash_t_update(s, vt_blocks, m_scr, acc_scr):
    m_old = m_scr[...]
    m_new = jnp.maximum(m_old, jnp.max(_fold_rows(s, jnp.maximum), axis=0, keepdims=True))
    alpha = jnp.exp2(m_old - m_new)
    pb = jnp.exp2(s - m_new).astype(BF16)
    m_scr[...] = m_new
    pv = functools.reduce(lambda a, b: a + b, [
        jnp.dot(vt, pb[k0:k1], preferred_element_type=F32) for (k0, k1), vt in vt_blocks])
    acc_scr[...] = alpha * acc_scr[...] + pv


ONES_ROWS = 16


def _with_ones_rows(vt):
    shape = vt.shape[:-2]
    return jnp.concatenate([vt, jnp.ones(shape + (1, vt.shape[-1]), vt.dtype),
                            jnp.zeros(shape + (ONES_ROWS - 1, vt.shape[-1]), vt.dtype)], axis=-2)


def _flash_update(s, v, m_scr, l_scr, acc_scr, rows=slice(None)):
    groups = [s[:, c:c + LANES] for c in range(0, s.shape[1], LANES)]
    m_old = m_scr[rows, :]
    lane_max = functools.reduce(jnp.maximum, groups)
    m_new = jnp.maximum(m_old, jnp.max(lane_max, axis=-1, keepdims=True))
    alpha = jnp.exp2(m_old - m_new)
    p_groups = [jnp.exp2(g - m_new) for g in groups]
    l_scr[rows, :] = alpha * l_scr[rows, :] + functools.reduce(lambda a, b: a + b, p_groups)
    m_scr[rows, :] = m_new
    pb = jnp.concatenate([g.astype(BF16) for g in p_groups], axis=1)
    acc_scr[rows, :] = alpha * acc_scr[rows, :] + jnp.dot(pb, v, preferred_element_type=F32)


def _diff_attn_kernel(lam_ref, q_ref, k_ref, v_ref, ng_ref, o_ref, m_scr, l_scr, acc_scr,
                      *, tq, tk, lam_init):
    i = pl.program_id(2)
    per_tile = tq // tk
    qs = jnp.concatenate(_split_pair(q_ref[0], True), axis=0)
    m_scr[...] = jnp.full(m_scr.shape, NEG, F32)
    l_scr[...] = jnp.zeros(l_scr.shape, F32)
    acc_scr[...] = jnp.zeros(acc_scr.shape, F32)

    def kv_block(j):
        start = pl.multiple_of(j * tk, tk)
        return k_ref[0, pl.ds(start, tk), :], v_ref[0, pl.ds(start, tk), :]

    def body(j, carry):
        k, v = kv_block(j)
        _flash_update(_qk(qs, k), v, m_scr, l_scr, acc_scr)
        return carry

    lax.fori_loop(0, i * per_tile, body, 0)
    for d in range(per_tile):
        k, v = kv_block(i * per_tile + d)
        for base in (0, tq):
            rows = slice(base + d * tk, base + tq)
            s = _qk(qs[rows], k)
            row = lax.broadcasted_iota(jnp.int32, s.shape, 0) + d * tk
            col = lax.broadcasted_iota(jnp.int32, s.shape, 1) + d * tk
            s = jnp.where((col // CHUNK) <= (row // CHUNK), s, NEG)
            _flash_update(s, v, m_scr, l_scr, acc_scr, rows)

    o = acc_scr[...] / jnp.sum(l_scr[...], axis=-1, keepdims=True)
    o = o[:tq] - lam_ref[0] * o[tq:]
    o = o * lax.rsqrt(jnp.mean(o * o, axis=-1, keepdims=True) + EPS)
    o_ref[0] = ((o * ng_ref[...]) * (1.0 - lam_init)).astype(BF16)


def _diff_attention(rope_slab, plain_slab, lam, norm_g, lam_init, tq, tk):
    bsz, s, _ = rope_slab.shape
    assert tq % tk == 0
    kern = functools.partial(_diff_attn_kernel, tq=tq, tk=tk, lam_init=lam_init)
    cb = COL_TILE // LANES
    return pl.pallas_call(
        kern,
        grid=(bsz, A_HEADS, s // tq),
        in_specs=[
            pl.BlockSpec(memory_space=pltpu.SMEM),
            pl.BlockSpec((1, tq, LANES), lambda b, h, i: (b, i, R_AQ * cb + h)),
            pl.BlockSpec((1, s, LANES), lambda b, h, i: (b, 0, R_AK * cb + h)),
            pl.BlockSpec((1, s, LANES), lambda b, h, i: (b, 0, P_AV * cb + h)),
            pl.BlockSpec((1, LANES), lambda b, h, i: (0, 0)),
        ],
        out_specs=pl.BlockSpec((1, tq, LANES), lambda b, h, i: (b, i, h)),
        out_shape=jax.ShapeDtypeStruct((bsz, s, A_HEADS * LANES), BF16),
        scratch_shapes=[pltpu.VMEM((2 * tq, LANES), F32), pltpu.VMEM((2 * tq, LANES), F32),
                        pltpu.VMEM((2 * tq, LANES), F32)],
        compiler_params=_cparams(("arbitrary", "arbitrary", "arbitrary")),
        name="diff_attn",
    )(lam.reshape(1), rope_slab, rope_slab, plain_slab, norm_g.reshape(1, LANES))


KEY_NEG_INF = -2139095040
KEY_POS_INF = 2139095040
SEARCH_MAX_STEPS = 80
NO_TIE_LIMIT = 1e9
P3_PAIRS = 2
SCAN_PAIRS = 8
VT_ROWS = HEAD_DIM + ONES_ROWS


def _key_to_float(key):
    bits = jnp.where(key >= 0, key, (key - 1) ^ jnp.int32(0x7FFFFFFF))
    return pltpu.bitcast(bits, F32)


def _float_to_key(t):
    bits = pltpu.bitcast(t, jnp.int32)
    return jnp.where(bits >= 0, bits, (bits ^ jnp.int32(0x7FFFFFFF)) + 1)


def _dsa_kernel(q_ref, iq_ref, iw_ref, k_ref, ik_ref, vt_ref, o_ref,
                sc_scr, qs_scr, m_scr, acc_scr, *, tq, topk):
    i = pl.program_id(1)
    nblk = i + 1
    npair = (nblk + 1) // 2
    nq = B_HEADS * tq

    iq = iq_ref[0]
    parts = []
    for p in range(IDX_HEADS // 2):
        parts += list(_split_pair(iq[:, p * LANES:(p + 1) * LANES], True))
    iqs = jnp.concatenate(parts, axis=0)
    iw_t = iw_ref[0].astype(F32).T

    def index_block(j):
        start = pl.multiple_of(j * tq, tq)
        ik = ik_ref[0, pl.ds(start, tq), :]
        score = None
        for h in range(IDX_HEADS):
            hs = jnp.maximum(_qk(ik, iqs[h * tq:(h + 1) * tq]), 0.0)
            score = iw_t[h:h + 1] * hs if score is None else score + iw_t[h:h + 1] * hs
        return score

    def p1_pair(jj, carry):
        sc_scr[jj, 0:tq, :] = index_block(2 * jj)
        sc_scr[jj, tq:2 * tq, :] = index_block(2 * jj + 1)
        return carry

    def p1_two_pairs(g, carry):
        return p1_pair(2 * g + 1, p1_pair(2 * g, carry))

    full_pairs = i // 2
    lax.fori_loop(0, full_pairs // 2, p1_two_pairs, 0)
    lax.fori_loop((full_pairs // 2) * 2, full_pairs, p1_pair, 0)
    score = index_block(i)
    key_i = lax.broadcasted_iota(jnp.int32, score.shape, 0)
    qry_i = lax.broadcasted_iota(jnp.int32, score.shape, 1)
    diagonal = jnp.where((key_i // CHUNK) <= (qry_i // CHUNK), score, -jnp.inf)

    @pl.when(i % 2 == 1)
    def _():
        sc_scr[npair - 1, 0:tq, :] = index_block(i - 1)
        sc_scr[npair - 1, tq:2 * tq, :] = diagonal

    @pl.when(i % 2 == 0)
    def _():
        sc_scr[npair - 1, 0:tq, :] = diagonal
        sc_scr[npair - 1, tq:2 * tq, :] = jnp.full((tq, tq), -jnp.inf, F32)

    kf = float(topk)

    def scan(hit_fn, ext_fn, ext_op, ext_init):
        def one_pair(jj, carry):
            cnt, ext = carry
            sblk = sc_scr[jj]
            cnt = cnt + _fold_rows(jnp.where(hit_fn(sblk), 1.0, 0.0), lambda a, b: a + b)
            if ext_fn is not None:
                ext = ext_op(ext, _fold_rows(ext_fn(sblk), ext_op))
            return cnt, ext

        def pairs(start, n, carry):
            for t in range(n):
                carry = one_pair(start + t, carry)
            return carry

        carry = lax.fori_loop(0, npair // SCAN_PAIRS, lambda g, c: pairs(SCAN_PAIRS * g, SCAN_PAIRS, c),
                              (jnp.zeros((8, tq), F32), jnp.full((8, tq), ext_init, F32)))
        start = (npair // SCAN_PAIRS) * SCAN_PAIRS
        width = SCAN_PAIRS // 2
        while width >= 1:
            take = (npair & width) != 0
            carry = lax.cond(take, lambda c, s=start, w=width: pairs(s, w, c), lambda c: c, carry)
            start = start + jnp.where(take, width, 0)
            width //= 2
        cnt, ext = carry
        return jnp.sum(cnt, axis=0, keepdims=True), ext

    def count_ge(t):
        return scan(lambda sblk: sblk >= t, None, None, 0.0)[0]

    c_ge0, ext = scan(lambda sblk: sblk >= 0.0, lambda sblk: sblk, jnp.maximum, -jnp.inf)
    col_max = jnp.max(ext, axis=0, keepdims=True)
    c_gt0, ext = scan(lambda sblk: sblk > 0.0,
                      lambda sblk: jnp.where(sblk > -jnp.inf, sblk, jnp.inf), jnp.minimum, jnp.inf)
    col_min = jnp.min(ext, axis=0, keepdims=True)
    qpos = i * tq + lax.broadcasted_iota(jnp.int32, (1, tq), 1)
    n_valid = (((qpos // CHUNK) + 1) * CHUNK).astype(F32)
    open_q = n_valid < kf
    above = c_gt0 >= kf
    below = c_ge0 < kf
    ikey = lambda v: jnp.full((1, tq), v, jnp.int32)
    lo_k = jnp.where(below, _float_to_key(col_min), ikey(0))
    hi_k = jnp.where(above, _float_to_key(col_max) + 1, jnp.where(below, ikey(0), ikey(1)))
    c_lo = jnp.where(below, n_valid, c_ge0)
    c_hi = jnp.where(above, 0.0, jnp.where(below, c_ge0, c_gt0))
    done = jnp.logical_or(open_q, jnp.logical_not(jnp.logical_or(above, below)))

    def next_probe(lo_k, hi_k, c_lo, c_hi, force_bisect):
        finite = jnp.logical_and(lo_k > KEY_NEG_INF, hi_k < KEY_POS_INF)
        t_lo, t_hi = _key_to_float(lo_k), _key_to_float(hi_k)
        log_lo = jnp.log(c_lo)
        frac = (log_lo - math.log(kf - 0.5)) / (log_lo - jnp.log(jnp.maximum(c_hi, 0.5)))
        k_interp = _float_to_key(t_lo + (t_hi - t_lo) * frac)
        k_mid = (lo_k & hi_k) + ((lo_k ^ hi_k) >> 1)
        interp = jnp.logical_and(finite, force_bisect == 0)
        k = jnp.where(interp, k_interp, k_mid)
        return jnp.minimum(jnp.maximum(k, lo_k + 1), hi_k - 1), interp

    def search_cond(st):
        return jnp.logical_and(st[0] < SEARCH_MAX_STEPS, jnp.min(st[5]) == 0)

    def search_step(st):
        step, lo_k, hi_k, c_lo, c_hi, done, k, interp = st
        c = count_ge(_key_to_float(k))
        ok = c >= kf
        live = done == 0
        new_lo = jnp.where(jnp.logical_and(live, ok), k, lo_k)
        new_hi = jnp.where(jnp.logical_and(live, jnp.logical_not(ok)), k, hi_k)
        new_c_lo = jnp.where(jnp.logical_and(live, ok), c, c_lo)
        new_c_hi = jnp.where(jnp.logical_and(live, jnp.logical_not(ok)), c, c_hi)
        width = lambda a, b: b.astype(F32) - a.astype(F32)
        slow = width(new_lo, new_hi) > 0.5 * width(lo_k, hi_k)
        force = jnp.logical_and(interp != 0, slow).astype(jnp.int32)
        finished = jnp.logical_or(new_c_lo == kf, new_hi - new_lo == 1)
        new_done = jnp.maximum(done, finished.astype(jnp.int32))
        new_k, new_interp = next_probe(new_lo, new_hi, new_c_lo, new_c_hi, force)
        return (step + 1, new_lo, new_hi, new_c_lo, new_c_hi, new_done, new_k,
                new_interp.astype(jnp.int32))

    k0, interp0 = next_probe(lo_k, hi_k, c_lo, c_hi, ikey(0))
    state = (jnp.int32(0), lo_k, hi_k, c_lo, c_hi, done.astype(jnp.int32), k0,
             interp0.astype(jnp.int32))
    _, lo_k, hi_k, c_lo, c_hi, _, _, _ = lax.while_loop(
        search_cond, lambda st: search_step(search_step(st)), state)
    tau = jnp.where(open_q, -jnp.inf, _key_to_float(lo_k))
    need = jnp.where(open_q, 0.0, jnp.where(c_lo == kf, NO_TIE_LIMIT, kf - c_hi))

    q = q_ref[0]
    for p in range(B_HEADS // 2):
        even, odd = _split_pair(q[:, p * LANES:(p + 1) * LANES], True)
        qs_scr[p * tq:(p + 1) * tq, :] = even
        qs_scr[(B_HEADS // 2 + p) * tq:(B_HEADS // 2 + p + 1) * tq, :] = odd
    m_scr[...] = jnp.full(m_scr.shape, NEG, F32)
    acc_scr[...] = jnp.zeros(acc_scr.shape, F32)
    key_i = lax.broadcasted_iota(jnp.int32, (tq, tq), 0)
    lower = (lax.broadcasted_iota(jnp.int32, (tq, tq), 1) <= key_i).astype(BF16)

    def select_bias(jj, tie_count):
        sblk = sc_scr[jj]
        tie = sblk == tau
        tie01 = jnp.where(tie, 1.0, 0.0).astype(BF16)
        rank_a = tie_count + jnp.dot(lower, tie01[0:tq], preferred_element_type=F32)
        rank_b = rank_a[tq - 1:tq, :] + jnp.dot(lower, tie01[tq:2 * tq], preferred_element_type=F32)
        rank = jnp.concatenate([rank_a, rank_b], axis=0)
        keep_tie = jnp.where(rank <= need, 0.0, NEG)
        return jnp.where(sblk > tau, 0.0, jnp.where(tie, keep_tie, NEG)), rank[2 * tq - 1:2 * tq, :]

    def p3_step(jj, pairs, tie_count):
        nkeys = pairs * 2 * tq
        start = pl.multiple_of(jj * 2 * tq, 2 * tq)
        biases = []
        for t in range(pairs):
            bias, tie_count = select_bias(jj + t, tie_count)
            biases.append(bias)
        bias = jnp.concatenate(biases, axis=0)
        s = _qk(k_ref[0, pl.ds(start, nkeys), :], qs_scr[...])
        s = jnp.concatenate([s[:, h * tq:(h + 1) * tq] + bias for h in range(B_HEADS)], axis=1)
        _flash_t_update(s, [((t * tq, (t + 1) * tq), vt_ref[0, 2 * jj + t]) for t in range(2 * pairs)],
                        m_scr, acc_scr)
        return tie_count

    tie_count = lax.fori_loop(0, npair // P3_PAIRS,
                              lambda g, tc: p3_step(g * P3_PAIRS, P3_PAIRS, tc),
                              jnp.zeros((1, tq), F32))
    for r in range(1, P3_PAIRS):
        @pl.when(npair % P3_PAIRS == r)
        def _():
            p3_step(npair - r, r, tie_count)

    acc = acc_scr[...]
    o = acc[0:HEAD_DIM] / acc[HEAD_DIM:HEAD_DIM + 1]
    half_cols = (B_HEADS // 2) * tq
    for p in range(B_HEADS // 2):
        pair = jnp.concatenate([o[:, p * tq:(p + 1) * tq],
                                o[:, half_cols + p * tq:half_cols + (p + 1) * tq]], axis=0)
        o_ref[0, :, p * LANES:(p + 1) * LANES] = pair.T.astype(BF16)


def _dsa_attention(rope_slab, plain_slab, tq):
    bsz, s, _ = rope_slab.shape
    topk = min(TOPK_MAX, s // 4)
    nblk = s // tq
    assert tq >= topk and s % tq == 0 and nblk % 2 == 0
    kern = functools.partial(_dsa_kernel, tq=tq, topk=topk)
    idx0 = R_IDX * COL_TILE
    misc0 = P_MISC * COL_TILE
    iq_w = IDX_HEADS * HEAD_DIM
    width = B_HEADS * HEAD_DIM
    vt = _with_ones_rows(
        plain_slab[:, :, misc0:misc0 + HEAD_DIM].reshape(bsz, nblk, tq, HEAD_DIM).transpose(0, 1, 3, 2))
    return pl.pallas_call(
        kern,
        grid=(bsz, s // tq),
        in_specs=[
            pl.BlockSpec((1, tq, COL_TILE), lambda b, i: (b, i, R_BQ)),
            pl.BlockSpec((1, tq, iq_w), lambda b, i: (b, i, idx0 // iq_w)),
            pl.BlockSpec((1, tq, LANES), lambda b, i: (b, i, (misc0 + LANES) // LANES)),
            pl.BlockSpec((1, s, LANES), lambda b, i: (b, 0, (idx0 + iq_w) // LANES)),
            pl.BlockSpec((1, s, LANES), lambda b, i: (b, 0, (idx0 + iq_w + LANES) // LANES)),
            pl.BlockSpec((1, nblk, VT_ROWS, tq), lambda b, i: (b, 0, 0, 0)),
        ],
        out_specs=pl.BlockSpec((1, tq, width), lambda b, i: (b, i, 0)),
        out_shape=jax.ShapeDtypeStruct((bsz, s, width), BF16),
        scratch_shapes=[
            pltpu.VMEM((nblk // 2, 2 * tq, tq), F32),
            pltpu.VMEM((B_HEADS * tq, LANES), BF16),
            pltpu.VMEM((1, B_HEADS * tq), F32),
            pltpu.VMEM((VT_ROWS, B_HEADS * tq), F32),
        ],
        compiler_params=_cparams(("arbitrary", "arbitrary")),
        name="dsa_attn",
    )(rope_slab, rope_slab, plain_slab, rope_slab, rope_slab, vt)


BAND_TQ = 2 * CHUNK
BAND_KEYS = (C_LEFT_CHUNKS + BAND_TQ // CHUNK) * CHUNK
BAND_PAD = C_LEFT_CHUNKS * CHUNK


def _band_kernel(q_ref, k_ref, v_ref, bias_ref, o_ref):
    i = pl.program_id(1)
    tq = BAND_TQ
    start = pl.multiple_of(i * tq, tq)
    col = lax.broadcasted_iota(jnp.int32, (2 * tq, BAND_KEYS), 1)
    key_ok = col + start >= BAND_PAD
    for p in range(C_HEADS // 2):
        lanes = slice(p * LANES, (p + 1) * LANES)
        k = k_ref[0, pl.ds(start, BAND_KEYS), lanes]
        v = v_ref[0, pl.ds(start, BAND_KEYS), lanes]
        qs = jnp.concatenate(_split_pair(q_ref[0, :, lanes], False), axis=0)
        s = _qk(qs, k) + bias_ref[2 * p:2 * p + 2].reshape(2 * tq, BAND_KEYS)
        s = jnp.where(key_ok, s, NEG)
        m = jnp.max(s, axis=-1, keepdims=True)
        e = jnp.exp2(s - m)
        l = jnp.sum(e, axis=-1, keepdims=True)
        eb = e.astype(BF16)
        v_even, v_odd = _split_pair(v, False)
        o = (jnp.dot(eb[:tq], v_even, preferred_element_type=F32) / l[:tq]
             + jnp.dot(eb[tq:], v_odd, preferred_element_type=F32) / l[tq:])
        o_ref[0, :, lanes] = o.astype(BF16)


def _band_bias(rel_bias):
    r = jnp.arange(BAND_TQ)[:, None]
    cidx = jnp.arange(BAND_KEYS)[None, :]
    n_diag = BAND_TQ + BAND_KEYS - 1
    rel = BAND_PAD - (BAND_KEYS - 1) + jnp.arange(n_diag)
    g = rel_bias.astype(F32)[:, jnp.clip(rel, -REL_CLIP, REL_CLIP) + REL_CLIP] * LOG2E
    skew = jnp.tile(g, (1, BAND_TQ + 2))[:, :BAND_TQ * (n_diag + 1)]
    bias = skew.reshape(-1, BAND_TQ, n_diag + 1)[:, :, :BAND_KEYS][:, :, ::-1]
    dchunk = (r // CHUNK + C_LEFT_CHUNKS) - cidx // CHUNK
    in_band = jnp.logical_and(dchunk >= 0, dchunk <= C_LEFT_CHUNKS)
    return jnp.where(in_band[None], bias, NEG)


def _band_attention(plain_slab, kp, vp, bias):
    bsz, s, _ = plain_slab.shape
    tq = BAND_TQ
    sp = kp.shape[1]
    width = C_HEADS * HEAD_DIM
    return pl.pallas_call(
        _band_kernel,
        grid=(bsz, s // tq),
        in_specs=[
            pl.BlockSpec((1, tq, width), lambda b, i: (b, i, P_CQ)),
            pl.BlockSpec((1, sp, width), lambda b, i: (b, 0, 0)),
            pl.BlockSpec((1, sp, width), lambda b, i: (b, 0, 0)),
            pl.BlockSpec((C_HEADS, tq, BAND_KEYS), lambda b, i: (0, 0, 0)),
        ],
        out_specs=pl.BlockSpec((1, tq, width), lambda b, i: (b, i, 0)),
        out_shape=jax.ShapeDtypeStruct((bsz, s, width), BF16),
        compiler_params=_cparams(("arbitrary", "arbitrary")),
        name="band_attn",
    )(plain_slab, kp, vp, bias)


def _merge_kernel(x_ref, g1_ref, ya_ref, yb_ref, yc_ref, ga_ref, gb_ref, gc_ref,
                  wa_ref, wb_ref, wc_ref, wo_ref, o_ref):
    for r0 in range(0, x_ref.shape[1], PROJ_ROWS):
        rows = slice(r0, r0 + PROJ_ROWS)

        def branch(y_ref, w_ref, gate_ref):
            return gate_ref[0, rows, :].astype(F32) * jnp.dot(y_ref[0, rows, :], w_ref[...],
                                                             preferred_element_type=F32)

        merged = (branch(ya_ref, wa_ref, ga_ref) + branch(yb_ref, wb_ref, gb_ref)
                  + branch(yc_ref, wc_ref, gc_ref))
        mixed = jnp.dot(merged.astype(BF16), wo_ref[...], preferred_element_type=F32)
        o_ref[0, rows, :] = x_ref[0, rows, :] + g1_ref[0] * mixed


def _merge(x, g1, ya, yb, yc, gate_slab, wa, wb, wc, wo, tm):
    bsz, s, d = x.shape
    tok = lambda b, i: (b, i, 0)
    full = lambda b, i: (0, 0)
    return pl.pallas_call(
        _merge_kernel,
        grid=(bsz, s // tm),
        in_specs=[
            pl.BlockSpec((1, tm, d), tok),
            pl.BlockSpec((1, 1, d), lambda b, i: (b, 0, 0)),
            pl.BlockSpec((1, tm, ya.shape[2]), tok),
            pl.BlockSpec((1, tm, yb.shape[2]), tok),
            pl.BlockSpec((1, tm, yc.shape[2]), tok),
            pl.BlockSpec((1, tm, d), lambda b, i: (b, i, 0)),
            pl.BlockSpec((1, tm, d), lambda b, i: (b, i, 1)),
            pl.BlockSpec((1, tm, d), lambda b, i: (b, i, 2)),
            pl.BlockSpec(wa.shape, full),
            pl.BlockSpec(wb.shape, full),
            pl.BlockSpec(wc.shape, full),
            pl.BlockSpec(wo.shape, full),
        ],
        out_specs=pl.BlockSpec((1, tm, d), tok),
        out_shape=jax.ShapeDtypeStruct((bsz, s, d), F32),
        compiler_params=_cparams(("arbitrary", "arbitrary")),
        name="merge",
    )(x, g1, ya, yb, yc, gate_slab, gate_slab, gate_slab, wa, wb, wc, wo)


def _router_kernel(x_ref, g_ref, sc_ref, sh_ref, rw_ref, rb_ref, u_ref, comb_ref):
    u = _modulated_norm(x_ref[0], g_ref[...], sc_ref[0], sh_ref[0])
    u_ref[0] = u.astype(BF16)
    logits = lax.dot_general(rw_ref[...], u, (((1,), (1,)), ((), ())),
                             preferred_element_type=F32, precision=lax.Precision.HIGHEST)
    aff = jax.nn.sigmoid(logits)
    sel = aff + rb_ref[...]
    rows = [sel[e:e + 1] for e in range(N_EXPERTS)]
    gscore = []
    for g in range(N_GROUPS):
        r = rows[g * EXPERTS_PER_GROUP:(g + 1) * EXPERTS_PER_GROUP]
        best = None
        for a in range(EXPERTS_PER_GROUP):
            for b in range(a + 1, EXPERTS_PER_GROUP):
                pair = r[a] + r[b]
                best = pair if best is None else jnp.maximum(best, pair)
        gscore.append(best)
    gmax = functools.reduce(jnp.maximum, gscore)
    taken = jnp.zeros_like(gmax) > 1.0
    in_best = []
    for g in range(N_GROUPS):
        is_g = jnp.logical_and(gscore[g] == gmax, jnp.logical_not(taken))
        in_best.append(is_g)
        taken = jnp.logical_or(taken, is_g)
    keep = []
    for e in range(N_EXPERTS):
        g = e // EXPERTS_PER_GROUP
        rank = jnp.zeros_like(gmax)
        for o in range(g * EXPERTS_PER_GROUP, (g + 1) * EXPERTS_PER_GROUP):
            if o == e:
                continue
            ahead = rows[o] > rows[e] if o > e else rows[o] >= rows[e]
            rank = rank + jnp.where(ahead, 1.0, 0.0)
        keep.append(jnp.logical_and(in_best[g], rank < 2.0))
    w = [jnp.where(keep[e], aff[e:e + 1], 0.0) for e in range(N_EXPERTS)]
    total = functools.reduce(lambda a, b: a + b, w)
    comb = jnp.concatenate([we / total for we in w]
                           + [jnp.zeros((LANES - N_EXPERTS, total.shape[1]), F32)], axis=0)
    comb_ref[0] = comb.T


def _router(x, g, sc, sh, router_w, router_b, tm):
    bsz, s, d = x.shape
    tok = lambda b, i: (b, i, 0)
    return pl.pallas_call(
        _router_kernel,
        grid=(bsz, s // tm),
        in_specs=[
            pl.BlockSpec((1, tm, d), tok),
            pl.BlockSpec((1, d), lambda b, i: (0, 0)),
            pl.BlockSpec((1, 1, d), lambda b, i: (b, 0, 0)),
            pl.BlockSpec((1, 1, d), lambda b, i: (b, 0, 0)),
            pl.BlockSpec((N_EXPERTS, d), lambda b, i: (0, 0)),
            pl.BlockSpec((N_EXPERTS, 1), lambda b, i: (0, 0)),
        ],
        out_specs=[pl.BlockSpec((1, tm, d), tok), pl.BlockSpec((1, tm, LANES), tok)],
        out_shape=[jax.ShapeDtypeStruct((bsz, s, d), BF16),
                   jax.ShapeDtypeStruct((bsz, s, LANES), F32)],
        compiler_params=_cparams(("arbitrary", "arbitrary")),
        name="router",
    )(x, g.reshape(1, d), sc, sh, router_w.T, router_b.reshape(N_EXPERTS, 1))


MOE_SUB = 512
MOE_CAP = 128


def _expert_ffn(rows, w1_ref, w3_ref, w2_ref):
    h1 = jnp.dot(rows, w1_ref[0], preferred_element_type=F32)
    h3 = jnp.dot(rows, w3_ref[0], preferred_element_type=F32)
    h = (h1 * jax.nn.sigmoid(h1)) * h3
    return jnp.dot(h.astype(BF16), w2_ref[0], preferred_element_type=F32)


def _moe_kernel(x_ref, g2_ref, fg_ref, u_ref, comb_ref, w1_ref, w3_ref, w2_ref, o_ref,
                acc_scr, pos_scr, pos_t_scr, comb_t_scr, scatter_scr, y_scr, *, final_norm):
    e = pl.program_id(2)
    tm = u_ref.shape[1]
    sub = min(MOE_SUB, tm)
    subs = [slice(r0, r0 + sub) for r0 in range(0, tm, sub)]

    @pl.when(e == 0)
    def _():
        acc_scr[...] = jnp.zeros(acc_scr.shape, F32)
        before = (lax.broadcasted_iota(jnp.int32, (sub, sub), 1)
                  < lax.broadcasted_iota(jnp.int32, (sub, sub), 0)).astype(BF16)
        for rows in subs:
            comb0 = comb_ref[0, rows, :]
            member = jnp.where(comb0 > 0.0, 1.0, 0.0).astype(BF16)
            pos = jnp.dot(before, member, preferred_element_type=F32)
            pos_scr[rows, :] = pos
            pos_t_scr[:, rows] = pos.T
            comb_t_scr[:, rows] = comb0.T

    comb = comb_ref[0]
    lane = lax.broadcasted_iota(jnp.int32, comb.shape, 1)
    ce = jnp.sum(jnp.where(lane == e, comb, 0.0), axis=-1, keepdims=True)
    pe = jnp.sum(jnp.where(lane == e, pos_scr[...], 0.0), axis=-1, keepdims=True)
    member_e = jnp.where(ce > 0.0, 1.0, 0.0)
    fullest = functools.reduce(jnp.maximum, [jnp.sum(member_e[rows]) for rows in subs])
    routed = fullest <= float(MOE_CAP)
    slots = pl.ds(pl.multiple_of(e * MOE_CAP, MOE_CAP), MOE_CAP)

    @pl.when(routed)
    def _():
        slot_l = lax.broadcasted_iota(jnp.int32, (1, MOE_CAP), 1).astype(F32)
        scatter_scr[e] = jnp.where(jnp.logical_and(ce > 0.0, pe == slot_l), 1.0, 0.0).astype(BF16)
        slot_s = lax.broadcasted_iota(jnp.int32, (MOE_CAP, 1), 0).astype(F32)
        packed, weights = [], []
        for rows in subs:
            ce_t = comb_t_scr[pl.ds(e, 1), rows]
            pe_t = pos_t_scr[pl.ds(e, 1), rows]
            hit = jnp.logical_and(ce_t > 0.0, pe_t == slot_s)
            weights.append(jnp.sum(jnp.where(hit, ce_t, 0.0), axis=-1, keepdims=True))
            packed.append(jnp.dot(jnp.where(hit, 1.0, 0.0).astype(BF16), u_ref[0, rows, :],
                                  preferred_element_type=F32).astype(BF16))
        y = jnp.concatenate(weights, axis=0) * _expert_ffn(jnp.concatenate(packed, axis=0),
                                                           w1_ref, w3_ref, w2_ref)
        for k in range(len(subs)):
            y_scr[k, slots, :] = y[k * MOE_CAP:(k + 1) * MOE_CAP].astype(BF16)

    @pl.when(jnp.logical_not(routed))
    def _():
        scatter_scr[e] = jnp.zeros(scatter_scr.shape[1:], BF16)
        for k in range(len(subs)):
            y_scr[k, slots, :] = jnp.zeros((MOE_CAP, y_scr.shape[2]), BF16)
        acc_scr[...] += ce * _expert_ffn(u_ref[0], w1_ref, w3_ref, w2_ref)

    @pl.when(e == pl.num_programs(2) - 1)
    def _():
        for k, rows in enumerate(subs):
            scatter = jnp.concatenate([scatter_scr[j, rows, :] for j in range(N_EXPERTS)], axis=1)
            y = acc_scr[rows, :] + jnp.dot(scatter, y_scr[k], preferred_element_type=F32)
            out = x_ref[0, rows, :] + g2_ref[0] * y
            if final_norm:
                out = (out * lax.rsqrt(jnp.mean(out * out, axis=-1, keepdims=True) + EPS)) * fg_ref[...]
            o_ref[0, rows, :] = out


def _moe(x, g2, final_g, u, comb, w1, w3, w2, tm, final_norm):
    bsz, s, d = x.shape
    ne, _, dff = w1.shape
    assert ne == N_EXPERTS
    tok = lambda b, i, e: (b, i, 0)
    return pl.pallas_call(
        functools.partial(_moe_kernel, final_norm=final_norm),
        grid=(bsz, s // tm, ne),
        in_specs=[
            pl.BlockSpec((1, tm, d), tok),
            pl.BlockSpec((1, 1, d), lambda b, i, e: (b, 0, 0)),
            pl.BlockSpec((1, d), lambda b, i, e: (0, 0)),
            pl.BlockSpec((1, tm, d), tok),
            pl.BlockSpec((1, tm, LANES), tok),
            pl.BlockSpec((1, d, dff), lambda b, i, e: (e, 0, 0)),
            pl.BlockSpec((1, d, dff), lambda b, i, e: (e, 0, 0)),
            pl.BlockSpec((1, dff, d), lambda b, i, e: (e, 0, 0)),
        ],
        out_specs=pl.BlockSpec((1, tm, d), tok),
        out_shape=jax.ShapeDtypeStruct((bsz, s, d), F32),
        scratch_shapes=[
            pltpu.VMEM((tm, d), F32),
            pltpu.VMEM((tm, LANES), F32),
            pltpu.VMEM((LANES, tm), F32),
            pltpu.VMEM((LANES, tm), F32),
            pltpu.VMEM((ne, tm, MOE_CAP), BF16),
            pltpu.VMEM((pl.cdiv(tm, MOE_SUB), ne * MOE_CAP, d), BF16),
        ],
        compiler_params=_cparams(("arbitrary", "arbitrary", "arbitrary")),
        name="moe",
    )(x, g2, final_g.reshape(1, d), u, comb, w1, w3, w2)


def _tile(s, want):
    t = min(want, s)
    assert s % t == 0
    return t


def _tiles(s):
    return dict(
        rows=_tile(s, 1024),
        diff_q=_tile(s, 1024),
        diff_k=_tile(s, 512),
        dsa_q=_tile(s, 256),
        merge=_tile(s, 512),
    )


def kernel(x, c, positions, norm1_g, norm2_g, w_mod, b_mod, w_in, lambda_q1, lambda_k1, lambda_q2, lambda_k2, a_norm_g, c_rel_bias, w_branch_a, w_branch_b, w_branch_c, w_out, router_w, router_b, exp_w1, exp_w3, exp_w2, final_g):
    bsz, s, d = x.shape
    depth = w_mod.shape[0]
    t = _tiles(s)
    tm = t["rows"]

    mod = _modulation(c, w_mod, b_mod)
    rope_tables = _rope_tables(positions)

    for layer in range(depth):
        lam_init = 0.8 - 0.6 * math.exp(-0.3 * layer)
        sh1, sc1, g1, sh2, sc2, g2 = [m[:, None, :] for m in jnp.split(mod[layer], 6, axis=-1)]
        lam = (jnp.exp(jnp.sum(lambda_q1[layer] * lambda_k1[layer]))
               - jnp.exp(jnp.sum(lambda_q2[layer] * lambda_k2[layer])) + lam_init)

        w_rope, w_gate, w_plain = _build_weights(w_in[layer])
        u = _norm(x, norm1_g[layer], sc1, sh1, tm)
        rope_slab = _project(u, w_rope, rope_tables, "rope", tm)
        gate_slab = _project(u, w_gate, (), "gate", tm)
        plain_slab = _project(u, w_plain, (), "plain", tm)

        ya = _diff_attention(rope_slab, plain_slab, lam, a_norm_g[layer], lam_init,
                             t["diff_q"], t["diff_k"])
        yb = _dsa_attention(rope_slab, plain_slab, t["dsa_q"])
        pad = ((0, 0), (BAND_PAD, 0), (0, 0))
        kp = jnp.pad(plain_slab[:, :, P_CK * COL_TILE:(P_CK + 1) * COL_TILE], pad)
        vp = jnp.pad(plain_slab[:, :, P_CV * COL_TILE:(P_CV + 1) * COL_TILE], pad)
        yc = _band_attention(plain_slab, kp, vp, _band_bias(c_rel_bias[layer]))
        x = _merge(x, g1, ya, yb, yc, gate_slab,
                   w_branch_a[layer].astype(BF16), w_branch_b[layer].astype(BF16),
                   w_branch_c[layer].astype(BF16), w_out[layer].astype(BF16), t["merge"])

        u, comb = _router(x, norm2_g[layer], sc2, sh2, router_w, router_b, t["merge"])
        x = _moe(x, g2, final_g, u, comb, exp_w1[layer].astype(BF16), exp_w3[layer].astype(BF16),
                 exp_w2[layer].astype(BF16), tm, final_norm=(layer == depth - 1))

    return x
```

```python
import functools
import math

import jax
import jax.numpy as jnp
from jax import lax
from jax.experimental import pallas as pl
from jax.experimental.pallas import tpu as pltpu

F32 = jnp.float32
BF16 = jnp.bfloat16

CHUNK = 64
ROPE_THETA = 10000.0
EPS = 1e-6
A_HEADS = 4
HEAD_DIM = 64
B_HEADS = 8
IDX_HEADS = 4
TOPK_MAX = 256
C_HEADS = 8
C_LEFT_CHUNKS = 8
REL_CLIP = 256
N_EXPERTS = 16
N_GROUPS = 4
EXPERTS_PER_GROUP = 4
N_BRANCHES = 3

LANES = 128
NEG = -1e30
LOG2E = math.log2(math.e)
VMEM_LIMIT = 56 * 1024 * 1024

COL_TILE = 512
PROJ_ROWS = 256
R_AQ, R_AK, R_BQ, R_IDX = 0, 1, 2, 3
P_AV, P_CQ, P_CK, P_CV, P_MISC = 0, 1, 2, 3, 4


def _cparams(sem):
    return pltpu.CompilerParams(dimension_semantics=sem, vmem_limit_bytes=VMEM_LIMIT)


def _mod_kernel(c_ref, w_ref, b_ref, o_ref):
    c = c_ref[...]
    ca = c * jax.nn.sigmoid(c)
    o_ref[0] = jnp.dot(ca, w_ref[0], preferred_element_type=F32) + b_ref[0]


def _modulation(c, w_mod, b_mod):
    depth, d, n6 = w_mod.shape
    bsz = c.shape[0]
    tn = 1024
    return pl.pallas_call(
        _mod_kernel,
        grid=(depth, n6 // tn),
        in_specs=[
            pl.BlockSpec((bsz, d), lambda l, j: (0, 0)),
            pl.BlockSpec((1, d, tn), lambda l, j: (l, 0, j)),
            pl.BlockSpec((1, 1, tn), lambda l, j: (l, 0, j)),
        ],
        out_specs=pl.BlockSpec((1, bsz, tn), lambda l, j: (l, 0, j)),
        out_shape=jax.ShapeDtypeStruct((depth, bsz, n6), F32),
        compiler_params=_cparams(("arbitrary", "arbitrary")),
        name="modulation",
    )(c, w_mod, b_mod.reshape(depth, 1, n6))


def _modulated_norm(x, g, sc, sh):
    y = x * lax.rsqrt(jnp.mean(x * x, axis=-1, keepdims=True) + EPS)
    return (y * g) * (1.0 + sc) + sh


def _norm_kernel(x_ref, g_ref, sc_ref, sh_ref, u_ref):
    u_ref[0] = _modulated_norm(x_ref[0], g_ref[...], sc_ref[0], sh_ref[0]).astype(BF16)


def _norm(x, g, sc, sh, tm):
    bsz, s, d = x.shape
    tok = lambda b, i: (b, i, 0)
    per_batch = lambda b, i: (b, 0, 0)
    return pl.pallas_call(
        _norm_kernel,
        grid=(bsz, s // tm),
        in_specs=[pl.BlockSpec((1, tm, d), tok), pl.BlockSpec((1, d), lambda b, i: (0, 0)),
                  pl.BlockSpec((1, 1, d), per_batch), pl.BlockSpec((1, 1, d), per_batch)],
        out_specs=pl.BlockSpec((1, tm, d), tok),
        out_shape=jax.ShapeDtypeStruct((bsz, s, d), BF16),
        compiler_params=_cparams(("arbitrary", "arbitrary")),
        name="norm",
    )(x, g.reshape(1, d), sc, sh)


def _proj_kernel(*refs, mode):
    if mode == "rope":
        u_ref, cos_ref, sin_ref, w_ref, o_ref = refs
    else:
        u_ref, w_ref, o_ref = refs
    tm = u_ref.shape[1]
    for r0 in range(0, tm, PROJ_ROWS):
        rows = slice(r0, r0 + PROJ_ROWS)
        acc = jnp.dot(u_ref[0, rows, :], w_ref[...], preferred_element_type=F32)
        if mode == "rope":
            cos, sin = cos_ref[0, rows, :], sin_ref[0, rows, :]
            groups = [acc[:, c:c + LANES] for c in range(0, COL_TILE, LANES)]
            acc = jnp.concatenate(
                [g * cos + pltpu.roll(g, LANES // 2, axis=1) * sin for g in groups], axis=1)
        elif mode == "gate":
            acc = 1.0 / (1.0 + jnp.exp(-acc))
        o_ref[0, rows, :] = acc.astype(BF16)


def _project(u, w, rope_tables, mode, tm):
    bsz, s, d = u.shape
    ncols = w.shape[1]
    tok = lambda b, i, j: (b, i, 0)
    in_specs = [pl.BlockSpec((1, tm, d), tok)]
    in_specs += [pl.BlockSpec((1, tm, LANES), tok) for _ in rope_tables]
    in_specs += [pl.BlockSpec((d, COL_TILE), lambda b, i, j: (0, j))]
    return pl.pallas_call(
        functools.partial(_proj_kernel, mode=mode),
        grid=(bsz, s // tm, ncols // COL_TILE),
        in_specs=in_specs,
        out_specs=pl.BlockSpec((1, tm, COL_TILE), lambda b, i, j: (b, i, j)),
        out_shape=jax.ShapeDtypeStruct((bsz, s, ncols), BF16),
        compiler_params=_cparams(("arbitrary", "arbitrary", "arbitrary")),
        name="proj_" + mode,
    )(u, *rope_tables, w)


def _pair_layout(w):
    half = HEAD_DIM // 2
    col = jnp.arange(w.shape[1])
    base, r = (col // LANES) * LANES, col % LANES
    src = base + ((r // half) % 2) * HEAD_DIM + (r // HEAD_DIM) * half + r % half
    return w[:, src]


def _build_weights(w_in):
    sizes = (512, 512, 512, 512, 64, 64, 256, 64, 4, 512, 512, 512, 3072)
    parts, start = [], 0
    for sz in sizes:
        parts.append(w_in[:, start:start + sz])
        start += sz
    aq, ak, av, bq, bk, bv, iq, ik, iw, cq, ck, cv, gates = parts
    d = w_in.shape[0]
    qscale = HEAD_DIM ** -0.5 * LOG2E
    iw_scale = IDX_HEADS ** -0.5 * HEAD_DIM ** -0.5
    zeros = lambda n: jnp.zeros((d, n), w_in.dtype)
    w_rope = _pair_layout(jnp.concatenate([aq * qscale, ak, bq * qscale, iq, bk, bk, ik, ik], axis=1))
    w_plain = jnp.concatenate([av, cq * qscale, ck, cv,
                               bv, bv, iw * iw_scale, zeros(LANES - IDX_HEADS),
                               zeros(COL_TILE - 2 * LANES)], axis=1)
    return w_rope.astype(BF16), gates.astype(BF16), w_plain.astype(BF16)


def _rope_tables(positions):
    half = HEAD_DIM // 2
    inv = ROPE_THETA ** (-jnp.arange(half, dtype=F32) / half)
    ang = positions.astype(F32)[..., None] * inv
    cos, sin = jnp.cos(ang), jnp.sin(ang)
    cos_t = jnp.tile(cos, (1, 1, LANES // half))
    sin_t = jnp.concatenate([-sin, -sin, sin, sin], axis=-1)
    return cos_t, sin_t


def _split_pair(pair, interleaved):
    lane = lax.broadcasted_iota(jnp.int32, pair.shape, 1)
    first = ((lane // (HEAD_DIM // 2)) % 2 == 0) if interleaved else (lane < HEAD_DIM)
    zero = jnp.zeros_like(pair)
    return jnp.where(first, pair, zero), jnp.where(first, zero, pair)


def _qk(q, k):
    return lax.dot_general(q, k, (((1,), (1,)), ((), ())), preferred_element_type=F32)


FOLD_CHAINS = 4


def _fold_rows(x, op):
    parts = x.reshape(x.shape[0] // 8, 8, x.shape[1])
    chains = [functools.reduce(op, [parts[g] for g in range(c, parts.shape[0], FOLD_CHAINS)])
              for c in range(min(FOLD_CHAINS, parts.shape[0]))]
    return functools.reduce(op, chains)


def _flash_t_update(s, vt_blocks, m_scr, acc_scr):
    m_old = m_scr[...]
    m_new = jnp.maximum(m_old, jnp.max(_fold_rows(s, jnp.maximum), axis=0, keepdims=True))
    alpha = jnp.exp2(m_old - m_new)
    pb = jnp.exp2(s - m_new).astype(BF16)
    m_scr[...] = m_new
    pv = functools.reduce(lambda a, b: a + b, [
        jnp.dot(vt, pb[k0:k1], preferred_element_type=F32) for (k0, k1), vt in vt_blocks])
    acc_scr[...] = alpha * acc_scr[...] + pv


ONES_ROWS = 16


def _with_ones_rows(vt):
    shape = vt.shape[:-2]
    return jnp.concatenate([vt, jnp.ones(shape + (1, vt.shape[-1]), vt.dtype),
                            jnp.zeros(shape + (ONES_ROWS - 1, vt.shape[-1]), vt.dtype)], axis=-2)


def _flash_update(s, v, m_scr, l_scr, acc_scr, rows=slice(None)):
    groups = [s[:, c:c + LANES] for c in range(0, s.shape[1], LANES)]
    m_old = m_scr[rows, :]
    lane_max = functools.reduce(jnp.maximum, groups)
    m_new = jnp.maximum(m_old, jnp.max(lane_max, axis=-1, keepdims=True))
    alpha = jnp.exp2(m_old - m_new)
    p_groups = [jnp.exp2(g - m_new) for g in groups]
    l_scr[rows, :] = alpha * l_scr[rows, :] + functools.reduce(lambda a, b: a + b, p_groups)
    m_scr[rows, :] = m_new
    pb = jnp.concatenate([g.astype(BF16) for g in p_groups], axis=1)
    acc_scr[rows, :] = alpha * acc_scr[rows, :] + jnp.dot(pb, v, preferred_element_type=F32)


def _diff_attn_kernel(lam_ref, q_ref, k_ref, v_ref, ng_ref, o_ref, m_scr, l_scr, acc_scr,
                      *, tq, tk, lam_init):
    i = pl.program_id(2)
    per_tile = tq // tk
    qs = jnp.concatenate(_split_pair(q_ref[0], True), axis=0)
    m_scr[...] = jnp.full(m_scr.shape, NEG, F32)
    l_scr[...] = jnp.zeros(l_scr.shape, F32)
    acc_scr[...] = jnp.zeros(acc_scr.shape, F32)

    def kv_block(j):
        start = pl.multiple_of(j * tk, tk)
        return k_ref[0, pl.ds(start, tk), :], v_ref[0, pl.ds(start, tk), :]

    def body(j, carry):
        k, v = kv_block(j)
        _flash_update(_qk(qs, k), v, m_scr, l_scr, acc_scr)
        return carry

    lax.fori_loop(0, i * per_tile, body, 0)
    for d in range(per_tile):
        k, v = kv_block(i * per_tile + d)
        for base in (0, tq):
            rows = slice(base + d * tk, base + tq)
            s = _qk(qs[rows], k)
            row = lax.broadcasted_iota(jnp.int32, s.shape, 0) + d * tk
            col = lax.broadcasted_iota(jnp.int32, s.shape, 1) + d * tk
            s = jnp.where((col // CHUNK) <= (row // CHUNK), s, NEG)
            _flash_update(s, v, m_scr, l_scr, acc_scr, rows)

    o = acc_scr[...] / jnp.sum(l_scr[...], axis=-1, keepdims=True)
    o = o[:tq] - lam_ref[0] * o[tq:]
    o = o * lax.rsqrt(jnp.mean(o * o, axis=-1, keepdims=True) + EPS)
    o_ref[0] = ((o * ng_ref[...]) * (1.0 - lam_init)).astype(BF16)


def _diff_attention(rope_slab, plain_slab, lam, norm_g, lam_init, tq, tk):
    bsz, s, _ = rope_slab.shape
    assert tq % tk == 0
    kern = functools.partial(_diff_attn_kernel, tq=tq, tk=tk, lam_init=lam_init)
    cb = COL_TILE // LANES
    return pl.pallas_call(
        kern,
        grid=(bsz, A_HEADS, s // tq),
        in_specs=[
            pl.BlockSpec(memory_space=pltpu.SMEM),
            pl.BlockSpec((1, tq, LANES), lambda b, h, i: (b, i, R_AQ * cb + h)),
            pl.BlockSpec((1, s, LANES), lambda b, h, i: (b, 0, R_AK * cb + h)),
            pl.BlockSpec((1, s, LANES), lambda b, h, i: (b, 0, P_AV * cb + h)),
            pl.BlockSpec((1, LANES), lambda b, h, i: (0, 0)),
        ],
        out_specs=pl.BlockSpec((1, tq, LANES), lambda b, h, i: (b, i, h)),
        out_shape=jax.ShapeDtypeStruct((bsz, s, A_HEADS * LANES), BF16),
        scratch_shapes=[pltpu.VMEM((2 * tq, LANES), F32), pltpu.VMEM((2 * tq, LANES), F32),
                        pltpu.VMEM((2 * tq, LANES), F32)],
        compiler_params=_cparams(("arbitrary", "arbitrary", "arbitrary")),
        name="diff_attn",
    )(lam.reshape(1), rope_slab, rope_slab, plain_slab, norm_g.reshape(1, LANES))


KEY_NEG_INF = -2139095040
KEY_POS_INF = 2139095040
SEARCH_MAX_STEPS = 80
NO_TIE_LIMIT = 1e9
P3_PAIRS = 2
SCAN_PAIRS = 4
VT_ROWS = HEAD_DIM + ONES_ROWS


def _key_to_float(key):
    bits = jnp.where(key >= 0, key, (key - 1) ^ jnp.int32(0x7FFFFFFF))
    return pltpu.bitcast(bits, F32)


def _float_to_key(t):
    bits = pltpu.bitcast(t, jnp.int32)
    return jnp.where(bits >= 0, bits, (bits ^ jnp.int32(0x7FFFFFFF)) + 1)


def _dsa_kernel(q_ref, iq_ref, iw_ref, k_ref, ik_ref, vt_ref, o_ref,
                sc_scr, qs_scr, m_scr, acc_scr, *, tq, topk):
    i = pl.program_id(1)
    nblk = i + 1
    npair = (nblk + 1) // 2
    nq = B_HEADS * tq

    iq = iq_ref[0]
    parts = []
    for p in range(IDX_HEADS // 2):
        parts += list(_split_pair(iq[:, p * LANES:(p + 1) * LANES], True))
    iqs = jnp.concatenate(parts, axis=0)
    iw_t = iw_ref[0].astype(F32).T

    def index_block(j):
        start = pl.multiple_of(j * tq, tq)
        ik = ik_ref[0, pl.ds(start, tq), :]
        score = None
        for h in range(IDX_HEADS):
            hs = jnp.maximum(_qk(ik, iqs[h * tq:(h + 1) * tq]), 0.0)
            score = iw_t[h:h + 1] * hs if score is None else score + iw_t[h:h + 1] * hs
        return score

    def p1_pair(jj, carry):
        sc_scr[jj, 0:tq, :] = index_block(2 * jj)
        sc_scr[jj, tq:2 * tq, :] = index_block(2 * jj + 1)
        return carry

    def p1_two_pairs(g, carry):
        return p1_pair(2 * g + 1, p1_pair(2 * g, carry))

    full_pairs = i // 2
    lax.fori_loop(0, full_pairs // 2, p1_two_pairs, 0)
    lax.fori_loop((full_pairs // 2) * 2, full_pairs, p1_pair, 0)
    score = index_block(i)
    key_i = lax.broadcasted_iota(jnp.int32, score.shape, 0)
    qry_i = lax.broadcasted_iota(jnp.int32, score.shape, 1)
    diagonal = jnp.where((key_i // CHUNK) <= (qry_i // CHUNK), score, -jnp.inf)

    @pl.when(i % 2 == 1)
    def _():
        sc_scr[npair - 1, 0:tq, :] = index_block(i - 1)
        sc_scr[npair - 1, tq:2 * tq, :] = diagonal

    @pl.when(i % 2 == 0)
    def _():
        sc_scr[npair - 1, 0:tq, :] = diagonal
        sc_scr[npair - 1, tq:2 * tq, :] = jnp.full((tq, tq), -jnp.inf, F32)

    kf = float(topk)

    def scan(hit_fn, ext_fn, ext_op, ext_init):
        def one_pair(jj, carry):
            cnt, ext = carry
            sblk = sc_scr[jj]
            cnt = cnt + _fold_rows(jnp.where(hit_fn(sblk), 1.0, 0.0), lambda a, b: a + b)
            if ext_fn is not None:
                ext = ext_op(ext, _fold_rows(ext_fn(sblk), ext_op))
            return cnt, ext

        def several_pairs(g, carry):
            for t in range(SCAN_PAIRS):
                carry = one_pair(SCAN_PAIRS * g + t, carry)
            return carry

        carry = lax.fori_loop(0, npair // SCAN_PAIRS, several_pairs,
                              (jnp.zeros((8, tq), F32), jnp.full((8, tq), ext_init, F32)))
        cnt, ext = lax.fori_loop((npair // SCAN_PAIRS) * SCAN_PAIRS, npair, one_pair, carry)
        return jnp.sum(cnt, axis=0, keepdims=True), ext

    def count_ge(t):
        return scan(lambda sblk: sblk >= t, None, None, 0.0)[0]

    c_ge0, ext = scan(lambda sblk: sblk >= 0.0, lambda sblk: sblk, jnp.maximum, -jnp.inf)
    col_max = jnp.max(ext, axis=0, keepdims=True)
    c_gt0, ext = scan(lambda sblk: sblk > 0.0,
                      lambda sblk: jnp.where(sblk > -jnp.inf, sblk, jnp.inf), jnp.minimum, jnp.inf)
    col_min = jnp.min(ext, axis=0, keepdims=True)
    qpos = i * tq + lax.broadcasted_iota(jnp.int32, (1, tq), 1)
    n_valid = (((qpos // CHUNK) + 1) * CHUNK).astype(F32)
    open_q = n_valid < kf
    above = c_gt0 >= kf
    below = c_ge0 < kf
    ikey = lambda v: jnp.full((1, tq), v, jnp.int32)
    lo_k = jnp.where(below, _float_to_key(col_min), ikey(0))
    hi_k = jnp.where(above, _float_to_key(col_max) + 1, jnp.where(below, ikey(0), ikey(1)))
    c_lo = jnp.where(below, n_valid, c_ge0)
    c_hi = jnp.where(above, 0.0, jnp.where(below, c_ge0, c_gt0))
    done = jnp.logical_or(open_q, jnp.logical_not(jnp.logical_or(above, below)))

    def next_probe(lo_k, hi_k, c_lo, c_hi, force_bisect):
        finite = jnp.logical_and(lo_k > KEY_NEG_INF, hi_k < KEY_POS_INF)
        t_lo, t_hi = _key_to_float(lo_k), _key_to_float(hi_k)
        log_lo = jnp.log(c_lo)
        frac = (log_lo - math.log(kf - 0.5)) / (log_lo - jnp.log(jnp.maximum(c_hi, 0.5)))
        k_interp = _float_to_key(t_lo + (t_hi - t_lo) * frac)
        k_mid = (lo_k & hi_k) + ((lo_k ^ hi_k) >> 1)
        interp = jnp.logical_and(finite, force_bisect == 0)
        k = jnp.where(interp, k_interp, k_mid)
        return jnp.minimum(jnp.maximum(k, lo_k + 1), hi_k - 1), interp

    def search_cond(st):
        return jnp.logical_and(st[0] < SEARCH_MAX_STEPS, jnp.min(st[5]) == 0)

    def search_step(st):
        step, lo_k, hi_k, c_lo, c_hi, done, k, interp = st
        c = count_ge(_key_to_float(k))
        ok = c >= kf
        live = done == 0
        new_lo = jnp.where(jnp.logical_and(live, ok), k, lo_k)
        new_hi = jnp.where(jnp.logical_and(live, jnp.logical_not(ok)), k, hi_k)
        new_c_lo = jnp.where(jnp.logical_and(live, ok), c, c_lo)
        new_c_hi = jnp.where(jnp.logical_and(live, jnp.logical_not(ok)), c, c_hi)
        width = lambda a, b: b.astype(F32) - a.astype(F32)
        slow = width(new_lo, new_hi) > 0.5 * width(lo_k, hi_k)
        force = jnp.logical_and(interp != 0, slow).astype(jnp.int32)
        finished = jnp.logical_or(new_c_lo == kf, new_hi - new_lo == 1)
        new_done = jnp.maximum(done, finished.astype(jnp.int32))
        new_k, new_interp = next_probe(new_lo, new_hi, new_c_lo, new_c_hi, force)
        return (step + 1, new_lo, new_hi, new_c_lo, new_c_hi, new_done, new_k,
                new_interp.astype(jnp.int32))

    k0, interp0 = next_probe(lo_k, hi_k, c_lo, c_hi, ikey(0))
    state = (jnp.int32(0), lo_k, hi_k, c_lo, c_hi, done.astype(jnp.int32), k0,
             interp0.astype(jnp.int32))
    _, lo_k, hi_k, c_lo, c_hi, _, _, _ = lax.while_loop(
        search_cond, lambda st: search_step(search_step(search_step(st))), state)
    tau = jnp.where(open_q, -jnp.inf, _key_to_float(lo_k))
    need = jnp.where(open_q, 0.0, jnp.where(c_lo == kf, NO_TIE_LIMIT, kf - c_hi))

    q = q_ref[0]
    for p in range(B_HEADS // 2):
        even, odd = _split_pair(q[:, p * LANES:(p + 1) * LANES], True)
        qs_scr[p * tq:(p + 1) * tq, :] = even
        qs_scr[(B_HEADS // 2 + p) * tq:(B_HEADS // 2 + p + 1) * tq, :] = odd
    m_scr[...] = jnp.full(m_scr.shape, NEG, F32)
    acc_scr[...] = jnp.zeros(acc_scr.shape, F32)
    key_i = lax.broadcasted_iota(jnp.int32, (tq, tq), 0)
    lower = (lax.broadcasted_iota(jnp.int32, (tq, tq), 1) <= key_i).astype(BF16)

    def select_bias(jj, tie_count):
        sblk = sc_scr[jj]
        tie = sblk == tau
        tie01 = jnp.where(tie, 1.0, 0.0).astype(BF16)
        rank_a = tie_count + jnp.dot(lower, tie01[0:tq], preferred_element_type=F32)
        rank_b = rank_a[tq - 1:tq, :] + jnp.dot(lower, tie01[tq:2 * tq], preferred_element_type=F32)
        rank = jnp.concatenate([rank_a, rank_b], axis=0)
        keep_tie = jnp.where(rank <= need, 0.0, NEG)
        return jnp.where(sblk > tau, 0.0, jnp.where(tie, keep_tie, NEG)), rank[2 * tq - 1:2 * tq, :]

    def p3_step(jj, pairs, tie_count):
        nkeys = pairs * 2 * tq
        start = pl.multiple_of(jj * 2 * tq, 2 * tq)
        biases = []
        for t in range(pairs):
            bias, tie_count = select_bias(jj + t, tie_count)
            biases.append(bias)
        bias = jnp.concatenate(biases, axis=0)
        s = _qk(k_ref[0, pl.ds(start, nkeys), :], qs_scr[...])
        s = jnp.concatenate([s[:, h * tq:(h + 1) * tq] + bias for h in range(B_HEADS)], axis=1)
        _flash_t_update(s, [((t * tq, (t + 1) * tq), vt_ref[0, 2 * jj + t]) for t in range(2 * pairs)],
                        m_scr, acc_scr)
        return tie_count

    tie_count = lax.fori_loop(0, npair // P3_PAIRS,
                              lambda g, tc: p3_step(g * P3_PAIRS, P3_PAIRS, tc),
                              jnp.zeros((1, tq), F32))
    for r in range(1, P3_PAIRS):
        @pl.when(npair % P3_PAIRS == r)
        def _():
            p3_step(npair - r, r, tie_count)

    acc = acc_scr[...]
    o = acc[0:HEAD_DIM] / acc[HEAD_DIM:HEAD_DIM + 1]
    half_cols = (B_HEADS // 2) * tq
    for p in range(B_HEADS // 2):
        pair = jnp.concatenate([o[:, p * tq:(p + 1) * tq],
                                o[:, half_cols + p * tq:half_cols + (p + 1) * tq]], axis=0)
        o_ref[0, :, p * LANES:(p + 1) * LANES] = pair.T.astype(BF16)


def _dsa_attention(rope_slab, plain_slab, tq):
    bsz, s, _ = rope_slab.shape
    topk = min(TOPK_MAX, s // 4)
    nblk = s // tq
    assert tq >= topk and s % tq == 0 and nblk % 2 == 0
    kern = functools.partial(_dsa_kernel, tq=tq, topk=topk)
    idx0 = R_IDX * COL_TILE
    misc0 = P_MISC * COL_TILE
    iq_w = IDX_HEADS * HEAD_DIM
    width = B_HEADS * HEAD_DIM
    vt = _with_ones_rows(
        plain_slab[:, :, misc0:misc0 + HEAD_DIM].reshape(bsz, nblk, tq, HEAD_DIM).transpose(0, 1, 3, 2))
    return pl.pallas_call(
        kern,
        grid=(bsz, s // tq),
        in_specs=[
            pl.BlockSpec((1, tq, COL_TILE), lambda b, i: (b, i, R_BQ)),
            pl.BlockSpec((1, tq, iq_w), lambda b, i: (b, i, idx0 // iq_w)),
            pl.BlockSpec((1, tq, LANES), lambda b, i: (b, i, (misc0 + LANES) // LANES)),
            pl.BlockSpec((1, s, LANES), lambda b, i: (b, 0, (idx0 + iq_w) // LANES)),
            pl.BlockSpec((1, s, LANES), lambda b, i: (b, 0, (idx0 + iq_w + LANES) // LANES)),
            pl.BlockSpec((1, nblk, VT_ROWS, tq), lambda b, i: (b, 0, 0, 0)),
        ],
        out_specs=pl.BlockSpec((1, tq, width), lambda b, i: (b, i, 0)),
        out_shape=jax.ShapeDtypeStruct((bsz, s, width), BF16),
        scratch_shapes=[
            pltpu.VMEM((nblk // 2, 2 * tq, tq), F32),
            pltpu.VMEM((B_HEADS * tq, LANES), BF16),
            pltpu.VMEM((1, B_HEADS * tq), F32),
            pltpu.VMEM((VT_ROWS, B_HEADS * tq), F32),
        ],
        compiler_params=_cparams(("arbitrary", "arbitrary")),
        name="dsa_attn",
    )(rope_slab, rope_slab, plain_slab, rope_slab, rope_slab, vt)


BAND_TQ = 2 * CHUNK
BAND_KEYS = (C_LEFT_CHUNKS + BAND_TQ // CHUNK) * CHUNK
BAND_PAD = C_LEFT_CHUNKS * CHUNK


def _band_kernel(q_ref, k_ref, v_ref, bias_ref, o_ref):
    i = pl.program_id(1)
    tq = BAND_TQ
    start = pl.multiple_of(i * tq, tq)
    col = lax.broadcasted_iota(jnp.int32, (2 * tq, BAND_KEYS), 1)
    key_ok = col + start >= BAND_PAD
    for p in range(C_HEADS // 2):
        lanes = slice(p * LANES, (p + 1) * LANES)
        k = k_ref[0, pl.ds(start, BAND_KEYS), lanes]
        v = v_ref[0, pl.ds(start, BAND_KEYS), lanes]
        qs = jnp.concatenate(_split_pair(q_ref[0, :, lanes], False), axis=0)
        s = _qk(qs, k) + bias_ref[2 * p:2 * p + 2].reshape(2 * tq, BAND_KEYS)
        s = jnp.where(key_ok, s, NEG)
        m = jnp.max(s, axis=-1, keepdims=True)
        e = jnp.exp2(s - m)
        l = jnp.sum(e, axis=-1, keepdims=True)
        eb = e.astype(BF16)
        v_even, v_odd = _split_pair(v, False)
        o = (jnp.dot(eb[:tq], v_even, preferred_element_type=F32) / l[:tq]
             + jnp.dot(eb[tq:], v_odd, preferred_element_type=F32) / l[tq:])
        o_ref[0, :, lanes] = o.astype(BF16)


def _band_bias(rel_bias):
    r = jnp.arange(BAND_TQ)[:, None]
    cidx = jnp.arange(BAND_KEYS)[None, :]
    n_diag = BAND_TQ + BAND_KEYS - 1
    rel = BAND_PAD - (BAND_KEYS - 1) + jnp.arange(n_diag)
    g = rel_bias.astype(F32)[:, jnp.clip(rel, -REL_CLIP, REL_CLIP) + REL_CLIP] * LOG2E
    skew = jnp.tile(g, (1, BAND_TQ + 2))[:, :BAND_TQ * (n_diag + 1)]
    bias = skew.reshape(-1, BAND_TQ, n_diag + 1)[:, :, :BAND_KEYS][:, :, ::-1]
    dchunk = (r // CHUNK + C_LEFT_CHUNKS) - cidx // CHUNK
    in_band = jnp.logical_and(dchunk >= 0, dchunk <= C_LEFT_CHUNKS)
    return jnp.where(in_band[None], bias, NEG)


def _band_attention(plain_slab, kp, vp, bias):
    bsz, s, _ = plain_slab.shape
    tq = BAND_TQ
    sp = kp.shape[1]
    width = C_HEADS * HEAD_DIM
    return pl.pallas_call(
        _band_kernel,
        grid=(bsz, s // tq),
        in_specs=[
            pl.BlockSpec((1, tq, width), lambda b, i: (b, i, P_CQ)),
            pl.BlockSpec((1, sp, width), lambda b, i: (b, 0, 0)),
            pl.BlockSpec((1, sp, width), lambda b, i: (b, 0, 0)),
            pl.BlockSpec((C_HEADS, tq, BAND_KEYS), lambda b, i: (0, 0, 0)),
        ],
        out_specs=pl.BlockSpec((1, tq, width), lambda b, i: (b, i, 0)),
        out_shape=jax.ShapeDtypeStruct((bsz, s, width), BF16),
        compiler_params=_cparams(("arbitrary", "arbitrary")),
        name="band_attn",
    )(plain_slab, kp, vp, bias)


def _merge_kernel(x_ref, g1_ref, ya_ref, yb_ref, yc_ref, ga_ref, gb_ref, gc_ref,
                  wa_ref, wb_ref, wc_ref, wo_ref, o_ref):
    for r0 in range(0, x_ref.shape[1], PROJ_ROWS):
        rows = slice(r0, r0 + PROJ_ROWS)

        def branch(y_ref, w_ref, gate_ref):
            return gate_ref[0, rows, :].astype(F32) * jnp.dot(y_ref[0, rows, :], w_ref[...],
                                                             preferred_element_type=F32)

        merged = (branch(ya_ref, wa_ref, ga_ref) + branch(yb_ref, wb_ref, gb_ref)
                  + branch(yc_ref, wc_ref, gc_ref))
        mixed = jnp.dot(merged.astype(BF16), wo_ref[...], preferred_element_type=F32)
        o_ref[0, rows, :] = x_ref[0, rows, :] + g1_ref[0] * mixed


def _merge(x, g1, ya, yb, yc, gate_slab, wa, wb, wc, wo, tm):
    bsz, s, d = x.shape
    tok = lambda b, i: (b, i, 0)
    full = lambda b, i: (0, 0)
    return pl.pallas_call(
        _merge_kernel,
        grid=(bsz, s // tm),
        in_specs=[
            pl.BlockSpec((1, tm, d), tok),
            pl.BlockSpec((1, 1, d), lambda b, i: (b, 0, 0)),
            pl.BlockSpec((1, tm, ya.shape[2]), tok),
            pl.BlockSpec((1, tm, yb.shape[2]), tok),
            pl.BlockSpec((1, tm, yc.shape[2]), tok),
            pl.BlockSpec((1, tm, d), lambda b, i: (b, i, 0)),
            pl.BlockSpec((1, tm, d), lambda b, i: (b, i, 1)),
            pl.BlockSpec((1, tm, d), lambda b, i: (b, i, 2)),
            pl.BlockSpec(wa.shape, full),
            pl.BlockSpec(wb.shape, full),
            pl.BlockSpec(wc.shape, full),
            pl.BlockSpec(wo.shape, full),
        ],
        out_specs=pl.BlockSpec((1, tm, d), tok),
        out_shape=jax.ShapeDtypeStruct((bsz, s, d), F32),
        compiler_params=_cparams(("arbitrary", "arbitrary")),
        name="merge",
    )(x, g1, ya, yb, yc, gate_slab, gate_slab, gate_slab, wa, wb, wc, wo)


def _router_kernel(x_ref, g_ref, sc_ref, sh_ref, rw_ref, rb_ref, u_ref, comb_ref):
    u = _modulated_norm(x_ref[0], g_ref[...], sc_ref[0], sh_ref[0])
    u_ref[0] = u.astype(BF16)
    logits = lax.dot_general(rw_ref[...], u, (((1,), (1,)), ((), ())),
                             preferred_element_type=F32, precision=lax.Precision.HIGHEST)
    aff = jax.nn.sigmoid(logits)
    sel = aff + rb_ref[...]
    rows = [sel[e:e + 1] for e in range(N_EXPERTS)]
    gscore = []
    for g in range(N_GROUPS):
        r = rows[g * EXPERTS_PER_GROUP:(g + 1) * EXPERTS_PER_GROUP]
        best = None
        for a in range(EXPERTS_PER_GROUP):
            for b in range(a + 1, EXPERTS_PER_GROUP):
                pair = r[a] + r[b]
                best = pair if best is None else jnp.maximum(best, pair)
        gscore.append(best)
    gmax = functools.reduce(jnp.maximum, gscore)
    taken = jnp.zeros_like(gmax) > 1.0
    in_best = []
    for g in range(N_GROUPS):
        is_g = jnp.logical_and(gscore[g] == gmax, jnp.logical_not(taken))
        in_best.append(is_g)
        taken = jnp.logical_or(taken, is_g)
    keep = []
    for e in range(N_EXPERTS):
        g = e // EXPERTS_PER_GROUP
        rank = jnp.zeros_like(gmax)
        for o in range(g * EXPERTS_PER_GROUP, (g + 1) * EXPERTS_PER_GROUP):
            if o == e:
                continue
            ahead = rows[o] > rows[e] if o > e else rows[o] >= rows[e]
            rank = rank + jnp.where(ahead, 1.0, 0.0)
        keep.append(jnp.logical_and(in_best[g], rank < 2.0))
    w = [jnp.where(keep[e], aff[e:e + 1], 0.0) for e in range(N_EXPERTS)]
    total = functools.reduce(lambda a, b: a + b, w)
    comb = jnp.concatenate([we / total for we in w]
                           + [jnp.zeros((LANES - N_EXPERTS, total.shape[1]), F32)], axis=0)
    comb_ref[0] = comb.T


def _router(x, g, sc, sh, router_w, router_b, tm):
    bsz, s, d = x.shape
    tok = lambda b, i: (b, i, 0)
    return pl.pallas_call(
        _router_kernel,
        grid=(bsz, s // tm),
        in_specs=[
            pl.BlockSpec((1, tm, d), tok),
            pl.BlockSpec((1, d), lambda b, i: (0, 0)),
            pl.BlockSpec((1, 1, d), lambda b, i: (b, 0, 0)),
            pl.BlockSpec((1, 1, d), lambda b, i: (b, 0, 0)),
            pl.BlockSpec((N_EXPERTS, d), lambda b, i: (0, 0)),
            pl.BlockSpec((N_EXPERTS, 1), lambda b, i: (0, 0)),
        ],
        out_specs=[pl.BlockSpec((1, tm, d), tok), pl.BlockSpec((1, tm, LANES), tok)],
        out_shape=[jax.ShapeDtypeStruct((bsz, s, d), BF16),
                   jax.ShapeDtypeStruct((bsz, s, LANES), F32)],
        compiler_params=_cparams(("arbitrary", "arbitrary")),
        name="router",
    )(x, g.reshape(1, d), sc, sh, router_w.T, router_b.reshape(N_EXPERTS, 1))


MOE_SUB = 512
MOE_CAP = 128


def _expert_ffn(rows, w1_ref, w3_ref, w2_ref):
    h1 = jnp.dot(rows, w1_ref[0], preferred_element_type=F32)
    h3 = jnp.dot(rows, w3_ref[0], preferred_element_type=F32)
    h = (h1 * jax.nn.sigmoid(h1)) * h3
    return jnp.dot(h.astype(BF16), w2_ref[0], preferred_element_type=F32)


def _moe_kernel(x_ref, g2_ref, fg_ref, u_ref, comb_ref, w1_ref, w3_ref, w2_ref, o_ref,
                acc_scr, pos_scr, pos_t_scr, comb_t_scr, scatter_scr, y_scr, *, final_norm):
    e = pl.program_id(2)
    tm = u_ref.shape[1]
    sub = min(MOE_SUB, tm)
    subs = [slice(r0, r0 + sub) for r0 in range(0, tm, sub)]

    @pl.when(e == 0)
    def _():
        acc_scr[...] = jnp.zeros(acc_scr.shape, F32)
        before = (lax.broadcasted_iota(jnp.int32, (sub, sub), 1)
                  < lax.broadcasted_iota(jnp.int32, (sub, sub), 0)).astype(BF16)
        for rows in subs:
            comb0 = comb_ref[0, rows, :]
            member = jnp.where(comb0 > 0.0, 1.0, 0.0).astype(BF16)
            pos = jnp.dot(before, member, preferred_element_type=F32)
            pos_scr[rows, :] = pos
            pos_t_scr[:, rows] = pos.T
            comb_t_scr[:, rows] = comb0.T

    comb = comb_ref[0]
    lane = lax.broadcasted_iota(jnp.int32, comb.shape, 1)
    ce = jnp.sum(jnp.where(lane == e, comb, 0.0), axis=-1, keepdims=True)
    pe = jnp.sum(jnp.where(lane == e, pos_scr[...], 0.0), axis=-1, keepdims=True)
    member_e = jnp.where(ce > 0.0, 1.0, 0.0)
    fullest = functools.reduce(jnp.maximum, [jnp.sum(member_e[rows]) for rows in subs])
    routed = fullest <= float(MOE_CAP)
    slots = pl.ds(pl.multiple_of(e * MOE_CAP, MOE_CAP), MOE_CAP)

    @pl.when(routed)
    def _():
        slot_l = lax.broadcasted_iota(jnp.int32, (1, MOE_CAP), 1).astype(F32)
        scatter_scr[e] = jnp.where(jnp.logical_and(ce > 0.0, pe == slot_l), 1.0, 0.0).astype(BF16)
        slot_s = lax.broadcasted_iota(jnp.int32, (MOE_CAP, 1), 0).astype(F32)
        packed, weights = [], []
        for rows in subs:
            ce_t = comb_t_scr[pl.ds(e, 1), rows]
            pe_t = pos_t_scr[pl.ds(e, 1), rows]
            hit = jnp.logical_and(ce_t > 0.0, pe_t == slot_s)
            weights.append(jnp.sum(jnp.where(hit, ce_t, 0.0), axis=-1, keepdims=True))
            packed.append(jnp.dot(jnp.where(hit, 1.0, 0.0).astype(BF16), u_ref[0, rows, :],
                                  preferred_element_type=F32).astype(BF16))
        y = jnp.concatenate(weights, axis=0) * _expert_ffn(jnp.concatenate(packed, axis=0),
                                                           w1_ref, w3_ref, w2_ref)
        for k in range(len(subs)):
            y_scr[k, slots, :] = y[k * MOE_CAP:(k + 1) * MOE_CAP].astype(BF16)

    @pl.when(jnp.logical_not(routed))
    def _():
        scatter_scr[e] = jnp.zeros(scatter_scr.shape[1:], BF16)
        for k in range(len(subs)):
            y_scr[k, slots, :] = jnp.zeros((MOE_CAP, y_scr.shape[2]), BF16)
        acc_scr[...] += ce * _expert_ffn(u_ref[0], w1_ref, w3_ref, w2_ref)

    @pl.when(e == pl.num_programs(2) - 1)
    def _():
        for k, rows in enumerate(subs):
            scatter = jnp.concatenate([scatter_scr[j, rows, :] for j in range(N_EXPERTS)], axis=1)
            y = acc_scr[rows, :] + jnp.dot(scatter, y_scr[k], preferred_element_type=F32)
            out = x_ref[0, rows, :] + g2_ref[0] * y
            if final_norm:
                out = (out * lax.rsqrt(jnp.mean(out * out, axis=-1, keepdims=True) + EPS)) * fg_ref[...]
            o_ref[0, rows, :] = out


def _moe(x, g2, final_g, u, comb, w1, w3, w2, tm, final_norm):
    bsz, s, d = x.shape
    ne, _, dff = w1.shape
    assert ne == N_EXPERTS
    tok = lambda b, i, e: (b, i, 0)
    return pl.pallas_call(
        functools.partial(_moe_kernel, final_norm=final_norm),
        grid=(bsz, s // tm, ne),
        in_specs=[
            pl.BlockSpec((1, tm, d), tok),
            pl.BlockSpec((1, 1, d), lambda b, i, e: (b, 0, 0)),
            pl.BlockSpec((1, d), lambda b, i, e: (0, 0)),
            pl.BlockSpec((1, tm, d), tok),
            pl.BlockSpec((1, tm, LANES), tok),
            pl.BlockSpec((1, d, dff), lambda b, i, e: (e, 0, 0)),
            pl.BlockSpec((1, d, dff), lambda b, i, e: (e, 0, 0)),
            pl.BlockSpec((1, dff, d), lambda b, i, e: (e, 0, 0)),
        ],
        out_specs=pl.BlockSpec((1, tm, d), tok),
        out_shape=jax.ShapeDtypeStruct((bsz, s, d), F32),
        scratch_shapes=[
            pltpu.VMEM((tm, d), F32),
            pltpu.VMEM((tm, LANES), F32),
            pltpu.VMEM((LANES, tm), F32),
            pltpu.VMEM((LANES, tm), F32),
            pltpu.VMEM((ne, tm, MOE_CAP), BF16),
            pltpu.VMEM((pl.cdiv(tm, MOE_SUB), ne * MOE_CAP, d), BF16),
        ],
        compiler_params=_cparams(("arbitrary", "arbitrary", "arbitrary")),
        name="moe",
    )(x, g2, final_g.reshape(1, d), u, comb, w1, w3, w2)


def _tile(s, want):
    t = min(want, s)
    assert s % t == 0
    return t


def _tiles(s):
    return dict(
        rows=_tile(s, 1024),
        diff_q=_tile(s, 1024),
        diff_k=_tile(s, 512),
        dsa_q=_tile(s, 256),
        merge=_tile(s, 512),
    )


def kernel(x, c, positions, norm1_g, norm2_g, w_mod, b_mod, w_in, lambda_q1, lambda_k1, lambda_q2, lambda_k2, a_norm_g, c_rel_bias, w_branch_a, w_branch_b, w_branch_c, w_out, router_w, router_b, exp_w1, exp_w3, exp_w2, final_g):
    bsz, s, d = x.shape
    depth = w_mod.shape[0]
    t = _tiles(s)
    tm = t["rows"]

    mod = _modulation(c, w_mod, b_mod)
    rope_tables = _rope_tables(positions)

    for layer in range(depth):
        lam_init = 0.8 - 0.6 * math.exp(-0.3 * layer)
        sh1, sc1, g1, sh2, sc2, g2 = [m[:, None, :] for m in jnp.split(mod[layer], 6, axis=-1)]
        lam = (jnp.exp(jnp.sum(lambda_q1[layer] * lambda_k1[layer]))
               - jnp.exp(jnp.sum(lambda_q2[layer] * lambda_k2[layer])) + lam_init)

        w_rope, w_gate, w_plain = _build_weights(w_in[layer])
        u = _norm(x, norm1_g[layer], sc1, sh1, tm)
        rope_slab = _project(u, w_rope, rope_tables, "rope", tm)
        gate_slab = _project(u, w_gate, (), "gate", tm)
        plain_slab = _project(u, w_plain, (), "plain", tm)

        ya = _diff_attention(rope_slab, plain_slab, lam, a_norm_g[layer], lam_init,
                             t["diff_q"], t["diff_k"])
        yb = _dsa_attention(rope_slab, plain_slab, t["dsa_q"])
        pad = ((0, 0), (BAND_PAD, 0), (0, 0))
        kp = jnp.pad(plain_slab[:, :, P_CK * COL_TILE:(P_CK + 1) * COL_TILE], pad)
        vp = jnp.pad(plain_slab[:, :, P_CV * COL_TILE:(P_CV + 1) * COL_TILE], pad)
        yc = _band_attention(plain_slab, kp, vp, _band_bias(c_rel_bias[layer]))
        x = _merge(x, g1, ya, yb, yc, gate_slab,
                   w_branch_a[layer].astype(BF16), w_branch_b[layer].astype(BF16),
                   w_branch_c[layer].astype(BF16), w_out[layer].astype(BF16), t["merge"])

        u, comb = _router(x, norm2_g[layer], sc2, sh2, router_w, router_b, t["merge"])
        x = _moe(x, g2, final_g, u, comb, exp_w1[layer].astype(BF16), exp_w3[layer].astype(BF16),
                 exp_w2[layer].astype(BF16), tm, final_norm=(layer == depth - 1))

    return x
```
